```python
import math
import jax, jax.numpy as jnp
from jax import lax
import numpy as np

D_MODEL = 2048
BATCH = 4
SEQ = 2048
DEPTH = 4
DEC_BATCH = 8
DEC_SEQ = 1
PAST_LEN = 16384
PAGE_SIZE = 128

HEAD_DIM = 128
N_MIX_HEADS = D_MODEL // HEAD_DIM
NSA_HEADS = N_MIX_HEADS // 2
NSA_KV_HEADS = 2
NSA_GROUP = NSA_HEADS // NSA_KV_HEADS
NSA_BLOCK = 64
NSA_TOPN = 16
NSA_WINDOW = 512
NSA_PHI_HIDDEN = 2 * HEAD_DIM
NSA_FORCE_SCORE = 1.0e4
HG_HEADS = N_MIX_HEADS - NSA_HEADS
HG_DK = 128
HG_DV = HEAD_DIM
HG_CHUNK = 64
MOBA_HEADS = N_MIX_HEADS // 2
MOBA_BLOCK = 256
MOBA_TOPK = 3
FOX_HEADS = N_MIX_HEADS - MOBA_HEADS
REL_BUCKETS = 32
REL_MAX_DIST = 128
D_FF = 256 * ((8 * D_MODEL // 3 + 255) // 256)
QBLK = 128
GATHER_QBLK = 16
EPS = 1e-6
N_EVEN = (DEPTH + 1) // 2
N_ODD = DEPTH // 2
EVEN_SPLITS = (NSA_HEADS * HEAD_DIM, 2 * NSA_KV_HEADS * HEAD_DIM, 2 * NSA_KV_HEADS * HEAD_DIM,
               2 * NSA_KV_HEADS * HEAD_DIM, 3 * NSA_HEADS,
               HG_HEADS * HG_DK, HG_HEADS * HG_DK, HG_HEADS * HG_DV, HG_HEADS * HG_DV)
EVEN_IN = sum(EVEN_SPLITS)
ODD_SPLITS = (MOBA_HEADS * HEAD_DIM, 2 * MOBA_HEADS * HEAD_DIM, FOX_HEADS * HEAD_DIM,
              2 * FOX_HEADS * HEAD_DIM, FOX_HEADS)
ODD_IN = sum(ODD_SPLITS)

kernel_name = 'hybrid_nsa_hgrn2_moba_fox_decode_step'


def _split(h, sizes):
    cuts = [int(c) for c in np.cumsum(sizes)[:-1]]
    return jnp.split(h, cuts, axis=-1)


def _rms(x, g):
    xf = x.astype(jnp.float32)
    y = xf * lax.rsqrt(jnp.mean(jnp.square(xf), axis=-1, keepdims=True) + EPS)
    return (y * g.astype(jnp.float32)).astype(x.dtype)


def _swiglu(x, w_in, w_out):
    a, b = jnp.split(x @ w_in, 2, axis=-1)
    return (jax.nn.silu(a) * b) @ w_out


def _rel_bucket(dist):
    n = jnp.maximum(dist, 0)
    exact = REL_BUCKETS // 2
    nf = jnp.maximum(n, 1).astype(jnp.float32)
    big = exact + (jnp.log(nf / exact) / math.log(REL_MAX_DIST / exact) * (REL_BUCKETS - exact)).astype(jnp.int32)
    return jnp.where(n < exact, n, jnp.minimum(big, REL_BUCKETS - 1))


def _masked_softmax(logits, mask):
    logits = jnp.where(mask, logits.astype(jnp.float32), -jnp.inf)
    m = jnp.max(logits, axis=-1, keepdims=True)
    m = jnp.where(jnp.isfinite(m), m, 0.0)
    p = jnp.exp(logits - m)
    s = jnp.sum(p, axis=-1, keepdims=True)
    return p / jnp.where(s > 0, s, 1.0)


def _norm_keys(rows, g):
    return jnp.stack([_rms(rows[:, :, 0], g), rows[:, :, 1]], axis=2)


def _nsa_compress(rows, pos, w1, w2):
    B, L, KH, D = rows.shape
    nb = L // NSA_BLOCK
    blk = rows[:, : nb * NSA_BLOCK].reshape(B, nb, NSA_BLOCK, KH, D) + pos[None, None, :, None, :]
    blk = jnp.transpose(blk, (0, 1, 3, 2, 4)).reshape(B, nb, KH, NSA_BLOCK * D)
    return jax.nn.gelu(blk @ w1) @ w2


def _nsa_attention(q, cmp_rows, sel_rows, win_rows, q_start, win_start, gates,
                   k_cmp_norm, phi_pos, phi_w1, phi_w2, rel_bias):
    B, T, H, D = q.shape
    KH, G = NSA_KV_HEADS, NSA_GROUP
    L = cmp_rows.shape[1]
    qt = jnp.transpose(q.reshape(B, T, KH, G, D), (0, 2, 3, 1, 4))
    t_pos = q_start + jnp.arange(T)
    kc = _rms(_nsa_compress(cmp_rows[:, :, 0], phi_pos[0], phi_w1[0], phi_w2[0]), k_cmp_norm)
    vc = _nsa_compress(cmp_rows[:, :, 1], phi_pos[1], phi_w1[1], phi_w2[1])
    nb = kc.shape[1]
    dist_c = t_pos[:, None] - (jnp.arange(nb) * NSA_BLOCK + NSA_BLOCK - 1)[None, :]
    bias_c = rel_bias[_rel_bucket(dist_c)].reshape(T, nb, KH, G).transpose(2, 3, 0, 1)
    p_c = _masked_softmax(jnp.einsum('bkgtd,bnkd->bkgtn', qt, kc) + bias_c, dist_c >= 0)
    o_c = jnp.einsum('bkgtn,bnkd->btkgd', p_c.astype(vc.dtype), vc)
    nbs = -(-L // NSA_BLOCK)
    imp = jnp.pad(jnp.sum(p_c, axis=2), ((0, 0), (0, 0), (0, 0), (0, nbs - nb)))
    blk = jnp.arange(nbs)[None, :]
    cur = (t_pos // NSA_BLOCK)[:, None]
    forced = (blk == 0) | (blk == cur) | (blk == cur - 1)
    score = jnp.where(forced, NSA_FORCE_SCORE, jnp.where(blk <= cur, imp, -1.0))
    n_sel = min(NSA_TOPN, nbs)
    _, sel_idx = lax.top_k(score, n_sel)
    kvb = jnp.pad(sel_rows, ((0, 0), (0, nbs * NSA_BLOCK - L), (0, 0), (0, 0), (0, 0)))
    kvb = kvb.reshape(B, nbs, NSA_BLOCK, 2, KH, D).transpose(0, 4, 1, 2, 3, 5)
    rbt = rel_bias.T.reshape(KH, G, REL_BUCKETS)
    bi = jnp.arange(B)[:, None, None, None]
    ki = jnp.arange(KH)[None, :, None, None]
    ki5 = ki[..., None]
    qbs = math.gcd(T, GATHER_QBLK)
    nq = T // qbs
    ns = n_sel * NSA_BLOCK

    def sel_block(args):
        qblk, iblk, i = args
        tq = q_start + i * qbs + jnp.arange(qbs)
        kv = kvb[bi, ki, iblk]
        pos = iblk[..., None] * NSA_BLOCK + jnp.arange(NSA_BLOCK)
        dist = tq[:, None, None] - pos
        bias = jnp.moveaxis(rbt[ki5, :, _rel_bucket(dist)], -1, 2).reshape(B, KH, G, qbs, ns)
        k = kv[..., 0, :].reshape(B, KH, qbs, ns, D)
        v = kv[..., 1, :].reshape(B, KH, qbs, ns, D)
        p = _masked_softmax(jnp.einsum('bkgqd,bkqsd->bkgqs', qblk, k) + bias,
                            (dist >= 0).reshape(B, KH, 1, qbs, ns))
        return jnp.einsum('bkgqs,bkqsd->bkgqd', p.astype(v.dtype), v)

    q_blocks = qt.reshape(B, KH, G, nq, qbs, D).transpose(3, 0, 1, 2, 4, 5)
    i_blocks = sel_idx.reshape(B, KH, nq, qbs, n_sel).transpose(2, 0, 1, 3, 4)
    o_s = lax.map(sel_block, (q_blocks, i_blocks, jnp.arange(nq)))
    o_s = o_s.transpose(1, 0, 4, 2, 3, 5).reshape(B, T, KH, G, D)
    qbw = math.gcd(T, QBLK)
    nqw = T // qbw
    span = NSA_WINDOW + qbw
    wpad = jnp.pad(win_rows, ((0, 0), (NSA_WINDOW, 0), (0, 0), (0, 0), (0, 0)))

    def win_block(args):
        qblk, i = args
        p0 = q_start + i * qbw
        kv = lax.dynamic_slice_in_dim(wpad, p0 - win_start, span, axis=1)
        tq = p0 + jnp.arange(qbw)
        s = p0 - NSA_WINDOW + jnp.arange(span)
        dist = tq[:, None] - s[None, :]
        mask = (s[None, :] >= win_start) & (dist >= 0) & (dist <= NSA_WINDOW)
        bias = rel_bias[_rel_bucket(dist)].reshape(qbw, span, KH, G).transpose(2, 3, 0, 1)
        p = _masked_softmax(jnp.einsum('bkgqd,bskd->bkgqs', qblk, kv[:, :, 0]) + bias, mask)
        return jnp.einsum('bkgqs,bskd->bkgqd', p.astype(kv.dtype), kv[:, :, 1])

    qw_blocks = qt.reshape(B, KH, G, nqw, qbw, D).transpose(3, 0, 1, 2, 4, 5)
    o_w = lax.map(win_block, (qw_blocks, jnp.arange(nqw)))
    o_w = o_w.transpose(1, 0, 4, 2, 3, 5).reshape(B, T, KH, G, D)
    g = gates.reshape(B, T, KH, G, 3)
    o = g[..., 0:1] * o_c + g[..., 1:2] * o_s + g[..., 2:3] * o_w
    return o.reshape(B, T, H * D)


def _hgrn2(q, logf, k, v, s0):
    B, T, H, _ = q.shape
    C = HG_CHUNK
    nc = -(-T // C)
    pad = nc * C - T

    def prep(a):
        a = jnp.pad(a.astype(jnp.float32), ((0, 0), (0, pad), (0, 0), (0, 0)))
        return a.reshape(B, nc, C, H, a.shape[-1]).transpose(1, 0, 3, 2, 4)

    causal = jnp.tril(jnp.ones((C, C), dtype=bool))

    def step(S, xs):
        qc, lfc, kc, vc = xs
        A = jnp.cumsum(lfc, axis=2)
        diff = A[:, :, :, None, :] - A[:, :, None, :, :]
        decay = jnp.exp(jnp.where(causal[:, :, None], diff, -jnp.inf))
        scores = jnp.einsum('bhtd,bhtsd,bhsd->bhts', qc, decay, kc)
        o = jnp.einsum('bhts,bhse->bhte', scores, vc) + jnp.einsum('bhtd,bhde->bhte', qc * jnp.exp(A), S)
        a_last = A[:, :, -1:, :]
        S = jnp.exp(a_last[:, :, 0, :, None]) * S + jnp.einsum('bhsd,bhse->bhde', kc * jnp.exp(a_last - A), vc)
        return S, o

    S, o = lax.scan(step, s0.astype(jnp.float32), (prep(q), prep(logf), prep(k), prep(v)))
    o = o.transpose(1, 0, 3, 2, 4).reshape(B, nc * C, H, -1)[:, :T]
    return o, S


def _moba_attention(q, kv_rows, q_start, rel_bias):
    B, T, H, D = q.shape
    L = kv_rows.shape[1]
    nbl = -(-L // MOBA_BLOCK)
    kvb = jnp.pad(kv_rows, ((0, 0), (0, nbl * MOBA_BLOCK - L), (0, 0), (0, 0), (0, 0)))
    kvb = kvb.reshape(B, nbl, MOBA_BLOCK, 2, H, D).transpose(0, 4, 1, 2, 3, 5)
    k_mean = jnp.mean(kvb[..., 0, :].astype(jnp.float32), axis=3)
    qt = jnp.transpose(q, (0, 2, 1, 3))
    t_pos = q_start + jnp.arange(T)
    past_ok = jnp.arange(nbl)[None, :] < (t_pos // MOBA_BLOCK)[:, None]
    gate = jnp.where(past_ok, jnp.einsum('bhtd,bhnd->bhtn', qt.astype(jnp.float32), k_mean), -jnp.inf)
    n_sel = min(MOBA_TOPK, nbl)
    g_val, sel_idx = lax.top_k(gate, n_sel)
    sel_ok = jnp.isfinite(g_val)
    rbt = rel_bias.T
    bi = jnp.arange(B)[:, None, None, None]
    hi = jnp.arange(H)[None, :, None, None]
    hi5 = hi[..., None]
    qbs = math.gcd(T, GATHER_QBLK)
    nq = T // qbs
    ns = n_sel * MOBA_BLOCK

    def blk(args):
        qblk, iblk, okblk, i = args
        p0 = q_start + i * qbs
        tq = p0 + jnp.arange(qbs)
        kv = kvb[bi, hi, iblk]
        pos = iblk[..., None] * MOBA_BLOCK + jnp.arange(MOBA_BLOCK)
        bias_s = rbt[hi5, _rel_bucket(tq[:, None, None] - pos)]
        logit_s = (jnp.einsum('bhqd,bhqnsd->bhqns', qblk, kv[..., 0, :]) + bias_s).reshape(B, H, qbs, ns)
        mask_s = jnp.broadcast_to(okblk[..., None], (B, H, qbs, n_sel, MOBA_BLOCK)).reshape(B, H, qbs, ns)
        c = p0 // MOBA_BLOCK
        own = lax.dynamic_index_in_dim(kvb, c, axis=2, keepdims=False)
        dist_o = tq[:, None] - (c * MOBA_BLOCK + jnp.arange(MOBA_BLOCK))[None, :]
        bias_o = rel_bias[_rel_bucket(dist_o)].transpose(2, 0, 1)
        logit_o = jnp.einsum('bhqd,bhsd->bhqs', qblk, own[..., 0, :]) + bias_o
        mask_o = jnp.broadcast_to(dist_o >= 0, (B, H, qbs, MOBA_BLOCK))
        p = _masked_softmax(jnp.concatenate([logit_s, logit_o], axis=-1),
                            jnp.concatenate([mask_s, mask_o], axis=-1))
        v_s = kv[..., 1, :].reshape(B, H, qbs, ns, D)
        return (jnp.einsum('bhqs,bhqsd->bhqd', p[..., :ns].astype(v_s.dtype), v_s)
                + jnp.einsum('bhqs,bhsd->bhqd', p[..., ns:].astype(v_s.dtype), own[..., 1, :]))

    q_blocks = qt.reshape(B, H, nq, qbs, D).transpose(2, 0, 1, 3, 4)
    i_blocks = sel_idx.reshape(B, H, nq, qbs, n_sel).transpose(2, 0, 1, 3, 4)
    ok_blocks = sel_ok.reshape(B, H, nq, qbs, n_sel).transpose(2, 0, 1, 3, 4)
    o = lax.map(blk, (q_blocks, i_blocks, ok_blocks, jnp.arange(nq)))
    return o.transpose(1, 0, 3, 2, 4).reshape(B, T, H * D)


def _fox_attention(q, kv_rows, cum_q, cum_k, q_start):
    B, T, H, D = q.shape
    L = kv_rows.shape[1]
    qbs = math.gcd(T, QBLK)
    nq = T // qbs
    k = kv_rows[:, :, 0]
    v = kv_rows[:, :, 1]
    ck = jnp.transpose(cum_k, (0, 2, 1))
    s_pos = jnp.arange(L)

    def blk(args):
        qblk, cq, i = args
        tq = q_start + i * qbs + jnp.arange(qbs)
        logit = jnp.einsum('bhqd,bshd->bhqs', qblk, k).astype(jnp.float32) + cq[..., None] - ck[:, :, None, :]
        p = _masked_softmax(logit, s_pos[None, :] <= tq[:, None])
        return jnp.einsum('bhqs,bshd->bhqd', p.astype(v.dtype), v)

    q_blocks = q.reshape(B, nq, qbs, H, D).transpose(1, 0, 3, 2, 4)
    c_blocks = cum_q.reshape(B, nq, qbs, H).transpose(1, 0, 3, 2)
    o = lax.map(blk, (q_blocks, c_blocks, jnp.arange(nq)))
    return o.transpose(1, 0, 3, 2, 4).reshape(B, T, H * D)


def _even_mixer(h, past_cmp, past_sel, past_win, s0, lb, w_in, w_out, q_norm, k_norm,
                phi_pos, phi_w1, phi_w2, o_norm, rel_bias):
    B, T, _ = h.shape
    P = past_cmp.shape[1]
    Pw = past_win.shape[1]
    nq_, ncmp, nsel, nwin, ngate, hq, hf, hi, hg = _split(h @ w_in, EVEN_SPLITS)
    kv_shape = (B, T, 2, NSA_KV_HEADS, HEAD_DIM)
    q = _rms(nq_.reshape(B, T, NSA_HEADS, HEAD_DIM), q_norm) * (HEAD_DIM ** -0.5)
    cmp_new = ncmp.reshape(kv_shape)
    sel_new = _norm_keys(nsel.reshape(kv_shape), k_norm[1])
    win_new = _norm_keys(nwin.reshape(kv_shape), k_norm[2])
    win_all = jnp.concatenate([past_win, win_new], axis=1)
    gates = jax.nn.sigmoid(ngate.astype(jnp.float32)).reshape(B, T, NSA_HEADS, 3)
    o_nsa = _nsa_attention(q, jnp.concatenate([past_cmp, cmp_new], axis=1),
                           jnp.concatenate([past_sel, sel_new], axis=1), win_all,
                           P, P - Pw, gates, k_norm[0], phi_pos, phi_w1, phi_w2, rel_bias)
    z = hf.reshape(B, T, HG_HEADS, HG_DK).astype(jnp.float32)
    lbh = lb.reshape(HG_HEADS, HG_DK)
    logf = jnp.logaddexp(jnp.log(lbh), jnp.log1p(-lbh) + jax.nn.log_sigmoid(z))
    k_in = (1.0 - lbh) * jax.nn.sigmoid(-z)
    o_hg, S = _hgrn2(hq.reshape(B, T, HG_HEADS, HG_DK), logf, k_in, hi.reshape(B, T, HG_HEADS, HG_DV), s0)
    o_hg = _rms(o_hg, o_norm) * jax.nn.silu(hg.reshape(B, T, HG_HEADS, HG_DV).astype(jnp.float32))
    out = jnp.concatenate([o_nsa.astype(h.dtype), o_hg.reshape(B, T, -1).astype(h.dtype)], axis=-1) @ w_out
    keep = min(NSA_WINDOW, Pw + T)
    return out, cmp_new, sel_new, win_all[:, win_all.shape[1] - keep:], S


def _odd_mixer(h, past_moba, past_fox, past_logf, w_in, f_bias, w_out, moba_qk, fox_qk, rel_bias):
    B, T, _ = h.shape
    P = past_moba.shape[1]
    mq, mkv, fq, fkv, fz = _split(h @ w_in, ODD_SPLITS)
    scale = HEAD_DIM ** -0.5
    q_m = _rms(mq.reshape(B, T, MOBA_HEADS, HEAD_DIM), moba_qk[0]) * scale
    moba_new = _norm_keys(mkv.reshape(B, T, 2, MOBA_HEADS, HEAD_DIM), moba_qk[1])
    o_m = _moba_attention(q_m, jnp.concatenate([past_moba, moba_new], axis=1), P, rel_bias)
    q_f = _rms(fq.reshape(B, T, FOX_HEADS, HEAD_DIM), fox_qk[0]) * scale
    fox_new = _norm_keys(fkv.reshape(B, T, 2, FOX_HEADS, HEAD_DIM), fox_qk[1])
    logf_new = jax.nn.log_sigmoid(fz.astype(jnp.float32) + f_bias.astype(jnp.float32))
    cum = jnp.cumsum(jnp.concatenate([past_logf.astype(jnp.float32), logf_new], axis=1), axis=1)
    o_f = _fox_attention(q_f, jnp.concatenate([past_fox, fox_new], axis=1), cum[:, P:], cum, P)
    out = jnp.concatenate([o_m.astype(h.dtype), o_f.astype(h.dtype)], axis=-1) @ w_out
    return out, moba_new, fox_new, logf_new


def _trunk(x, even_past, odd_past, norm_ffn1, w_ffn1_in, w_ffn1_out, norm_mix, norm_ffn2, w_ffn2_in,
           w_ffn2_out, rel_bias, w_in_even, w_out_even, nsa_q_norm, nsa_k_norm, nsa_phi_pos, nsa_phi_w1,
           nsa_phi_w2, hg_lb, hg_o_norm, w_in_odd, w_out_odd, fox_f_bias, moba_qk_norm, fox_qk_norm):
    lb_all = jnp.cumsum(jax.nn.softmax(hg_lb.astype(jnp.float32), axis=0), axis=0)
    lb_all = lb_all - lb_all[0:1]
    even_new = []
    odd_new = []
    for layer in range(DEPTH):
        x = x + 0.5 * _swiglu(_rms(x, norm_ffn1[layer]), w_ffn1_in[layer], w_ffn1_out[layer])
        h = _rms(x, norm_mix[layer])
        j = layer // 2
        if layer % 2 == 0:
            pc, ps, pw, s0 = even_past[j]
            out, cmp_new, sel_new, win_buf, S = _even_mixer(
                h, pc, ps, pw, s0, lb_all[j], w_in_even[j], w_out_even[j], nsa_q_norm[j], nsa_k_norm[j],
                nsa_phi_pos[j], nsa_phi_w1[j], nsa_phi_w2[j], hg_o_norm[j], rel_bias)
            even_new.append((cmp_new, sel_new, win_buf, S))
        else:
            pm, pf, pl = odd_past[j]
            out, moba_new, fox_new, logf_new = _odd_mixer(
                h, pm, pf, pl, w_in_odd[j], fox_f_bias[j], w_out_odd[j], moba_qk_norm[j], fox_qk_norm[j], rel_bias)
            odd_new.append((moba_new, fox_new, logf_new))
        x = x + out
        x = x + 0.5 * _swiglu(_rms(x, norm_ffn2[layer]), w_ffn2_in[layer], w_ffn2_out[layer])
    return x, even_new, odd_new


def setup_inputs(seed: int = 0) -> dict:
    key = jax.random.key(seed)
    keys = list(jax.random.split(key, 40))
    f32 = jnp.float32
    n_pages = PAST_LEN // PAGE_SIZE
    n_phys = (DEC_BATCH * n_pages * 5) // 4
    win_keep = min(NSA_WINDOW, PAST_LEN)

    def nrm(shape, scale=1.0):
        return jax.random.normal(keys.pop(), shape, f32) * scale

    def gain(shape):
        return 1.0 + nrm(shape, 0.05)

    perm = jax.random.permutation(keys.pop(), n_phys)
    page_table = perm[: DEC_BATCH * n_pages].reshape(DEC_BATCH, n_pages).astype(jnp.int32)
    D = D_MODEL
    return {
        'x_prompt': nrm((BATCH, SEQ, D)),
        'x_sample': nrm((DEC_BATCH, DEC_SEQ, D)),
        'cache_nsa_cmp': nrm((N_EVEN, n_phys, PAGE_SIZE, 2, NSA_KV_HEADS, HEAD_DIM)),
        'cache_nsa_sel': nrm((N_EVEN, n_phys, PAGE_SIZE, 2, NSA_KV_HEADS, HEAD_DIM)),
        'cache_moba': nrm((N_ODD, n_phys, PAGE_SIZE, 2, MOBA_HEADS, HEAD_DIM)),
        'cache_fox': nrm((N_ODD, n_phys, PAGE_SIZE, 2, FOX_HEADS, HEAD_DIM)),
        'cache_fox_logf': jax.nn.log_sigmoid(3.0 + nrm((N_ODD, n_phys, PAGE_SIZE, FOX_HEADS))),
        'cache_nsa_win': nrm((N_EVEN, DEC_BATCH, win_keep, 2, NSA_KV_HEADS, HEAD_DIM)),
        'state_hgrn': nrm((N_EVEN, DEC_BATCH, HG_HEADS, HG_DK, HG_DV), 0.5),
        'page_table': page_table,
        'norm_ffn1': gain((DEPTH, D)),
        'w_ffn1_in': nrm((DEPTH, D, 2 * D_FF), D ** -0.5),
        'w_ffn1_out': nrm((DEPTH, D_FF, D), D_FF ** -0.5),
        'norm_mix': gain((DEPTH, D)),
        'norm_ffn2': gain((DEPTH, D)),
        'w_ffn2_in': nrm((DEPTH, D, 2 * D_FF), D ** -0.5),
        'w_ffn2_out': nrm((DEPTH, D_FF, D), D_FF ** -0.5),
        'rel_bias': nrm((REL_BUCKETS, NSA_HEADS), 0.3),
        'w_in_even': nrm((N_EVEN, D, EVEN_IN), D ** -0.5),
        'w_out_even': nrm((N_EVEN, D, D), D ** -0.5),
        'nsa_q_norm': gain((N_EVEN, HEAD_DIM)),
        'nsa_k_norm': gain((N_EVEN, 3, HEAD_DIM)),
        'nsa_phi_pos': nrm((N_EVEN, 2, NSA_BLOCK, HEAD_DIM), 0.1),
        'nsa_phi_w1': nrm((N_EVEN, 2, NSA_BLOCK * HEAD_DIM, NSA_PHI_HIDDEN), (NSA_BLOCK * HEAD_DIM) ** -0.5),
        'nsa_phi_w2': nrm((N_EVEN, 2, NSA_PHI_HIDDEN, HEAD_DIM), NSA_PHI_HIDDEN ** -0.5),
        'hg_lb': nrm((N_EVEN, HG_HEADS * HG_DK), 0.5),
        'hg_o_norm': gain((N_EVEN, HG_DV)),
        'w_in_odd': nrm((N_ODD, D, ODD_IN), D ** -0.5),
        'w_out_odd': nrm((N_ODD, D, D), D ** -0.5),
        'fox_f_bias': 1.0 + nrm((N_ODD, FOX_HEADS), 0.1),
        'moba_qk_norm': gain((N_ODD, 2, HEAD_DIM)),
        'fox_qk_norm': gain((N_ODD, 2, HEAD_DIM)),
    }


def reference(x_prompt, x_sample, cache_nsa_cmp, cache_nsa_sel, cache_moba, cache_fox, cache_fox_logf,
              cache_nsa_win, state_hgrn, page_table, norm_ffn1, w_ffn1_in, w_ffn1_out, norm_mix, norm_ffn2,
              w_ffn2_in, w_ffn2_out, rel_bias, w_in_even, w_out_even, nsa_q_norm, nsa_k_norm, nsa_phi_pos,
              nsa_phi_w1, nsa_phi_w2, hg_lb, hg_o_norm, w_in_odd, w_out_odd, fox_f_bias, moba_qk_norm,
              fox_qk_norm):
    weights = (norm_ffn1, w_ffn1_in, w_ffn1_out, norm_mix, norm_ffn2, w_ffn2_in, w_ffn2_out, rel_bias,
               w_in_even, w_out_even, nsa_q_norm, nsa_k_norm, nsa_phi_pos, nsa_phi_w1, nsa_phi_w2, hg_lb,
               hg_o_norm, w_in_odd, w_out_odd, fox_f_bias, moba_qk_norm, fox_qk_norm)
    bp = x_prompt.shape[0]
    dt = x_prompt.dtype

    def empty_kv(heads):
        return jnp.zeros((bp, 0, 2, heads, HEAD_DIM), dt)

    even_prompt = [(empty_kv(NSA_KV_HEADS), empty_kv(NSA_KV_HEADS), empty_kv(NSA_KV_HEADS),
                    jnp.zeros((bp, HG_HEADS, HG_DK, HG_DV), jnp.float32)) for _ in range(N_EVEN)]
    odd_prompt = [(empty_kv(MOBA_HEADS), empty_kv(FOX_HEADS), jnp.zeros((bp, 0, FOX_HEADS), jnp.float32))
                  for _ in range(N_ODD)]
    y_prompt, pe, po = _trunk(x_prompt, even_prompt, odd_prompt, *weights)
    db, n_pages = page_table.shape

    def paged(pool):
        rows = pool[page_table]
        return rows.reshape((db, n_pages * pool.shape[1]) + pool.shape[2:])

    even_sample = [(paged(cache_nsa_cmp[j]), paged(cache_nsa_sel[j]), cache_nsa_win[j], state_hgrn[j])
                   for j in range(N_EVEN)]
    odd_sample = [(paged(cache_moba[j]), paged(cache_fox[j]), paged(cache_fox_logf[j])) for j in range(N_ODD)]
    y_sample, se, so = _trunk(x_sample, even_sample, odd_sample, *weights)
    p_nsa_cmp = jnp.stack([e[0] for e in pe])
    p_nsa_sel = jnp.stack([e[1] for e in pe])
    p_nsa_win = jnp.stack([e[2] for e in pe])
    p_hgrn = jnp.stack([e[3] for e in pe])
    p_moba = jnp.stack([o[0] for o in po])
    p_fox = jnp.stack([o[1] for o in po])
    p_fox_logf = jnp.stack([o[2] for o in po])
    s_nsa_cmp = jnp.stack([e[0] for e in se])
    s_nsa_sel = jnp.stack([e[1] for e in se])
    s_nsa_win = jnp.stack([e[2] for e in se])
    s_hgrn = jnp.stack([e[3] for e in se])
    s_moba = jnp.stack([o[0] for o in so])
    s_fox = jnp.stack([o[1] for o in so])
    s_fox_logf = jnp.stack([o[2] for o in so])
    return (y_prompt, y_sample, p_nsa_cmp, p_nsa_sel, p_nsa_win, p_hgrn, p_moba, p_fox, p_fox_logf,
            s_nsa_cmp, s_nsa_sel, s_nsa_win, s_hgrn, s_moba, s_fox, s_fox_logf)
```

```python
import functools
import math

import jax
import jax.numpy as jnp
import numpy as np
from jax import lax
from jax.experimental import pallas as pl
from jax.experimental.pallas import tpu as pltpu

F32 = jnp.float32
BF16 = jnp.bfloat16

HEAD_DIM = 128
NSA_HEADS = 8
NSA_KV_HEADS = 2
NSA_GROUP = NSA_HEADS // NSA_KV_HEADS
NSA_BLOCK = 64
NSA_TOPN = 16
NSA_WINDOW = 512
NSA_PHI_HIDDEN = 2 * HEAD_DIM
NSA_FORCE_SCORE = 1.0e4
HG_HEADS = 8
MOBA_HEADS = 8
MOBA_BLOCK = 256
MOBA_TOPK = 3
FOX_HEADS = 8
REL_BUCKETS = 32
REL_MAX_DIST = 128
EPS = 1e-6
PAGE_ROWS = 128

LANES = 128
VMEM_LIMIT_BYTES = 56 * 1024 * 1024
ROW_TILE = 512
Q_TILE = 128
NEG_BIG = -1e30


def _cparams(n_axes):
    return pltpu.CompilerParams(dimension_semantics=("arbitrary",) * n_axes,
                                vmem_limit_bytes=VMEM_LIMIT_BYTES)


def _rel_bucket(dist):
    n = jnp.maximum(dist, 0)
    exact = REL_BUCKETS // 2
    nf = jnp.maximum(n, 1).astype(F32)
    big = exact + (jnp.log(nf / exact) / math.log(REL_MAX_DIST / exact) * (REL_BUCKETS - exact)).astype(jnp.int32)
    return jnp.where(n < exact, n, jnp.minimum(big, REL_BUCKETS - 1))


def _bucket_bias(dist, table_rows):
    bucket = _rel_bucket(dist)
    out = jnp.zeros(dist.shape, F32) + table_rows(0)
    for b in range(1, REL_BUCKETS):
        out = jnp.where(bucket == b, table_rows(b), out)
    return out


def _rms_rows(x, g):
    return x * lax.rsqrt(jnp.mean(x * x, axis=-1, keepdims=True) + EPS) * g


def _log_sigmoid(z):
    return jnp.minimum(z, 0.0) - jnp.log1p(jnp.exp(-jnp.abs(z)))


def _sigmoid(z):
    return 1.0 / (1.0 + jnp.exp(-z))


def _rms_cast_kernel(x_ref, g_ref, o_ref):
    o_ref[...] = _rms_rows(x_ref[...], g_ref[...]).astype(o_ref.dtype)


def _rms_cast(x2d, g_stack, layer):
    m, d = x2d.shape
    tm = min(m, ROW_TILE)
    g3 = g_stack.reshape(g_stack.shape[0], 1, d)
    return pl.pallas_call(
        _rms_cast_kernel,
        grid=(m // tm,),
        in_specs=[pl.BlockSpec((tm, d), lambda i: (i, 0)),
                  pl.BlockSpec((None, 1, d), lambda i: (layer, 0, 0))],
        out_specs=pl.BlockSpec((tm, d), lambda i: (i, 0)),
        out_shape=jax.ShapeDtypeStruct((m, d), BF16),
        compiler_params=_cparams(1),
        name="rms_cast",
    )(x2d, g3)


def _dense_kernel(*refs, n_a, mode, scale):
    a_refs = refs[:n_a]
    pos = n_a
    w_ref = refs[pos]
    pos += 1
    w2_ref = res_ref = None
    if mode == "swiglu":
        w2_ref = refs[pos]
        pos += 1
    if mode == "res":
        res_ref = refs[pos]
        pos += 1
    o_ref = refs[pos]
    pos += 1
    wb_ref = refs[pos]
    pos += 1
    wb2_ref = refs[pos] if mode == "swiglu" else None

    @pl.when(pl.program_id(1) == 0)
    def _():
        wb_ref[...] = w_ref[...].astype(BF16)
        if mode == "swiglu":
            wb2_ref[...] = w2_ref[...].astype(BF16)

    if n_a == 1:
        a = a_refs[0][...]
    else:
        a = jnp.concatenate([r[...] for r in a_refs], axis=-1)
    y = jnp.dot(a, wb_ref[...], preferred_element_type=F32)
    if mode == "swiglu":
        y2 = jnp.dot(a, wb2_ref[...], preferred_element_type=F32)
        y = y * _sigmoid(y) * y2
    elif mode == "res":
        y = res_ref[...] + scale * y
    o_ref[...] = y.astype(o_ref.dtype)


def _dense(a_parts, w, lead, col0, n_out, *, tn, mode="plain", res=None, scale=1.0,
           out_dtype=F32, col0_b=None, name="dense"):
    m = a_parts[0].shape[0]
    k = sum(a.shape[1] for a in a_parts)
    tm = min(m, ROW_TILE)
    assert m % tm == 0 and n_out % tn == 0 and col0 % tn == 0
    if w.ndim == 3:
        wblock = (None, k, tn)

        def wmap(off):
            return lambda j, i: (lead, 0, j + off)
    else:
        wblock = (k, tn)

        def wmap(off):
            return lambda j, i: (0, j + off)
    in_specs = [pl.BlockSpec((tm, a.shape[1]), lambda j, i: (i, 0)) for a in a_parts]
    args = list(a_parts)
    in_specs.append(pl.BlockSpec(wblock, wmap(col0 // tn)))
    args.append(w)
    scratch = [pltpu.VMEM((k, tn), BF16)]
    if mode == "swiglu":
        assert col0_b % tn == 0
        in_specs.append(pl.BlockSpec(wblock, wmap(col0_b // tn)))
        args.append(w)
        scratch.append(pltpu.VMEM((k, tn), BF16))
    if mode == "res":
        in_specs.append(pl.BlockSpec((tm, tn), lambda j, i: (i, j)))
        args.append(res)
    return pl.pallas_call(
        functools.partial(_dense_kernel, n_a=len(a_parts), mode=mode, scale=scale),
        grid=(n_out // tn, m // tm),
        in_specs=in_specs,
        out_specs=pl.BlockSpec((tm, tn), lambda j, i: (i, j)),
        out_shape=jax.ShapeDtypeStruct((m, n_out), out_dtype),
        scratch_shapes=scratch,
        compiler_params=_cparams(2),
        name=name,
    )(*args)


def _ffn(x2d, norm_g, w_in, w_out, layer):
    d_ff = w_out.shape[1]
    xn = _rms_cast(x2d, norm_g, layer)
    h = _dense([xn], w_in, layer, 0, d_ff, tn=512, mode="swiglu", col0_b=d_ff,
               out_dtype=BF16, name="ffn_in")
    return _dense([h], w_out, layer, 0, x2d.shape[1], tn=256, mode="res", res=x2d, scale=0.5,
                  name="ffn_out")


def _bias_tiles(rel_bias):
    i = jnp.arange(Q_TILE)[:, None]
    j = jnp.arange(Q_TILE)[None, :]
    dist = jnp.stack([i - j, Q_TILE + i - j])
    return jnp.transpose(rel_bias[_rel_bucket(dist)], (3, 0, 1, 2))


def _tables(w):
    lb_all = jnp.cumsum(jax.nn.softmax(w["hg_lb"].astype(F32), axis=0), axis=0)
    lb_all = lb_all - lb_all[0:1]
    lbh = lb_all.reshape(lb_all.shape[0], HG_HEADS, HEAD_DIM)
    lb_rows = jnp.stack([jnp.log(lbh), jnp.log1p(-lbh), 1.0 - lbh], axis=2)
    return {"tiles": _bias_tiles(w["rel_bias"]), "lb_rows": lb_rows}


def _bias_cmp_table(rel_bias, q_start, t_len, nb):
    t_pos = q_start + jnp.arange(t_len)
    dist = t_pos[:, None] - (jnp.arange(nb) * NSA_BLOCK + NSA_BLOCK - 1)[None, :]
    return jnp.transpose(rel_bias[_rel_bucket(dist)], (2, 0, 1))


def _even_post_kernel(pa_ref, graw_ref, qg_ref, kg_ref, q_ref, cmp_ref, sel_ref, win_ref,
                      selb_ref, winb_ref, gates_ref):
    scale = HEAD_DIM ** -0.5
    qg = qg_ref[...]
    for h in range(NSA_HEADS):
        sl = slice(h * HEAD_DIM, (h + 1) * HEAD_DIM)
        q_ref[:, sl] = (_rms_rows(pa_ref[:, sl], qg) * scale).astype(q_ref.dtype)
    base = NSA_HEADS * HEAD_DIM
    kv_w = 2 * NSA_KV_HEADS * HEAD_DIM
    cmp_ref[...] = pa_ref[:, base:base + kv_w]
    for which, (o_ref, ob_ref) in enumerate(((sel_ref, selb_ref), (win_ref, winb_ref))):
        off = base + (which + 1) * kv_w
        kg = kg_ref[which + 1:which + 2, :]
        for c in range(2 * NSA_KV_HEADS):
            src = pa_ref[:, off + c * HEAD_DIM: off + (c + 1) * HEAD_DIM]
            val = _rms_rows(src, kg) if c < NSA_KV_HEADS else src
            o_ref[:, c * HEAD_DIM:(c + 1) * HEAD_DIM] = val
            ob_ref[:, c * HEAD_DIM:(c + 1) * HEAD_DIM] = val.astype(BF16)
    gates_ref[...] = _sigmoid(graw_ref[...])


def _even_post(pa, graw, q_norm, k_norm):
    m = pa.shape[0]
    tm = min(m, ROW_TILE)
    kv_w = 2 * NSA_KV_HEADS * HEAD_DIM
    qw = NSA_HEADS * HEAD_DIM
    row = lambda w: pl.BlockSpec((tm, w), lambda i: (i, 0))
    full = lambda a: pl.BlockSpec(a.shape, lambda i: (0,) * a.ndim)
    qg = q_norm.reshape(1, HEAD_DIM)
    return pl.pallas_call(
        _even_post_kernel,
        grid=(m // tm,),
        in_specs=[row(pa.shape[1]), row(LANES), full(qg), full(k_norm)],
        out_specs=[row(qw), row(kv_w), row(kv_w), row(kv_w), row(kv_w), row(kv_w), row(LANES)],
        out_shape=[jax.ShapeDtypeStruct((m, qw), BF16)] + [jax.ShapeDtypeStruct((m, kv_w), F32)] * 3
        + [jax.ShapeDtypeStruct((m, kv_w), BF16)] * 2 + [jax.ShapeDtypeStruct((m, LANES), F32)],
        compiler_params=_cparams(1),
        name="even_post",
    )(pa, graw, qg, k_norm)


def _gelu_tanh(x):
    return 0.5 * x * (1.0 + jnp.tanh(math.sqrt(2.0 / math.pi) * (x + 0.044715 * (x * x * x))))


def _compress_kernel(tbl_ref, src_ref, pos_ref, w1_ref, w2_ref, kg_ref, o_ref, xs_ref, w1b_ref, acc_ref,
                     *, npg, nb):
    w = pl.program_id(0)
    b = pl.program_id(1)
    kh = pl.program_id(2)
    pg = pl.program_id(3)

    @pl.when((b == 0) & (kh == 0) & (pg == 0))
    def _():
        w1b_ref[...] = w1_ref[...].astype(BF16)

    pos2 = jnp.concatenate([pos_ref[...]] * (PAGE_ROWS // NSA_BLOCK), axis=0)
    xs_ref[pl.ds(pl.multiple_of(pg * PAGE_ROWS, PAGE_ROWS), PAGE_ROWS), :] = src_ref[...] + pos2

    @pl.when(pg == npg - 1)
    def _():
        acc_ref[...] = jnp.zeros_like(acc_ref)

        def body(i, carry):
            x = xs_ref[pl.ds(i, nb, stride=NSA_BLOCK), :].astype(BF16)
            wi = w1b_ref[pl.ds(pl.multiple_of(i * HEAD_DIM, HEAD_DIM), HEAD_DIM), :]
            acc_ref[...] += jnp.dot(x, wi, preferred_element_type=F32)
            return carry

        lax.fori_loop(0, NSA_BLOCK, body, 0)
        hid = _gelu_tanh(acc_ref[...]).astype(BF16)
        y = jnp.dot(hid, w2_ref[...].astype(BF16), preferred_element_type=F32)
        o_ref[...] = jnp.where(w == 0, _rms_rows(y, kg_ref[0:1, :]), y)


def _compress(src_pages, tbl, pos, w1, w2, k_norm, j):
    bsz, npg = tbl.shape
    nb = npg * (PAGE_ROWS // NSA_BLOCK)
    w1r = w1.reshape(w1.shape[0] * 2, NSA_BLOCK * HEAD_DIM, NSA_PHI_HIDDEN)
    w2r = w2.reshape(w2.shape[0] * 2, NSA_PHI_HIDDEN, HEAD_DIM)
    posr = pos.reshape(pos.shape[0] * 2, NSA_BLOCK, HEAD_DIM)
    grid_spec = pltpu.PrefetchScalarGridSpec(
        num_scalar_prefetch=1,
        grid=(2, bsz, NSA_KV_HEADS, npg),
        in_specs=[
            pl.BlockSpec((None, PAGE_ROWS, HEAD_DIM), lambda w, b, kh, pg, t: (t[b, pg], 0, w * NSA_KV_HEADS + kh)),
            pl.BlockSpec((None, NSA_BLOCK, HEAD_DIM), lambda w, b, kh, pg, t: (2 * j + w, 0, 0)),
            pl.BlockSpec((None, NSA_BLOCK * HEAD_DIM, NSA_PHI_HIDDEN), lambda w, b, kh, pg, t: (2 * j + w, 0, 0)),
            pl.BlockSpec((None, NSA_PHI_HIDDEN, HEAD_DIM), lambda w, b, kh, pg, t: (2 * j + w, 0, 0)),
            pl.BlockSpec(k_norm.shape, lambda w, b, kh, pg, t: (0, 0)),
        ],
        out_specs=pl.BlockSpec((None, None, nb, HEAD_DIM), lambda w, b, kh, pg, t: (w, b, 0, kh)),
        scratch_shapes=[pltpu.VMEM((npg * PAGE_ROWS, HEAD_DIM), F32),
                        pltpu.VMEM((NSA_BLOCK * HEAD_DIM, NSA_PHI_HIDDEN), BF16),
                        pltpu.VMEM((nb, NSA_PHI_HIDDEN), F32)],
    )
    return pl.pallas_call(
        functools.partial(_compress_kernel, npg=npg, nb=nb),
        grid_spec=grid_spec,
        out_shape=jax.ShapeDtypeStruct((2, bsz, nb, NSA_KV_HEADS * HEAD_DIM), F32),
        compiler_params=_cparams(4),
        name="nsa_compress",
    )(tbl, src_pages, posr, w1r, w2r, k_norm)


def _nsa_cmp_kernel(q_ref, kc_ref, vc_ref, bias_ref, gates_ref, oc_ref, sel_ref, *, q_start, tq, nb, extra):
    kh = pl.program_id(1)
    qi = pl.program_id(2)
    t_pos = q_start + qi * tq + lax.broadcasted_iota(jnp.int32, (tq, 1), 0)
    blk = lax.broadcasted_iota(jnp.int32, (1, nb), 1)
    valid = t_pos >= blk * NSA_BLOCK + (NSA_BLOCK - 1)
    kc = kc_ref[...].astype(BF16)
    vc = vc_ref[...].astype(BF16)
    gates = gates_ref[...]
    imp = jnp.zeros((tq, nb), F32)
    for g in range(NSA_GROUP):
        qg = q_ref[:, g * HEAD_DIM:(g + 1) * HEAD_DIM]
        s = lax.dot_general(qg, kc, (((1,), (1,)), ((), ())), preferred_element_type=F32) + bias_ref[g]
        s = jnp.where(valid, s, NEG_BIG)
        m = jnp.max(s, axis=-1, keepdims=True)
        p = jnp.where(valid, jnp.exp(s - m), 0.0)
        l = jnp.sum(p, axis=-1, keepdims=True)
        p = p / jnp.where(l > 0, l, 1.0)
        imp = imp + p
        o = jnp.dot(p.astype(BF16), vc, preferred_element_type=F32)
        onehot = lax.broadcasted_iota(jnp.int32, (1, LANES), 1) == (kh * NSA_GROUP + g) * 3
        gcol = jnp.sum(jnp.where(onehot, gates, 0.0), axis=-1, keepdims=True)
        oc_ref[:, g * HEAD_DIM:(g + 1) * HEAD_DIM] = o * gcol
    cur = t_pos // NSA_BLOCK
    forced = (blk == 0) | (blk == cur) | (blk == cur - 1)
    score = jnp.where(forced, NSA_FORCE_SCORE, jnp.where(blk <= cur, imp, -1.0))
    rank = jnp.zeros((tq, nb), jnp.int32)
    for mcol in range(nb):
        sm = score[:, mcol:mcol + 1]
        ahead = (sm > score) | ((sm == score) & (mcol < blk))
        rank = rank + ahead.astype(jnp.int32)
    if not extra:
        sel_ref[...] = (rank < NSA_TOPN).astype(F32)
    else:
        rank = rank + (score < NSA_FORCE_SCORE).astype(jnp.int32)
        rank_extra = jnp.sum((score >= NSA_FORCE_SCORE).astype(jnp.int32), axis=-1, keepdims=True)
        lane = lax.broadcasted_iota(jnp.int32, (tq, NSA_TOPN), 1)
        out = jnp.zeros((tq, NSA_TOPN), jnp.int32)
        for r in range(NSA_TOPN):
            idx_r = jnp.sum(jnp.where(rank == r, blk, 0), axis=-1, keepdims=True)
            idx_r = idx_r + jnp.where(rank_extra == r, nb, 0)
            out = jnp.where(lane == r, idx_r, out)
        sel_ref[...] = out


def _nsa_cmp(q3, kvc, bias_c, gates3, *, q_start, extra):
    bsz, t_len, _ = q3.shape
    nb = kvc.shape[2]
    tq = min(t_len, Q_TILE)
    gw = NSA_GROUP * HEAD_DIM
    if extra:
        assert t_len == 1 and q_start // NSA_BLOCK == nb
        sel_shape = jax.ShapeDtypeStruct((bsz, NSA_KV_HEADS, t_len, NSA_TOPN), jnp.int32)
        sel_spec = pl.BlockSpec((None, None, tq, NSA_TOPN), lambda b, kh, qi: (b, kh, qi, 0))
    else:
        assert (q_start + t_len) == nb * NSA_BLOCK
        sel_shape = jax.ShapeDtypeStruct((bsz, NSA_KV_HEADS, t_len, nb), F32)
        sel_spec = pl.BlockSpec((None, None, tq, nb), lambda b, kh, qi: (b, kh, qi, 0))
    return pl.pallas_call(
        functools.partial(_nsa_cmp_kernel, q_start=q_start, tq=tq, nb=nb, extra=extra),
        grid=(bsz, NSA_KV_HEADS, t_len // tq),
        in_specs=[
            pl.BlockSpec((None, tq, gw), lambda b, kh, qi: (b, qi, kh)),
            pl.BlockSpec((None, None, nb, HEAD_DIM), lambda b, kh, qi: (0, b, 0, kh)),
            pl.BlockSpec((None, None, nb, HEAD_DIM), lambda b, kh, qi: (1, b, 0, kh)),
            pl.BlockSpec((NSA_GROUP, tq, nb), lambda b, kh, qi: (kh, qi, 0)),
            pl.BlockSpec((None, tq, LANES), lambda b, kh, qi: (b, qi, 0)),
        ],
        out_specs=[pl.BlockSpec((None, tq, gw), lambda b, kh, qi: (b, qi, kh)), sel_spec],
        out_shape=[jax.ShapeDtypeStruct((bsz, t_len, NSA_HEADS * HEAD_DIM), F32), sel_shape],
        compiler_params=_cparams(3),
        name="nsa_cmp_attn",
    )(q3, kvc, kvc, bias_c, gates3)


def _lane_column(x, col):
    onehot = lax.broadcasted_iota(jnp.int32, (1, x.shape[1]), 1) == col
    return jnp.sum(jnp.where(onehot, x, 0.0), axis=-1, keepdims=True)


def _row_to_column(row):
    n = row.shape[1]
    eye = lax.broadcasted_iota(jnp.int32, (n, n), 0) == lax.broadcasted_iota(jnp.int32, (n, n), 1)
    return jnp.sum(jnp.where(eye, row, 0.0), axis=-1, keepdims=True)


def _flash_kernel(*refs, kind, group, tq, blk, gate_branch):
    q_ref, k_ref, v_ref = refs[:3]
    pos = 3
    bias_ref = bm_ref = gates_ref = c_ref = None
    if kind in ("nsa_sel", "nsa_win", "moba"):
        bias_ref = refs[pos]
        pos += 1
    if kind in ("nsa_sel", "moba"):
        bm_ref = refs[pos]
        pos += 1
    if kind in ("nsa_sel", "nsa_win"):
        gates_ref = refs[pos]
        pos += 1
    if kind == "fox":
        c_ref = refs[pos]
        pos += 1
    o_ref = refs[pos]
    tk = tq
    hk = pl.program_id(1)
    qi = pl.program_id(2)
    q0 = pl.multiple_of(qi * tq, tq)
    row = q0 + lax.broadcasted_iota(jnp.int32, (tq, tk), 0)
    jcol = lax.broadcasted_iota(jnp.int32, (tq, tk), 1)
    lo = jnp.maximum(qi - NSA_WINDOW // tk, 0) if kind == "nsa_win" else 0
    bm = bm_ref[...] if bm_ref is not None else None
    cq = _row_to_column(c_ref[:, pl.ds(q0, tq)]) if kind == "fox" else None

    for g in range(group):
        qg = q_ref[:, g * HEAD_DIM:(g + 1) * HEAD_DIM]

        def body(ki, carry, g=g, qg=qg):
            m, l, acc = carry
            k0 = pl.multiple_of(ki * tk, tk)
            kt = k_ref[pl.ds(k0, tk), :]
            vt = v_ref[pl.ds(k0, tk), :]
            s = lax.dot_general(qg, kt, (((1,), (1,)), ((), ())), preferred_element_type=F32)
            col = k0 + jcol
            mask = col <= row
            if kind == "nsa_win":
                mask = mask & (row - col <= NSA_WINDOW)
            if kind in ("nsa_sel", "moba"):
                msel = _lane_column(bm, k0 // blk)
                for sub in range(1, tk // blk):
                    msel = jnp.where(jcol >= sub * blk, _lane_column(bm, k0 // blk + sub), msel)
                mask = mask & (msel > 0.5)
            if kind == "fox":
                s = s + (cq - c_ref[:, pl.ds(k0, tk)])
            else:
                delta = qi - ki
                near = bias_ref[g, 1]
                far = near[tq - 1:tq, 0:1]
                s = s + jnp.where(delta == 0, bias_ref[g, 0], jnp.where(delta == 1, near, far))
            s = jnp.where(mask, s, NEG_BIG)
            m_new = jnp.maximum(m, jnp.max(s, axis=-1, keepdims=True))
            p = jnp.where(mask, jnp.exp(s - m_new), 0.0)
            alpha = jnp.exp(m - m_new)
            l = alpha * l + jnp.sum(p, axis=-1, keepdims=True)
            acc = alpha * acc + jnp.dot(p.astype(BF16), vt, preferred_element_type=F32)
            return m_new, l, acc

        init = (jnp.full((tq, 1), NEG_BIG, F32), jnp.zeros((tq, 1), F32), jnp.zeros((tq, HEAD_DIM), F32))
        _, l, acc = lax.fori_loop(lo, qi + 1, body, init)
        o = acc / jnp.where(l > 0, l, 1.0)
        if gates_ref is not None:
            o = o * _lane_column(gates_ref[...], (hk * group + g) * 3 + gate_branch)
        o_ref[:, g * HEAD_DIM:(g + 1) * HEAD_DIM] = o.astype(o_ref.dtype)


def _flash(kind, q3, kv3, k_col, v_col, n_kv_heads, *, bias_tiles=None, bm=None, gates3=None, cum=None,
           blk=0, gate_branch=0, out_dtype=F32):
    bsz, t_len, qw = q3.shape
    group = qw // HEAD_DIM // n_kv_heads
    tq = Q_TILE
    assert t_len % tq == 0
    gw = group * HEAD_DIM
    in_specs = [pl.BlockSpec((None, tq, gw), lambda b, h, qi: (b, qi, h)),
                pl.BlockSpec((None, t_len, HEAD_DIM), lambda b, h, qi: (b, 0, k_col + h)),
                pl.BlockSpec((None, t_len, HEAD_DIM), lambda b, h, qi: (b, 0, v_col + h))]
    args = [q3, kv3, kv3]
    if bias_tiles is not None:
        in_specs.append(pl.BlockSpec((group, 2, tq, tq), lambda b, h, qi: (h, 0, 0, 0)))
        args.append(bias_tiles)
    if bm is not None:
        nb = bm.shape[-1]
        in_specs.append(pl.BlockSpec((None, None, tq, nb), lambda b, h, qi: (b, h, qi, 0)))
        args.append(bm)
    if gates3 is not None:
        in_specs.append(pl.BlockSpec((None, tq, LANES), lambda b, h, qi: (b, qi, 0)))
        args.append(gates3)
    if cum is not None:
        in_specs.append(pl.BlockSpec((None, None, 1, t_len), lambda b, h, qi: (b, h, 0, 0)))
        args.append(cum)
    return pl.pallas_call(
        functools.partial(_flash_kernel, kind=kind, group=group, tq=tq, blk=blk, gate_branch=gate_branch),
        grid=(bsz, n_kv_heads, t_len // tq),
        in_specs=in_specs,
        out_specs=pl.BlockSpec((None, tq, gw), lambda b, h, qi: (b, qi, h)),
        out_shape=jax.ShapeDtypeStruct((bsz, t_len, qw), out_dtype),
        compiler_params=_cparams(3),
        name="flash_" + kind,
    )(*args)


def _sum3_kernel(a_ref, b_ref, c_ref, o_ref):
    o_ref[...] = (a_ref[...] + b_ref[...] + c_ref[...]).astype(o_ref.dtype)


def _sum3_cast(a, b, c):
    m, n = a.shape
    tm = min(m, ROW_TILE)
    spec = pl.BlockSpec((tm, n), lambda i: (i, 0))
    return pl.pallas_call(
        _sum3_kernel, grid=(m // tm,), in_specs=[spec] * 3, out_specs=spec,
        out_shape=jax.ShapeDtypeStruct((m, n), BF16), compiler_params=_cparams(1), name="nsa_sum",
    )(a, b, c)


def _hgrn_kernel(q_ref, z_ref, v_ref, g_ref, lb_ref, on_ref, s0_ref, o_ref, s_ref, *, t_len, chunk):
    loglb = lb_ref[0:1, :]
    log1mlb = lb_ref[1:2, :]
    one_m_lb = lb_ref[2:3, :]
    on = on_ref[...]
    ri = lax.broadcasted_iota(jnp.int32, (chunk, chunk), 0)
    ci = lax.broadcasted_iota(jnp.int32, (chunk, chunk), 1)
    tril = (ci <= ri).astype(F32)
    rows = lax.broadcasted_iota(jnp.int32, (chunk, 1), 0)

    single = t_len < chunk

    def load(ref, r0):
        if single:
            return jnp.broadcast_to(ref[0:1, :], (chunk, HEAD_DIM))
        return ref[pl.ds(r0, chunk), :]

    def body(c, st):
        r0 = pl.multiple_of(c * chunk, chunk)
        q = load(q_ref, r0)
        z = load(z_ref, r0)
        v = load(v_ref, r0)
        a_term = loglb
        b_term = log1mlb + _log_sigmoid(z)
        logf = jnp.maximum(a_term, b_term) + jnp.log1p(jnp.exp(-jnp.abs(a_term - b_term)))
        k = one_m_lb * _sigmoid(-z)
        if single:
            logf = jnp.where(rows < t_len, logf, 0.0)
            k = jnp.where(rows < t_len, k, 0.0)
        cum = jnp.dot(tril, logf, preferred_element_type=F32, precision=lax.Precision.HIGHEST)
        a_last = cum[chunk - 1:chunk, :]
        o = jnp.zeros((chunk, HEAD_DIM), F32)
        for s_row in range(chunk):
            diff = jnp.where(rows >= s_row, cum - cum[s_row:s_row + 1, :], -jnp.inf)
            wgt = q * jnp.exp(diff) * k[s_row:s_row + 1, :]
            o = o + jnp.sum(wgt, axis=-1, keepdims=True) * v[s_row:s_row + 1, :]
        qa = (q * jnp.exp(cum)).astype(BF16)
        o = o + lax.dot_general(qa, st.astype(BF16), (((1,), (1,)), ((), ())), preferred_element_type=F32)
        kd = (k * jnp.exp(a_last - cum)).astype(BF16)
        st = jnp.exp(a_last) * st + lax.dot_general(v.astype(BF16), kd, (((0,), (0,)), ((), ())),
                                                    preferred_element_type=F32)
        gate = load(g_ref, r0)
        o = _rms_rows(o, on) * (gate * _sigmoid(gate))
        if single:
            o_ref[...] = o[0:t_len, :].astype(o_ref.dtype)
        else:
            o_ref[pl.ds(r0, chunk), :] = o.astype(o_ref.dtype)
        return st

    st = lax.fori_loop(0, max(t_len // chunk, 1), body, s0_ref[...].T)
    s_ref[...] = st.T


def _hgrn(ph3, lb_rows, o_norm, s0):
    bsz, t_len, _ = ph3.shape
    chunk = 16
    assert t_len % chunk == 0 or t_len == 1
    col = lambda off: pl.BlockSpec((None, t_len, HEAD_DIM), lambda b, h: (b, 0, off + h))
    on = o_norm.reshape(1, HEAD_DIM)
    return pl.pallas_call(
        functools.partial(_hgrn_kernel, t_len=t_len, chunk=chunk),
        grid=(bsz, HG_HEADS),
        in_specs=[col(0), col(HG_HEADS), col(2 * HG_HEADS), col(3 * HG_HEADS),
                  pl.BlockSpec((None, 3, HEAD_DIM), lambda b, h: (h, 0, 0)),
                  pl.BlockSpec((1, HEAD_DIM), lambda b, h: (0, 0)),
                  pl.BlockSpec((None, None, HEAD_DIM, HEAD_DIM), lambda b, h: (b, h, 0, 0))],
        out_specs=[pl.BlockSpec((None, t_len, HEAD_DIM), lambda b, h: (b, 0, h)),
                   pl.BlockSpec((None, None, HEAD_DIM, HEAD_DIM), lambda b, h: (b, h, 0, 0))],
        out_shape=[jax.ShapeDtypeStruct((bsz, t_len, HG_HEADS * HEAD_DIM), BF16),
                   jax.ShapeDtypeStruct((bsz, HG_HEADS, HEAD_DIM, HEAD_DIM), F32)],
        compiler_params=_cparams(2),
        name="hgrn2",
    )(ph3, ph3, ph3, ph3, lb_rows, on, s0)


def _group_rows(q_ref, group):
    return jnp.concatenate([q_ref[:, g * HEAD_DIM:(g + 1) * HEAD_DIM].astype(F32) for g in range(group)], axis=0)


def _store_group_rows(o_ref, o):
    for g in range(o.shape[0]):
        o_ref[:, g * HEAD_DIM:(g + 1) * HEAD_DIM] = o[g:g + 1, :].astype(o_ref.dtype)


def _nsa_sel_step_kernel(tbl_ref, idx_ref, q_ref, k_ref, v_ref, kn_ref, vn_ref, rbt_ref, gates_ref, o_ref,
                         m_ref, l_ref, acc_ref, *, t_pos, nb):
    b = pl.program_id(0)
    kh = pl.program_id(1)
    j = pl.program_id(2)
    q4 = _group_rows(q_ref, NSA_GROUP)
    rbt = rbt_ref[...]
    col = lambda bk: rbt[:, bk:bk + 1]

    @pl.when(j == 0)
    def _():
        m_ref[...] = jnp.sum(q4 * kn_ref[...], axis=-1, keepdims=True) + col(0)
        l_ref[...] = jnp.ones_like(l_ref)
        acc_ref[...] = jnp.broadcast_to(vn_ref[...], acc_ref.shape)

    blk_id = idx_ref[b * NSA_KV_HEADS + kh, j]

    @pl.when(blk_id < nb)
    def _():
        kt = k_ref[...].astype(BF16)
        s = lax.dot_general(q4.astype(BF16), kt, (((1,), (1,)), ((), ())), preferred_element_type=F32)
        dist = t_pos - (blk_id * NSA_BLOCK + lax.broadcasted_iota(jnp.int32, (1, NSA_BLOCK), 1))
        s = s + _bucket_bias(jnp.broadcast_to(dist, s.shape), col)
        m_old = m_ref[...]
        m_new = jnp.maximum(m_old, jnp.max(s, axis=-1, keepdims=True))
        p = jnp.exp(s - m_new)
        alpha = jnp.exp(m_old - m_new)
        l_ref[...] = alpha * l_ref[...] + jnp.sum(p, axis=-1, keepdims=True)
        acc_ref[...] = alpha * acc_ref[...] + jnp.dot(p.astype(BF16), v_ref[...].astype(BF16),
                                                      preferred_element_type=F32)
        m_ref[...] = m_new

    @pl.when(j == NSA_TOPN - 1)
    def _():
        o = acc_ref[...] / l_ref[...]
        gates = gates_ref[...]
        gcol = jnp.concatenate([_lane_column(gates, (kh * NSA_GROUP + g) * 3 + 1) for g in range(NSA_GROUP)], axis=0)
        _store_group_rows(o_ref, o * gcol)


def _nsa_sel_step(q3, cache_pages, tbl, idx, sel_new3, rbt, gates3, *, t_pos):
    bsz = q3.shape[0]
    nb = tbl.shape[1] * (PAGE_ROWS // NSA_BLOCK)
    gw = NSA_GROUP * HEAD_DIM
    halves = PAGE_ROWS // NSA_BLOCK

    def cache_map(col_off):
        def index_map(b, kh, j, tbl_ref, idx_ref):
            blk = jnp.minimum(idx_ref[b * NSA_KV_HEADS + kh, j], nb - 1)
            return (tbl_ref[b, blk // halves], blk % halves, col_off + kh)
        return index_map

    grid_spec = pltpu.PrefetchScalarGridSpec(
        num_scalar_prefetch=2,
        grid=(bsz, NSA_KV_HEADS, NSA_TOPN),
        in_specs=[
            pl.BlockSpec((None, 1, gw), lambda b, kh, j, t, i: (b, 0, kh)),
            pl.BlockSpec((None, NSA_BLOCK, HEAD_DIM), cache_map(0)),
            pl.BlockSpec((None, NSA_BLOCK, HEAD_DIM), cache_map(NSA_KV_HEADS)),
            pl.BlockSpec((None, 1, HEAD_DIM), lambda b, kh, j, t, i: (b, 0, kh)),
            pl.BlockSpec((None, 1, HEAD_DIM), lambda b, kh, j, t, i: (b, 0, NSA_KV_HEADS + kh)),
            pl.BlockSpec((None, NSA_GROUP, REL_BUCKETS), lambda b, kh, j, t, i: (kh, 0, 0)),
            pl.BlockSpec((None, 1, LANES), lambda b, kh, j, t, i: (b, 0, 0)),
        ],
        out_specs=pl.BlockSpec((None, 1, gw), lambda b, kh, j, t, i: (b, 0, kh)),
        scratch_shapes=[pltpu.VMEM((NSA_GROUP, 1), F32), pltpu.VMEM((NSA_GROUP, 1), F32),
                        pltpu.VMEM((NSA_GROUP, HEAD_DIM), F32)],
    )
    return pl.pallas_call(
        functools.partial(_nsa_sel_step_kernel, t_pos=t_pos, nb=nb),
        grid_spec=grid_spec,
        out_shape=jax.ShapeDtypeStruct((bsz, 1, NSA_HEADS * HEAD_DIM), F32),
        compiler_params=_cparams(3),
        name="nsa_sel_step",
    )(tbl, idx, q3, cache_pages, cache_pages, sel_new3, sel_new3, rbt, gates3)


def _nsa_win_step_kernel(q_ref, k_ref, v_ref, kn_ref, vn_ref, rbt_ref, gates_ref, o_ref, *, pw):
    kh = pl.program_id(1)
    q4 = _group_rows(q_ref, NSA_GROUP)
    rbt = rbt_ref[...]
    col = lambda bk: rbt[:, bk:bk + 1]
    s = lax.dot_general(q4.astype(BF16), k_ref[...].astype(BF16), (((1,), (1,)), ((), ())),
                        preferred_element_type=F32)
    dist = pw - lax.broadcasted_iota(jnp.int32, (NSA_GROUP, pw), 1)
    mask = dist <= NSA_WINDOW
    s = jnp.where(mask, s + _bucket_bias(dist, col), NEG_BIG)
    s_self = jnp.sum(q4 * kn_ref[...], axis=-1, keepdims=True) + col(0)
    m = jnp.maximum(jnp.max(s, axis=-1, keepdims=True), s_self)
    p = jnp.where(mask, jnp.exp(s - m), 0.0)
    p_self = jnp.exp(s_self - m)
    l = jnp.sum(p, axis=-1, keepdims=True) + p_self
    o = jnp.dot(p.astype(BF16), v_ref[...].astype(BF16), preferred_element_type=F32) + p_self * vn_ref[...]
    gates = gates_ref[...]
    gcol = jnp.concatenate([_lane_column(gates, (kh * NSA_GROUP + g) * 3 + 2) for g in range(NSA_GROUP)], axis=0)
    _store_group_rows(o_ref, o / l * gcol)


def _nsa_win_step(q3, win_cache, lead, win_new3, rbt, gates3):
    bsz = q3.shape[0]
    pw = win_cache.shape[1]
    gw = NSA_GROUP * HEAD_DIM
    return pl.pallas_call(
        functools.partial(_nsa_win_step_kernel, pw=pw),
        grid=(bsz, NSA_KV_HEADS),
        in_specs=[
            pl.BlockSpec((None, 1, gw), lambda b, kh: (b, 0, kh)),
            pl.BlockSpec((None, pw, HEAD_DIM), lambda b, kh: (lead + b, 0, kh)),
            pl.BlockSpec((None, pw, HEAD_DIM), lambda b, kh: (lead + b, 0, NSA_KV_HEADS + kh)),
            pl.BlockSpec((None, 1, HEAD_DIM), lambda b, kh: (b, 0, kh)),
            pl.BlockSpec((None, 1, HEAD_DIM), lambda b, kh: (b, 0, NSA_KV_HEADS + kh)),
            pl.BlockSpec((None, NSA_GROUP, REL_BUCKETS), lambda b, kh: (kh, 0, 0)),
            pl.BlockSpec((None, 1, LANES), lambda b, kh: (b, 0, 0)),
        ],
        out_specs=pl.BlockSpec((None, 1, gw), lambda b, kh: (b, 0, kh)),
        out_shape=jax.ShapeDtypeStruct((bsz, 1, NSA_HEADS * HEAD_DIM), F32),
        compiler_params=_cparams(2),
        name="nsa_win_step",
    )(q3, win_cache, win_cache, win_new3, win_new3, rbt, gates3)


def _even_mixer(x2d, h, bsz, t_len, j, w, tabs, past):
    m, d = x2d.shape
    qw = NSA_HEADS * HEAD_DIM
    kv_w = 2 * NSA_KV_HEADS * HEAD_DIM
    n_main = qw + 3 * kv_w
    n_gate = 3 * NSA_HEADS
    w_in = w["w_in_even"]
    pa = _dense([h], w_in, j, 0, n_main, tn=512, name="even_in_attn")
    w_gate = jnp.pad(w_in[j, :, n_main:n_main + n_gate], ((0, 0), (0, LANES - n_gate)))
    graw = _dense([h], w_gate, None, 0, LANES, tn=LANES, name="even_in_gate")
    w_hg = w_in[j, :, n_main + n_gate:]
    ph = _dense([h], w_hg, None, 0, w_hg.shape[1], tn=512, name="even_in_hgrn")
    q, cmp_new, sel_new, win_new, selb, winb, gates = _even_post(pa, graw, w["nsa_q_norm"][j], w["nsa_k_norm"][j])
    q3 = q.reshape(bsz, t_len, qw)
    gates3 = gates.reshape(bsz, t_len, LANES)
    rel_bias = w["rel_bias"]
    k_norm = w["nsa_k_norm"][j]
    if past is None:
        q_start = 0
        src_pages = cmp_new.reshape(m // PAGE_ROWS, PAGE_ROWS, kv_w)
        tbl = jnp.arange(m // PAGE_ROWS, dtype=jnp.int32).reshape(bsz, t_len // PAGE_ROWS)
        s0 = jnp.zeros((bsz, HG_HEADS, HEAD_DIM, HEAD_DIM), F32)
    else:
        cmp_pages, sel_pages, tbl, win_cache, s0 = past
        q_start = tbl.shape[1] * PAGE_ROWS
        src_pages = cmp_pages
    kvc = _compress(src_pages, tbl, w["nsa_phi_pos"], w["nsa_phi_w1"], w["nsa_phi_w2"], k_norm, j)
    nb = kvc.shape[2]
    bias_c = _bias_cmp_table(rel_bias, q_start, t_len, nb)
    o_c, sel = _nsa_cmp(q3, kvc, bias_c, gates3, q_start=q_start, extra=past is not None)
    if past is None:
        selb3 = selb.reshape(bsz, t_len, kv_w)
        winb3 = winb.reshape(bsz, t_len, kv_w)
        o_s = _flash("nsa_sel", q3, selb3, 0, NSA_KV_HEADS, NSA_KV_HEADS, bias_tiles=tabs["tiles"], bm=sel,
                     gates3=gates3, blk=NSA_BLOCK, gate_branch=1)
        o_w = _flash("nsa_win", q3, winb3, 0, NSA_KV_HEADS, NSA_KV_HEADS, bias_tiles=tabs["tiles"],
                     gates3=gates3, gate_branch=2)
        keep = min(NSA_WINDOW, t_len)
        win_buf = win_new.reshape(bsz, t_len, kv_w)[:, t_len - keep:]
    else:
        rbt = rel_bias.T.reshape(NSA_KV_HEADS, NSA_GROUP, REL_BUCKETS)
        sel_new3 = sel_new.reshape(bsz, t_len, kv_w)
        win_new3 = win_new.reshape(bsz, t_len, kv_w)
        idx = sel.reshape(bsz * NSA_KV_HEADS, NSA_TOPN)
        o_s = _nsa_sel_step(q3, sel_pages, tbl, idx, sel_new3, rbt, gates3, t_pos=q_start)
        wc = win_cache.reshape(win_cache.shape[0] * win_cache.shape[1], win_cache.shape[2], kv_w)
        o_w = _nsa_win_step(q3, wc, j * bsz, win_new3, rbt, gates3)
        win_all = jnp.concatenate([wc[j * bsz:(j + 1) * bsz], win_new3], axis=1)
        keep = min(NSA_WINDOW, win_all.shape[1])
        win_buf = win_all[:, win_all.shape[1] - keep:]
    a_nsa = _sum3_cast(o_c.reshape(m, qw), o_s.reshape(m, qw), o_w.reshape(m, qw))
    o_hg, s_new = _hgrn(ph.reshape(bsz, t_len, ph.shape[1]), tabs["lb_rows"][j], w["hg_o_norm"][j], s0)
    x_new = _dense([a_nsa, o_hg.reshape(m, HG_HEADS * HEAD_DIM)], w["w_out_even"], j, 0, d, tn=min(d, 512),
                   mode="res", res=x2d, scale=1.0, name="even_out")
    kv_shape = (bsz, t_len, 2, NSA_KV_HEADS, HEAD_DIM)
    return (x_new, cmp_new.reshape(kv_shape), sel_new.reshape(kv_shape),
            win_buf.reshape(bsz, win_buf.shape[1], 2, NSA_KV_HEADS, HEAD_DIM), s_new)


ODD_HEADS = MOBA_HEADS
ODD_QW = ODD_HEADS * HEAD_DIM
ODD_KVW = 2 * ODD_HEADS * HEAD_DIM


def _odd_post_kernel(pm_ref, fz_ref, mg_ref, fg_ref, fb_ref, qm_ref, moba_ref, mobab_ref, qf_ref, fox_ref,
                     foxb_ref, logf_ref):
    scale = HEAD_DIM ** -0.5
    off = 0
    for g_ref, q_ref, kv_ref, kvb_ref in ((mg_ref, qm_ref, moba_ref, mobab_ref), (fg_ref, qf_ref, fox_ref, foxb_ref)):
        qg = g_ref[0:1, :]
        kg = g_ref[1:2, :]
        for h in range(ODD_HEADS):
            sl = slice(h * HEAD_DIM, (h + 1) * HEAD_DIM)
            q_ref[:, sl] = (_rms_rows(pm_ref[:, off + h * HEAD_DIM: off + (h + 1) * HEAD_DIM], qg) * scale
                            ).astype(q_ref.dtype)
        off += ODD_QW
        for c in range(2 * ODD_HEADS):
            src = pm_ref[:, off + c * HEAD_DIM: off + (c + 1) * HEAD_DIM]
            val = _rms_rows(src, kg) if c < ODD_HEADS else src
            kv_ref[:, c * HEAD_DIM:(c + 1) * HEAD_DIM] = val
            kvb_ref[:, c * HEAD_DIM:(c + 1) * HEAD_DIM] = val.astype(BF16)
        off += ODD_KVW
    logf_ref[...] = _log_sigmoid(fz_ref[...] + fb_ref[...])


def _odd_post(pm, fz, moba_qk, fox_qk, fb):
    m = pm.shape[0]
    tm = min(m, ROW_TILE // 2)
    row = lambda w: pl.BlockSpec((tm, w), lambda i: (i, 0))
    full = lambda a: pl.BlockSpec(a.shape, lambda i: (0,) * a.ndim)
    return pl.pallas_call(
        _odd_post_kernel,
        grid=(m // tm,),
        in_specs=[row(pm.shape[1]), row(LANES), full(moba_qk), full(fox_qk), full(fb)],
        out_specs=[row(ODD_QW), row(ODD_KVW), row(ODD_KVW), row(ODD_QW), row(ODD_KVW), row(ODD_KVW), row(LANES)],
        out_shape=[jax.ShapeDtypeStruct((m, ODD_QW), BF16), jax.ShapeDtypeStruct((m, ODD_KVW), F32),
                   jax.ShapeDtypeStruct((m, ODD_KVW), BF16), jax.ShapeDtypeStruct((m, ODD_QW), BF16),
                   jax.ShapeDtypeStruct((m, ODD_KVW), F32), jax.ShapeDtypeStruct((m, ODD_KVW), BF16),
                   jax.ShapeDtypeStruct((m, LANES), F32)],
        compiler_params=_cparams(1),
        name="odd_post",
    )(pm, fz, moba_qk, fox_qk, fb)


def _topk_rank(score, n):
    idx = lax.broadcasted_iota(jnp.int32, (1, n), 1)
    rank = jnp.zeros(score.shape, jnp.int32)
    for mcol in range(n):
        sm = score[:, mcol:mcol + 1]
        ahead = (sm > score) | ((sm == score) & (mcol < idx))
        rank = rank + ahead.astype(jnp.int32)
    return rank


def _moba_gate_kernel(q_ref, k_ref, bm_ref, *, t_len, nbl):
    k_mean = jnp.mean(k_ref[...].reshape(nbl, MOBA_BLOCK, HEAD_DIM), axis=1)
    gate = lax.dot_general(q_ref[...], k_mean.astype(BF16), (((1,), (1,)), ((), ())), preferred_element_type=F32)
    cur = lax.broadcasted_iota(jnp.int32, (t_len, 1), 0) // MOBA_BLOCK
    blk = lax.broadcasted_iota(jnp.int32, (1, nbl), 1)
    past_ok = blk < cur
    gate = jnp.where(past_ok, gate, -jnp.inf)
    rank = _topk_rank(gate, nbl)
    sel = (rank < MOBA_TOPK) & past_ok & (jnp.abs(gate) < jnp.inf)
    bm_ref[...] = (sel | (blk == cur)).astype(F32)


def _moba_gate(qm3, moba_new3):
    bsz, t_len, _ = qm3.shape
    assert t_len % MOBA_BLOCK == 0
    nbl = t_len // MOBA_BLOCK
    return pl.pallas_call(
        functools.partial(_moba_gate_kernel, t_len=t_len, nbl=nbl),
        grid=(bsz, MOBA_HEADS),
        in_specs=[pl.BlockSpec((None, t_len, HEAD_DIM), lambda b, h: (b, 0, h)),
                  pl.BlockSpec((None, t_len, HEAD_DIM), lambda b, h: (b, 0, h))],
        out_specs=pl.BlockSpec((None, None, t_len, nbl), lambda b, h: (b, h, 0, 0)),
        out_shape=jax.ShapeDtypeStruct((bsz, MOBA_HEADS, t_len, nbl), F32),
        compiler_params=_cparams(2),
        name="moba_gate",
    )(qm3, moba_new3)


def _cumsum_kernel(x_ref, o_ref, *, t_len):
    n = Q_TILE
    upper = (lax.broadcasted_iota(jnp.int32, (n, n), 0) <= lax.broadcasted_iota(jnp.int32, (n, n), 1)).astype(F32)
    carry = jnp.zeros((FOX_HEADS, 1), F32)
    for c in range(t_len // n):
        xt = x_ref[c * n:(c + 1) * n, :].T[0:FOX_HEADS, :]
        cum = jnp.dot(xt, upper, preferred_element_type=F32, precision=lax.Precision.HIGHEST) + carry
        o_ref[:, c * n:(c + 1) * n] = cum
        carry = cum[:, n - 1:n]


def _cumsum_heads(logf3):
    bsz, t_len, _ = logf3.shape
    assert t_len % Q_TILE == 0
    return pl.pallas_call(
        functools.partial(_cumsum_kernel, t_len=t_len),
        grid=(bsz,),
        in_specs=[pl.BlockSpec((None, t_len, LANES), lambda b: (b, 0, 0))],
        out_specs=pl.BlockSpec((None, FOX_HEADS, t_len), lambda b: (b, 0, 0)),
        out_shape=jax.ShapeDtypeStruct((bsz, FOX_HEADS, t_len), F32),
        compiler_params=_cparams(1),
        name="fox_cumsum",
    )(logf3)


def _moba_gate_step_kernel(tbl_ref, q_ref, k_ref, idx_ref, ksum_ref, gate_ref, *, npg, ppb):
    pg = pl.program_id(1)
    nblk = npg // ppb

    @pl.when(pg == 0)
    def _():
        gate_ref[...] = jnp.zeros_like(gate_ref)

    @pl.when(pg % ppb == 0)
    def _():
        ksum_ref[...] = jnp.zeros_like(ksum_ref)

    ksum_ref[...] += jnp.sum(k_ref[...], axis=0, keepdims=True)

    @pl.when(pg % ppb == ppb - 1)
    def _():
        prod = q_ref[...].astype(F32) * (ksum_ref[...] * (1.0 / MOBA_BLOCK))
        gcol = jnp.concatenate([jnp.sum(prod[:, h * HEAD_DIM:(h + 1) * HEAD_DIM], axis=-1, keepdims=True)
                                for h in range(MOBA_HEADS)], axis=0)
        lane = lax.broadcasted_iota(jnp.int32, (1, nblk), 1)
        gate_ref[...] = jnp.where(lane == pg // ppb, gcol, gate_ref[...])

    @pl.when(pg == npg - 1)
    def _():
        gate = gate_ref[...]
        rank = _topk_rank(gate, nblk)
        blk = lax.broadcasted_iota(jnp.int32, (1, nblk), 1)
        lane = lax.broadcasted_iota(jnp.int32, (MOBA_HEADS, MOBA_TOPK), 1)
        out = jnp.full((MOBA_HEADS, MOBA_TOPK), -1, jnp.int32)
        for r in range(min(MOBA_TOPK, nblk)):
            hit = (rank == r) & (jnp.abs(gate) < jnp.inf)
            idx_r = jnp.sum(jnp.where(hit, blk + 1, 0), axis=-1, keepdims=True) - 1
            out = jnp.where(lane == r, idx_r, out)
        idx_ref[...] = out


def _moba_gate_step(qm3, pages, tbl):
    bsz, npg = tbl.shape
    ppb = MOBA_BLOCK // PAGE_ROWS
    assert npg % ppb == 0
    grid_spec = pltpu.PrefetchScalarGridSpec(
        num_scalar_prefetch=1,
        grid=(bsz, npg),
        in_specs=[pl.BlockSpec((None, 1, ODD_QW), lambda b, pg, t: (b, 0, 0)),
                  pl.BlockSpec((None, PAGE_ROWS, ODD_QW), lambda b, pg, t: (t[b, pg], 0, 0))],
        out_specs=pl.BlockSpec((None, MOBA_HEADS, MOBA_TOPK), lambda b, pg, t: (b, 0, 0)),
        scratch_shapes=[pltpu.VMEM((1, ODD_QW), F32), pltpu.VMEM((MOBA_HEADS, npg // ppb), F32)],
    )
    return pl.pallas_call(
        functools.partial(_moba_gate_step_kernel, npg=npg, ppb=ppb),
        grid_spec=grid_spec,
        out_shape=jax.ShapeDtypeStruct((bsz, MOBA_HEADS, MOBA_TOPK), jnp.int32),
        compiler_params=_cparams(2),
        name="moba_gate_step",
    )(tbl, qm3, pages)


def _moba_attn_step_kernel(tbl_ref, idx_ref, q_ref, k_ref, v_ref, kn_ref, vn_ref, rbt_ref, o_ref,
                           m_ref, l_ref, acc_ref, *, t_pos, ppb):
    b = pl.program_id(0)
    h = pl.program_id(1)
    s_id = pl.program_id(2)
    q = q_ref[...].astype(F32)
    rbt = rbt_ref[...]
    col = lambda bk: rbt[:, bk:bk + 1]

    @pl.when(s_id == 0)
    def _():
        m_ref[...] = jnp.sum(q * kn_ref[...], axis=-1, keepdims=True) + col(0)
        l_ref[...] = jnp.ones_like(l_ref)
        acc_ref[...] = vn_ref[...]

    blk_id = idx_ref[b * MOBA_HEADS + h, s_id // ppb]

    @pl.when(blk_id >= 0)
    def _():
        s = jnp.sum(k_ref[...] * q, axis=-1, keepdims=True)
        key_pos = blk_id * MOBA_BLOCK + (s_id % ppb) * PAGE_ROWS + lax.broadcasted_iota(jnp.int32, (PAGE_ROWS, 1), 0)
        s = s + _bucket_bias(t_pos - key_pos, col)
        m_old = m_ref[...]
        m_new = jnp.maximum(m_old, jnp.max(s, axis=0, keepdims=True))
        p = jnp.exp(s - m_new)
        alpha = jnp.exp(m_old - m_new)
        l_ref[...] = alpha * l_ref[...] + jnp.sum(p, axis=0, keepdims=True)
        acc_ref[...] = alpha * acc_ref[...] + jnp.sum(p * v_ref[...], axis=0, keepdims=True)
        m_ref[...] = m_new

    @pl.when(s_id == pl.num_programs(2) - 1)
    def _():
        o_ref[...] = (acc_ref[...] / l_ref[...]).astype(o_ref.dtype)


def _moba_attn_step(qm3, pages, tbl, idx, moba_new3, rbt3, *, t_pos):
    bsz = qm3.shape[0]
    ppb = MOBA_BLOCK // PAGE_ROWS

    def cache_map(col_off):
        def index_map(b, h, s, tbl_ref, idx_ref):
            blk = jnp.maximum(idx_ref[b * MOBA_HEADS + h, s // ppb], 0)
            return (tbl_ref[b, blk * ppb + s % ppb], 0, col_off + h)
        return index_map

    head = lambda off: pl.BlockSpec((None, 1, HEAD_DIM), lambda b, h, s, t, i: (b, 0, off + h))
    grid_spec = pltpu.PrefetchScalarGridSpec(
        num_scalar_prefetch=2,
        grid=(bsz, MOBA_HEADS, MOBA_TOPK * ppb),
        in_specs=[head(0),
                  pl.BlockSpec((None, PAGE_ROWS, HEAD_DIM), cache_map(0)),
                  pl.BlockSpec((None, PAGE_ROWS, HEAD_DIM), cache_map(MOBA_HEADS)),
                  head(0), head(MOBA_HEADS),
                  pl.BlockSpec((None, 1, REL_BUCKETS), lambda b, h, s, t, i: (h, 0, 0))],
        out_specs=head(0),
        scratch_shapes=[pltpu.VMEM((1, 1), F32), pltpu.VMEM((1, 1), F32), pltpu.VMEM((1, HEAD_DIM), F32)],
    )
    return pl.pallas_call(
        functools.partial(_moba_attn_step_kernel, t_pos=t_pos, ppb=ppb),
        grid_spec=grid_spec,
        out_shape=jax.ShapeDtypeStruct((bsz, 1, ODD_QW), BF16),
        compiler_params=_cparams(3),
        name="moba_attn_step",
    )(tbl, idx, qm3, pages, pages, moba_new3, moba_new3, rbt3)


def _fox_step_kernel(tbl_ref, q_ref, k_ref, v_ref, lf_ref, kn_ref, vn_ref, lfn_ref, o_ref,
                     m_ref, l_ref, acc_ref, carry_ref, *, npg):
    pg = pl.program_id(1)
    q = q_ref[...].astype(F32)

    @pl.when(pg == 0)
    def _():
        prod = q * kn_ref[...]
        for h in range(FOX_HEADS):
            sl = slice(h * HEAD_DIM, (h + 1) * HEAD_DIM)
            s_self = jnp.sum(prod[:, sl], axis=-1, keepdims=True)
            m_ref[h:h + 1, :] = jnp.broadcast_to(s_self, (1, HEAD_DIM))
            acc_ref[h:h + 1, :] = vn_ref[:, sl]
        l_ref[...] = jnp.ones_like(l_ref)
        carry_ref[...] = lfn_ref[:, 0:FOX_HEADS]

    lf = lf_ref[...]
    n = PAGE_ROWS
    after = (lax.broadcasted_iota(jnp.int32, (n, n), 1) > lax.broadcasted_iota(jnp.int32, (n, n), 0)).astype(F32)
    bias = carry_ref[...] + jnp.dot(after, lf, preferred_element_type=F32, precision=lax.Precision.HIGHEST)
    carry_ref[...] += jnp.sum(lf, axis=0, keepdims=True)
    for h in range(FOX_HEADS):
        sl = slice(h * HEAD_DIM, (h + 1) * HEAD_DIM)
        s = jnp.sum(k_ref[:, sl] * q[:, sl], axis=-1, keepdims=True) + bias[:, h:h + 1]
        m_old = m_ref[h:h + 1, 0:1]
        m_new = jnp.maximum(m_old, jnp.max(s, axis=0, keepdims=True))
        p = jnp.exp(s - m_new)
        alpha = jnp.exp(m_old - m_new)
        l_ref[h:h + 1, :] = alpha * l_ref[h:h + 1, :] + jnp.sum(p, axis=0, keepdims=True)
        acc_ref[h:h + 1, :] = alpha * acc_ref[h:h + 1, :] + jnp.sum(p * v_ref[:, sl], axis=0, keepdims=True)
        m_ref[h:h + 1, :] = jnp.broadcast_to(m_new, (1, HEAD_DIM))

    @pl.when(pg == npg - 1)
    def _():
        o = acc_ref[...] / l_ref[...]
        _store_group_rows(o_ref, o)


def _fox_step(qf3, pages, logf_pages, tbl, fox_new3, logf_new3):
    bsz, npg = tbl.shape
    rev = lambda col: (lambda b, pg, t: (t[b, npg - 1 - pg], 0, col))
    new = lambda col: pl.BlockSpec((None, 1, ODD_QW), lambda b, pg, t: (b, 0, col))
    grid_spec = pltpu.PrefetchScalarGridSpec(
        num_scalar_prefetch=1,
        grid=(bsz, npg),
        in_specs=[new(0),
                  pl.BlockSpec((None, PAGE_ROWS, ODD_QW), rev(0)),
                  pl.BlockSpec((None, PAGE_ROWS, ODD_QW), rev(1)),
                  pl.BlockSpec((None, PAGE_ROWS, FOX_HEADS), rev(0)),
                  new(0), new(1),
                  pl.BlockSpec((None, 1, LANES), lambda b, pg, t: (b, 0, 0))],
        out_specs=new(0),
        scratch_shapes=[pltpu.VMEM((FOX_HEADS, HEAD_DIM), F32), pltpu.VMEM((FOX_HEADS, HEAD_DIM), F32),
                        pltpu.VMEM((FOX_HEADS, HEAD_DIM), F32), pltpu.VMEM((1, FOX_HEADS), F32)],
    )
    return pl.pallas_call(
        functools.partial(_fox_step_kernel, npg=npg),
        grid_spec=grid_spec,
        out_shape=jax.ShapeDtypeStruct((bsz, 1, ODD_QW), BF16),
        compiler_params=_cparams(2),
        name="fox_step",
    )(tbl, qf3, pages, pages, logf_pages, fox_new3, fox_new3, logf_new3)


def _odd_mixer(x2d, h, bsz, t_len, j, w, tabs, past):
    m, d = x2d.shape
    n_main = 2 * (ODD_QW + ODD_KVW)
    w_in = w["w_in_odd"]
    pm = _dense([h], w_in, j, 0, n_main, tn=512, name="odd_in")
    w_fz = jnp.pad(w_in[j, :, n_main:n_main + FOX_HEADS], ((0, 0), (0, LANES - FOX_HEADS)))
    fz = _dense([h], w_fz, None, 0, LANES, tn=LANES, name="odd_in_forget")
    fb = jnp.pad(w["fox_f_bias"][j].astype(F32), (0, LANES - FOX_HEADS)).reshape(1, LANES)
    qm, moba_new, mobab, qf, fox_new, foxb, logf = _odd_post(pm, fz, w["moba_qk_norm"][j], w["fox_qk_norm"][j], fb)
    qm3 = qm.reshape(bsz, t_len, ODD_QW)
    qf3 = qf.reshape(bsz, t_len, ODD_QW)
    logf3 = logf.reshape(bsz, t_len, LANES)
    if past is None:
        bm = _moba_gate(qm3, moba_new.reshape(bsz, t_len, ODD_KVW))
        o_m = _flash("moba", qm3, mobab.reshape(bsz, t_len, ODD_KVW), 0, MOBA_HEADS, MOBA_HEADS,
                     bias_tiles=tabs["tiles"], bm=bm, blk=MOBA_BLOCK, out_dtype=BF16)
        cum = _cumsum_heads(logf3).reshape(bsz, FOX_HEADS, 1, t_len)
        o_f = _flash("fox", qf3, foxb.reshape(bsz, t_len, ODD_KVW), 0, FOX_HEADS, FOX_HEADS, cum=cum,
                     out_dtype=BF16)
    else:
        moba_pages, fox_pages, logf_pages, tbl = past
        t_pos = tbl.shape[1] * PAGE_ROWS
        idx = _moba_gate_step(qm3, moba_pages, tbl)
        rbt3 = w["rel_bias"].T.reshape(MOBA_HEADS, 1, REL_BUCKETS)
        o_m = _moba_attn_step(qm3, moba_pages, tbl, idx.reshape(bsz * MOBA_HEADS, MOBA_TOPK),
                              moba_new.reshape(bsz, t_len, ODD_KVW), rbt3, t_pos=t_pos)
        o_f = _fox_step(qf3, fox_pages, logf_pages, tbl, fox_new.reshape(bsz, t_len, ODD_KVW), logf3)
    x_new = _dense([o_m.reshape(m, ODD_QW), o_f.reshape(m, ODD_QW)], w["w_out_odd"], j, 0, d, tn=min(d, 512),
                   mode="res", res=x2d, scale=1.0, name="odd_out")
    kv_shape = (bsz, t_len, 2, ODD_HEADS, HEAD_DIM)
    return (x_new, moba_new.reshape(kv_shape), fox_new.reshape(kv_shape),
            logf3[:, :, :FOX_HEADS])


def _trunk(x, w, tabs, caches):
    bsz, t_len, d = x.shape
    depth = w["norm_mix"].shape[0]
    x2d = x.reshape(bsz * t_len, d)
    even_new, odd_new = [], []
    for layer in range(depth):
        x2d = _ffn(x2d, w["norm_ffn1"], w["w_ffn1_in"], w["w_ffn1_out"], layer)
        h = _rms_cast(x2d, w["norm_mix"], layer)
        j = layer // 2
        if layer % 2 == 0:
            past = None
            if caches is not None:
                past = (caches["cmp"], caches["sel"], caches["tbl"] + j * caches["n_phys"], caches["win"],
                        caches["hgrn"][j])
            x2d, *new = _even_mixer(x2d, h, bsz, t_len, j, w, tabs, past)
            even_new.append(new)
        else:
            past = None
            if caches is not None:
                past = (caches["moba"], caches["fox"], caches["logf"], caches["tbl"] + j * caches["n_phys"])
            x2d, *new = _odd_mixer(x2d, h, bsz, t_len, j, w, tabs, past)
            odd_new.append(new)
        x2d = _ffn(x2d, w["norm_ffn2"], w["w_ffn2_in"], w["w_ffn2_out"], layer)
    return x2d.reshape(bsz, t_len, d), even_new, odd_new


def kernel(x_prompt, x_sample, cache_nsa_cmp, cache_nsa_sel, cache_moba, cache_fox, cache_fox_logf,
           cache_nsa_win, state_hgrn, page_table, norm_ffn1, w_ffn1_in, w_ffn1_out, norm_mix, norm_ffn2,
           w_ffn2_in, w_ffn2_out, rel_bias, w_in_even, w_out_even, nsa_q_norm, nsa_k_norm, nsa_phi_pos,
           nsa_phi_w1, nsa_phi_w2, hg_lb, hg_o_norm, w_in_odd, w_out_odd, fox_f_bias, moba_qk_norm,
           fox_qk_norm):
    w = dict(norm_ffn1=norm_ffn1, w_ffn1_in=w_ffn1_in, w_ffn1_out=w_ffn1_out, norm_mix=norm_mix,
             norm_ffn2=norm_ffn2, w_ffn2_in=w_ffn2_in, w_ffn2_out=w_ffn2_out, rel_bias=rel_bias,
             w_in_even=w_in_even, w_out_even=w_out_even, nsa_q_norm=nsa_q_norm, nsa_k_norm=nsa_k_norm,
             nsa_phi_pos=nsa_phi_pos, nsa_phi_w1=nsa_phi_w1, nsa_phi_w2=nsa_phi_w2, hg_lb=hg_lb,
             hg_o_norm=hg_o_norm, w_in_odd=w_in_odd, w_out_odd=w_out_odd, fox_f_bias=fox_f_bias,
             moba_qk_norm=moba_qk_norm, fox_qk_norm=fox_qk_norm)
    tabs = _tables(w)
    y_prompt, pe, po = _trunk(x_prompt, w, tabs, None)

    n_phys = cache_nsa_cmp.shape[1]
    pages = lambda pool: pool.reshape(pool.shape[0] * n_phys, PAGE_ROWS, -1)
    caches = dict(cmp=pages(cache_nsa_cmp), sel=pages(cache_nsa_sel), moba=pages(cache_moba),
                  fox=pages(cache_fox), logf=pages(cache_fox_logf), win=cache_nsa_win, hgrn=state_hgrn,
                  tbl=page_table.astype(jnp.int32), n_phys=n_phys)
    y_sample, se, so = _trunk(x_sample, w, tabs, caches)

    stack = lambda items, i: jnp.stack([it[i] for it in items])
    return (y_prompt, y_sample,
            stack(pe, 0), stack(pe, 1), stack(pe, 2), stack(pe, 3),
            stack(po, 0), stack(po, 1), stack(po, 2),
            stack(se, 0), stack(se, 1), stack(se, 2), stack(se, 3),
            stack(so, 0), stack(so, 1), stack(so, 2))
```

```python
import functools
import math

import jax
import jax.numpy as jnp
import numpy as np
from jax import lax
from jax.experimental import pallas as pl
from jax.experimental.pallas import tpu as pltpu

F32 = jnp.float32
BF16 = jnp.bfloat16

HEAD_DIM = 128
NSA_HEADS = 8
NSA_KV_HEADS = 2
NSA_GROUP = NSA_HEADS // NSA_KV_HEADS
NSA_BLOCK = 64
NSA_TOPN = 16
NSA_WINDOW = 512
NSA_PHI_HIDDEN = 2 * HEAD_DIM
NSA_FORCE_SCORE = 1.0e4
HG_HEADS = 8
MOBA_HEADS = 8
MOBA_BLOCK = 256
MOBA_TOPK = 3
FOX_HEADS = 8
REL_BUCKETS = 32
REL_MAX_DIST = 128
EPS = 1e-6
PAGE_ROWS = 128

LANES = 128
VMEM_LIMIT_BYTES = 56 * 1024 * 1024
ROW_TILE = 512
Q_TILE = 128
FLASH_TILE = 256
NEG_BIG = -1e30


def _cparams(n_axes):
    return pltpu.CompilerParams(dimension_semantics=("arbitrary",) * n_axes,
                                vmem_limit_bytes=VMEM_LIMIT_BYTES)


def _rel_bucket(dist):
    n = jnp.maximum(dist, 0)
    exact = REL_BUCKETS // 2
    nf = jnp.maximum(n, 1).astype(F32)
    big = exact + (jnp.log(nf / exact) / math.log(REL_MAX_DIST / exact) * (REL_BUCKETS - exact)).astype(jnp.int32)
    return jnp.where(n < exact, n, jnp.minimum(big, REL_BUCKETS - 1))


def _bucket_bias(dist, table_rows):
    bucket = _rel_bucket(dist)
    out = jnp.zeros(dist.shape, F32) + table_rows(0)
    for b in range(1, REL_BUCKETS):
        out = jnp.where(bucket == b, table_rows(b), out)
    return out


def _rms_rows(x, g):
    return x * lax.rsqrt(jnp.mean(x * x, axis=-1, keepdims=True) + EPS) * g


def _log_sigmoid(z):
    return jnp.minimum(z, 0.0) - jnp.log1p(jnp.exp(-jnp.abs(z)))


def _sigmoid(z):
    return 1.0 / (1.0 + jnp.exp(-z))


def _rms_cast_kernel(x_ref, g_ref, o_ref):
    o_ref[...] = _rms_rows(x_ref[...], g_ref[...]).astype(o_ref.dtype)


def _rms_cast(x2d, g_stack, layer):
    m, d = x2d.shape
    tm = min(m, ROW_TILE)
    g3 = g_stack.reshape(g_stack.shape[0], 1, d)
    return pl.pallas_call(
        _rms_cast_kernel,
        grid=(m // tm,),
        in_specs=[pl.BlockSpec((tm, d), lambda i: (i, 0)),
                  pl.BlockSpec((None, 1, d), lambda i: (layer, 0, 0))],
        out_specs=pl.BlockSpec((tm, d), lambda i: (i, 0)),
        out_shape=jax.ShapeDtypeStruct((m, d), BF16),
        compiler_params=_cparams(1),
        name="rms_cast",
    )(x2d, g3)


def _dense_kernel(*refs, n_a, mode, scale):
    a_refs = refs[:n_a]
    pos = n_a
    w_ref = refs[pos]
    pos += 1
    w2_ref = res_ref = None
    if mode == "swiglu":
        w2_ref = refs[pos]
        pos += 1
    if mode == "res":
        res_ref = refs[pos]
        pos += 1
    o_ref = refs[pos]
    pos += 1
    wb_ref = refs[pos]
    pos += 1
    wb2_ref = refs[pos] if mode == "swiglu" else None

    @pl.when(pl.program_id(1) == 0)
    def _():
        wb_ref[...] = w_ref[...].astype(BF16)
        if mode == "swiglu":
            wb2_ref[...] = w2_ref[...].astype(BF16)

    if n_a == 1:
        a = a_refs[0][...]
    else:
        a = jnp.concatenate([r[...] for r in a_refs], axis=-1)
    y = jnp.dot(a, wb_ref[...], preferred_element_type=F32)
    if mode == "swiglu":
        y2 = jnp.dot(a, wb2_ref[...], preferred_element_type=F32)
        y = y * _sigmoid(y) * y2
    elif mode == "res":
        y = res_ref[...] + scale * y
    o_ref[...] = y.astype(o_ref.dtype)


def _dense(a_parts, w, lead, col0, n_out, *, tn, mode="plain", res=None, scale=1.0,
           out_dtype=F32, col0_b=None, name="dense"):
    m = a_parts[0].shape[0]
    k = sum(a.shape[1] for a in a_parts)
    tm = min(m, ROW_TILE)
    assert m % tm == 0 and n_out % tn == 0 and col0 % tn == 0
    if w.ndim == 3:
        wblock = (None, k, tn)

        def wmap(off):
            return lambda j, i: (lead, 0, j + off)
    else:
        wblock = (k, tn)

        def wmap(off):
            return lambda j, i: (0, j + off)
    in_specs = [pl.BlockSpec((tm, a.shape[1]), lambda j, i: (i, 0)) for a in a_parts]
    args = list(a_parts)
    in_specs.append(pl.BlockSpec(wblock, wmap(col0 // tn)))
    args.append(w)
    scratch = [pltpu.VMEM((k, tn), BF16)]
    if mode == "swiglu":
        assert col0_b % tn == 0
        in_specs.append(pl.BlockSpec(wblock, wmap(col0_b // tn)))
        args.append(w)
        scratch.append(pltpu.VMEM((k, tn), BF16))
    if mode == "res":
        in_specs.append(pl.BlockSpec((tm, tn), lambda j, i: (i, j)))
        args.append(res)
    return pl.pallas_call(
        functools.partial(_dense_kernel, n_a=len(a_parts), mode=mode, scale=scale),
        grid=(n_out // tn, m // tm),
        in_specs=in_specs,
        out_specs=pl.BlockSpec((tm, tn), lambda j, i: (i, j)),
        out_shape=jax.ShapeDtypeStruct((m, n_out), out_dtype),
        scratch_shapes=scratch,
        compiler_params=_cparams(2),
        name=name,
    )(*args)


def _ffn(x2d, norm_g, w_in, w_out, layer):
    d_ff = w_out.shape[1]
    xn = _rms_cast(x2d, norm_g, layer)
    h = _dense([xn], w_in, layer, 0, d_ff, tn=512, mode="swiglu", col0_b=d_ff,
               out_dtype=BF16, name="ffn_in")
    return _dense([h], w_out, layer, 0, x2d.shape[1], tn=512, mode="res", res=x2d, scale=0.5,
                  name="ffn_out")


def _bias_tiles(rel_bias):
    i = jnp.arange(FLASH_TILE)[:, None]
    j = jnp.arange(FLASH_TILE)[None, :]
    dist = jnp.stack([i - j, FLASH_TILE + i - j])
    tiles = jnp.transpose(rel_bias[_rel_bucket(dist)], (3, 0, 1, 2))
    return jnp.where((dist < 0)[None], NEG_BIG, tiles)


def _tables(w):
    lb_all = jnp.cumsum(jax.nn.softmax(w["hg_lb"].astype(F32), axis=0), axis=0)
    lb_all = lb_all - lb_all[0:1]
    lbh = lb_all.reshape(lb_all.shape[0], HG_HEADS, HEAD_DIM)
    lb_rows = jnp.stack([jnp.log(lbh), jnp.log1p(-lbh), 1.0 - lbh], axis=2)
    return {"tiles": _bias_tiles(w["rel_bias"]), "lb_rows": lb_rows, "phi_w1": w["nsa_phi_w1"].astype(BF16)}


def _bias_cmp_table(rel_bias, q_start, t_len, nb):
    t_pos = q_start + jnp.arange(t_len)
    dist = t_pos[:, None] - (jnp.arange(nb) * NSA_BLOCK + NSA_BLOCK - 1)[None, :]
    return jnp.transpose(rel_bias[_rel_bucket(dist)], (2, 0, 1))


def _even_post_kernel(pa_ref, graw_ref, qg_ref, kg_ref, q_ref, cmp_ref, sel_ref, win_ref,
                      selb_ref, winb_ref, gates_ref):
    scale = HEAD_DIM ** -0.5
    qg = qg_ref[...]
    for h in range(NSA_HEADS):
        sl = slice(h * HEAD_DIM, (h + 1) * HEAD_DIM)
        q_ref[:, sl] = (_rms_rows(pa_ref[:, sl], qg) * scale).astype(q_ref.dtype)
    base = NSA_HEADS * HEAD_DIM
    kv_w = 2 * NSA_KV_HEADS * HEAD_DIM
    cmp_ref[...] = pa_ref[:, base:base + kv_w]
    for which, (o_ref, ob_ref) in enumerate(((sel_ref, selb_ref), (win_ref, winb_ref))):
        off = base + (which + 1) * kv_w
        kg = kg_ref[which + 1:which + 2, :]
        for c in range(2 * NSA_KV_HEADS):
            src = pa_ref[:, off + c * HEAD_DIM: off + (c + 1) * HEAD_DIM]
            val = _rms_rows(src, kg) if c < NSA_KV_HEADS else src
            o_ref[:, c * HEAD_DIM:(c + 1) * HEAD_DIM] = val
            ob_ref[:, c * HEAD_DIM:(c + 1) * HEAD_DIM] = val.astype(BF16)
    gates_ref[...] = _sigmoid(graw_ref[...])


def _even_post(pa, graw, q_norm, k_norm):
    m = pa.shape[0]
    tm = min(m, ROW_TILE)
    kv_w = 2 * NSA_KV_HEADS * HEAD_DIM
    qw = NSA_HEADS * HEAD_DIM
    row = lambda w: pl.BlockSpec((tm, w), lambda i: (i, 0))
    full = lambda a: pl.BlockSpec(a.shape, lambda i: (0,) * a.ndim)
    qg = q_norm.reshape(1, HEAD_DIM)
    return pl.pallas_call(
        _even_post_kernel,
        grid=(m // tm,),
        in_specs=[row(pa.shape[1]), row(LANES), full(qg), full(k_norm)],
        out_specs=[row(qw), row(kv_w), row(kv_w), row(kv_w), row(kv_w), row(kv_w), row(LANES)],
        out_shape=[jax.ShapeDtypeStruct((m, qw), BF16)] + [jax.ShapeDtypeStruct((m, kv_w), F32)] * 3
        + [jax.ShapeDtypeStruct((m, kv_w), BF16)] * 2 + [jax.ShapeDtypeStruct((m, LANES), F32)],
        compiler_params=_cparams(1),
        name="even_post",
    )(pa, graw, qg, k_norm)


def _gelu_tanh(x):
    return 0.5 * x * (1.0 + jnp.tanh(math.sqrt(2.0 / math.pi) * (x + 0.044715 * (x * x * x))))


NSA_KV_COLS = 2 * NSA_KV_HEADS
CMP_PAGES_PER_STEP = 8
CMP_PAGES_PER_GROUP = 64


def _compress_mlp(x_of, nblk, pos_ref, w1_ref, w2_ref, kg_ref, o_ref, acc_ref):
    acc_ref[...] = jnp.zeros_like(acc_ref)

    def body(i2, carry):
        for c in range(NSA_KV_COLS):
            w = c // NSA_KV_HEADS
            xa = x_of(2 * i2, c) + pos_ref[w, pl.ds(2 * i2, 1), :]
            xb = x_of(2 * i2 + 1, c) + pos_ref[w, pl.ds(2 * i2 + 1, 1), :]
            x = jnp.concatenate([xa, xb], axis=-1).astype(BF16)
            wi = w1_ref[w, pl.ds(pl.multiple_of(i2 * 2 * HEAD_DIM, 2 * HEAD_DIM), 2 * HEAD_DIM), :]
            acc_ref[c] += jnp.dot(x, wi, preferred_element_type=F32)
        return carry

    lax.fori_loop(0, NSA_BLOCK // 2, body, 0)
    for c in range(NSA_KV_COLS):
        w, kh = divmod(c, NSA_KV_HEADS)
        hid = _gelu_tanh(acc_ref[c]).astype(BF16)
        y = jnp.dot(hid, w2_ref[w].astype(BF16), preferred_element_type=F32)
        if w == 0:
            y = _rms_rows(y, kg_ref[0:1, :])
        o_ref[w, :, kh * HEAD_DIM:(kh + 1) * HEAD_DIM] = y


def _compress_rows_kernel(*refs, nblk):
    x_refs = refs[:NSA_KV_COLS]
    pos_ref, w1_ref, w2_ref, kg_ref, o_ref, acc_ref = refs[NSA_KV_COLS:]

    def x_of(i, c):
        return x_refs[c][pl.ds(i, nblk, stride=NSA_BLOCK), :]

    _compress_mlp(x_of, nblk, pos_ref, w1_ref, w2_ref, kg_ref, o_ref, acc_ref)


def _compress_pages_kernel(*refs, pps, nblk):
    tbl_ref = refs[0]
    page_refs = refs[1:1 + pps]
    pos_ref, w1_ref, w2_ref, kg_ref, o_ref, xs_ref, acc_ref = refs[1 + pps:]
    st = pl.program_id(2)
    page_rows = PAGE_ROWS * NSA_KV_COLS
    for p in range(pps):
        xs_ref[pl.ds(pl.multiple_of((st * pps + p) * page_rows, page_rows), page_rows), :] = page_refs[p][...]

    @pl.when(st == pl.num_programs(2) - 1)
    def _():
        def x_of(i, c):
            return xs_ref[pl.ds(i * NSA_KV_COLS + c, nblk, stride=NSA_BLOCK * NSA_KV_COLS), :]

        _compress_mlp(x_of, nblk, pos_ref, w1_ref, w2_ref, kg_ref, o_ref, acc_ref)


def _compress(src, tbl, pos, w1b, w2, k_norm, j):
    weight_specs = lambda nidx: [
        pl.BlockSpec((None, 2, NSA_BLOCK, HEAD_DIM), lambda *a: (j, 0, 0, 0)),
        pl.BlockSpec((None, 2, NSA_BLOCK * HEAD_DIM, NSA_PHI_HIDDEN), lambda *a: (j, 0, 0, 0),
                     pipeline_mode=pl.Buffered(1)),
        pl.BlockSpec((None, 2, NSA_PHI_HIDDEN, HEAD_DIM), lambda *a: (j, 0, 0, 0)),
        pl.BlockSpec(k_norm.shape, lambda *a: (0, 0)),
    ]
    kvw = NSA_KV_HEADS * HEAD_DIM
    if tbl is None:
        bsz, t_len, _ = src.shape
        nblk = t_len // NSA_BLOCK
        return pl.pallas_call(
            functools.partial(_compress_rows_kernel, nblk=nblk),
            grid=(bsz,),
            in_specs=[pl.BlockSpec((None, t_len, HEAD_DIM), lambda b, c=c: (b, 0, c)) for c in range(NSA_KV_COLS)]
            + weight_specs(1),
            out_specs=pl.BlockSpec((2, None, nblk, kvw), lambda b: (0, b, 0, 0)),
            out_shape=jax.ShapeDtypeStruct((2, bsz, nblk, kvw), F32),
            scratch_shapes=[pltpu.VMEM((NSA_KV_COLS, nblk, NSA_PHI_HIDDEN), F32)],
            compiler_params=_cparams(1),
            name="nsa_compress_rows",
        )(*([src] * NSA_KV_COLS), pos, w1b, w2, k_norm)
    bsz, npg = tbl.shape
    pps = math.gcd(CMP_PAGES_PER_STEP, npg)
    ppg = math.gcd(CMP_PAGES_PER_GROUP, npg)
    bpp = PAGE_ROWS // NSA_BLOCK
    nblk = ppg * bpp
    page_rows = PAGE_ROWS * NSA_KV_COLS

    def page_spec(p):
        return pl.BlockSpec((None, page_rows, HEAD_DIM),
                            lambda b, grp, st, t: (t[b, grp * ppg + st * pps + p], 0, 0))

    grid_spec = pltpu.PrefetchScalarGridSpec(
        num_scalar_prefetch=1,
        grid=(bsz, npg // ppg, ppg // pps),
        in_specs=[page_spec(p) for p in range(pps)] + weight_specs(4),
        out_specs=pl.BlockSpec((2, None, nblk, kvw), lambda b, grp, st, t: (0, b, grp, 0)),
        scratch_shapes=[pltpu.VMEM((ppg * page_rows, HEAD_DIM), F32),
                        pltpu.VMEM((NSA_KV_COLS, nblk, NSA_PHI_HIDDEN), F32)],
    )
    return pl.pallas_call(
        functools.partial(_compress_pages_kernel, pps=pps, nblk=nblk),
        grid_spec=grid_spec,
        out_shape=jax.ShapeDtypeStruct((2, bsz, npg * bpp, kvw), F32),
        compiler_params=_cparams(3),
        name="nsa_compress_pages",
    )(tbl, *([src] * pps), pos, w1b, w2, k_norm)


def _nsa_cmp_kernel(q_ref, kc_ref, vc_ref, bias_ref, gates_ref, oc_ref, sel_ref, *, q_start, tq, nb, extra):
    kh = pl.program_id(1)
    qi = pl.program_id(2)
    t_pos = q_start + qi * tq + lax.broadcasted_iota(jnp.int32, (tq, 1), 0)
    blk = lax.broadcasted_iota(jnp.int32, (1, nb), 1)
    valid = t_pos >= blk * NSA_BLOCK + (NSA_BLOCK - 1)
    kc = kc_ref[...].astype(BF16)
    vc = vc_ref[...].astype(BF16)
    gates = gates_ref[...]
    imp = jnp.zeros((tq, nb), F32)
    for g in range(NSA_GROUP):
        qg = q_ref[:, g * HEAD_DIM:(g + 1) * HEAD_DIM]
        s = lax.dot_general(qg, kc, (((1,), (1,)), ((), ())), preferred_element_type=F32) + bias_ref[g]
        s = jnp.where(valid, s, NEG_BIG)
        m = jnp.max(s, axis=-1, keepdims=True)
        p = jnp.where(valid, jnp.exp(s - m), 0.0)
        l = jnp.sum(p, axis=-1, keepdims=True)
        p = p / jnp.where(l > 0, l, 1.0)
        imp = imp + p
        o = jnp.dot(p.astype(BF16), vc, preferred_element_type=F32)
        onehot = lax.broadcasted_iota(jnp.int32, (1, LANES), 1) == (kh * NSA_GROUP + g) * 3
        gcol = jnp.sum(jnp.where(onehot, gates, 0.0), axis=-1, keepdims=True)
        oc_ref[:, g * HEAD_DIM:(g + 1) * HEAD_DIM] = o * gcol
    cur = t_pos // NSA_BLOCK
    forced = (blk == 0) | (blk == cur) | (blk == cur - 1)
    score = jnp.where(forced, NSA_FORCE_SCORE, jnp.where(blk <= cur, imp, -1.0))
    rank = jnp.zeros((tq, nb), jnp.int32)
    for mcol in range(nb):
        sm = score[:, mcol:mcol + 1]
        ahead = (sm > score) | ((sm == score) & (mcol < blk))
        rank = rank + ahead.astype(jnp.int32)
    if not extra:
        sel_ref[...] = (rank < NSA_TOPN).astype(F32)
    else:
        rank = rank + (score < NSA_FORCE_SCORE).astype(jnp.int32)
        rank_extra = jnp.sum((score >= NSA_FORCE_SCORE).astype(jnp.int32), axis=-1, keepdims=True)
        lane = lax.broadcasted_iota(jnp.int32, (tq, NSA_TOPN), 1)
        out = jnp.zeros((tq, NSA_TOPN), jnp.int32)
        for r in range(NSA_TOPN):
            idx_r = jnp.sum(jnp.where(rank == r, blk, 0), axis=-1, keepdims=True)
            idx_r = idx_r + jnp.where(rank_extra == r, nb, 0)
            out = jnp.where(lane == r, idx_r, out)
        sel_ref[...] = out


def _nsa_cmp(q3, kvc, bias_c, gates3, *, q_start, extra):
    bsz, t_len, _ = q3.shape
    nb = kvc.shape[2]
    tq = min(t_len, Q_TILE)
    gw = NSA_GROUP * HEAD_DIM
    if extra:
        assert t_len == 1 and q_start // NSA_BLOCK == nb
        sel_shape = jax.ShapeDtypeStruct((bsz, NSA_KV_HEADS, t_len, NSA_TOPN), jnp.int32)
        sel_spec = pl.BlockSpec((None, None, tq, NSA_TOPN), lambda b, kh, qi: (b, kh, qi, 0))
    else:
        assert (q_start + t_len) == nb * NSA_BLOCK
        sel_shape = jax.ShapeDtypeStruct((bsz, NSA_KV_HEADS, t_len, nb), F32)
        sel_spec = pl.BlockSpec((None, None, tq, nb), lambda b, kh, qi: (b, kh, qi, 0))
    return pl.pallas_call(
        functools.partial(_nsa_cmp_kernel, q_start=q_start, tq=tq, nb=nb, extra=extra),
        grid=(bsz, NSA_KV_HEADS, t_len // tq),
        in_specs=[
            pl.BlockSpec((None, tq, gw), lambda b, kh, qi: (b, qi, kh)),
            pl.BlockSpec((None, None, nb, HEAD_DIM), lambda b, kh, qi: (0, b, 0, kh)),
            pl.BlockSpec((None, None, nb, HEAD_DIM), lambda b, kh, qi: (1, b, 0, kh)),
            pl.BlockSpec((NSA_GROUP, tq, nb), lambda b, kh, qi: (kh, qi, 0)),
            pl.BlockSpec((None, tq, LANES), lambda b, kh, qi: (b, qi, 0)),
        ],
        out_specs=[pl.BlockSpec((None, tq, gw), lambda b, kh, qi: (b, qi, kh)), sel_spec],
        out_shape=[jax.ShapeDtypeStruct((bsz, t_len, NSA_HEADS * HEAD_DIM), F32), sel_shape],
        compiler_params=_cparams(3),
        name="nsa_cmp_attn",
    )(q3, kvc, kvc, bias_c, gates3)


def _lane_column(x, col):
    onehot = lax.broadcasted_iota(jnp.int32, (1, x.shape[1]), 1) == col
    return jnp.sum(jnp.where(onehot, x, 0.0), axis=-1, keepdims=True)


def _row_to_column(row):
    n = row.shape[1]
    eye = lax.broadcasted_iota(jnp.int32, (n, n), 0) == lax.broadcasted_iota(jnp.int32, (n, n), 1)
    return jnp.sum(jnp.where(eye, row, 0.0), axis=-1, keepdims=True)


def _flash_kernel(*refs, kind, hps, shared_kv, tq, blk, gate_branch):
    q_ref, k_ref, v_ref = refs[:3]
    pos = 3
    bias_ref = bm_ref = gates_ref = c_ref = None
    if kind in ("nsa_sel", "nsa_win", "moba"):
        bias_ref = refs[pos]
        pos += 1
    if kind in ("nsa_sel", "moba"):
        bm_ref = refs[pos]
        pos += 1
    if kind in ("nsa_sel", "nsa_win"):
        gates_ref = refs[pos]
        pos += 1
    if kind == "fox":
        c_ref = refs[pos]
        pos += 1
    o_ref, m_ref, l_ref, acc_ref = refs[pos:pos + 4]
    tk = tq
    step = pl.program_id(1)
    qi = pl.program_id(2)
    q0 = pl.multiple_of(qi * tq, tq)
    ii = lax.broadcasted_iota(jnp.int32, (tq, tk), 0)
    jj = lax.broadcasted_iota(jnp.int32, (tq, tk), 1)
    n_back = NSA_WINDOW // tk
    lo = jnp.maximum(qi - n_back, 0) if kind == "nsa_win" else 0
    m_ref[...] = jnp.full(m_ref.shape, NEG_BIG, F32)
    l_ref[...] = jnp.zeros(l_ref.shape, F32)
    acc_ref[...] = jnp.zeros(acc_ref.shape, F32)

    def head_cols(g):
        return slice(g * HEAD_DIM, (g + 1) * HEAD_DIM)

    def tile_step(ki, diag):
        k0 = pl.multiple_of(ki * tk, tk)
        delta = qi - ki
        shared_add = None
        if kind == "nsa_sel":
            bm = bm_ref[...]
            shared_add = (_lane_column(bm, k0 // blk) - 1.0) * (-NEG_BIG)
            for sub in range(1, tk // blk):
                shared_add = jnp.where(jj >= sub * blk, (_lane_column(bm, k0 // blk + sub) - 1.0) * (-NEG_BIG),
                                       shared_add)
        elif kind == "nsa_win" and not diag:
            shared_add = jnp.where(delta == n_back, jnp.where(jj < ii, NEG_BIG, 0.0), 0.0)
        elif kind == "fox" and diag:
            shared_add = jnp.where(jj <= ii, 0.0, NEG_BIG)
        for g in range(hps):
            kv_cols = slice(0, HEAD_DIM) if shared_kv else head_cols(g)
            kt = k_ref[pl.ds(k0, tk), kv_cols]
            vt = v_ref[pl.ds(k0, tk), kv_cols]
            s = lax.dot_general(q_ref[:, head_cols(g)], kt, (((1,), (1,)), ((), ())), preferred_element_type=F32)
            if kind == "fox":
                s = s + (c_ref[g, :, pl.ds(q0, LANES)][:, 0:1] - c_ref[g, :, pl.ds(k0, tk)])
            elif diag:
                s = s + bias_ref[g, 0]
            else:
                near = bias_ref[g, 1]
                s = s + jnp.where(delta == 1, near, near[tq - 1:tq, 0:1])
            if shared_add is not None:
                s = s + shared_add
            if kind == "moba":
                s = s + (_lane_column(bm_ref[g], k0 // blk) - 1.0) * (-NEG_BIG)
            m_old = m_ref[g]
            m_new = jnp.maximum(m_old, jnp.max(s, axis=-1, keepdims=True))
            p = jnp.exp(s - m_new)
            alpha = jnp.exp(m_old - m_new)
            l_ref[g] = alpha * l_ref[g] + jnp.sum(p, axis=-1, keepdims=True)
            acc_ref[g] = alpha * acc_ref[g] + jnp.dot(p.astype(BF16), vt, preferred_element_type=F32)
            m_ref[g] = m_new

    def body(ki, carry):
        tile_step(ki, False)
        return carry

    lax.fori_loop(lo, qi, body, 0)
    tile_step(qi, True)
    for g in range(hps):
        o = acc_ref[g] / l_ref[g]
        if gates_ref is not None:
            o = o * _lane_column(gates_ref[...], (step * hps + g) * 3 + gate_branch)
        o_ref[:, head_cols(g)] = o.astype(o_ref.dtype)


def _flash(kind, q3, kv3, k_col, v_col, n_kv_heads, *, bias_tiles=None, bm=None, gates3=None, cum=None,
           blk=0, gate_branch=0, out_dtype=F32):
    bsz, t_len, qw = q3.shape
    n_heads = qw // HEAD_DIM
    shared_kv = n_kv_heads < n_heads
    hps = n_heads // n_kv_heads if shared_kv else 2
    tq = FLASH_TILE
    assert t_len % tq == 0 and (shared_kv or (k_col % hps == 0 and v_col % hps == 0))
    gw = hps * HEAD_DIM
    if shared_kv:
        kv_spec = lambda c0: pl.BlockSpec((None, t_len, HEAD_DIM), lambda b, h, qi: (b, 0, c0 + h))
    else:
        kv_spec = lambda c0: pl.BlockSpec((None, t_len, gw), lambda b, h, qi: (b, 0, c0 // hps + h))
    in_specs = [pl.BlockSpec((None, tq, gw), lambda b, h, qi: (b, qi, h)), kv_spec(k_col), kv_spec(v_col)]
    args = [q3, kv3, kv3]
    if bias_tiles is not None:
        in_specs.append(pl.BlockSpec((hps, 2, tq, tq), lambda b, h, qi: (h, 0, 0, 0)))
        args.append(bias_tiles)
    if bm is not None:
        nb = bm.shape[-1]
        if shared_kv:
            in_specs.append(pl.BlockSpec((None, None, tq, nb), lambda b, h, qi: (b, h, qi, 0)))
        else:
            in_specs.append(pl.BlockSpec((None, hps, tq, nb), lambda b, h, qi: (b, h, qi, 0)))
        args.append(bm)
    if gates3 is not None:
        in_specs.append(pl.BlockSpec((None, tq, LANES), lambda b, h, qi: (b, qi, 0)))
        args.append(gates3)
    if cum is not None:
        in_specs.append(pl.BlockSpec((None, hps, 1, t_len), lambda b, h, qi: (b, h, 0, 0)))
        args.append(cum)
    return pl.pallas_call(
        functools.partial(_flash_kernel, kind=kind, hps=hps, shared_kv=shared_kv, tq=tq, blk=blk,
                          gate_branch=gate_branch),
        grid=(bsz, n_heads // hps, t_len // tq),
        in_specs=in_specs,
        out_specs=pl.BlockSpec((None, tq, gw), lambda b, h, qi: (b, qi, h)),
        out_shape=jax.ShapeDtypeStruct((bsz, t_len, qw), out_dtype),
        scratch_shapes=[pltpu.VMEM((hps, tq, 1), F32), pltpu.VMEM((hps, tq, 1), F32),
                        pltpu.VMEM((hps, tq, HEAD_DIM), F32)],
        compiler_params=_cparams(3),
        name="flash_" + kind,
    )(*args)


def _sum3_kernel(a_ref, b_ref, c_ref, o_ref):
    o_ref[...] = (a_ref[...] + b_ref[...] + c_ref[...]).astype(o_ref.dtype)


def _sum3_cast(a, b, c):
    m, n = a.shape
    tm = min(m, ROW_TILE)
    spec = pl.BlockSpec((tm, n), lambda i: (i, 0))
    return pl.pallas_call(
        _sum3_kernel, grid=(m // tm,), in_specs=[spec] * 3, out_specs=spec,
        out_shape=jax.ShapeDtypeStruct((m, n), BF16), compiler_params=_cparams(1), name="nsa_sum",
    )(a, b, c)


def _hgrn_kernel(q_ref, z_ref, v_ref, g_ref, lb_ref, on_ref, s0_ref, o_ref, s_ref, st_ref, *, t_len, chunk, hps):
    on = on_ref[...]
    ri = lax.broadcasted_iota(jnp.int32, (chunk, chunk), 0)
    ci = lax.broadcasted_iota(jnp.int32, (chunk, chunk), 1)
    tril = (ci <= ri).astype(F32)
    rows = lax.broadcasted_iota(jnp.int32, (chunk, 1), 0)

    single = t_len < chunk

    def load(ref, r0, cols):
        if single:
            return jnp.broadcast_to(ref[0:1, cols], (chunk, HEAD_DIM))
        return ref[pl.ds(r0, chunk), cols]

    for g in range(hps):
        st_ref[g] = s0_ref[g].T

    def body(c, carry):
        r0 = pl.multiple_of(c * chunk, chunk)
        for g in range(hps):
            cols = slice(g * HEAD_DIM, (g + 1) * HEAD_DIM)
            q = load(q_ref, r0, cols)
            z = load(z_ref, r0, cols)
            v = load(v_ref, r0, cols)
            a_term = lb_ref[g, 0:1, :]
            b_term = lb_ref[g, 1:2, :] + _log_sigmoid(z)
            logf = jnp.maximum(a_term, b_term) + jnp.log1p(jnp.exp(-jnp.abs(a_term - b_term)))
            k = lb_ref[g, 2:3, :] * _sigmoid(-z)
            if single:
                logf = jnp.where(rows < t_len, logf, 0.0)
                k = jnp.where(rows < t_len, k, 0.0)
            cum = jnp.dot(tril, logf, preferred_element_type=F32, precision=lax.Precision.HIGHEST)
            a_last = cum[chunk - 1:chunk, :]
            o = jnp.zeros((chunk, HEAD_DIM), F32)
            for s_row in range(chunk):
                diff = jnp.where(rows >= s_row, cum - cum[s_row:s_row + 1, :], -jnp.inf)
                wgt = q * jnp.exp(diff) * k[s_row:s_row + 1, :]
                o = o + jnp.sum(wgt, axis=-1, keepdims=True) * v[s_row:s_row + 1, :]
            st = st_ref[g]
            qa = (q * jnp.exp(cum)).astype(BF16)
            o = o + lax.dot_general(qa, st.astype(BF16), (((1,), (1,)), ((), ())), preferred_element_type=F32)
            kd = (k * jnp.exp(a_last - cum)).astype(BF16)
            st_ref[g] = jnp.exp(a_last) * st + lax.dot_general(v.astype(BF16), kd, (((0,), (0,)), ((), ())),
                                                               preferred_element_type=F32)
            gate = load(g_ref, r0, cols)
            o = _rms_rows(o, on) * (gate * _sigmoid(gate))
            if single:
                o_ref[:, cols] = o[0:t_len, :].astype(o_ref.dtype)
            else:
                o_ref[pl.ds(r0, chunk), cols] = o.astype(o_ref.dtype)
        return carry

    lax.fori_loop(0, max(t_len // chunk, 1), body, 0)
    for g in range(hps):
        s_ref[g] = st_ref[g].T


HGRN_HEADS_PER_STEP = 4


def _hgrn(ph3, lb_rows, o_norm, s0):
    bsz, t_len, _ = ph3.shape
    chunk = 16
    hps = HGRN_HEADS_PER_STEP
    assert t_len % chunk == 0 or t_len == 1
    gw = hps * HEAD_DIM
    n_steps = HG_HEADS // hps
    col = lambda part: pl.BlockSpec((None, t_len, gw), lambda b, h: (b, 0, part * n_steps + h))
    on = o_norm.reshape(1, HEAD_DIM)
    state_spec = pl.BlockSpec((None, hps, HEAD_DIM, HEAD_DIM), lambda b, h: (b, h, 0, 0))
    return pl.pallas_call(
        functools.partial(_hgrn_kernel, t_len=t_len, chunk=chunk, hps=hps),
        grid=(bsz, n_steps),
        in_specs=[col(0), col(1), col(2), col(3),
                  pl.BlockSpec((hps, 3, HEAD_DIM), lambda b, h: (h, 0, 0)),
                  pl.BlockSpec((1, HEAD_DIM), lambda b, h: (0, 0)),
                  state_spec],
        out_specs=[pl.BlockSpec((None, t_len, gw), lambda b, h: (b, 0, h)), state_spec],
        out_shape=[jax.ShapeDtypeStruct((bsz, t_len, HG_HEADS * HEAD_DIM), BF16),
                   jax.ShapeDtypeStruct((bsz, HG_HEADS, HEAD_DIM, HEAD_DIM), F32)],
        scratch_shapes=[pltpu.VMEM((hps, HEAD_DIM, HEAD_DIM), F32)],
        compiler_params=_cparams(2),
        name="hgrn2",
    )(ph3, ph3, ph3, ph3, lb_rows, on, s0)


def _group_rows(q_ref, group):
    return jnp.concatenate([q_ref[:, g * HEAD_DIM:(g + 1) * HEAD_DIM].astype(F32) for g in range(group)], axis=0)


def _store_group_rows(o_ref, o):
    for g in range(o.shape[0]):
        o_ref[:, g * HEAD_DIM:(g + 1) * HEAD_DIM] = o[g:g + 1, :].astype(o_ref.dtype)


def _nsa_sel_step_kernel(tbl_ref, idx_ref, q_ref, kv_ref, kn_ref, vn_ref, rbt_ref, gates_ref, o_ref,
                         m_ref, l_ref, acc_ref, *, t_pos, nb):
    b = pl.program_id(0)
    kh = pl.program_id(1)
    j = pl.program_id(2)
    q4 = _group_rows(q_ref, NSA_GROUP)
    rbt = rbt_ref[...]
    col = lambda bk: rbt[:, bk:bk + 1]

    @pl.when(j == 0)
    def _():
        m_ref[...] = jnp.sum(q4 * kn_ref[...], axis=-1, keepdims=True) + col(0)
        l_ref[...] = jnp.ones_like(l_ref)
        acc_ref[...] = jnp.broadcast_to(vn_ref[...], acc_ref.shape)

    blk_id = idx_ref[b * NSA_KV_HEADS + kh, j]

    @pl.when(blk_id < nb)
    def _():
        kt = kv_ref[pl.ds(kh, NSA_BLOCK, stride=NSA_KV_COLS), :].astype(BF16)
        vt = kv_ref[pl.ds(NSA_KV_HEADS + kh, NSA_BLOCK, stride=NSA_KV_COLS), :].astype(BF16)
        s = lax.dot_general(q4.astype(BF16), kt, (((1,), (1,)), ((), ())), preferred_element_type=F32)
        dist = t_pos - (blk_id * NSA_BLOCK + lax.broadcasted_iota(jnp.int32, (1, NSA_BLOCK), 1))
        s = s + _bucket_bias(jnp.broadcast_to(dist, s.shape), col)
        m_old = m_ref[...]
        m_new = jnp.maximum(m_old, jnp.max(s, axis=-1, keepdims=True))
        p = jnp.exp(s - m_new)
        alpha = jnp.exp(m_old - m_new)
        l_ref[...] = alpha * l_ref[...] + jnp.sum(p, axis=-1, keepdims=True)
        acc_ref[...] = alpha * acc_ref[...] + jnp.dot(p.astype(BF16), vt, preferred_element_type=F32)
        m_ref[...] = m_new

    @pl.when(j == NSA_TOPN - 1)
    def _():
        o = acc_ref[...] / l_ref[...]
        gates = gates_ref[...]
        gcol = jnp.concatenate([_lane_column(gates, (kh * NSA_GROUP + g) * 3 + 1) for g in range(NSA_GROUP)], axis=0)
        _store_group_rows(o_ref, o * gcol)


def _nsa_sel_step(q3, cache_pages, tbl, idx, sel_new3, rbt, gates3, *, t_pos):
    bsz = q3.shape[0]
    nb = tbl.shape[1] * (PAGE_ROWS // NSA_BLOCK)
    gw = NSA_GROUP * HEAD_DIM
    halves = PAGE_ROWS // NSA_BLOCK

    def cache_map(b, kh, j, tbl_ref, idx_ref):
        blk = jnp.minimum(idx_ref[b * NSA_KV_HEADS + kh, j], nb - 1)
        return (tbl_ref[b, blk // halves], blk % halves, 0)

    grid_spec = pltpu.PrefetchScalarGridSpec(
        num_scalar_prefetch=2,
        grid=(bsz, NSA_KV_HEADS, NSA_TOPN),
        in_specs=[
            pl.BlockSpec((None, 1, gw), lambda b, kh, j, t, i: (b, 0, kh)),
            pl.BlockSpec((None, NSA_BLOCK * NSA_KV_COLS, HEAD_DIM), cache_map),
            pl.BlockSpec((None, 1, HEAD_DIM), lambda b, kh, j, t, i: (b, 0, kh)),
            pl.BlockSpec((None, 1, HEAD_DIM), lambda b, kh, j, t, i: (b, 0, NSA_KV_HEADS + kh)),
            pl.BlockSpec((None, NSA_GROUP, REL_BUCKETS), lambda b, kh, j, t, i: (kh, 0, 0)),
            pl.BlockSpec((None, 1, LANES), lambda b, kh, j, t, i: (b, 0, 0)),
        ],
        out_specs=pl.BlockSpec((None, 1, gw), lambda b, kh, j, t, i: (b, 0, kh)),
        scratch_shapes=[pltpu.VMEM((NSA_GROUP, 1), F32), pltpu.VMEM((NSA_GROUP, 1), F32),
                        pltpu.VMEM((NSA_GROUP, HEAD_DIM), F32)],
    )
    return pl.pallas_call(
        functools.partial(_nsa_sel_step_kernel, t_pos=t_pos, nb=nb),
        grid_spec=grid_spec,
        out_shape=jax.ShapeDtypeStruct((bsz, 1, NSA_HEADS * HEAD_DIM), F32),
        compiler_params=_cparams(3),
        name="nsa_sel_step",
    )(tbl, idx, q3, cache_pages, sel_new3, sel_new3, rbt, gates3)


def _nsa_win_step_kernel(q_ref, k_ref, v_ref, kn_ref, vn_ref, rbt_ref, gates_ref, o_ref, *, pw):
    kh = pl.program_id(1)
    q4 = _group_rows(q_ref, NSA_GROUP)
    rbt = rbt_ref[...]
    col = lambda bk: rbt[:, bk:bk + 1]
    s = lax.dot_general(q4.astype(BF16), k_ref[...].astype(BF16), (((1,), (1,)), ((), ())),
                        preferred_element_type=F32)
    dist = pw - lax.broadcasted_iota(jnp.int32, (NSA_GROUP, pw), 1)
    mask = dist <= NSA_WINDOW
    s = jnp.where(mask, s + _bucket_bias(dist, col), NEG_BIG)
    s_self = jnp.sum(q4 * kn_ref[...], axis=-1, keepdims=True) + col(0)
    m = jnp.maximum(jnp.max(s, axis=-1, keepdims=True), s_self)
    p = jnp.where(mask, jnp.exp(s - m), 0.0)
    p_self = jnp.exp(s_self - m)
    l = jnp.sum(p, axis=-1, keepdims=True) + p_self
    o = jnp.dot(p.astype(BF16), v_ref[...].astype(BF16), preferred_element_type=F32) + p_self * vn_ref[...]
    gates = gates_ref[...]
    gcol = jnp.concatenate([_lane_column(gates, (kh * NSA_GROUP + g) * 3 + 2) for g in range(NSA_GROUP)], axis=0)
    _store_group_rows(o_ref, o / l * gcol)


def _nsa_win_step(q3, win_cache, lead, win_new3, rbt, gates3):
    bsz = q3.shape[0]
    pw = win_cache.shape[1]
    gw = NSA_GROUP * HEAD_DIM
    return pl.pallas_call(
        functools.partial(_nsa_win_step_kernel, pw=pw),
        grid=(bsz, NSA_KV_HEADS),
        in_specs=[
            pl.BlockSpec((None, 1, gw), lambda b, kh: (b, 0, kh)),
            pl.BlockSpec((None, pw, HEAD_DIM), lambda b, kh: (lead + b, 0, kh)),
            pl.BlockSpec((None, pw, HEAD_DIM), lambda b, kh: (lead + b, 0, NSA_KV_HEADS + kh)),
            pl.BlockSpec((None, 1, HEAD_DIM), lambda b, kh: (b, 0, kh)),
            pl.BlockSpec((None, 1, HEAD_DIM), lambda b, kh: (b, 0, NSA_KV_HEADS + kh)),
            pl.BlockSpec((None, NSA_GROUP, REL_BUCKETS), lambda b, kh: (kh, 0, 0)),
            pl.BlockSpec((None, 1, LANES), lambda b, kh: (b, 0, 0)),
        ],
        out_specs=pl.BlockSpec((None, 1, gw), lambda b, kh: (b, 0, kh)),
        out_shape=jax.ShapeDtypeStruct((bsz, 1, NSA_HEADS * HEAD_DIM), F32),
        compiler_params=_cparams(2),
        name="nsa_win_step",
    )(q3, win_cache, win_cache, win_new3, win_new3, rbt, gates3)


def _even_mixer(x2d, h, bsz, t_len, j, w, tabs, past):
    m, d = x2d.shape
    qw = NSA_HEADS * HEAD_DIM
    kv_w = 2 * NSA_KV_HEADS * HEAD_DIM
    n_main = qw + 3 * kv_w
    n_gate = 3 * NSA_HEADS
    w_in = w["w_in_even"]
    pa = _dense([h], w_in, j, 0, n_main, tn=512, name="even_in_attn")
    w_gate = jnp.pad(w_in[j, :, n_main:n_main + n_gate], ((0, 0), (0, LANES - n_gate)))
    graw = _dense([h], w_gate, None, 0, LANES, tn=LANES, name="even_in_gate")
    w_hg = w_in[j, :, n_main + n_gate:]
    ph = _dense([h], w_hg, None, 0, w_hg.shape[1], tn=512, name="even_in_hgrn")
    q, cmp_new, sel_new, win_new, selb, winb, gates = _even_post(pa, graw, w["nsa_q_norm"][j], w["nsa_k_norm"][j])
    q3 = q.reshape(bsz, t_len, qw)
    gates3 = gates.reshape(bsz, t_len, LANES)
    rel_bias = w["rel_bias"]
    k_norm = w["nsa_k_norm"][j]
    if past is None:
        q_start = 0
        src, tbl = cmp_new.reshape(bsz, t_len, kv_w), None
        s0 = jnp.zeros((bsz, HG_HEADS, HEAD_DIM, HEAD_DIM), F32)
    else:
        src, sel_pages, tbl, win_cache, s0 = past
        q_start = tbl.shape[1] * PAGE_ROWS
    kvc = _compress(src, tbl, w["nsa_phi_pos"], tabs["phi_w1"], w["nsa_phi_w2"], k_norm, j)
    nb = kvc.shape[2]
    bias_c = _bias_cmp_table(rel_bias, q_start, t_len, nb)
    o_c, sel = _nsa_cmp(q3, kvc, bias_c, gates3, q_start=q_start, extra=past is not None)
    if past is None:
        selb3 = selb.reshape(bsz, t_len, kv_w)
        winb3 = winb.reshape(bsz, t_len, kv_w)
        o_s = _flash("nsa_sel", q3, selb3, 0, NSA_KV_HEADS, NSA_KV_HEADS, bias_tiles=tabs["tiles"], bm=sel,
                     gates3=gates3, blk=NSA_BLOCK, gate_branch=1)
        o_w = _flash("nsa_win", q3, winb3, 0, NSA_KV_HEADS, NSA_KV_HEADS, bias_tiles=tabs["tiles"],
                     gates3=gates3, gate_branch=2)
        keep = min(NSA_WINDOW, t_len)
        win_buf = win_new.reshape(bsz, t_len, kv_w)[:, t_len - keep:]
    else:
        rbt = rel_bias.T.reshape(NSA_KV_HEADS, NSA_GROUP, REL_BUCKETS)
        sel_new3 = sel_new.reshape(bsz, t_len, kv_w)
        win_new3 = win_new.reshape(bsz, t_len, kv_w)
        idx = sel.reshape(bsz * NSA_KV_HEADS, NSA_TOPN)
        o_s = _nsa_sel_step(q3, sel_pages, tbl, idx, sel_new3, rbt, gates3, t_pos=q_start)
        wc = win_cache.reshape(win_cache.shape[0] * win_cache.shape[1], win_cache.shape[2], kv_w)
        o_w = _nsa_win_step(q3, wc, j * bsz, win_new3, rbt, gates3)
        win_all = jnp.concatenate([wc[j * bsz:(j + 1) * bsz], win_new3], axis=1)
        keep = min(NSA_WINDOW, win_all.shape[1])
        win_buf = win_all[:, win_all.shape[1] - keep:]
    a_nsa = _sum3_cast(o_c.reshape(m, qw), o_s.reshape(m, qw), o_w.reshape(m, qw))
    o_hg, s_new = _hgrn(ph.reshape(bsz, t_len, ph.shape[1]), tabs["lb_rows"][j], w["hg_o_norm"][j], s0)
    x_new = _dense([a_nsa, o_hg.reshape(m, HG_HEADS * HEAD_DIM)], w["w_out_even"], j, 0, d, tn=min(d, 512),
                   mode="res", res=x2d, scale=1.0, name="even_out")
    kv_shape = (bsz, t_len, 2, NSA_KV_HEADS, HEAD_DIM)
    return (x_new, cmp_new.reshape(kv_shape), sel_new.reshape(kv_shape),
            win_buf.reshape(bsz, win_buf.shape[1], 2, NSA_KV_HEADS, HEAD_DIM), s_new)


ODD_HEADS = MOBA_HEADS
ODD_QW = ODD_HEADS * HEAD_DIM
ODD_KVW = 2 * ODD_HEADS * HEAD_DIM


def _odd_post_kernel(pm_ref, fz_ref, mg_ref, fg_ref, fb_ref, qm_ref, moba_ref, mobab_ref, qf_ref, fox_ref,
                     foxb_ref, logf_ref):
    scale = HEAD_DIM ** -0.5
    off = 0
    for g_ref, q_ref, kv_ref, kvb_ref in ((mg_ref, qm_ref, moba_ref, mobab_ref), (fg_ref, qf_ref, fox_ref, foxb_ref)):
        qg = g_ref[0:1, :]
        kg = g_ref[1:2, :]
        for h in range(ODD_HEADS):
            sl = slice(h * HEAD_DIM, (h + 1) * HEAD_DIM)
            q_ref[:, sl] = (_rms_rows(pm_ref[:, off + h * HEAD_DIM: off + (h + 1) * HEAD_DIM], qg) * scale
                            ).astype(q_ref.dtype)
        off += ODD_QW
        for c in range(2 * ODD_HEADS):
            src = pm_ref[:, off + c * HEAD_DIM: off + (c + 1) * HEAD_DIM]
            val = _rms_rows(src, kg) if c < ODD_HEADS else src
            kv_ref[:, c * HEAD_DIM:(c + 1) * HEAD_DIM] = val
            kvb_ref[:, c * HEAD_DIM:(c + 1) * HEAD_DIM] = val.astype(BF16)
        off += ODD_KVW
    logf_ref[...] = _log_sigmoid(fz_ref[...] + fb_ref[...])


def _odd_post(pm, fz, moba_qk, fox_qk, fb):
    m = pm.shape[0]
    tm = min(m, ROW_TILE // 2)
    row = lambda w: pl.BlockSpec((tm, w), lambda i: (i, 0))
    full = lambda a: pl.BlockSpec(a.shape, lambda i: (0,) * a.ndim)
    return pl.pallas_call(
        _odd_post_kernel,
        grid=(m // tm,),
        in_specs=[row(pm.shape[1]), row(LANES), full(moba_qk), full(fox_qk), full(fb)],
        out_specs=[row(ODD_QW), row(ODD_KVW), row(ODD_KVW), row(ODD_QW), row(ODD_KVW), row(ODD_KVW), row(LANES)],
        out_shape=[jax.ShapeDtypeStruct((m, ODD_QW), BF16), jax.ShapeDtypeStruct((m, ODD_KVW), F32),
                   jax.ShapeDtypeStruct((m, ODD_KVW), BF16), jax.ShapeDtypeStruct((m, ODD_QW), BF16),
                   jax.ShapeDtypeStruct((m, ODD_KVW), F32), jax.ShapeDtypeStruct((m, ODD_KVW), BF16),
                   jax.ShapeDtypeStruct((m, LANES), F32)],
        compiler_params=_cparams(1),
        name="odd_post",
    )(pm, fz, moba_qk, fox_qk, fb)


def _topk_rank(score, n):
    idx = lax.broadcasted_iota(jnp.int32, (1, n), 1)
    rank = jnp.zeros(score.shape, jnp.int32)
    for mcol in range(n):
        sm = score[:, mcol:mcol + 1]
        ahead = (sm > score) | ((sm == score) & (mcol < idx))
        rank = rank + ahead.astype(jnp.int32)
    return rank


def _moba_gate_kernel(q_ref, k_ref, bm_ref, *, t_len, nbl):
    k_mean = jnp.mean(k_ref[...].reshape(nbl, MOBA_BLOCK, HEAD_DIM), axis=1)
    gate = lax.dot_general(q_ref[...], k_mean.astype(BF16), (((1,), (1,)), ((), ())), preferred_element_type=F32)
    cur = lax.broadcasted_iota(jnp.int32, (t_len, 1), 0) // MOBA_BLOCK
    blk = lax.broadcasted_iota(jnp.int32, (1, nbl), 1)
    past_ok = blk < cur
    gate = jnp.where(past_ok, gate, -jnp.inf)
    rank = _topk_rank(gate, nbl)
    sel = (rank < MOBA_TOPK) & past_ok & (jnp.abs(gate) < jnp.inf)
    bm_ref[...] = (sel | (blk == cur)).astype(F32)


def _moba_gate(qm3, moba_new3):
    bsz, t_len, _ = qm3.shape
    assert t_len % MOBA_BLOCK == 0
    nbl = t_len // MOBA_BLOCK
    return pl.pallas_call(
        functools.partial(_moba_gate_kernel, t_len=t_len, nbl=nbl),
        grid=(bsz, MOBA_HEADS),
        in_specs=[pl.BlockSpec((None, t_len, HEAD_DIM), lambda b, h: (b, 0, h)),
                  pl.BlockSpec((None, t_len, HEAD_DIM), lambda b, h: (b, 0, h))],
        out_specs=pl.BlockSpec((None, None, t_len, nbl), lambda b, h: (b, h, 0, 0)),
        out_shape=jax.ShapeDtypeStruct((bsz, MOBA_HEADS, t_len, nbl), F32),
        compiler_params=_cparams(2),
        name="moba_gate",
    )(qm3, moba_new3)


def _cumsum_kernel(x_ref, o_ref, *, t_len):
    n = Q_TILE
    upper = (lax.broadcasted_iota(jnp.int32, (n, n), 0) <= lax.broadcasted_iota(jnp.int32, (n, n), 1)).astype(F32)
    carry = jnp.zeros((FOX_HEADS, 1), F32)
    for c in range(t_len // n):
        xt = x_ref[c * n:(c + 1) * n, :].T[0:FOX_HEADS, :]
        cum = jnp.dot(xt, upper, preferred_element_type=F32, precision=lax.Precision.HIGHEST) + carry
        o_ref[:, c * n:(c + 1) * n] = cum
        carry = cum[:, n - 1:n]


def _cumsum_heads(logf3):
    bsz, t_len, _ = logf3.shape
    assert t_len % Q_TILE == 0
    return pl.pallas_call(
        functools.partial(_cumsum_kernel, t_len=t_len),
        grid=(bsz,),
        in_specs=[pl.BlockSpec((None, t_len, LANES), lambda b: (b, 0, 0))],
        out_specs=pl.BlockSpec((None, FOX_HEADS, t_len), lambda b: (b, 0, 0)),
        out_shape=jax.ShapeDtypeStruct((bsz, FOX_HEADS, t_len), F32),
        compiler_params=_cparams(1),
        name="fox_cumsum",
    )(logf3)


MOBA_GATE_PAGES_PER_STEP = 4


def _moba_gate_step_kernel(*refs, pps, nblk, ppb):
    q_ref = refs[1]
    k_refs = refs[2:2 + pps]
    idx_ref, gate_ref = refs[2 + pps:]
    st = pl.program_id(1)

    @pl.when(st == 0)
    def _():
        gate_ref[...] = jnp.zeros_like(gate_ref)

    q8 = _group_rows(q_ref, MOBA_HEADS)
    lane = lax.broadcasted_iota(jnp.int32, (1, nblk), 1)
    for blk_in_step in range(pps // ppb):
        ksum = jnp.zeros((MOBA_HEADS, HEAD_DIM), F32)
        for p in range(ppb):
            ksum = ksum + jnp.sum(k_refs[blk_in_step * ppb + p][...], axis=0)
        gcol = jnp.sum(q8 * (ksum * (1.0 / MOBA_BLOCK)), axis=-1, keepdims=True)
        gate_ref[...] = jnp.where(lane == st * (pps // ppb) + blk_in_step, gcol, gate_ref[...])

    @pl.when(st == pl.num_programs(1) - 1)
    def _():
        gate = gate_ref[...]
        rank = _topk_rank(gate, nblk)
        blk = lax.broadcasted_iota(jnp.int32, (1, nblk), 1)
        lane = lax.broadcasted_iota(jnp.int32, (MOBA_HEADS, MOBA_TOPK), 1)
        out = jnp.full((MOBA_HEADS, MOBA_TOPK), -1, jnp.int32)
        for r in range(min(MOBA_TOPK, nblk)):
            hit = (rank == r) & (jnp.abs(gate) < jnp.inf)
            idx_r = jnp.sum(jnp.where(hit, blk + 1, 0), axis=-1, keepdims=True) - 1
            out = jnp.where(lane == r, idx_r, out)
        idx_ref[...] = out


def _moba_gate_step(qm3, pages4, tbl):
    bsz, npg = tbl.shape
    ppb = MOBA_BLOCK // PAGE_ROWS
    pps = MOBA_GATE_PAGES_PER_STEP
    assert npg % pps == 0 and pps % ppb == 0

    def key_spec(p):
        return pl.BlockSpec((None, PAGE_ROWS, MOBA_HEADS, HEAD_DIM), lambda b, st, t: (t[b, st * pps + p], 0, 0, 0))

    grid_spec = pltpu.PrefetchScalarGridSpec(
        num_scalar_prefetch=1,
        grid=(bsz, npg // pps),
        in_specs=[pl.BlockSpec((None, 1, ODD_QW), lambda b, st, t: (b, 0, 0))] + [key_spec(p) for p in range(pps)],
        out_specs=pl.BlockSpec((None, MOBA_HEADS, MOBA_TOPK), lambda b, st, t: (b, 0, 0)),
        scratch_shapes=[pltpu.VMEM((MOBA_HEADS, npg // ppb), F32)],
    )
    return pl.pallas_call(
        functools.partial(_moba_gate_step_kernel, pps=pps, nblk=npg // ppb, ppb=ppb),
        grid_spec=grid_spec,
        out_shape=jax.ShapeDtypeStruct((bsz, MOBA_HEADS, MOBA_TOPK), jnp.int32),
        compiler_params=_cparams(2),
        name="moba_gate_step",
    )(tbl, qm3, *([pages4] * pps))


def _moba_attn_step_kernel(tbl_ref, idx_ref, q_ref, kv_ref, kn_ref, vn_ref, rbt_ref, o_ref,
                           m_ref, l_ref, acc_ref, *, t_pos, ppb):
    b = pl.program_id(0)
    h = pl.program_id(1)
    s_id = pl.program_id(2)
    sub = 8
    q = q_ref[...].astype(F32)
    rbt = rbt_ref[...]
    col = lambda bk: rbt[:, bk:bk + 1]

    @pl.when(s_id == 0)
    def _():
        s_self = jnp.sum(q * kn_ref[...], axis=-1, keepdims=True) + col(0)
        m_ref[...] = jnp.broadcast_to(s_self, m_ref.shape)
        l_ref[...] = jnp.ones_like(l_ref)
        acc_ref[...] = jnp.broadcast_to(vn_ref[...], acc_ref.shape)

    blk_id = idx_ref[b * MOBA_HEADS + h, s_id // ppb]

    @pl.when(blk_id >= 0)
    def _():
        kt = kv_ref[pl.ds(h, PAGE_ROWS, stride=2 * MOBA_HEADS), :].astype(BF16)
        vt = kv_ref[pl.ds(MOBA_HEADS + h, PAGE_ROWS, stride=2 * MOBA_HEADS), :].astype(BF16)
        q8 = jnp.broadcast_to(q, (sub, HEAD_DIM)).astype(BF16)
        s = lax.dot_general(q8, kt, (((1,), (1,)), ((), ())), preferred_element_type=F32)
        key_pos = blk_id * MOBA_BLOCK + (s_id % ppb) * PAGE_ROWS + lax.broadcasted_iota(jnp.int32, (sub, PAGE_ROWS), 1)
        s = s + _bucket_bias(t_pos - key_pos, col)
        m_old = m_ref[...]
        m_new = jnp.maximum(m_old, jnp.max(s, axis=-1, keepdims=True))
        p = jnp.exp(s - m_new)
        alpha = jnp.exp(m_old - m_new)
        l_ref[...] = alpha * l_ref[...] + jnp.sum(p, axis=-1, keepdims=True)
        acc_ref[...] = alpha * acc_ref[...] + jnp.dot(p.astype(BF16), vt, preferred_element_type=F32)
        m_ref[...] = m_new

    @pl.when(s_id == pl.num_programs(2) - 1)
    def _():
        o_ref[...] = (acc_ref[0:1, :] / l_ref[0:1, :]).astype(o_ref.dtype)


def _moba_attn_step(qm3, pages3, tbl, idx, moba_new3, rbt3, *, t_pos):
    bsz = qm3.shape[0]
    ppb = MOBA_BLOCK // PAGE_ROWS

    def cache_map(b, h, s, tbl_ref, idx_ref):
        blk = jnp.maximum(idx_ref[b * MOBA_HEADS + h, s // ppb], 0)
        return (tbl_ref[b, blk * ppb + s % ppb], 0, 0)

    head = lambda off: pl.BlockSpec((None, 1, HEAD_DIM), lambda b, h, s, t, i: (b, 0, off + h))
    grid_spec = pltpu.PrefetchScalarGridSpec(
        num_scalar_prefetch=2,
        grid=(bsz, MOBA_HEADS, MOBA_TOPK * ppb),
        in_specs=[head(0),
                  pl.BlockSpec((None, PAGE_ROWS * 2 * MOBA_HEADS, HEAD_DIM), cache_map),
                  head(0), head(MOBA_HEADS),
                  pl.BlockSpec((None, 1, REL_BUCKETS), lambda b, h, s, t, i: (h, 0, 0))],
        out_specs=head(0),
        scratch_shapes=[pltpu.VMEM((8, 1), F32), pltpu.VMEM((8, 1), F32), pltpu.VMEM((8, HEAD_DIM), F32)],
    )
    return pl.pallas_call(
        functools.partial(_moba_attn_step_kernel, t_pos=t_pos, ppb=ppb),
        grid_spec=grid_spec,
        out_shape=jax.ShapeDtypeStruct((bsz, 1, ODD_QW), BF16),
        compiler_params=_cparams(3),
        name="moba_attn_step",
    )(tbl, idx, qm3, pages3, moba_new3, moba_new3, rbt3)


def _fox_step_kernel(tbl_ref, q_ref, k_ref, v_ref, lf_ref, kn_ref, vn_ref, lfn_ref, o_ref,
                     m_ref, l_ref, acc_ref, carry_ref, *, npg):
    pg = pl.program_id(1)
    q = q_ref[...].astype(F32)

    @pl.when(pg == 0)
    def _():
        prod = q * kn_ref[...]
        for h in range(FOX_HEADS):
            sl = slice(h * HEAD_DIM, (h + 1) * HEAD_DIM)
            s_self = jnp.sum(prod[:, sl], axis=-1, keepdims=True)
            m_ref[h:h + 1, :] = jnp.broadcast_to(s_self, (1, HEAD_DIM))
            acc_ref[h:h + 1, :] = vn_ref[:, sl]
        l_ref[...] = jnp.ones_like(l_ref)
        carry_ref[...] = lfn_ref[:, 0:FOX_HEADS]

    n = PAGE_ROWS
    lf = jnp.concatenate([lf_ref[...], jnp.zeros((n - FOX_HEADS, n), F32)], axis=0).T[:, 0:FOX_HEADS]
    after = (lax.broadcasted_iota(jnp.int32, (n, n), 1) > lax.broadcasted_iota(jnp.int32, (n, n), 0)).astype(F32)
    bias = carry_ref[...] + jnp.dot(after, lf, preferred_element_type=F32, precision=lax.Precision.HIGHEST)
    carry_ref[...] += jnp.sum(lf, axis=0, keepdims=True)
    for h in range(FOX_HEADS):
        sl = slice(h * HEAD_DIM, (h + 1) * HEAD_DIM)
        s = jnp.sum(k_ref[:, h, :] * q[:, sl], axis=-1, keepdims=True) + bias[:, h:h + 1]
        m_old = m_ref[h:h + 1, 0:1]
        m_new = jnp.maximum(m_old, jnp.max(s, axis=0, keepdims=True))
        p = jnp.exp(s - m_new)
        alpha = jnp.exp(m_old - m_new)
        l_ref[h:h + 1, :] = alpha * l_ref[h:h + 1, :] + jnp.sum(p, axis=0, keepdims=True)
        acc_ref[h:h + 1, :] = alpha * acc_ref[h:h + 1, :] + jnp.sum(p * v_ref[:, h, :], axis=0, keepdims=True)
        m_ref[h:h + 1, :] = jnp.broadcast_to(m_new, (1, HEAD_DIM))

    @pl.when(pg == npg - 1)
    def _():
        o = acc_ref[...] / l_ref[...]
        _store_group_rows(o_ref, o)


def _fox_step(qf3, pages4, logf_pages, tbl, fox_new3, logf_new3):
    bsz, npg = tbl.shape
    rev = lambda part: (lambda b, pg, t: (t[b, npg - 1 - pg], 0, part, 0))
    new = lambda col: pl.BlockSpec((None, 1, ODD_QW), lambda b, pg, t: (b, 0, col))
    grid_spec = pltpu.PrefetchScalarGridSpec(
        num_scalar_prefetch=1,
        grid=(bsz, npg),
        in_specs=[new(0),
                  pl.BlockSpec((None, PAGE_ROWS, FOX_HEADS, HEAD_DIM), rev(0)),
                  pl.BlockSpec((None, PAGE_ROWS, FOX_HEADS, HEAD_DIM), rev(1)),
                  pl.BlockSpec((None, FOX_HEADS, PAGE_ROWS), lambda b, pg, t: (t[b, npg - 1 - pg], 0, 0)),
                  new(0), new(1),
                  pl.BlockSpec((None, 1, LANES), lambda b, pg, t: (b, 0, 0))],
        out_specs=new(0),
        scratch_shapes=[pltpu.VMEM((FOX_HEADS, HEAD_DIM), F32), pltpu.VMEM((FOX_HEADS, HEAD_DIM), F32),
                        pltpu.VMEM((FOX_HEADS, HEAD_DIM), F32), pltpu.VMEM((1, FOX_HEADS), F32)],
    )
    return pl.pallas_call(
        functools.partial(_fox_step_kernel, npg=npg),
        grid_spec=grid_spec,
        out_shape=jax.ShapeDtypeStruct((bsz, 1, ODD_QW), BF16),
        compiler_params=_cparams(2),
        name="fox_step",
    )(tbl, qf3, pages4, pages4, logf_pages, fox_new3, fox_new3, logf_new3)


def _odd_mixer(x2d, h, bsz, t_len, j, w, tabs, past):
    m, d = x2d.shape
    n_main = 2 * (ODD_QW + ODD_KVW)
    w_in = w["w_in_odd"]
    pm = _dense([h], w_in, j, 0, n_main, tn=512, name="odd_in")
    w_fz = jnp.pad(w_in[j, :, n_main:n_main + FOX_HEADS], ((0, 0), (0, LANES - FOX_HEADS)))
    fz = _dense([h], w_fz, None, 0, LANES, tn=LANES, name="odd_in_forget")
    fb = jnp.pad(w["fox_f_bias"][j].astype(F32), (0, LANES - FOX_HEADS)).reshape(1, LANES)
    qm, moba_new, mobab, qf, fox_new, foxb, logf = _odd_post(pm, fz, w["moba_qk_norm"][j], w["fox_qk_norm"][j], fb)
    qm3 = qm.reshape(bsz, t_len, ODD_QW)
    qf3 = qf.reshape(bsz, t_len, ODD_QW)
    logf3 = logf.reshape(bsz, t_len, LANES)
    if past is None:
        bm = _moba_gate(qm3, moba_new.reshape(bsz, t_len, ODD_KVW))
        o_m = _flash("moba", qm3, mobab.reshape(bsz, t_len, ODD_KVW), 0, MOBA_HEADS, MOBA_HEADS,
                     bias_tiles=tabs["tiles"], bm=bm, blk=MOBA_BLOCK, out_dtype=BF16)
        cum = _cumsum_heads(logf3).reshape(bsz, FOX_HEADS, 1, t_len)
        o_f = _flash("fox", qf3, foxb.reshape(bsz, t_len, ODD_KVW), 0, FOX_HEADS, FOX_HEADS, cum=cum,
                     out_dtype=BF16)
    else:
        moba_pages4, fox_pages, logf_pages, tbl = past
        t_pos = tbl.shape[1] * PAGE_ROWS
        idx = _moba_gate_step(qm3, moba_pages4, tbl)
        rbt3 = w["rel_bias"].T.reshape(MOBA_HEADS, 1, REL_BUCKETS)
        moba_pages3 = moba_pages4.reshape(moba_pages4.shape[0], PAGE_ROWS * 2 * MOBA_HEADS, HEAD_DIM)
        o_m = _moba_attn_step(qm3, moba_pages3, tbl, idx.reshape(bsz * MOBA_HEADS, MOBA_TOPK),
                              moba_new.reshape(bsz, t_len, ODD_KVW), rbt3, t_pos=t_pos)
        o_f = _fox_step(qf3, fox_pages, logf_pages, tbl, fox_new.reshape(bsz, t_len, ODD_KVW), logf3)
    x_new = _dense([o_m.reshape(m, ODD_QW), o_f.reshape(m, ODD_QW)], w["w_out_odd"], j, 0, d, tn=min(d, 512),
                   mode="res", res=x2d, scale=1.0, name="odd_out")
    kv_shape = (bsz, t_len, 2, ODD_HEADS, HEAD_DIM)
    return (x_new, moba_new.reshape(kv_shape), fox_new.reshape(kv_shape),
            logf3[:, :, :FOX_HEADS])


def _trunk(x, w, tabs, caches):
    bsz, t_len, d = x.shape
    depth = w["norm_mix"].shape[0]
    x2d = x.reshape(bsz * t_len, d)
    even_new, odd_new = [], []
    for layer in range(depth):
        x2d = _ffn(x2d, w["norm_ffn1"], w["w_ffn1_in"], w["w_ffn1_out"], layer)
        h = _rms_cast(x2d, w["norm_mix"], layer)
        j = layer // 2
        if layer % 2 == 0:
            past = None
            if caches is not None:
                past = (caches["cmp"], caches["sel"], caches["tbl"] + j * caches["n_phys"], caches["win"],
                        caches["hgrn"][j])
            x2d, *new = _even_mixer(x2d, h, bsz, t_len, j, w, tabs, past)
            even_new.append(new)
        else:
            past = None
            if caches is not None:
                past = (caches["moba"], caches["fox"], caches["logf"], caches["tbl"] + j * caches["n_phys"])
            x2d, *new = _odd_mixer(x2d, h, bsz, t_len, j, w, tabs, past)
            odd_new.append(new)
        x2d = _ffn(x2d, w["norm_ffn2"], w["w_ffn2_in"], w["w_ffn2_out"], layer)
    return x2d.reshape(bsz, t_len, d), even_new, odd_new


def kernel(x_prompt, x_sample, cache_nsa_cmp, cache_nsa_sel, cache_moba, cache_fox, cache_fox_logf,
           cache_nsa_win, state_hgrn, page_table, norm_ffn1, w_ffn1_in, w_ffn1_out, norm_mix, norm_ffn2,
           w_ffn2_in, w_ffn2_out, rel_bias, w_in_even, w_out_even, nsa_q_norm, nsa_k_norm, nsa_phi_pos,
           nsa_phi_w1, nsa_phi_w2, hg_lb, hg_o_norm, w_in_odd, w_out_odd, fox_f_bias, moba_qk_norm,
           fox_qk_norm):
    w = dict(norm_ffn1=norm_ffn1, w_ffn1_in=w_ffn1_in, w_ffn1_out=w_ffn1_out, norm_mix=norm_mix,
             norm_ffn2=norm_ffn2, w_ffn2_in=w_ffn2_in, w_ffn2_out=w_ffn2_out, rel_bias=rel_bias,
             w_in_even=w_in_even, w_out_even=w_out_even, nsa_q_norm=nsa_q_norm, nsa_k_norm=nsa_k_norm,
             nsa_phi_pos=nsa_phi_pos, nsa_phi_w1=nsa_phi_w1, nsa_phi_w2=nsa_phi_w2, hg_lb=hg_lb,
             hg_o_norm=hg_o_norm, w_in_odd=w_in_odd, w_out_odd=w_out_odd, fox_f_bias=fox_f_bias,
             moba_qk_norm=moba_qk_norm, fox_qk_norm=fox_qk_norm)
    tabs = _tables(w)
    y_prompt, pe, po = _trunk(x_prompt, w, tabs, None)

    n_phys = cache_nsa_cmp.shape[1]
    n_layers = cache_nsa_cmp.shape[0]
    rows3 = lambda pool: pool.reshape(n_layers * n_phys, -1, HEAD_DIM)
    rows4 = lambda pool: pool.reshape(n_layers * n_phys, PAGE_ROWS, -1, HEAD_DIM)
    logf_t = jnp.swapaxes(cache_fox_logf.reshape(n_layers * n_phys, PAGE_ROWS, FOX_HEADS), 1, 2)
    caches = dict(cmp=rows3(cache_nsa_cmp), sel=rows3(cache_nsa_sel), moba=rows4(cache_moba),
                  fox=rows4(cache_fox), logf=logf_t, win=cache_nsa_win, hgrn=state_hgrn,
                  tbl=page_table.astype(jnp.int32), n_phys=n_phys)
    y_sample, se, so = _trunk(x_sample, w, tabs, caches)

    stack = lambda items, i: jnp.stack([it[i] for it in items])
    return (y_prompt, y_sample,
            stack(pe, 0), stack(pe, 1), stack(pe, 2), stack(pe, 3),
            stack(po, 0), stack(po, 1), stack(po, 2),
            stack(se, 0), stack(se, 1), stack(se, 2), stack(se, 3),
            stack(so, 0), stack(so, 1), stack(so, 2))
```

```python
import functools
import math

import jax
import jax.numpy as jnp
import numpy as np
from jax import lax
from jax.experimental import pallas as pl
from jax.experimental.pallas import tpu as pltpu

F32 = jnp.float32
BF16 = jnp.bfloat16

HEAD_DIM = 128
NSA_HEADS = 8
NSA_KV_HEADS = 2
NSA_GROUP = NSA_HEADS // NSA_KV_HEADS
NSA_BLOCK = 64
NSA_TOPN = 16
NSA_WINDOW = 512
NSA_PHI_HIDDEN = 2 * HEAD_DIM
NSA_FORCE_SCORE = 1.0e4
HG_HEADS = 8
MOBA_HEADS = 8
MOBA_BLOCK = 256
MOBA_TOPK = 3
FOX_HEADS = 8
REL_BUCKETS = 32
REL_MAX_DIST = 128
EPS = 1e-6
PAGE_ROWS = 128

LANES = 128
VMEM_LIMIT_BYTES = 56 * 1024 * 1024
ROW_TILE = 512
DEEP_K = 2048
Q_TILE = 128
FLASH_TILE = 256
NEG_BIG = -1e30


def _cparams(n_axes):
    return pltpu.CompilerParams(dimension_semantics=("arbitrary",) * n_axes,
                                vmem_limit_bytes=VMEM_LIMIT_BYTES)


def _rel_bucket(dist):
    n = jnp.maximum(dist, 0)
    exact = REL_BUCKETS // 2
    nf = jnp.maximum(n, 1).astype(F32)
    big = exact + (jnp.log(nf / exact) / math.log(REL_MAX_DIST / exact) * (REL_BUCKETS - exact)).astype(jnp.int32)
    return jnp.where(n < exact, n, jnp.minimum(big, REL_BUCKETS - 1))


def _bucket_bias(dist, table_rows):
    bucket = _rel_bucket(dist)
    out = jnp.zeros(dist.shape, F32) + table_rows(0)
    for b in range(1, REL_BUCKETS):
        out = jnp.where(bucket == b, table_rows(b), out)
    return out


def _rms_rows(x, g):
    return x * lax.rsqrt(jnp.mean(x * x, axis=-1, keepdims=True) + EPS) * g


def _log_sigmoid(z):
    return jnp.minimum(z, 0.0) - jnp.log1p(jnp.exp(-jnp.abs(z)))


def _sigmoid(z):
    return 1.0 / (1.0 + jnp.exp(-z))


def _rms_cast_kernel(x_ref, g_ref, o_ref):
    o_ref[...] = _rms_rows(x_ref[...], g_ref[...]).astype(o_ref.dtype)


def _rms_cast(x2d, g_stack, layer):
    m, d = x2d.shape
    tm = min(m, ROW_TILE)
    g3 = g_stack.reshape(g_stack.shape[0], 1, d)
    return pl.pallas_call(
        _rms_cast_kernel,
        grid=(m // tm,),
        in_specs=[pl.BlockSpec((tm, d), lambda i: (i, 0)),
                  pl.BlockSpec((None, 1, d), lambda i: (layer, 0, 0))],
        out_specs=pl.BlockSpec((tm, d), lambda i: (i, 0)),
        out_shape=jax.ShapeDtypeStruct((m, d), BF16),
        compiler_params=_cparams(1),
        name="rms_cast",
    )(x2d, g3)


def _dense_kernel(*refs, n_a, mode, scale):
    a_refs = refs[:n_a]
    pos = n_a
    w_ref = refs[pos]
    pos += 1
    w2_ref = res_ref = None
    if mode == "swiglu":
        w2_ref = refs[pos]
        pos += 1
    if mode == "res":
        res_ref = refs[pos]
        pos += 1
    o_ref = refs[pos]
    pos += 1
    wb_ref = refs[pos]
    pos += 1
    wb2_ref = refs[pos] if mode == "swiglu" else None

    @pl.when(pl.program_id(1) == 0)
    def _():
        wb_ref[...] = w_ref[...].astype(BF16)
        if mode == "swiglu":
            wb2_ref[...] = w2_ref[...].astype(BF16)

    if n_a == 1:
        a = a_refs[0][...]
    else:
        a = jnp.concatenate([r[...] for r in a_refs], axis=-1)
    y = jnp.dot(a, wb_ref[...], preferred_element_type=F32)
    if mode == "swiglu":
        y2 = jnp.dot(a, wb2_ref[...], preferred_element_type=F32)
        y = y * _sigmoid(y) * y2
    elif mode == "res":
        y = res_ref[...] + scale * y
    o_ref[...] = y.astype(o_ref.dtype)


def _dense(a_parts, w, lead, col0, n_out, *, tn, mode="plain", res=None, scale=1.0,
           out_dtype=F32, col0_b=None, name="dense"):
    m = a_parts[0].shape[0]
    k = sum(a.shape[1] for a in a_parts)
    tm = min(m, 2 * ROW_TILE if k <= DEEP_K else ROW_TILE)
    assert m % tm == 0 and n_out % tn == 0 and col0 % tn == 0
    if w.ndim == 3:
        wblock = (None, k, tn)

        def wmap(off):
            return lambda j, i: (lead, 0, j + off)
    else:
        wblock = (k, tn)

        def wmap(off):
            return lambda j, i: (0, j + off)
    in_specs = [pl.BlockSpec((tm, a.shape[1]), lambda j, i: (i, 0)) for a in a_parts]
    args = list(a_parts)
    in_specs.append(pl.BlockSpec(wblock, wmap(col0 // tn)))
    args.append(w)
    scratch = [pltpu.VMEM((k, tn), BF16)]
    if mode == "swiglu":
        assert col0_b % tn == 0
        in_specs.append(pl.BlockSpec(wblock, wmap(col0_b // tn)))
        args.append(w)
        scratch.append(pltpu.VMEM((k, tn), BF16))
    if mode == "res":
        in_specs.append(pl.BlockSpec((tm, tn), lambda j, i: (i, j)))
        args.append(res)
    return pl.pallas_call(
        functools.partial(_dense_kernel, n_a=len(a_parts), mode=mode, scale=scale),
        grid=(n_out // tn, m // tm),
        in_specs=in_specs,
        out_specs=pl.BlockSpec((tm, tn), lambda j, i: (i, j)),
        out_shape=jax.ShapeDtypeStruct((m, n_out), out_dtype),
        scratch_shapes=scratch,
        compiler_params=_cparams(2),
        name=name,
    )(*args)


def _ffn(x2d, norm_g, w_in, w_out, layer):
    d_ff = w_out.shape[1]
    xn = _rms_cast(x2d, norm_g, layer)
    h = _dense([xn], w_in, layer, 0, d_ff, tn=512, mode="swiglu", col0_b=d_ff,
               out_dtype=BF16, name="ffn_in")
    return _dense([h], w_out, layer, 0, x2d.shape[1], tn=512, mode="res", res=x2d, scale=0.5,
                  name="ffn_out")


def _bias_tiles(rel_bias):
    i = jnp.arange(FLASH_TILE)[:, None]
    j = jnp.arange(FLASH_TILE)[None, :]
    dist = jnp.stack([i - j, FLASH_TILE + i - j])
    onehot = (_rel_bucket(dist)[..., None] == jnp.arange(REL_BUCKETS)).astype(F32)
    tiles = jnp.einsum("ktsb,bh->hkts", onehot, rel_bias.astype(F32), precision=lax.Precision.HIGHEST)
    return jnp.where((dist < 0)[None], NEG_BIG, tiles)


def _tables(w):
    lb_all = jnp.cumsum(jax.nn.softmax(w["hg_lb"].astype(F32), axis=0), axis=0)
    lb_all = lb_all - lb_all[0:1]
    lbh = lb_all.reshape(lb_all.shape[0], HG_HEADS, HEAD_DIM)
    lb_rows = jnp.stack([jnp.log(lbh), jnp.log1p(-lbh), 1.0 - lbh], axis=2)
    return {"tiles": _bias_tiles(w["rel_bias"]), "lb_rows": lb_rows, "phi_w1": w["nsa_phi_w1"].astype(BF16)}


def _bias_cmp_table(rel_bias, q_start, t_len, nb):
    t_pos = q_start + jnp.arange(t_len)
    dist = t_pos[:, None] - (jnp.arange(nb) * NSA_BLOCK + NSA_BLOCK - 1)[None, :]
    return jnp.transpose(rel_bias[_rel_bucket(dist)], (2, 0, 1))


def _even_post_kernel(pa_ref, graw_ref, qg_ref, kg_ref, q_ref, cmp_ref, sel_ref, win_ref,
                      selb_ref, winb_ref, gates_ref):
    scale = HEAD_DIM ** -0.5
    qg = qg_ref[...]
    for h in range(NSA_HEADS):
        sl = slice(h * HEAD_DIM, (h + 1) * HEAD_DIM)
        q_ref[:, sl] = (_rms_rows(pa_ref[:, sl], qg) * scale).astype(q_ref.dtype)
    base = NSA_HEADS * HEAD_DIM
    kv_w = 2 * NSA_KV_HEADS * HEAD_DIM
    cmp_ref[...] = pa_ref[:, base:base + kv_w]
    for which, (o_ref, ob_ref) in enumerate(((sel_ref, selb_ref), (win_ref, winb_ref))):
        off = base + (which + 1) * kv_w
        kg = kg_ref[which + 1:which + 2, :]
        for c in range(2 * NSA_KV_HEADS):
            src = pa_ref[:, off + c * HEAD_DIM: off + (c + 1) * HEAD_DIM]
            val = _rms_rows(src, kg) if c < NSA_KV_HEADS else src
            o_ref[:, c * HEAD_DIM:(c + 1) * HEAD_DIM] = val
            ob_ref[:, c * HEAD_DIM:(c + 1) * HEAD_DIM] = val.astype(BF16)
    gates_ref[...] = _sigmoid(graw_ref[...])


def _even_post(pa, graw, q_norm, k_norm):
    m = pa.shape[0]
    tm = min(m, ROW_TILE)
    kv_w = 2 * NSA_KV_HEADS * HEAD_DIM
    qw = NSA_HEADS * HEAD_DIM
    row = lambda w: pl.BlockSpec((tm, w), lambda i: (i, 0))
    full = lambda a: pl.BlockSpec(a.shape, lambda i: (0,) * a.ndim)
    qg = q_norm.reshape(1, HEAD_DIM)
    return pl.pallas_call(
        _even_post_kernel,
        grid=(m // tm,),
        in_specs=[row(pa.shape[1]), row(LANES), full(qg), full(k_norm)],
        out_specs=[row(qw), row(kv_w), row(kv_w), row(kv_w), row(kv_w), row(kv_w), row(LANES)],
        out_shape=[jax.ShapeDtypeStruct((m, qw), BF16)] + [jax.ShapeDtypeStruct((m, kv_w), F32)] * 3
        + [jax.ShapeDtypeStruct((m, kv_w), BF16)] * 2 + [jax.ShapeDtypeStruct((m, LANES), F32)],
        compiler_params=_cparams(1),
        name="even_post",
    )(pa, graw, qg, k_norm)


def _gelu_tanh(x):
    return 0.5 * x * (1.0 + jnp.tanh(math.sqrt(2.0 / math.pi) * (x + 0.044715 * (x * x * x))))


NSA_KV_COLS = 2 * NSA_KV_HEADS
CMP_PAGES_PER_STEP = 8
CMP_PAGES_PER_GROUP = 64


def _compress_mlp(x_of, nblk, pos_ref, w1_ref, w2_ref, kg_ref, o_ref, acc_ref):
    acc_ref[...] = jnp.zeros_like(acc_ref)

    def body(i2, carry):
        for c in range(NSA_KV_COLS):
            w = c // NSA_KV_HEADS
            xa = x_of(2 * i2, c) + pos_ref[w, pl.ds(2 * i2, 1), :]
            xb = x_of(2 * i2 + 1, c) + pos_ref[w, pl.ds(2 * i2 + 1, 1), :]
            x = jnp.concatenate([xa, xb], axis=-1).astype(BF16)
            wi = w1_ref[w, pl.ds(pl.multiple_of(i2 * 2 * HEAD_DIM, 2 * HEAD_DIM), 2 * HEAD_DIM), :]
            acc_ref[c] += jnp.dot(x, wi, preferred_element_type=F32)
        return carry

    lax.fori_loop(0, NSA_BLOCK // 2, body, 0)
    for c in range(NSA_KV_COLS):
        w, kh = divmod(c, NSA_KV_HEADS)
        hid = _gelu_tanh(acc_ref[c]).astype(BF16)
        y = jnp.dot(hid, w2_ref[w].astype(BF16), preferred_element_type=F32)
        if w == 0:
            y = _rms_rows(y, kg_ref[0:1, :])
        o_ref[w, :, kh * HEAD_DIM:(kh + 1) * HEAD_DIM] = y


def _compress_rows_kernel(*refs, nblk):
    x_refs = refs[:NSA_KV_COLS]
    pos_ref, w1_ref, w2_ref, kg_ref, o_ref, acc_ref = refs[NSA_KV_COLS:]

    def x_of(i, c):
        return x_refs[c][pl.ds(i, nblk, stride=NSA_BLOCK), :]

    _compress_mlp(x_of, nblk, pos_ref, w1_ref, w2_ref, kg_ref, o_ref, acc_ref)


def _compress_pages_kernel(*refs, pps, nblk):
    tbl_ref = refs[0]
    page_refs = refs[1:1 + pps]
    pos_ref, w1_ref, w2_ref, kg_ref, o_ref, xs_ref, acc_ref = refs[1 + pps:]
    st = pl.program_id(2)
    page_rows = PAGE_ROWS * NSA_KV_COLS
    for p in range(pps):
        xs_ref[pl.ds(pl.multiple_of((st * pps + p) * page_rows, page_rows), page_rows), :] = page_refs[p][...]

    @pl.when(st == pl.num_programs(2) - 1)
    def _():
        def x_of(i, c):
            return xs_ref[pl.ds(i * NSA_KV_COLS + c, nblk, stride=NSA_BLOCK * NSA_KV_COLS), :]

        _compress_mlp(x_of, nblk, pos_ref, w1_ref, w2_ref, kg_ref, o_ref, acc_ref)


def _compress(src, tbl, pos, w1b, w2, k_norm, j):
    weight_specs = lambda nidx: [
        pl.BlockSpec((None, 2, NSA_BLOCK, HEAD_DIM), lambda *a: (j, 0, 0, 0)),
        pl.BlockSpec((None, 2, NSA_BLOCK * HEAD_DIM, NSA_PHI_HIDDEN), lambda *a: (j, 0, 0, 0),
                     pipeline_mode=pl.Buffered(1)),
        pl.BlockSpec((None, 2, NSA_PHI_HIDDEN, HEAD_DIM), lambda *a: (j, 0, 0, 0)),
        pl.BlockSpec(k_norm.shape, lambda *a: (0, 0)),
    ]
    kvw = NSA_KV_HEADS * HEAD_DIM
    if tbl is None:
        bsz, t_len, _ = src.shape
        nblk = t_len // NSA_BLOCK
        return pl.pallas_call(
            functools.partial(_compress_rows_kernel, nblk=nblk),
            grid=(bsz,),
            in_specs=[pl.BlockSpec((None, t_len, HEAD_DIM), lambda b, c=c: (b, 0, c)) for c in range(NSA_KV_COLS)]
            + weight_specs(1),
            out_specs=pl.BlockSpec((2, None, nblk, kvw), lambda b: (0, b, 0, 0)),
            out_shape=jax.ShapeDtypeStruct((2, bsz, nblk, kvw), F32),
            scratch_shapes=[pltpu.VMEM((NSA_KV_COLS, nblk, NSA_PHI_HIDDEN), F32)],
            compiler_params=_cparams(1),
            name="nsa_compress_rows",
        )(*([src] * NSA_KV_COLS), pos, w1b, w2, k_norm)
    bsz, npg = tbl.shape
    pps = math.gcd(CMP_PAGES_PER_STEP, npg)
    ppg = math.gcd(CMP_PAGES_PER_GROUP, npg)
    bpp = PAGE_ROWS // NSA_BLOCK
    nblk = ppg * bpp
    page_rows = PAGE_ROWS * NSA_KV_COLS

    def page_spec(p):
        return pl.BlockSpec((None, page_rows, HEAD_DIM),
                            lambda b, grp, st, t: (t[b, grp * ppg + st * pps + p], 0, 0))

    grid_spec = pltpu.PrefetchScalarGridSpec(
        num_scalar_prefetch=1,
        grid=(bsz, npg // ppg, ppg // pps),
        in_specs=[page_spec(p) for p in range(pps)] + weight_specs(4),
        out_specs=pl.BlockSpec((2, None, nblk, kvw), lambda b, grp, st, t: (0, b, grp, 0)),
        scratch_shapes=[pltpu.VMEM((ppg * page_rows, HEAD_DIM), F32),
                        pltpu.VMEM((NSA_KV_COLS, nblk, NSA_PHI_HIDDEN), F32)],
    )
    return pl.pallas_call(
        functools.partial(_compress_pages_kernel, pps=pps, nblk=nblk),
        grid_spec=grid_spec,
        out_shape=jax.ShapeDtypeStruct((2, bsz, npg * bpp, kvw), F32),
        compiler_params=_cparams(3),
        name="nsa_compress_pages",
    )(tbl, *([src] * pps), pos, w1b, w2, k_norm)


def _nsa_cmp_kernel(q_ref, kc_ref, vc_ref, bias_ref, gates_ref, oc_ref, sel_ref, *, q_start, tq, nb, extra):
    kh = pl.program_id(1)
    qi = pl.program_id(2)
    t_pos = q_start + qi * tq + lax.broadcasted_iota(jnp.int32, (tq, 1), 0)
    blk = lax.broadcasted_iota(jnp.int32, (1, nb), 1)
    valid = t_pos >= blk * NSA_BLOCK + (NSA_BLOCK - 1)
    kc = kc_ref[...].astype(BF16)
    vc = vc_ref[...].astype(BF16)
    gates = gates_ref[...]
    imp = jnp.zeros((tq, nb), F32)
    for g in range(NSA_GROUP):
        qg = q_ref[:, g * HEAD_DIM:(g + 1) * HEAD_DIM]
        s = lax.dot_general(qg, kc, (((1,), (1,)), ((), ())), preferred_element_type=F32) + bias_ref[g]
        s = jnp.where(valid, s, NEG_BIG)
        m = jnp.max(s, axis=-1, keepdims=True)
        p = jnp.where(valid, jnp.exp(s - m), 0.0)
        l = jnp.sum(p, axis=-1, keepdims=True)
        p = p / jnp.where(l > 0, l, 1.0)
        imp = imp + p
        o = jnp.dot(p.astype(BF16), vc, preferred_element_type=F32)
        onehot = lax.broadcasted_iota(jnp.int32, (1, LANES), 1) == (kh * NSA_GROUP + g) * 3
        gcol = jnp.sum(jnp.where(onehot, gates, 0.0), axis=-1, keepdims=True)
        oc_ref[:, g * HEAD_DIM:(g + 1) * HEAD_DIM] = o * gcol
    cur = t_pos // NSA_BLOCK
    forced = (blk == 0) | (blk == cur) | (blk == cur - 1)
    score = jnp.where(forced, NSA_FORCE_SCORE, jnp.where(blk <= cur, imp, -1.0))
    rank = jnp.zeros((tq, nb), jnp.int32)
    for mcol in range(nb):
        sm = score[:, mcol:mcol + 1]
        ahead = (sm > score) | ((sm == score) & (mcol < blk))
        rank = rank + ahead.astype(jnp.int32)
    if not extra:
        sel_ref[...] = (rank < NSA_TOPN).astype(F32)
    else:
        rank = rank + (score < NSA_FORCE_SCORE).astype(jnp.int32)
        rank_extra = jnp.sum((score >= NSA_FORCE_SCORE).astype(jnp.int32), axis=-1, keepdims=True)
        lane = lax.broadcasted_iota(jnp.int32, (tq, NSA_TOPN), 1)
        out = jnp.zeros((tq, NSA_TOPN), jnp.int32)
        for r in range(NSA_TOPN):
            idx_r = jnp.sum(jnp.where(rank == r, blk, 0), axis=-1, keepdims=True)
            idx_r = idx_r + jnp.where(rank_extra == r, nb, 0)
            out = jnp.where(lane == r, idx_r, out)
        sel_ref[...] = out


def _nsa_cmp(q3, kvc, bias_c, gates3, *, q_start, extra):
    bsz, t_len, _ = q3.shape
    nb = kvc.shape[2]
    tq = min(t_len, Q_TILE)
    gw = NSA_GROUP * HEAD_DIM
    if extra:
        assert t_len == 1 and q_start // NSA_BLOCK == nb
        sel_shape = jax.ShapeDtypeStruct((bsz, NSA_KV_HEADS, t_len, NSA_TOPN), jnp.int32)
        sel_spec = pl.BlockSpec((None, None, tq, NSA_TOPN), lambda b, kh, qi: (b, kh, qi, 0))
    else:
        assert (q_start + t_len) == nb * NSA_BLOCK
        sel_shape = jax.ShapeDtypeStruct((bsz, NSA_KV_HEADS, t_len, nb), F32)
        sel_spec = pl.BlockSpec((None, None, tq, nb), lambda b, kh, qi: (b, kh, qi, 0))
    return pl.pallas_call(
        functools.partial(_nsa_cmp_kernel, q_start=q_start, tq=tq, nb=nb, extra=extra),
        grid=(bsz, NSA_KV_HEADS, t_len // tq),
        in_specs=[
            pl.BlockSpec((None, tq, gw), lambda b, kh, qi: (b, qi, kh)),
            pl.BlockSpec((None, None, nb, HEAD_DIM), lambda b, kh, qi: (0, b, 0, kh)),
            pl.BlockSpec((None, None, nb, HEAD_DIM), lambda b, kh, qi: (1, b, 0, kh)),
            pl.BlockSpec((NSA_GROUP, tq, nb), lambda b, kh, qi: (kh, qi, 0)),
            pl.BlockSpec((None, tq, LANES), lambda b, kh, qi: (b, qi, 0)),
        ],
        out_specs=[pl.BlockSpec((None, tq, gw), lambda b, kh, qi: (b, qi, kh)), sel_spec],
        out_shape=[jax.ShapeDtypeStruct((bsz, t_len, NSA_HEADS * HEAD_DIM), F32), sel_shape],
        compiler_params=_cparams(3),
        name="nsa_cmp_attn",
    )(q3, kvc, kvc, bias_c, gates3)


def _lane_column(x, col):
    onehot = lax.broadcasted_iota(jnp.int32, (1, x.shape[1]), 1) == col
    return jnp.sum(jnp.where(onehot, x, 0.0), axis=-1, keepdims=True)


def _row_to_column(row):
    n = row.shape[1]
    eye = lax.broadcasted_iota(jnp.int32, (n, n), 0) == lax.broadcasted_iota(jnp.int32, (n, n), 1)
    return jnp.sum(jnp.where(eye, row, 0.0), axis=-1, keepdims=True)


def _flash_kernel(*refs, kind, hps, shared_kv, tq, blk, gate_branch):
    q_ref, k_ref, v_ref = refs[:3]
    pos = 3
    bias_ref = bm_ref = gates_ref = c_ref = None
    if kind in ("nsa_sel", "nsa_win", "moba"):
        bias_ref = refs[pos]
        pos += 1
    if kind in ("nsa_sel", "moba"):
        bm_ref = refs[pos]
        pos += 1
    if kind in ("nsa_sel", "nsa_win"):
        gates_ref = refs[pos]
        pos += 1
    if kind == "fox":
        c_ref = refs[pos]
        pos += 1
    o_ref, m_ref, l_ref, acc_ref = refs[pos:pos + 4]
    tk = tq
    step = pl.program_id(1)
    qi = pl.program_id(2)
    q0 = pl.multiple_of(qi * tq, tq)
    ii = lax.broadcasted_iota(jnp.int32, (tq, tk), 0)
    jj = lax.broadcasted_iota(jnp.int32, (tq, tk), 1)
    n_back = NSA_WINDOW // tk
    lo = jnp.maximum(qi - n_back, 0) if kind == "nsa_win" else 0
    m_ref[...] = jnp.full(m_ref.shape, NEG_BIG, F32)
    l_ref[...] = jnp.zeros(l_ref.shape, F32)
    acc_ref[...] = jnp.zeros(acc_ref.shape, F32)

    def head_cols(g):
        return slice(g * HEAD_DIM, (g + 1) * HEAD_DIM)

    def tile_step(ki, diag):
        k0 = pl.multiple_of(ki * tk, tk)
        delta = qi - ki
        shared_add = None
        if kind == "nsa_sel":
            bm = bm_ref[...]
            shared_add = (_lane_column(bm, k0 // blk) - 1.0) * (-NEG_BIG)
            for sub in range(1, tk // blk):
                shared_add = jnp.where(jj >= sub * blk, (_lane_column(bm, k0 // blk + sub) - 1.0) * (-NEG_BIG),
                                       shared_add)
        elif kind == "nsa_win" and not diag:
            shared_add = jnp.where(delta == n_back, jnp.where(jj < ii, NEG_BIG, 0.0), 0.0)
        elif kind == "fox" and diag:
            shared_add = jnp.where(jj <= ii, 0.0, NEG_BIG)
        for g in range(hps):
            kv_cols = slice(0, HEAD_DIM) if shared_kv else head_cols(g)
            kt = k_ref[pl.ds(k0, tk), kv_cols]
            vt = v_ref[pl.ds(k0, tk), kv_cols]
            s = lax.dot_general(q_ref[:, head_cols(g)], kt, (((1,), (1,)), ((), ())), preferred_element_type=F32)
            if kind == "fox":
                s = s + (c_ref[g, :, pl.ds(q0, LANES)][:, 0:1] - c_ref[g, :, pl.ds(k0, tk)])
            elif diag:
                s = s + bias_ref[g, 0]
            else:
                near = bias_ref[g, 1]
                s = s + jnp.where(delta == 1, near, near[tq - 1:tq, 0:1])
            if shared_add is not None:
                s = s + shared_add
            if kind == "moba":
                s = s + (_lane_column(bm_ref[g], k0 // blk) - 1.0) * (-NEG_BIG)
            m_old = m_ref[g]
            m_new = jnp.maximum(m_old, jnp.max(s, axis=-1, keepdims=True))
            alpha = jnp.exp(m_old - m_new)
            parts = [jnp.exp(s[:, c * LANES:(c + 1) * LANES] - m_new) for c in range(tk // LANES)]
            l_ref[g] = alpha * l_ref[g] + functools.reduce(lambda a, b: a + b, parts)
            p = jnp.concatenate(parts, axis=-1).astype(BF16)
            acc_ref[g] = alpha * acc_ref[g] + jnp.dot(p, vt, preferred_element_type=F32)
            m_ref[g] = m_new

    def body(ki, carry):
        tile_step(ki, False)
        return carry

    lax.fori_loop(lo, qi, body, 0)
    tile_step(qi, True)
    for g in range(hps):
        o = acc_ref[g] / jnp.sum(l_ref[g], axis=-1, keepdims=True)
        if gates_ref is not None:
            o = o * _lane_column(gates_ref[...], (step * hps + g) * 3 + gate_branch)
        o_ref[:, head_cols(g)] = o.astype(o_ref.dtype)


def _flash(kind, q3, kv3, k_col, v_col, n_kv_heads, *, bias_tiles=None, bm=None, gates3=None, cum=None,
           blk=0, gate_branch=0, out_dtype=F32):
    bsz, t_len, qw = q3.shape
    n_heads = qw // HEAD_DIM
    shared_kv = n_kv_heads < n_heads
    hps = n_heads // n_kv_heads if shared_kv else 2
    tq = FLASH_TILE
    assert t_len % tq == 0 and (shared_kv or (k_col % hps == 0 and v_col % hps == 0))
    gw = hps * HEAD_DIM
    if shared_kv:
        kv_spec = lambda c0: pl.BlockSpec((None, t_len, HEAD_DIM), lambda b, h, qi: (b, 0, c0 + h))
    else:
        kv_spec = lambda c0: pl.BlockSpec((None, t_len, gw), lambda b, h, qi: (b, 0, c0 // hps + h))
    in_specs = [pl.BlockSpec((None, tq, gw), lambda b, h, qi: (b, qi, h)), kv_spec(k_col), kv_spec(v_col)]
    args = [q3, kv3, kv3]
    if bias_tiles is not None:
        in_specs.append(pl.BlockSpec((hps, 2, tq, tq), lambda b, h, qi: (h, 0, 0, 0)))
        args.append(bias_tiles)
    if bm is not None:
        nb = bm.shape[-1]
        if shared_kv:
            in_specs.append(pl.BlockSpec((None, None, tq, nb), lambda b, h, qi: (b, h, qi, 0)))
        else:
            in_specs.append(pl.BlockSpec((None, hps, tq, nb), lambda b, h, qi: (b, h, qi, 0)))
        args.append(bm)
    if gates3 is not None:
        in_specs.append(pl.BlockSpec((None, tq, LANES), lambda b, h, qi: (b, qi, 0)))
        args.append(gates3)
    if cum is not None:
        in_specs.append(pl.BlockSpec((None, hps, 1, t_len), lambda b, h, qi: (b, h, 0, 0)))
        args.append(cum)
    return pl.pallas_call(
        functools.partial(_flash_kernel, kind=kind, hps=hps, shared_kv=shared_kv, tq=tq, blk=blk,
                          gate_branch=gate_branch),
        grid=(bsz, n_heads // hps, t_len // tq),
        in_specs=in_specs,
        out_specs=pl.BlockSpec((None, tq, gw), lambda b, h, qi: (b, qi, h)),
        out_shape=jax.ShapeDtypeStruct((bsz, t_len, qw), out_dtype),
        scratch_shapes=[pltpu.VMEM((hps, tq, LANES), F32), pltpu.VMEM((hps, tq, LANES), F32),
                        pltpu.VMEM((hps, tq, HEAD_DIM), F32)],
        compiler_params=_cparams(3),
        name="flash_" + kind,
    )(*args)


def _sum3_kernel(a_ref, b_ref, c_ref, o_ref):
    o_ref[...] = (a_ref[...] + b_ref[...] + c_ref[...]).astype(o_ref.dtype)


def _sum3_cast(a, b, c):
    m, n = a.shape
    tm = min(m, ROW_TILE)
    spec = pl.BlockSpec((tm, n), lambda i: (i, 0))
    return pl.pallas_call(
        _sum3_kernel, grid=(m // tm,), in_specs=[spec] * 3, out_specs=spec,
        out_shape=jax.ShapeDtypeStruct((m, n), BF16), compiler_params=_cparams(1), name="nsa_sum",
    )(a, b, c)


def _hgrn_kernel(q_ref, z_ref, v_ref, g_ref, lb_ref, on_ref, s0_ref, o_ref, s_ref, st_ref, *, t_len, chunk, hps):
    on = on_ref[...]
    ri = lax.broadcasted_iota(jnp.int32, (chunk, chunk), 0)
    ci = lax.broadcasted_iota(jnp.int32, (chunk, chunk), 1)
    tril = (ci <= ri).astype(F32)
    rows = lax.broadcasted_iota(jnp.int32, (chunk, 1), 0)

    single = t_len < chunk

    def load(ref, r0, cols):
        if single:
            return jnp.broadcast_to(ref[0:1, cols], (chunk, HEAD_DIM))
        return ref[pl.ds(r0, chunk), cols]

    for g in range(hps):
        st_ref[g] = s0_ref[g].T

    def body(c, carry):
        r0 = pl.multiple_of(c * chunk, chunk)
        for g in range(hps):
            cols = slice(g * HEAD_DIM, (g + 1) * HEAD_DIM)
            q = load(q_ref, r0, cols)
            z = load(z_ref, r0, cols)
            v = load(v_ref, r0, cols)
            a_term = lb_ref[g, 0:1, :]
            b_term = lb_ref[g, 1:2, :] + _log_sigmoid(z)
            logf = jnp.maximum(a_term, b_term) + jnp.log1p(jnp.exp(-jnp.abs(a_term - b_term)))
            k = lb_ref[g, 2:3, :] * _sigmoid(-z)
            if single:
                logf = jnp.where(rows < t_len, logf, 0.0)
                k = jnp.where(rows < t_len, k, 0.0)
            cum = jnp.dot(tril, logf, preferred_element_type=F32, precision=lax.Precision.HIGHEST)
            a_last = cum[chunk - 1:chunk, :]
            o = jnp.zeros((chunk, HEAD_DIM), F32)
            for s_row in range(chunk):
                diff = jnp.where(rows >= s_row, cum - cum[s_row:s_row + 1, :], -jnp.inf)
                wgt = q * jnp.exp(diff) * k[s_row:s_row + 1, :]
                o = o + jnp.sum(wgt, axis=-1, keepdims=True) * v[s_row:s_row + 1, :]
            st = st_ref[g]
            qa = (q * jnp.exp(cum)).astype(BF16)
            o = o + lax.dot_general(qa, st.astype(BF16), (((1,), (1,)), ((), ())), preferred_element_type=F32)
            kd = (k * jnp.exp(a_last - cum)).astype(BF16)
            st_ref[g] = jnp.exp(a_last) * st + lax.dot_general(v.astype(BF16), kd, (((0,), (0,)), ((), ())),
                                                               preferred_element_type=F32)
            gate = load(g_ref, r0, cols)
            o = _rms_rows(o, on) * (gate * _sigmoid(gate))
            if single:
                o_ref[:, cols] = o[0:t_len, :].astype(o_ref.dtype)
            else:
                o_ref[pl.ds(r0, chunk), cols] = o.astype(o_ref.dtype)
        return carry

    lax.fori_loop(0, max(t_len // chunk, 1), body, 0)
    for g in range(hps):
        s_ref[g] = st_ref[g].T


HGRN_HEADS_PER_STEP = 4


def _hgrn(ph3, lb_rows, o_norm, s0):
    bsz, t_len, _ = ph3.shape
    chunk = 16
    hps = HGRN_HEADS_PER_STEP
    assert t_len % chunk == 0 or t_len == 1
    gw = hps * HEAD_DIM
    n_steps = HG_HEADS // hps
    col = lambda part: pl.BlockSpec((None, t_len, gw), lambda b, h: (b, 0, part * n_steps + h))
    on = o_norm.reshape(1, HEAD_DIM)
    state_spec = pl.BlockSpec((None, hps, HEAD_DIM, HEAD_DIM), lambda b, h: (b, h, 0, 0))
    return pl.pallas_call(
        functools.partial(_hgrn_kernel, t_len=t_len, chunk=chunk, hps=hps),
        grid=(bsz, n_steps),
        in_specs=[col(0), col(1), col(2), col(3),
                  pl.BlockSpec((hps, 3, HEAD_DIM), lambda b, h: (h, 0, 0)),
                  pl.BlockSpec((1, HEAD_DIM), lambda b, h: (0, 0)),
                  state_spec],
        out_specs=[pl.BlockSpec((None, t_len, gw), lambda b, h: (b, 0, h)), state_spec],
        out_shape=[jax.ShapeDtypeStruct((bsz, t_len, HG_HEADS * HEAD_DIM), BF16),
                   jax.ShapeDtypeStruct((bsz, HG_HEADS, HEAD_DIM, HEAD_DIM), F32)],
        scratch_shapes=[pltpu.VMEM((hps, HEAD_DIM, HEAD_DIM), F32)],
        compiler_params=_cparams(2),
        name="hgrn2",
    )(ph3, ph3, ph3, ph3, lb_rows, on, s0)


def _group_rows(q_ref, group):
    return jnp.concatenate([q_ref[:, g * HEAD_DIM:(g + 1) * HEAD_DIM].astype(F32) for g in range(group)], axis=0)


def _store_group_rows(o_ref, o):
    for g in range(o.shape[0]):
        o_ref[:, g * HEAD_DIM:(g + 1) * HEAD_DIM] = o[g:g + 1, :].astype(o_ref.dtype)


def _nsa_sel_step_kernel(tbl_ref, idx_ref, q_ref, kv_ref, kn_ref, vn_ref, rbt_ref, gates_ref, o_ref,
                         m_ref, l_ref, acc_ref, *, t_pos, nb):
    b = pl.program_id(0)
    kh = pl.program_id(1)
    j = pl.program_id(2)
    q4 = _group_rows(q_ref, NSA_GROUP)
    rbt = rbt_ref[...]
    col = lambda bk: rbt[:, bk:bk + 1]

    @pl.when(j == 0)
    def _():
        m_ref[...] = jnp.sum(q4 * kn_ref[...], axis=-1, keepdims=True) + col(0)
        l_ref[...] = jnp.ones_like(l_ref)
        acc_ref[...] = jnp.broadcast_to(vn_ref[...], acc_ref.shape)

    blk_id = idx_ref[b * NSA_KV_HEADS + kh, j]

    @pl.when(blk_id < nb)
    def _():
        kt = kv_ref[pl.ds(kh, NSA_BLOCK, stride=NSA_KV_COLS), :].astype(BF16)
        vt = kv_ref[pl.ds(NSA_KV_HEADS + kh, NSA_BLOCK, stride=NSA_KV_COLS), :].astype(BF16)
        s = lax.dot_general(q4.astype(BF16), kt, (((1,), (1,)), ((), ())), preferred_element_type=F32)
        dist = t_pos - (blk_id * NSA_BLOCK + lax.broadcasted_iota(jnp.int32, (1, NSA_BLOCK), 1))
        s = s + _bucket_bias(jnp.broadcast_to(dist, s.shape), col)
        m_old = m_ref[...]
        m_new = jnp.maximum(m_old, jnp.max(s, axis=-1, keepdims=True))
        p = jnp.exp(s - m_new)
        alpha = jnp.exp(m_old - m_new)
        l_ref[...] = alpha * l_ref[...] + jnp.sum(p, axis=-1, keepdims=True)
        acc_ref[...] = alpha * acc_ref[...] + jnp.dot(p.astype(BF16), vt, preferred_element_type=F32)
        m_ref[...] = m_new

    @pl.when(j == NSA_TOPN - 1)
    def _():
        o = acc_ref[...] / l_ref[...]
        gates = gates_ref[...]
        gcol = jnp.concatenate([_lane_column(gates, (kh * NSA_GROUP + g) * 3 + 1) for g in range(NSA_GROUP)], axis=0)
        _store_group_rows(o_ref, o * gcol)


def _nsa_sel_step(q3, cache_pages, tbl, idx, sel_new3, rbt, gates3, *, t_pos):
    bsz = q3.shape[0]
    nb = tbl.shape[1] * (PAGE_ROWS // NSA_BLOCK)
    gw = NSA_GROUP * HEAD_DIM
    halves = PAGE_ROWS // NSA_BLOCK

    def cache_map(b, kh, j, tbl_ref, idx_ref):
        blk = jnp.minimum(idx_ref[b * NSA_KV_HEADS + kh, j], nb - 1)
        return (tbl_ref[b, blk // halves], blk % halves, 0)

    grid_spec = pltpu.PrefetchScalarGridSpec(
        num_scalar_prefetch=2,
        grid=(bsz, NSA_KV_HEADS, NSA_TOPN),
        in_specs=[
            pl.BlockSpec((None, 1, gw), lambda b, kh, j, t, i: (b, 0, kh)),
            pl.BlockSpec((None, NSA_BLOCK * NSA_KV_COLS, HEAD_DIM), cache_map),
            pl.BlockSpec((None, 1, HEAD_DIM), lambda b, kh, j, t, i: (b, 0, kh)),
            pl.BlockSpec((None, 1, HEAD_DIM), lambda b, kh, j, t, i: (b, 0, NSA_KV_HEADS + kh)),
            pl.BlockSpec((None, NSA_GROUP, REL_BUCKETS), lambda b, kh, j, t, i: (kh, 0, 0)),
            pl.BlockSpec((None, 1, LANES), lambda b, kh, j, t, i: (b, 0, 0)),
        ],
        out_specs=pl.BlockSpec((None, 1, gw), lambda b, kh, j, t, i: (b, 0, kh)),
        scratch_shapes=[pltpu.VMEM((NSA_GROUP, 1), F32), pltpu.VMEM((NSA_GROUP, 1), F32),
                        pltpu.VMEM((NSA_GROUP, HEAD_DIM), F32)],
    )
    return pl.pallas_call(
        functools.partial(_nsa_sel_step_kernel, t_pos=t_pos, nb=nb),
        grid_spec=grid_spec,
        out_shape=jax.ShapeDtypeStruct((bsz, 1, NSA_HEADS * HEAD_DIM), F32),
        compiler_params=_cparams(3),
        name="nsa_sel_step",
    )(tbl, idx, q3, cache_pages, sel_new3, sel_new3, rbt, gates3)


def _nsa_win_step_kernel(q_ref, k_ref, v_ref, kn_ref, vn_ref, rbt_ref, gates_ref, o_ref, *, pw):
    kh = pl.program_id(1)
    q4 = _group_rows(q_ref, NSA_GROUP)
    rbt = rbt_ref[...]
    col = lambda bk: rbt[:, bk:bk + 1]
    s = lax.dot_general(q4.astype(BF16), k_ref[...].astype(BF16), (((1,), (1,)), ((), ())),
                        preferred_element_type=F32)
    dist = pw - lax.broadcasted_iota(jnp.int32, (NSA_GROUP, pw), 1)
    mask = dist <= NSA_WINDOW
    s = jnp.where(mask, s + _bucket_bias(dist, col), NEG_BIG)
    s_self = jnp.sum(q4 * kn_ref[...], axis=-1, keepdims=True) + col(0)
    m = jnp.maximum(jnp.max(s, axis=-1, keepdims=True), s_self)
    p = jnp.where(mask, jnp.exp(s - m), 0.0)
    p_self = jnp.exp(s_self - m)
    l = jnp.sum(p, axis=-1, keepdims=True) + p_self
    o = jnp.dot(p.astype(BF16), v_ref[...].astype(BF16), preferred_element_type=F32) + p_self * vn_ref[...]
    gates = gates_ref[...]
    gcol = jnp.concatenate([_lane_column(gates, (kh * NSA_GROUP + g) * 3 + 2) for g in range(NSA_GROUP)], axis=0)
    _store_group_rows(o_ref, o / l * gcol)


def _nsa_win_step(q3, win_cache, lead, win_new3, rbt, gates3):
    bsz = q3.shape[0]
    pw = win_cache.shape[1]
    gw = NSA_GROUP * HEAD_DIM
    return pl.pallas_call(
        functools.partial(_nsa_win_step_kernel, pw=pw),
        grid=(bsz, NSA_KV_HEADS),
        in_specs=[
            pl.BlockSpec((None, 1, gw), lambda b, kh: (b, 0, kh)),
            pl.BlockSpec((None, pw, HEAD_DIM), lambda b, kh: (lead + b, 0, kh)),
            pl.BlockSpec((None, pw, HEAD_DIM), lambda b, kh: (lead + b, 0, NSA_KV_HEADS + kh)),
            pl.BlockSpec((None, 1, HEAD_DIM), lambda b, kh: (b, 0, kh)),
            pl.BlockSpec((None, 1, HEAD_DIM), lambda b, kh: (b, 0, NSA_KV_HEADS + kh)),
            pl.BlockSpec((None, NSA_GROUP, REL_BUCKETS), lambda b, kh: (kh, 0, 0)),
            pl.BlockSpec((None, 1, LANES), lambda b, kh: (b, 0, 0)),
        ],
        out_specs=pl.BlockSpec((None, 1, gw), lambda b, kh: (b, 0, kh)),
        out_shape=jax.ShapeDtypeStruct((bsz, 1, NSA_HEADS * HEAD_DIM), F32),
        compiler_params=_cparams(2),
        name="nsa_win_step",
    )(q3, win_cache, win_cache, win_new3, win_new3, rbt, gates3)


def _even_mixer(x2d, h, bsz, t_len, j, w, tabs, past):
    m, d = x2d.shape
    qw = NSA_HEADS * HEAD_DIM
    kv_w = 2 * NSA_KV_HEADS * HEAD_DIM
    n_main = qw + 3 * kv_w
    n_gate = 3 * NSA_HEADS
    w_in = w["w_in_even"]
    pa = _dense([h], w_in, j, 0, n_main, tn=512, name="even_in_attn")
    w_gate = jnp.pad(w_in[j, :, n_main:n_main + n_gate], ((0, 0), (0, LANES - n_gate)))
    graw = _dense([h], w_gate, None, 0, LANES, tn=LANES, name="even_in_gate")
    w_hg = w_in[j, :, n_main + n_gate:]
    ph = _dense([h], w_hg, None, 0, w_hg.shape[1], tn=512, name="even_in_hgrn")
    q, cmp_new, sel_new, win_new, selb, winb, gates = _even_post(pa, graw, w["nsa_q_norm"][j], w["nsa_k_norm"][j])
    q3 = q.reshape(bsz, t_len, qw)
    gates3 = gates.reshape(bsz, t_len, LANES)
    rel_bias = w["rel_bias"]
    k_norm = w["nsa_k_norm"][j]
    if past is None:
        q_start = 0
        src, tbl = cmp_new.reshape(bsz, t_len, kv_w), None
        s0 = jnp.zeros((bsz, HG_HEADS, HEAD_DIM, HEAD_DIM), F32)
    else:
        src, sel_pages, tbl, win_cache, s0 = past
        q_start = tbl.shape[1] * PAGE_ROWS
    kvc = _compress(src, tbl, w["nsa_phi_pos"], tabs["phi_w1"], w["nsa_phi_w2"], k_norm, j)
    nb = kvc.shape[2]
    bias_c = _bias_cmp_table(rel_bias, q_start, t_len, nb)
    o_c, sel = _nsa_cmp(q3, kvc, bias_c, gates3, q_start=q_start, extra=past is not None)
    if past is None:
        selb3 = selb.reshape(bsz, t_len, kv_w)
        winb3 = winb.reshape(bsz, t_len, kv_w)
        o_s = _flash("nsa_sel", q3, selb3, 0, NSA_KV_HEADS, NSA_KV_HEADS, bias_tiles=tabs["tiles"], bm=sel,
                     gates3=gates3, blk=NSA_BLOCK, gate_branch=1)
        o_w = _flash("nsa_win", q3, winb3, 0, NSA_KV_HEADS, NSA_KV_HEADS, bias_tiles=tabs["tiles"],
                     gates3=gates3, gate_branch=2)
        keep = min(NSA_WINDOW, t_len)
        win_buf = win_new.reshape(bsz, t_len, kv_w)[:, t_len - keep:]
    else:
        rbt = rel_bias.T.reshape(NSA_KV_HEADS, NSA_GROUP, REL_BUCKETS)
        sel_new3 = sel_new.reshape(bsz, t_len, kv_w)
        win_new3 = win_new.reshape(bsz, t_len, kv_w)
        idx = sel.reshape(bsz * NSA_KV_HEADS, NSA_TOPN)
        o_s = _nsa_sel_step(q3, sel_pages, tbl, idx, sel_new3, rbt, gates3, t_pos=q_start)
        wc = win_cache.reshape(win_cache.shape[0] * win_cache.shape[1], win_cache.shape[2], kv_w)
        o_w = _nsa_win_step(q3, wc, j * bsz, win_new3, rbt, gates3)
        win_all = jnp.concatenate([wc[j * bsz:(j + 1) * bsz], win_new3], axis=1)
        keep = min(NSA_WINDOW, win_all.shape[1])
        win_buf = win_all[:, win_all.shape[1] - keep:]
    a_nsa = _sum3_cast(o_c.reshape(m, qw), o_s.reshape(m, qw), o_w.reshape(m, qw))
    o_hg, s_new = _hgrn(ph.reshape(bsz, t_len, ph.shape[1]), tabs["lb_rows"][j], w["hg_o_norm"][j], s0)
    x_new = _dense([a_nsa, o_hg.reshape(m, HG_HEADS * HEAD_DIM)], w["w_out_even"], j, 0, d, tn=min(d, 512),
                   mode="res", res=x2d, scale=1.0, name="even_out")
    kv_shape = (bsz, t_len, 2, NSA_KV_HEADS, HEAD_DIM)
    return (x_new, cmp_new.reshape(kv_shape), sel_new.reshape(kv_shape),
            win_buf.reshape(bsz, win_buf.shape[1], 2, NSA_KV_HEADS, HEAD_DIM), s_new)


ODD_HEADS = MOBA_HEADS
ODD_QW = ODD_HEADS * HEAD_DIM
ODD_KVW = 2 * ODD_HEADS * HEAD_DIM


def _odd_post_kernel(pm_ref, fz_ref, mg_ref, fg_ref, fb_ref, qm_ref, moba_ref, mobab_ref, qf_ref, fox_ref,
                     foxb_ref, logf_ref):
    scale = HEAD_DIM ** -0.5
    off = 0
    for g_ref, q_ref, kv_ref, kvb_ref in ((mg_ref, qm_ref, moba_ref, mobab_ref), (fg_ref, qf_ref, fox_ref, foxb_ref)):
        qg = g_ref[0:1, :]
        kg = g_ref[1:2, :]
        for h in range(ODD_HEADS):
            sl = slice(h * HEAD_DIM, (h + 1) * HEAD_DIM)
            q_ref[:, sl] = (_rms_rows(pm_ref[:, off + h * HEAD_DIM: off + (h + 1) * HEAD_DIM], qg) * scale
                            ).astype(q_ref.dtype)
        off += ODD_QW
        for c in range(2 * ODD_HEADS):
            src = pm_ref[:, off + c * HEAD_DIM: off + (c + 1) * HEAD_DIM]
            val = _rms_rows(src, kg) if c < ODD_HEADS else src
            kv_ref[:, c * HEAD_DIM:(c + 1) * HEAD_DIM] = val
            kvb_ref[:, c * HEAD_DIM:(c + 1) * HEAD_DIM] = val.astype(BF16)
        off += ODD_KVW
    logf_ref[...] = _log_sigmoid(fz_ref[...] + fb_ref[...])


def _odd_post(pm, fz, moba_qk, fox_qk, fb):
    m = pm.shape[0]
    tm = min(m, ROW_TILE // 2)
    row = lambda w: pl.BlockSpec((tm, w), lambda i: (i, 0))
    full = lambda a: pl.BlockSpec(a.shape, lambda i: (0,) * a.ndim)
    return pl.pallas_call(
        _odd_post_kernel,
        grid=(m // tm,),
        in_specs=[row(pm.shape[1]), row(LANES), full(moba_qk), full(fox_qk), full(fb)],
        out_specs=[row(ODD_QW), row(ODD_KVW), row(ODD_KVW), row(ODD_QW), row(ODD_KVW), row(ODD_KVW), row(LANES)],
        out_shape=[jax.ShapeDtypeStruct((m, ODD_QW), BF16), jax.ShapeDtypeStruct((m, ODD_KVW), F32),
                   jax.ShapeDtypeStruct((m, ODD_KVW), BF16), jax.ShapeDtypeStruct((m, ODD_QW), BF16),
                   jax.ShapeDtypeStruct((m, ODD_KVW), F32), jax.ShapeDtypeStruct((m, ODD_KVW), BF16),
                   jax.ShapeDtypeStruct((m, LANES), F32)],
        compiler_params=_cparams(1),
        name="odd_post",
    )(pm, fz, moba_qk, fox_qk, fb)


def _topk_rank(score, n):
    idx = lax.broadcasted_iota(jnp.int32, (1, n), 1)
    rank = jnp.zeros(score.shape, jnp.int32)
    for mcol in range(n):
        sm = score[:, mcol:mcol + 1]
        ahead = (sm > score) | ((sm == score) & (mcol < idx))
        rank = rank + ahead.astype(jnp.int32)
    return rank


def _moba_gate_kernel(q_ref, k_ref, bm_ref, *, t_len, nbl):
    k_mean = jnp.mean(k_ref[...].reshape(nbl, MOBA_BLOCK, HEAD_DIM), axis=1)
    gate = lax.dot_general(q_ref[...], k_mean.astype(BF16), (((1,), (1,)), ((), ())), preferred_element_type=F32)
    cur = lax.broadcasted_iota(jnp.int32, (t_len, 1), 0) // MOBA_BLOCK
    blk = lax.broadcasted_iota(jnp.int32, (1, nbl), 1)
    past_ok = blk < cur
    gate = jnp.where(past_ok, gate, -jnp.inf)
    rank = _topk_rank(gate, nbl)
    sel = (rank < MOBA_TOPK) & past_ok & (jnp.abs(gate) < jnp.inf)
    bm_ref[...] = (sel | (blk == cur)).astype(F32)


def _moba_gate(qm3, moba_new3):
    bsz, t_len, _ = qm3.shape
    assert t_len % MOBA_BLOCK == 0
    nbl = t_len // MOBA_BLOCK
    return pl.pallas_call(
        functools.partial(_moba_gate_kernel, t_len=t_len, nbl=nbl),
        grid=(bsz, MOBA_HEADS),
        in_specs=[pl.BlockSpec((None, t_len, HEAD_DIM), lambda b, h: (b, 0, h)),
                  pl.BlockSpec((None, t_len, HEAD_DIM), lambda b, h: (b, 0, h))],
        out_specs=pl.BlockSpec((None, None, t_len, nbl), lambda b, h: (b, h, 0, 0)),
        out_shape=jax.ShapeDtypeStruct((bsz, MOBA_HEADS, t_len, nbl), F32),
        compiler_params=_cparams(2),
        name="moba_gate",
    )(qm3, moba_new3)


def _cumsum_kernel(x_ref, o_ref, *, t_len):
    n = Q_TILE
    upper = (lax.broadcasted_iota(jnp.int32, (n, n), 0) <= lax.broadcasted_iota(jnp.int32, (n, n), 1)).astype(F32)
    carry = jnp.zeros((FOX_HEADS, 1), F32)
    for c in range(t_len // n):
        xt = x_ref[c * n:(c + 1) * n, :].T[0:FOX_HEADS, :]
        cum = jnp.dot(xt, upper, preferred_element_type=F32, precision=lax.Precision.HIGHEST) + carry
        o_ref[:, c * n:(c + 1) * n] = cum
        carry = cum[:, n - 1:n]


def _cumsum_heads(logf3):
    bsz, t_len, _ = logf3.shape
    assert t_len % Q_TILE == 0
    return pl.pallas_call(
        functools.partial(_cumsum_kernel, t_len=t_len),
        grid=(bsz,),
        in_specs=[pl.BlockSpec((None, t_len, LANES), lambda b: (b, 0, 0))],
        out_specs=pl.BlockSpec((None, FOX_HEADS, t_len), lambda b: (b, 0, 0)),
        out_shape=jax.ShapeDtypeStruct((bsz, FOX_HEADS, t_len), F32),
        compiler_params=_cparams(1),
        name="fox_cumsum",
    )(logf3)


MOBA_GATE_PAGES_PER_STEP = 4


def _moba_gate_step_kernel(*refs, pps, nblk, ppb):
    q_ref = refs[1]
    k_refs = refs[2:2 + pps]
    idx_ref, gate_ref = refs[2 + pps:]
    st = pl.program_id(1)

    @pl.when(st == 0)
    def _():
        gate_ref[...] = jnp.zeros_like(gate_ref)

    q8 = _group_rows(q_ref, MOBA_HEADS)
    lane = lax.broadcasted_iota(jnp.int32, (1, nblk), 1)
    for blk_in_step in range(pps // ppb):
        ksum = jnp.zeros((MOBA_HEADS, HEAD_DIM), F32)
        for p in range(ppb):
            ksum = ksum + jnp.sum(k_refs[blk_in_step * ppb + p][...], axis=0)
        gcol = jnp.sum(q8 * (ksum * (1.0 / MOBA_BLOCK)), axis=-1, keepdims=True)
        gate_ref[...] = jnp.where(lane == st * (pps // ppb) + blk_in_step, gcol, gate_ref[...])

    @pl.when(st == pl.num_programs(1) - 1)
    def _():
        gate = gate_ref[...]
        rank = _topk_rank(gate, nblk)
        blk = lax.broadcasted_iota(jnp.int32, (1, nblk), 1)
        lane = lax.broadcasted_iota(jnp.int32, (MOBA_HEADS, MOBA_TOPK), 1)
        out = jnp.full((MOBA_HEADS, MOBA_TOPK), -1, jnp.int32)
        for r in range(min(MOBA_TOPK, nblk)):
            hit = (rank == r) & (jnp.abs(gate) < jnp.inf)
            idx_r = jnp.sum(jnp.where(hit, blk + 1, 0), axis=-1, keepdims=True) - 1
            out = jnp.where(lane == r, idx_r, out)
        idx_ref[...] = out


def _moba_gate_step(qm3, pages4, tbl):
    bsz, npg = tbl.shape
    ppb = MOBA_BLOCK // PAGE_ROWS
    pps = MOBA_GATE_PAGES_PER_STEP
    assert npg % pps == 0 and pps % ppb == 0

    def key_spec(p):
        return pl.BlockSpec((None, PAGE_ROWS, MOBA_HEADS, HEAD_DIM), lambda b, st, t: (t[b, st * pps + p], 0, 0, 0))

    grid_spec = pltpu.PrefetchScalarGridSpec(
        num_scalar_prefetch=1,
        grid=(bsz, npg // pps),
        in_specs=[pl.BlockSpec((None, 1, ODD_QW), lambda b, st, t: (b, 0, 0))] + [key_spec(p) for p in range(pps)],
        out_specs=pl.BlockSpec((None, MOBA_HEADS, MOBA_TOPK), lambda b, st, t: (b, 0, 0)),
        scratch_shapes=[pltpu.VMEM((MOBA_HEADS, npg // ppb), F32)],
    )
    return pl.pallas_call(
        functools.partial(_moba_gate_step_kernel, pps=pps, nblk=npg // ppb, ppb=ppb),
        grid_spec=grid_spec,
        out_shape=jax.ShapeDtypeStruct((bsz, MOBA_HEADS, MOBA_TOPK), jnp.int32),
        compiler_params=_cparams(2),
        name="moba_gate_step",
    )(tbl, qm3, *([pages4] * pps))


def _moba_attn_step_kernel(*refs, t_pos, ppb):
    idx_ref, q_ref = refs[1], refs[2]
    kv_refs = refs[3:3 + ppb]
    kn_ref, vn_ref, rbt_ref, o_ref, m_ref, l_ref, acc_ref = refs[3 + ppb:]
    b = pl.program_id(0)
    h = pl.program_id(1)
    s_id = pl.program_id(2)
    sub = 8
    q = q_ref[...].astype(F32)
    rbt = rbt_ref[...]
    col = lambda bk: rbt[:, bk:bk + 1]

    @pl.when(s_id == 0)
    def _():
        s_self = jnp.sum(q * kn_ref[...], axis=-1, keepdims=True) + col(0)
        m_ref[...] = jnp.broadcast_to(s_self, m_ref.shape)
        l_ref[...] = jnp.ones_like(l_ref)
        acc_ref[...] = jnp.broadcast_to(vn_ref[...], acc_ref.shape)

    blk_id = idx_ref[b * MOBA_HEADS + h, s_id]

    @pl.when(blk_id >= 0)
    def _():
        q8 = jnp.broadcast_to(q, (sub, HEAD_DIM)).astype(BF16)
        for pg in range(ppb):
            kt = kv_refs[pg][pl.ds(h, PAGE_ROWS, stride=2 * MOBA_HEADS), :].astype(BF16)
            vt = kv_refs[pg][pl.ds(MOBA_HEADS + h, PAGE_ROWS, stride=2 * MOBA_HEADS), :].astype(BF16)
            s = lax.dot_general(q8, kt, (((1,), (1,)), ((), ())), preferred_element_type=F32)
            key_pos = blk_id * MOBA_BLOCK + pg * PAGE_ROWS + lax.broadcasted_iota(jnp.int32, (sub, PAGE_ROWS), 1)
            s = s + _bucket_bias(t_pos - key_pos, col)
            m_old = m_ref[...]
            m_new = jnp.maximum(m_old, jnp.max(s, axis=-1, keepdims=True))
            p = jnp.exp(s - m_new)
            alpha = jnp.exp(m_old - m_new)
            l_ref[...] = alpha * l_ref[...] + jnp.sum(p, axis=-1, keepdims=True)
            acc_ref[...] = alpha * acc_ref[...] + jnp.dot(p.astype(BF16), vt, preferred_element_type=F32)
            m_ref[...] = m_new

    @pl.when(s_id == pl.num_programs(2) - 1)
    def _():
        o_ref[...] = (acc_ref[0:1, :] / l_ref[0:1, :]).astype(o_ref.dtype)


def _moba_attn_step(qm3, pages3, tbl, idx, moba_new3, rbt3, *, t_pos):
    bsz = qm3.shape[0]
    ppb = MOBA_BLOCK // PAGE_ROWS

    def cache_map(pg):
        def index_map(b, h, s, tbl_ref, idx_ref):
            blk = jnp.maximum(idx_ref[b * MOBA_HEADS + h, s], 0)
            return (tbl_ref[b, blk * ppb + pg], 0, 0)
        return index_map

    head = lambda off: pl.BlockSpec((None, 1, HEAD_DIM), lambda b, h, s, t, i: (b, 0, off + h))
    grid_spec = pltpu.PrefetchScalarGridSpec(
        num_scalar_prefetch=2,
        grid=(bsz, MOBA_HEADS, MOBA_TOPK),
        in_specs=[head(0)]
        + [pl.BlockSpec((None, PAGE_ROWS * 2 * MOBA_HEADS, HEAD_DIM), cache_map(pg)) for pg in range(ppb)]
        + [head(0), head(MOBA_HEADS),
           pl.BlockSpec((None, 1, REL_BUCKETS), lambda b, h, s, t, i: (h, 0, 0))],
        out_specs=head(0),
        scratch_shapes=[pltpu.VMEM((8, 1), F32), pltpu.VMEM((8, 1), F32), pltpu.VMEM((8, HEAD_DIM), F32)],
    )
    return pl.pallas_call(
        functools.partial(_moba_attn_step_kernel, t_pos=t_pos, ppb=ppb),
        grid_spec=grid_spec,
        out_shape=jax.ShapeDtypeStruct((bsz, 1, ODD_QW), BF16),
        compiler_params=_cparams(3),
        name="moba_attn_step",
    )(tbl, idx, qm3, *([pages3] * ppb), moba_new3, moba_new3, rbt3)


FOX_PAGES_PER_STEP = 4


def _fox_step_kernel(*refs, pps):
    q_ref = refs[1]
    k_refs = refs[2:2 + pps]
    v_refs = refs[2 + pps:2 + 2 * pps]
    lf_refs = refs[2 + 2 * pps:2 + 3 * pps]
    kn_ref, vn_ref, lfn_ref, o_ref, m_ref, l_ref, acc_ref, carry_ref = refs[2 + 3 * pps:]
    st = pl.program_id(1)
    n = PAGE_ROWS
    sub = 8
    q8 = _group_rows(q_ref, FOX_HEADS)
    lane = lax.broadcasted_iota(jnp.int32, (1, LANES), 1)
    lane8 = lax.broadcasted_iota(jnp.int32, (sub, LANES), 1)
    row8 = lax.broadcasted_iota(jnp.int32, (sub, LANES), 0)

    @pl.when(st == 0)
    def _():
        s_self = jnp.sum(q8 * _group_rows(kn_ref, FOX_HEADS), axis=-1, keepdims=True)
        m_ref[...] = jnp.sum(jnp.where(lane8 == row8, s_self, 0.0), axis=0, keepdims=True)
        l_ref[...] = jnp.ones_like(l_ref)
        for h in range(FOX_HEADS):
            vn = vn_ref[:, h * HEAD_DIM:(h + 1) * HEAD_DIM]
            acc_ref[h] = jnp.where(row8 == 0, vn, 0.0)
        carry_ref[...] = lfn_ref[...]

    after = (lax.broadcasted_iota(jnp.int32, (n, n), 1) > lax.broadcasted_iota(jnp.int32, (n, n), 0)).astype(F32)
    for p in range(pps):
        lf = jnp.concatenate([lf_refs[p][...], jnp.zeros((n - FOX_HEADS, n), F32)], axis=0).T
        s = carry_ref[...] + jnp.dot(after, lf, preferred_element_type=F32, precision=lax.Precision.HIGHEST)
        carry_ref[...] += jnp.sum(lf, axis=0, keepdims=True)
        for h in range(FOX_HEADS):
            qk = jnp.sum(k_refs[p][:, h, :] * q8[h:h + 1, :], axis=-1, keepdims=True)
            s = s + jnp.where(lane == h, qk, 0.0)
        m_old = m_ref[...]
        m_new = jnp.maximum(m_old, jnp.max(s, axis=0, keepdims=True))
        prob = jnp.exp(s - m_new)
        alpha = jnp.exp(m_old - m_new)
        l_ref[...] = alpha * l_ref[...] + jnp.sum(prob, axis=0, keepdims=True)
        m_ref[...] = m_new
        for h in range(FOX_HEADS):
            contrib = prob[:, h:h + 1] * v_refs[p][:, h, :]
            part = jnp.sum(contrib.reshape(n // sub, sub, HEAD_DIM), axis=0)
            acc_ref[h] = alpha[:, h:h + 1] * acc_ref[h] + part

    @pl.when(st == pl.num_programs(1) - 1)
    def _():
        l = l_ref[...]
        for h in range(FOX_HEADS):
            o = jnp.sum(acc_ref[h], axis=0, keepdims=True) / l[:, h:h + 1]
            o_ref[:, h * HEAD_DIM:(h + 1) * HEAD_DIM] = o.astype(o_ref.dtype)


def _fox_step(qf3, pages4, logf_pages, tbl, fox_new3, logf_new3):
    bsz, npg = tbl.shape
    pps = math.gcd(FOX_PAGES_PER_STEP, npg)
    page = lambda p: (lambda b, st, t: t[b, npg - 1 - (st * pps + p)])
    kv_spec = lambda p, part: pl.BlockSpec((None, PAGE_ROWS, FOX_HEADS, HEAD_DIM),
                                           lambda b, st, t: (page(p)(b, st, t), 0, part, 0))
    lf_spec = lambda p: pl.BlockSpec((None, FOX_HEADS, PAGE_ROWS), lambda b, st, t: (page(p)(b, st, t), 0, 0))
    new = lambda col: pl.BlockSpec((None, 1, ODD_QW), lambda b, st, t: (b, 0, col))
    grid_spec = pltpu.PrefetchScalarGridSpec(
        num_scalar_prefetch=1,
        grid=(bsz, npg // pps),
        in_specs=[new(0)] + [kv_spec(p, 0) for p in range(pps)] + [kv_spec(p, 1) for p in range(pps)]
        + [lf_spec(p) for p in range(pps)]
        + [new(0), new(1), pl.BlockSpec((None, 1, LANES), lambda b, st, t: (b, 0, 0))],
        out_specs=new(0),
        scratch_shapes=[pltpu.VMEM((1, LANES), F32), pltpu.VMEM((1, LANES), F32),
                        pltpu.VMEM((FOX_HEADS, 8, HEAD_DIM), F32), pltpu.VMEM((1, LANES), F32)],
    )
    return pl.pallas_call(
        functools.partial(_fox_step_kernel, pps=pps),
        grid_spec=grid_spec,
        out_shape=jax.ShapeDtypeStruct((bsz, 1, ODD_QW), BF16),
        compiler_params=_cparams(2),
        name="fox_step",
    )(tbl, qf3, *([pages4] * (2 * pps)), *([logf_pages] * pps), fox_new3, fox_new3, logf_new3)


def _odd_mixer(x2d, h, bsz, t_len, j, w, tabs, past):
    m, d = x2d.shape
    n_main = 2 * (ODD_QW + ODD_KVW)
    w_in = w["w_in_odd"]
    pm = _dense([h], w_in, j, 0, n_main, tn=512, name="odd_in")
    w_fz = jnp.pad(w_in[j, :, n_main:n_main + FOX_HEADS], ((0, 0), (0, LANES - FOX_HEADS)))
    fz = _dense([h], w_fz, None, 0, LANES, tn=LANES, name="odd_in_forget")
    fb = jnp.pad(w["fox_f_bias"][j].astype(F32), (0, LANES - FOX_HEADS)).reshape(1, LANES)
    qm, moba_new, mobab, qf, fox_new, foxb, logf = _odd_post(pm, fz, w["moba_qk_norm"][j], w["fox_qk_norm"][j], fb)
    qm3 = qm.reshape(bsz, t_len, ODD_QW)
    qf3 = qf.reshape(bsz, t_len, ODD_QW)
    logf3 = logf.reshape(bsz, t_len, LANES)
    if past is None:
        bm = _moba_gate(qm3, moba_new.reshape(bsz, t_len, ODD_KVW))
        o_m = _flash("moba", qm3, mobab.reshape(bsz, t_len, ODD_KVW), 0, MOBA_HEADS, MOBA_HEADS,
                     bias_tiles=tabs["tiles"], bm=bm, blk=MOBA_BLOCK, out_dtype=BF16)
        cum = _cumsum_heads(logf3).reshape(bsz, FOX_HEADS, 1, t_len)
        o_f = _flash("fox", qf3, foxb.reshape(bsz, t_len, ODD_KVW), 0, FOX_HEADS, FOX_HEADS, cum=cum,
                     out_dtype=BF16)
    else:
        moba_pages4, fox_pages, logf_pages, tbl = past
        t_pos = tbl.shape[1] * PAGE_ROWS
        idx = _moba_gate_step(qm3, moba_pages4, tbl)
        rbt3 = w["rel_bias"].T.reshape(MOBA_HEADS, 1, REL_BUCKETS)
        moba_pages3 = moba_pages4.reshape(moba_pages4.shape[0], PAGE_ROWS * 2 * MOBA_HEADS, HEAD_DIM)
        o_m = _moba_attn_step(qm3, moba_pages3, tbl, idx.reshape(bsz * MOBA_HEADS, MOBA_TOPK),
                              moba_new.reshape(bsz, t_len, ODD_KVW), rbt3, t_pos=t_pos)
        o_f = _fox_step(qf3, fox_pages, logf_pages, tbl, fox_new.reshape(bsz, t_len, ODD_KVW), logf3)
    x_new = _dense([o_m.reshape(m, ODD_QW), o_f.reshape(m, ODD_QW)], w["w_out_odd"], j, 0, d, tn=min(d, 512),
                   mode="res", res=x2d, scale=1.0, name="odd_out")
    kv_shape = (bsz, t_len, 2, ODD_HEADS, HEAD_DIM)
    return (x_new, moba_new.reshape(kv_shape), fox_new.reshape(kv_shape),
            logf3[:, :, :FOX_HEADS])


def _trunk(x, w, tabs, caches):
    bsz, t_len, d = x.shape
    depth = w["norm_mix"].shape[0]
    x2d = x.reshape(bsz * t_len, d)
    even_new, odd_new = [], []
    for layer in range(depth):
        x2d = _ffn(x2d, w["norm_ffn1"], w["w_ffn1_in"], w["w_ffn1_out"], layer)
        h = _rms_cast(x2d, w["norm_mix"], layer)
        j = layer // 2
        if layer % 2 == 0:
            past = None
            if caches is not None:
                past = (caches["cmp"], caches["sel"], caches["tbl"] + j * caches["n_phys"], caches["win"],
                        caches["hgrn"][j])
            x2d, *new = _even_mixer(x2d, h, bsz, t_len, j, w, tabs, past)
            even_new.append(new)
        else:
            past = None
            if caches is not None:
                past = (caches["moba"], caches["fox"], caches["logf"], caches["tbl"] + j * caches["n_phys"])
            x2d, *new = _odd_mixer(x2d, h, bsz, t_len, j, w, tabs, past)
            odd_new.append(new)
        x2d = _ffn(x2d, w["norm_ffn2"], w["w_ffn2_in"], w["w_ffn2_out"], layer)
    return x2d.reshape(bsz, t_len, d), even_new, odd_new


def kernel(x_prompt, x_sample, cache_nsa_cmp, cache_nsa_sel, cache_moba, cache_fox, cache_fox_logf,
           cache_nsa_win, state_hgrn, page_table, norm_ffn1, w_ffn1_in, w_ffn1_out, norm_mix, norm_ffn2,
           w_ffn2_in, w_ffn2_out, rel_bias, w_in_even, w_out_even, nsa_q_norm, nsa_k_norm, nsa_phi_pos,
           nsa_phi_w1, nsa_phi_w2, hg_lb, hg_o_norm, w_in_odd, w_out_odd, fox_f_bias, moba_qk_norm,
           fox_qk_norm):
    w = dict(norm_ffn1=norm_ffn1, w_ffn1_in=w_ffn1_in, w_ffn1_out=w_ffn1_out, norm_mix=norm_mix,
             norm_ffn2=norm_ffn2, w_ffn2_in=w_ffn2_in, w_ffn2_out=w_ffn2_out, rel_bias=rel_bias,
             w_in_even=w_in_even, w_out_even=w_out_even, nsa_q_norm=nsa_q_norm, nsa_k_norm=nsa_k_norm,
             nsa_phi_pos=nsa_phi_pos, nsa_phi_w1=nsa_phi_w1, nsa_phi_w2=nsa_phi_w2, hg_lb=hg_lb,
             hg_o_norm=hg_o_norm, w_in_odd=w_in_odd, w_out_odd=w_out_odd, fox_f_bias=fox_f_bias,
             moba_qk_norm=moba_qk_norm, fox_qk_norm=fox_qk_norm)
    tabs = _tables(w)
    y_prompt, pe, po = _trunk(x_prompt, w, tabs, None)

    n_phys = cache_nsa_cmp.shape[1]
    n_layers = cache_nsa_cmp.shape[0]
    rows3 = lambda pool: pool.reshape(n_layers * n_phys, -1, HEAD_DIM)
    rows4 = lambda pool: pool.reshape(n_layers * n_phys, PAGE_ROWS, -1, HEAD_DIM)
    logf_t = jnp.swapaxes(cache_fox_logf.reshape(n_layers * n_phys, PAGE_ROWS, FOX_HEADS), 1, 2)
    caches = dict(cmp=rows3(cache_nsa_cmp), sel=rows3(cache_nsa_sel), moba=rows4(cache_moba),
                  fox=rows4(cache_fox), logf=logf_t, win=cache_nsa_win, hgrn=state_hgrn,
                  tbl=page_table.astype(jnp.int32), n_phys=n_phys)
    y_sample, se, so = _trunk(x_sample, w, tabs, caches)

    stack = lambda items, i: jnp.stack([it[i] for it in items])
    return (y_prompt, y_sample,
            stack(pe, 0), stack(pe, 1), stack(pe, 2), stack(pe, 3),
            stack(po, 0), stack(po, 1), stack(po, 2),
            stack(se, 0), stack(se, 1), stack(se, 2), stack(se, 3),
            stack(so, 0), stack(so, 1), stack(so, 2))
```

```python
import functools
import math

import jax
import jax.numpy as jnp
import numpy as np
from jax import lax
from jax.experimental import pallas as pl
from jax.experimental.pallas import tpu as pltpu

F32 = jnp.float32
BF16 = jnp.bfloat16

HEAD_DIM = 128
NSA_HEADS = 8
NSA_KV_HEADS = 2
NSA_GROUP = NSA_HEADS // NSA_KV_HEADS
NSA_BLOCK = 64
NSA_TOPN = 16
NSA_WINDOW = 512
NSA_PHI_HIDDEN = 2 * HEAD_DIM
NSA_FORCE_SCORE = 1.0e4
HG_HEADS = 8
MOBA_HEADS = 8
MOBA_BLOCK = 256
MOBA_TOPK = 3
FOX_HEADS = 8
REL_BUCKETS = 32
REL_MAX_DIST = 128
EPS = 1e-6
PAGE_ROWS = 128

LANES = 128
VMEM_LIMIT_BYTES = 56 * 1024 * 1024
ROW_TILE = 512
DEEP_K = 2048
Q_TILE = 128
FLASH_TILE = 256
FLASH_HEADS_PER_STEP = 4
NEG_BIG = -1e30


def _cparams(n_axes):
    return pltpu.CompilerParams(dimension_semantics=("arbitrary",) * n_axes,
                                vmem_limit_bytes=VMEM_LIMIT_BYTES)


def _rel_bucket(dist):
    n = jnp.maximum(dist, 0)
    exact = REL_BUCKETS // 2
    nf = jnp.maximum(n, 1).astype(F32)
    big = exact + (jnp.log(nf / exact) / math.log(REL_MAX_DIST / exact) * (REL_BUCKETS - exact)).astype(jnp.int32)
    return jnp.where(n < exact, n, jnp.minimum(big, REL_BUCKETS - 1))


def _bucket_bias(dist, table_rows):
    bucket = _rel_bucket(dist)
    out = jnp.zeros(dist.shape, F32) + table_rows(0)
    for b in range(1, REL_BUCKETS):
        out = jnp.where(bucket == b, table_rows(b), out)
    return out


def _rms_rows(x, g):
    return x * lax.rsqrt(jnp.mean(x * x, axis=-1, keepdims=True) + EPS) * g


def _log_sigmoid(z):
    return jnp.minimum(z, 0.0) - jnp.log1p(jnp.exp(-jnp.abs(z)))


def _sigmoid(z):
    return 1.0 / (1.0 + jnp.exp(-z))


def _rms_cast_kernel(x_ref, g_ref, o_ref):
    o_ref[...] = _rms_rows(x_ref[...], g_ref[...]).astype(o_ref.dtype)


def _rms_cast(x2d, g_stack, layer):
    m, d = x2d.shape
    tm = min(m, ROW_TILE)
    g3 = g_stack.reshape(g_stack.shape[0], 1, d)
    return pl.pallas_call(
        _rms_cast_kernel,
        grid=(m // tm,),
        in_specs=[pl.BlockSpec((tm, d), lambda i: (i, 0)),
                  pl.BlockSpec((None, 1, d), lambda i: (layer, 0, 0))],
        out_specs=pl.BlockSpec((tm, d), lambda i: (i, 0)),
        out_shape=jax.ShapeDtypeStruct((m, d), BF16),
        compiler_params=_cparams(1),
        name="rms_cast",
    )(x2d, g3)


def _dense_kernel(*refs, n_a, mode, scale):
    a_refs = refs[:n_a]
    pos = n_a
    w_ref = refs[pos]
    pos += 1
    w2_ref = res_ref = None
    if mode == "swiglu":
        w2_ref = refs[pos]
        pos += 1
    if mode == "res":
        res_ref = refs[pos]
        pos += 1
    o_ref = refs[pos]
    pos += 1
    wb_ref = refs[pos]
    pos += 1
    wb2_ref = refs[pos] if mode == "swiglu" else None

    @pl.when(pl.program_id(1) == 0)
    def _():
        wb_ref[...] = w_ref[...].astype(BF16)
        if mode == "swiglu":
            wb2_ref[...] = w2_ref[...].astype(BF16)

    if n_a == 1:
        a = a_refs[0][...]
    else:
        a = jnp.concatenate([r[...] for r in a_refs], axis=-1)
    y = jnp.dot(a, wb_ref[...], preferred_element_type=F32)
    if mode == "swiglu":
        y2 = jnp.dot(a, wb2_ref[...], preferred_element_type=F32)
        y = y * _sigmoid(y) * y2
    elif mode == "res":
        y = res_ref[...] + scale * y
    o_ref[...] = y.astype(o_ref.dtype)


def _dense(a_parts, w, lead, col0, n_out, *, tn, mode="plain", res=None, scale=1.0,
           out_dtype=F32, col0_b=None, name="dense"):
    m = a_parts[0].shape[0]
    k = sum(a.shape[1] for a in a_parts)
    tm = min(m, 2 * ROW_TILE if k <= DEEP_K else ROW_TILE)
    assert m % tm == 0 and n_out % tn == 0 and col0 % tn == 0
    if w.ndim == 3:
        wblock = (None, k, tn)

        def wmap(off):
            return lambda j, i: (lead, 0, j + off)
    else:
        wblock = (k, tn)

        def wmap(off):
            return lambda j, i: (0, j + off)
    in_specs = [pl.BlockSpec((tm, a.shape[1]), lambda j, i: (i, 0)) for a in a_parts]
    args = list(a_parts)
    in_specs.append(pl.BlockSpec(wblock, wmap(col0 // tn)))
    args.append(w)
    scratch = [pltpu.VMEM((k, tn), BF16)]
    if mode == "swiglu":
        assert col0_b % tn == 0
        in_specs.append(pl.BlockSpec(wblock, wmap(col0_b // tn)))
        args.append(w)
        scratch.append(pltpu.VMEM((k, tn), BF16))
    if mode == "res":
        in_specs.append(pl.BlockSpec((tm, tn), lambda j, i: (i, j)))
        args.append(res)
    return pl.pallas_call(
        functools.partial(_dense_kernel, n_a=len(a_parts), mode=mode, scale=scale),
        grid=(n_out // tn, m // tm),
        in_specs=in_specs,
        out_specs=pl.BlockSpec((tm, tn), lambda j, i: (i, j)),
        out_shape=jax.ShapeDtypeStruct((m, n_out), out_dtype),
        scratch_shapes=scratch,
        compiler_params=_cparams(2),
        name=name,
    )(*args)


def _ffn(x2d, norm_g, w_in, w_out, layer):
    d_ff = w_out.shape[1]
    xn = _rms_cast(x2d, norm_g, layer)
    h = _dense([xn], w_in, layer, 0, d_ff, tn=512, mode="swiglu", col0_b=d_ff,
               out_dtype=BF16, name="ffn_in")
    return _dense([h], w_out, layer, 0, x2d.shape[1], tn=512, mode="res", res=x2d, scale=0.5,
                  name="ffn_out")


def _bias_tiles(rel_bias):
    i = jnp.arange(FLASH_TILE)[:, None]
    j = jnp.arange(FLASH_TILE)[None, :]
    dist = jnp.stack([i - j, FLASH_TILE + i - j])
    onehot = (_rel_bucket(dist)[..., None] == jnp.arange(REL_BUCKETS)).astype(F32)
    tiles = jnp.einsum("ktsb,bh->hkts", onehot, rel_bias.astype(F32), precision=lax.Precision.HIGHEST)
    return jnp.where((dist < 0)[None], NEG_BIG, tiles)


def _tables(w):
    lb_all = jnp.cumsum(jax.nn.softmax(w["hg_lb"].astype(F32), axis=0), axis=0)
    lb_all = lb_all - lb_all[0:1]
    lbh = lb_all.reshape(lb_all.shape[0], HG_HEADS, HEAD_DIM)
    lb_rows = jnp.stack([jnp.log(lbh), jnp.log1p(-lbh), 1.0 - lbh], axis=2)
    return {"tiles": _bias_tiles(w["rel_bias"]), "lb_rows": lb_rows, "phi_w1": w["nsa_phi_w1"].astype(BF16)}


def _bias_cmp_table(rel_bias, q_start, t_len, nb):
    t_pos = q_start + jnp.arange(t_len)
    dist = t_pos[:, None] - (jnp.arange(nb) * NSA_BLOCK + NSA_BLOCK - 1)[None, :]
    onehot = (_rel_bucket(dist)[..., None] == jnp.arange(REL_BUCKETS)).astype(F32)
    return jnp.einsum("tnb,bh->htn", onehot, rel_bias.astype(F32), precision=lax.Precision.HIGHEST)


def _even_post_kernel(pa_ref, graw_ref, qg_ref, kg_ref, q_ref, cmp_ref, sel_ref, win_ref,
                      selb_ref, winb_ref, gates_ref):
    scale = HEAD_DIM ** -0.5
    qg = qg_ref[...]
    for h in range(NSA_HEADS):
        sl = slice(h * HEAD_DIM, (h + 1) * HEAD_DIM)
        q_ref[:, sl] = (_rms_rows(pa_ref[:, sl], qg) * scale).astype(q_ref.dtype)
    base = NSA_HEADS * HEAD_DIM
    kv_w = 2 * NSA_KV_HEADS * HEAD_DIM
    cmp_ref[...] = pa_ref[:, base:base + kv_w]
    for which, (o_ref, ob_ref) in enumerate(((sel_ref, selb_ref), (win_ref, winb_ref))):
        off = base + (which + 1) * kv_w
        kg = kg_ref[which + 1:which + 2, :]
        for c in range(2 * NSA_KV_HEADS):
            src = pa_ref[:, off + c * HEAD_DIM: off + (c + 1) * HEAD_DIM]
            val = _rms_rows(src, kg) if c < NSA_KV_HEADS else src
            o_ref[:, c * HEAD_DIM:(c + 1) * HEAD_DIM] = val
            ob_ref[:, c * HEAD_DIM:(c + 1) * HEAD_DIM] = val.astype(BF16)
    gates_ref[...] = _sigmoid(graw_ref[...])


def _even_post(pa, graw, q_norm, k_norm):
    m = pa.shape[0]
    tm = min(m, ROW_TILE)
    kv_w = 2 * NSA_KV_HEADS * HEAD_DIM
    qw = NSA_HEADS * HEAD_DIM
    row = lambda w: pl.BlockSpec((tm, w), lambda i: (i, 0))
    full = lambda a: pl.BlockSpec(a.shape, lambda i: (0,) * a.ndim)
    qg = q_norm.reshape(1, HEAD_DIM)
    return pl.pallas_call(
        _even_post_kernel,
        grid=(m // tm,),
        in_specs=[row(pa.shape[1]), row(LANES), full(qg), full(k_norm)],
        out_specs=[row(qw), row(kv_w), row(kv_w), row(kv_w), row(kv_w), row(kv_w), row(LANES)],
        out_shape=[jax.ShapeDtypeStruct((m, qw), BF16)] + [jax.ShapeDtypeStruct((m, kv_w), F32)] * 3
        + [jax.ShapeDtypeStruct((m, kv_w), BF16)] * 2 + [jax.ShapeDtypeStruct((m, LANES), F32)],
        compiler_params=_cparams(1),
        name="even_post",
    )(pa, graw, qg, k_norm)


def _gelu_tanh(x):
    return 0.5 * x * (1.0 + jnp.tanh(math.sqrt(2.0 / math.pi) * (x + 0.044715 * (x * x * x))))


NSA_KV_COLS = 2 * NSA_KV_HEADS
CMP_PAGES_PER_STEP = 8
CMP_PAGES_PER_GROUP = 64


def _compress_mlp(x_of, nblk, pos_ref, w1_ref, w2_ref, kg_ref, o_ref, acc_ref):
    acc_ref[...] = jnp.zeros_like(acc_ref)

    def body(i2, carry):
        for c in range(NSA_KV_COLS):
            w = c // NSA_KV_HEADS
            xa = x_of(2 * i2, c) + pos_ref[w, pl.ds(2 * i2, 1), :]
            xb = x_of(2 * i2 + 1, c) + pos_ref[w, pl.ds(2 * i2 + 1, 1), :]
            x = jnp.concatenate([xa, xb], axis=-1).astype(BF16)
            wi = w1_ref[w, pl.ds(pl.multiple_of(i2 * 2 * HEAD_DIM, 2 * HEAD_DIM), 2 * HEAD_DIM), :]
            acc_ref[c] += jnp.dot(x, wi, preferred_element_type=F32)
        return carry

    lax.fori_loop(0, NSA_BLOCK // 2, body, 0)
    for c in range(NSA_KV_COLS):
        w, kh = divmod(c, NSA_KV_HEADS)
        hid = _gelu_tanh(acc_ref[c]).astype(BF16)
        y = jnp.dot(hid, w2_ref[w].astype(BF16), preferred_element_type=F32)
        if w == 0:
            y = _rms_rows(y, kg_ref[0:1, :])
        o_ref[w, :, kh * HEAD_DIM:(kh + 1) * HEAD_DIM] = y


def _compress_rows_kernel(*refs, nblk):
    x_refs = refs[:NSA_KV_COLS]
    pos_ref, w1_ref, w2_ref, kg_ref, o_ref, acc_ref = refs[NSA_KV_COLS:]

    def x_of(i, c):
        return x_refs[c][pl.ds(i, nblk, stride=NSA_BLOCK), :]

    _compress_mlp(x_of, nblk, pos_ref, w1_ref, w2_ref, kg_ref, o_ref, acc_ref)


def _compress_pages_kernel(*refs, pps, nblk):
    tbl_ref = refs[0]
    page_refs = refs[1:1 + pps]
    pos_ref, w1_ref, w2_ref, kg_ref, o_ref, xs_ref, acc_ref = refs[1 + pps:]
    st = pl.program_id(2)
    page_rows = PAGE_ROWS * NSA_KV_COLS
    for p in range(pps):
        xs_ref[pl.ds(pl.multiple_of((st * pps + p) * page_rows, page_rows), page_rows), :] = page_refs[p][...]

    @pl.when(st == pl.num_programs(2) - 1)
    def _():
        def x_of(i, c):
            return xs_ref[pl.ds(i * NSA_KV_COLS + c, nblk, stride=NSA_BLOCK * NSA_KV_COLS), :]

        _compress_mlp(x_of, nblk, pos_ref, w1_ref, w2_ref, kg_ref, o_ref, acc_ref)


def _compress(src, tbl, pos, w1b, w2, k_norm, j):
    weight_specs = lambda nidx: [
        pl.BlockSpec((None, 2, NSA_BLOCK, HEAD_DIM), lambda *a: (j, 0, 0, 0)),
        pl.BlockSpec((None, 2, NSA_BLOCK * HEAD_DIM, NSA_PHI_HIDDEN), lambda *a: (j, 0, 0, 0),
                     pipeline_mode=pl.Buffered(1)),
        pl.BlockSpec((None, 2, NSA_PHI_HIDDEN, HEAD_DIM), lambda *a: (j, 0, 0, 0)),
        pl.BlockSpec(k_norm.shape, lambda *a: (0, 0)),
    ]
    kvw = NSA_KV_HEADS * HEAD_DIM
    if tbl is None:
        bsz, t_len, _ = src.shape
        nblk = t_len // NSA_BLOCK
        return pl.pallas_call(
            functools.partial(_compress_rows_kernel, nblk=nblk),
            grid=(bsz,),
            in_specs=[pl.BlockSpec((None, t_len, HEAD_DIM), lambda b, c=c: (b, 0, c)) for c in range(NSA_KV_COLS)]
            + weight_specs(1),
            out_specs=pl.BlockSpec((2, None, nblk, kvw), lambda b: (0, b, 0, 0)),
            out_shape=jax.ShapeDtypeStruct((2, bsz, nblk, kvw), F32),
            scratch_shapes=[pltpu.VMEM((NSA_KV_COLS, nblk, NSA_PHI_HIDDEN), F32)],
            compiler_params=_cparams(1),
            name="nsa_compress_rows",
        )(*([src] * NSA_KV_COLS), pos, w1b, w2, k_norm)
    bsz, npg = tbl.shape
    pps = math.gcd(CMP_PAGES_PER_STEP, npg)
    ppg = math.gcd(CMP_PAGES_PER_GROUP, npg)
    bpp = PAGE_ROWS // NSA_BLOCK
    nblk = ppg * bpp
    page_rows = PAGE_ROWS * NSA_KV_COLS

    def page_spec(p):
        return pl.BlockSpec((None, page_rows, HEAD_DIM),
                            lambda b, grp, st, t: (t[b, grp * ppg + st * pps + p], 0, 0))

    grid_spec = pltpu.PrefetchScalarGridSpec(
        num_scalar_prefetch=1,
        grid=(bsz, npg // ppg, ppg // pps),
        in_specs=[page_spec(p) for p in range(pps)] + weight_specs(4),
        out_specs=pl.BlockSpec((2, None, nblk, kvw), lambda b, grp, st, t: (0, b, grp, 0)),
        scratch_shapes=[pltpu.VMEM((ppg * page_rows, HEAD_DIM), F32),
                        pltpu.VMEM((NSA_KV_COLS, nblk, NSA_PHI_HIDDEN), F32)],
    )
    return pl.pallas_call(
        functools.partial(_compress_pages_kernel, pps=pps, nblk=nblk),
        grid_spec=grid_spec,
        out_shape=jax.ShapeDtypeStruct((2, bsz, npg * bpp, kvw), F32),
        compiler_params=_cparams(3),
        name="nsa_compress_pages",
    )(tbl, *([src] * pps), pos, w1b, w2, k_norm)


def _nsa_cmp_kernel(q_ref, kc_ref, vc_ref, bias_ref, gates_ref, oc_ref, sel_ref, *, q_start, tq, nb, extra):
    kh = pl.program_id(1)
    qi = pl.program_id(2)
    t_pos = q_start + qi * tq + lax.broadcasted_iota(jnp.int32, (tq, 1), 0)
    blk = lax.broadcasted_iota(jnp.int32, (1, nb), 1)
    valid = t_pos >= blk * NSA_BLOCK + (NSA_BLOCK - 1)
    kc = kc_ref[...].astype(BF16)
    vc = vc_ref[...].astype(BF16)
    gates = gates_ref[...]
    imp = jnp.zeros((tq, nb), F32)
    for g in range(NSA_GROUP):
        qg = q_ref[:, g * HEAD_DIM:(g + 1) * HEAD_DIM]
        s = lax.dot_general(qg, kc, (((1,), (1,)), ((), ())), preferred_element_type=F32) + bias_ref[g]
        s = jnp.where(valid, s, NEG_BIG)
        m = jnp.max(s, axis=-1, keepdims=True)
        p = jnp.where(valid, jnp.exp(s - m), 0.0)
        l = jnp.sum(p, axis=-1, keepdims=True)
        p = p / jnp.where(l > 0, l, 1.0)
        imp = imp + p
        o = jnp.dot(p.astype(BF16), vc, preferred_element_type=F32)
        onehot = lax.broadcasted_iota(jnp.int32, (1, LANES), 1) == (kh * NSA_GROUP + g) * 3
        gcol = jnp.sum(jnp.where(onehot, gates, 0.0), axis=-1, keepdims=True)
        oc_ref[:, g * HEAD_DIM:(g + 1) * HEAD_DIM] = o * gcol
    cur = t_pos // NSA_BLOCK
    forced = (blk == 0) | (blk == cur) | (blk == cur - 1)
    score = jnp.where(forced, NSA_FORCE_SCORE, jnp.where(blk <= cur, imp, -1.0))
    rank = jnp.zeros((tq, nb), jnp.int32)
    for mcol in range(nb):
        sm = score[:, mcol:mcol + 1]
        ahead = (sm > score) | ((sm == score) & (mcol < blk))
        rank = rank + ahead.astype(jnp.int32)
    if not extra:
        sel_ref[...] = (rank < NSA_TOPN).astype(F32)
    else:
        rank = rank + (score < NSA_FORCE_SCORE).astype(jnp.int32)
        rank_extra = jnp.sum((score >= NSA_FORCE_SCORE).astype(jnp.int32), axis=-1, keepdims=True)
        lane = lax.broadcasted_iota(jnp.int32, (tq, NSA_TOPN), 1)
        out = jnp.zeros((tq, NSA_TOPN), jnp.int32)
        for r in range(NSA_TOPN):
            idx_r = jnp.sum(jnp.where(rank == r, blk, 0), axis=-1, keepdims=True)
            idx_r = idx_r + jnp.where(rank_extra == r, nb, 0)
            out = jnp.where(lane == r, idx_r, out)
        sel_ref[...] = out


def _nsa_cmp(q3, kvc, bias_c, gates3, *, q_start, extra):
    bsz, t_len, _ = q3.shape
    nb = kvc.shape[2]
    tq = min(t_len, Q_TILE)
    gw = NSA_GROUP * HEAD_DIM
    if extra:
        assert t_len == 1 and q_start // NSA_BLOCK == nb
        sel_shape = jax.ShapeDtypeStruct((bsz, NSA_KV_HEADS, t_len, NSA_TOPN), jnp.int32)
        sel_spec = pl.BlockSpec((None, None, tq, NSA_TOPN), lambda b, kh, qi: (b, kh, qi, 0))
    else:
        assert (q_start + t_len) == nb * NSA_BLOCK
        sel_shape = jax.ShapeDtypeStruct((bsz, NSA_KV_HEADS, t_len, nb), F32)
        sel_spec = pl.BlockSpec((None, None, tq, nb), lambda b, kh, qi: (b, kh, qi, 0))
    return pl.pallas_call(
        functools.partial(_nsa_cmp_kernel, q_start=q_start, tq=tq, nb=nb, extra=extra),
        grid=(bsz, NSA_KV_HEADS, t_len // tq),
        in_specs=[
            pl.BlockSpec((None, tq, gw), lambda b, kh, qi: (b, qi, kh)),
            pl.BlockSpec((None, None, nb, HEAD_DIM), lambda b, kh, qi: (0, b, 0, kh)),
            pl.BlockSpec((None, None, nb, HEAD_DIM), lambda b, kh, qi: (1, b, 0, kh)),
            pl.BlockSpec((NSA_GROUP, tq, nb), lambda b, kh, qi: (kh, qi, 0)),
            pl.BlockSpec((None, tq, LANES), lambda b, kh, qi: (b, qi, 0)),
        ],
        out_specs=[pl.BlockSpec((None, tq, gw), lambda b, kh, qi: (b, qi, kh)), sel_spec],
        out_shape=[jax.ShapeDtypeStruct((bsz, t_len, NSA_HEADS * HEAD_DIM), F32), sel_shape],
        compiler_params=_cparams(3),
        name="nsa_cmp_attn",
    )(q3, kvc, kvc, bias_c, gates3)


def _lane_column(x, col):
    onehot = lax.broadcasted_iota(jnp.int32, (1, x.shape[1]), 1) == col
    return jnp.sum(jnp.where(onehot, x, 0.0), axis=-1, keepdims=True)


def _row_to_column(row):
    n = row.shape[1]
    eye = lax.broadcasted_iota(jnp.int32, (n, n), 0) == lax.broadcasted_iota(jnp.int32, (n, n), 1)
    return jnp.sum(jnp.where(eye, row, 0.0), axis=-1, keepdims=True)


def _flash_kernel(*refs, kind, hps, shared_kv, tq, blk, gate_branch):
    q_ref, k_ref, v_ref = refs[:3]
    pos = 3
    bias_ref = bm_ref = gates_ref = c_ref = None
    if kind in ("nsa_sel", "nsa_win", "moba"):
        bias_ref = refs[pos]
        pos += 1
    if kind in ("nsa_sel", "moba"):
        bm_ref = refs[pos]
        pos += 1
    if kind in ("nsa_sel", "nsa_win"):
        gates_ref = refs[pos]
        pos += 1
    if kind == "fox":
        c_ref = refs[pos]
        pos += 1
    o_ref, m_ref, l_ref, acc_ref = refs[pos:pos + 4]
    tk = tq
    step = pl.program_id(1)
    qi = pl.program_id(2)
    q0 = pl.multiple_of(qi * tq, tq)
    ii = lax.broadcasted_iota(jnp.int32, (tq, tk), 0)
    jj = lax.broadcasted_iota(jnp.int32, (tq, tk), 1)
    n_back = NSA_WINDOW // tk
    lo = jnp.maximum(qi - n_back, 0) if kind == "nsa_win" else 0
    m_ref[...] = jnp.full(m_ref.shape, NEG_BIG, F32)
    l_ref[...] = jnp.zeros(l_ref.shape, F32)
    acc_ref[...] = jnp.zeros(acc_ref.shape, F32)

    def head_cols(g):
        return slice(g * HEAD_DIM, (g + 1) * HEAD_DIM)

    def tile_step(ki, diag):
        k0 = pl.multiple_of(ki * tk, tk)
        delta = qi - ki
        shared_add = None
        if kind == "nsa_sel":
            bm = bm_ref[...]
            shared_add = (_lane_column(bm, k0 // blk) - 1.0) * (-NEG_BIG)
            for sub in range(1, tk // blk):
                shared_add = jnp.where(jj >= sub * blk, (_lane_column(bm, k0 // blk + sub) - 1.0) * (-NEG_BIG),
                                       shared_add)
        elif kind == "nsa_win" and not diag:
            shared_add = jnp.where(delta == n_back, jnp.where(jj < ii, NEG_BIG, 0.0), 0.0)
        elif kind == "fox" and diag:
            shared_add = jnp.where(jj <= ii, 0.0, NEG_BIG)
        for g in range(hps):
            kv_cols = slice(0, HEAD_DIM) if shared_kv else head_cols(g)
            kt = k_ref[pl.ds(k0, tk), kv_cols]
            vt = v_ref[pl.ds(k0, tk), kv_cols]
            s = lax.dot_general(q_ref[:, head_cols(g)], kt, (((1,), (1,)), ((), ())), preferred_element_type=F32)
            if kind == "fox":
                s = s + (c_ref[g, :, pl.ds(q0, LANES)][:, 0:1] - c_ref[g, :, pl.ds(k0, tk)])
            elif diag:
                s = s + bias_ref[g, 0]
            else:
                near = bias_ref[g, 1]
                s = s + jnp.where(delta == 1, near, near[tq - 1:tq, 0:1])
            if shared_add is not None:
                s = s + shared_add
            if kind == "moba":
                s = s + (_lane_column(bm_ref[g], k0 // blk) - 1.0) * (-NEG_BIG)
            m_old = m_ref[g]
            m_new = jnp.maximum(m_old, jnp.max(s, axis=-1, keepdims=True))
            alpha = jnp.exp(m_old - m_new)
            parts = [jnp.exp(s[:, c * LANES:(c + 1) * LANES] - m_new) for c in range(tk // LANES)]
            l_ref[g] = alpha * l_ref[g] + functools.reduce(lambda a, b: a + b, parts)
            p = jnp.concatenate(parts, axis=-1).astype(BF16)
            acc_ref[g] = alpha * acc_ref[g] + jnp.dot(p, vt, preferred_element_type=F32)
            m_ref[g] = m_new

    def body(ki, carry):
        tile_step(ki, False)
        return carry

    lax.fori_loop(lo, qi, body, 0)
    tile_step(qi, True)
    for g in range(hps):
        o = acc_ref[g] / jnp.sum(l_ref[g], axis=-1, keepdims=True)
        if gates_ref is not None:
            o = o * _lane_column(gates_ref[...], (step * hps + g) * 3 + gate_branch)
        o_ref[:, head_cols(g)] = o.astype(o_ref.dtype)


def _flash(kind, q3, kv3, k_col, v_col, n_kv_heads, *, bias_tiles=None, bm=None, gates3=None, cum=None,
           blk=0, gate_branch=0, out_dtype=F32):
    bsz, t_len, qw = q3.shape
    n_heads = qw // HEAD_DIM
    shared_kv = n_kv_heads < n_heads
    hps = n_heads // n_kv_heads if shared_kv else FLASH_HEADS_PER_STEP
    tq = FLASH_TILE
    assert t_len % tq == 0 and (shared_kv or (k_col % hps == 0 and v_col % hps == 0))
    gw = hps * HEAD_DIM
    if shared_kv:
        kv_spec = lambda c0: pl.BlockSpec((None, t_len, HEAD_DIM), lambda b, h, qi: (b, 0, c0 + h))
    else:
        kv_spec = lambda c0: pl.BlockSpec((None, t_len, gw), lambda b, h, qi: (b, 0, c0 // hps + h))
    in_specs = [pl.BlockSpec((None, tq, gw), lambda b, h, qi: (b, qi, h)), kv_spec(k_col), kv_spec(v_col)]
    args = [q3, kv3, kv3]
    if bias_tiles is not None:
        in_specs.append(pl.BlockSpec((hps, 2, tq, tq), lambda b, h, qi: (h, 0, 0, 0)))
        args.append(bias_tiles)
    if bm is not None:
        nb = bm.shape[-1]
        if shared_kv:
            in_specs.append(pl.BlockSpec((None, None, tq, nb), lambda b, h, qi: (b, h, qi, 0)))
        else:
            in_specs.append(pl.BlockSpec((None, hps, tq, nb), lambda b, h, qi: (b, h, qi, 0)))
        args.append(bm)
    if gates3 is not None:
        in_specs.append(pl.BlockSpec((None, tq, LANES), lambda b, h, qi: (b, qi, 0)))
        args.append(gates3)
    if cum is not None:
        in_specs.append(pl.BlockSpec((None, hps, 1, t_len), lambda b, h, qi: (b, h, 0, 0)))
        args.append(cum)
    return pl.pallas_call(
        functools.partial(_flash_kernel, kind=kind, hps=hps, shared_kv=shared_kv, tq=tq, blk=blk,
                          gate_branch=gate_branch),
        grid=(bsz, n_heads // hps, t_len // tq),
        in_specs=in_specs,
        out_specs=pl.BlockSpec((None, tq, gw), lambda b, h, qi: (b, qi, h)),
        out_shape=jax.ShapeDtypeStruct((bsz, t_len, qw), out_dtype),
        scratch_shapes=[pltpu.VMEM((hps, tq, LANES), F32), pltpu.VMEM((hps, tq, LANES), F32),
                        pltpu.VMEM((hps, tq, HEAD_DIM), F32)],
        compiler_params=_cparams(3),
        name="flash_" + kind,
    )(*args)


def _sum3_kernel(a_ref, b_ref, c_ref, o_ref):
    o_ref[...] = (a_ref[...] + b_ref[...] + c_ref[...]).astype(o_ref.dtype)


def _sum3_cast(a, b, c):
    m, n = a.shape
    tm = min(m, ROW_TILE)
    spec = pl.BlockSpec((tm, n), lambda i: (i, 0))
    return pl.pallas_call(
        _sum3_kernel, grid=(m // tm,), in_specs=[spec] * 3, out_specs=spec,
        out_shape=jax.ShapeDtypeStruct((m, n), BF16), compiler_params=_cparams(1), name="nsa_sum",
    )(a, b, c)


def _hgrn_kernel(q_ref, z_ref, v_ref, g_ref, lb_ref, on_ref, s0_ref, o_ref, s_ref, st_ref, *, t_len, chunk, hps):
    on = on_ref[...]
    ri = lax.broadcasted_iota(jnp.int32, (chunk, chunk), 0)
    ci = lax.broadcasted_iota(jnp.int32, (chunk, chunk), 1)
    tril = (ci <= ri).astype(F32)
    rows = lax.broadcasted_iota(jnp.int32, (chunk, 1), 0)

    single = t_len < chunk

    def load(ref, r0, cols):
        if single:
            return jnp.broadcast_to(ref[0:1, cols], (chunk, HEAD_DIM))
        return ref[pl.ds(r0, chunk), cols]

    for g in range(hps):
        st_ref[g] = s0_ref[g].T

    def body(c, carry):
        r0 = pl.multiple_of(c * chunk, chunk)
        for g in range(hps):
            cols = slice(g * HEAD_DIM, (g + 1) * HEAD_DIM)
            q = load(q_ref, r0, cols)
            z = load(z_ref, r0, cols)
            v = load(v_ref, r0, cols)
            a_term = lb_ref[g, 0:1, :]
            b_term = lb_ref[g, 1:2, :] + _log_sigmoid(z)
            logf = jnp.maximum(a_term, b_term) + jnp.log1p(jnp.exp(-jnp.abs(a_term - b_term)))
            k = lb_ref[g, 2:3, :] * _sigmoid(-z)
            if single:
                logf = jnp.where(rows < t_len, logf, 0.0)
                k = jnp.where(rows < t_len, k, 0.0)
            cum = jnp.dot(tril, logf, preferred_element_type=F32, precision=lax.Precision.HIGHEST)
            a_last = cum[chunk - 1:chunk, :]
            o = jnp.zeros((chunk, HEAD_DIM), F32)
            for s_row in range(chunk):
                diff = jnp.where(rows >= s_row, cum - cum[s_row:s_row + 1, :], -jnp.inf)
                wgt = q * jnp.exp(diff) * k[s_row:s_row + 1, :]
                o = o + jnp.sum(wgt, axis=-1, keepdims=True) * v[s_row:s_row + 1, :]
            st = st_ref[g]
            qa = (q * jnp.exp(cum)).astype(BF16)
            o = o + lax.dot_general(qa, st.astype(BF16), (((1,), (1,)), ((), ())), preferred_element_type=F32)
            kd = (k * jnp.exp(a_last - cum)).astype(BF16)
            st_ref[g] = jnp.exp(a_last) * st + lax.dot_general(v.astype(BF16), kd, (((0,), (0,)), ((), ())),
                                                               preferred_element_type=F32)
            gate = load(g_ref, r0, cols)
            o = _rms_rows(o, on) * (gate * _sigmoid(gate))
            if single:
                o_ref[:, cols] = o[0:t_len, :].astype(o_ref.dtype)
            else:
                o_ref[pl.ds(r0, chunk), cols] = o.astype(o_ref.dtype)
        return carry

    lax.fori_loop(0, max(t_len // chunk, 1), body, 0)
    for g in range(hps):
        s_ref[g] = st_ref[g].T


HGRN_HEADS_PER_STEP = 4


def _hgrn(ph3, lb_rows, o_norm, s0):
    bsz, t_len, _ = ph3.shape
    chunk = 16
    hps = HGRN_HEADS_PER_STEP
    assert t_len % chunk == 0 or t_len == 1
    gw = hps * HEAD_DIM
    n_steps = HG_HEADS // hps
    col = lambda part: pl.BlockSpec((None, t_len, gw), lambda b, h: (b, 0, part * n_steps + h))
    on = o_norm.reshape(1, HEAD_DIM)
    state_spec = pl.BlockSpec((None, hps, HEAD_DIM, HEAD_DIM), lambda b, h: (b, h, 0, 0))
    return pl.pallas_call(
        functools.partial(_hgrn_kernel, t_len=t_len, chunk=chunk, hps=hps),
        grid=(bsz, n_steps),
        in_specs=[col(0), col(1), col(2), col(3),
                  pl.BlockSpec((hps, 3, HEAD_DIM), lambda b, h: (h, 0, 0)),
                  pl.BlockSpec((1, HEAD_DIM), lambda b, h: (0, 0)),
                  state_spec],
        out_specs=[pl.BlockSpec((None, t_len, gw), lambda b, h: (b, 0, h)), state_spec],
        out_shape=[jax.ShapeDtypeStruct((bsz, t_len, HG_HEADS * HEAD_DIM), BF16),
                   jax.ShapeDtypeStruct((bsz, HG_HEADS, HEAD_DIM, HEAD_DIM), F32)],
        scratch_shapes=[pltpu.VMEM((hps, HEAD_DIM, HEAD_DIM), F32)],
        compiler_params=_cparams(2),
        name="hgrn2",
    )(ph3, ph3, ph3, ph3, lb_rows, on, s0)


def _group_rows(q_ref, group):
    return jnp.concatenate([q_ref[:, g * HEAD_DIM:(g + 1) * HEAD_DIM].astype(F32) for g in range(group)], axis=0)


def _store_group_rows(o_ref, o):
    for g in range(o.shape[0]):
        o_ref[:, g * HEAD_DIM:(g + 1) * HEAD_DIM] = o[g:g + 1, :].astype(o_ref.dtype)


NSA_SEL_BLOCKS_PER_STEP = 4


def _nsa_sel_step_kernel(*refs, t_pos, nb, bps):
    idx_ref, q_ref = refs[1], refs[2]
    kv_refs = refs[3:3 + bps]
    kn_ref, vn_ref, rbt_ref, gates_ref, o_ref, m_ref, l_ref, acc_ref = refs[3 + bps:]
    b = pl.program_id(0)
    kh = pl.program_id(1)
    j = pl.program_id(2)
    q4 = _group_rows(q_ref, NSA_GROUP)
    rbt = rbt_ref[...]
    col = lambda bk: rbt[:, bk:bk + 1]

    @pl.when(j == 0)
    def _():
        m_ref[...] = jnp.sum(q4 * kn_ref[...], axis=-1, keepdims=True) + col(0)
        l_ref[...] = jnp.ones_like(l_ref)
        acc_ref[...] = jnp.broadcast_to(vn_ref[...], acc_ref.shape)

    for u in range(bps):
        blk_id = idx_ref[b * NSA_KV_HEADS + kh, j * bps + u]

        @pl.when(blk_id < nb)
        def _(u=u, blk_id=blk_id):
            kt = kv_refs[u][pl.ds(kh, NSA_BLOCK, stride=NSA_KV_COLS), :].astype(BF16)
            vt = kv_refs[u][pl.ds(NSA_KV_HEADS + kh, NSA_BLOCK, stride=NSA_KV_COLS), :].astype(BF16)
            s = lax.dot_general(q4.astype(BF16), kt, (((1,), (1,)), ((), ())), preferred_element_type=F32)
            dist = t_pos - (blk_id * NSA_BLOCK + lax.broadcasted_iota(jnp.int32, (1, NSA_BLOCK), 1))
            s = s + _bucket_bias(jnp.broadcast_to(dist, s.shape), col)
            m_old = m_ref[...]
            m_new = jnp.maximum(m_old, jnp.max(s, axis=-1, keepdims=True))
            p = jnp.exp(s - m_new)
            alpha = jnp.exp(m_old - m_new)
            l_ref[...] = alpha * l_ref[...] + jnp.sum(p, axis=-1, keepdims=True)
            acc_ref[...] = alpha * acc_ref[...] + jnp.dot(p.astype(BF16), vt, preferred_element_type=F32)
            m_ref[...] = m_new

    @pl.when(j == pl.num_programs(2) - 1)
    def _():
        o = acc_ref[...] / l_ref[...]
        gates = gates_ref[...]
        gcol = jnp.concatenate([_lane_column(gates, (kh * NSA_GROUP + g) * 3 + 1) for g in range(NSA_GROUP)], axis=0)
        _store_group_rows(o_ref, o * gcol)


def _nsa_sel_step(q3, cache_pages, tbl, idx, sel_new3, rbt, gates3, *, t_pos):
    bsz = q3.shape[0]
    nb = tbl.shape[1] * (PAGE_ROWS // NSA_BLOCK)
    gw = NSA_GROUP * HEAD_DIM
    halves = PAGE_ROWS // NSA_BLOCK

    bps = NSA_SEL_BLOCKS_PER_STEP
    assert NSA_TOPN % bps == 0

    def cache_map(u):
        def index_map(b, kh, j, tbl_ref, idx_ref):
            blk = jnp.minimum(idx_ref[b * NSA_KV_HEADS + kh, j * bps + u], nb - 1)
            return (tbl_ref[b, blk // halves], blk % halves, 0)
        return index_map

    grid_spec = pltpu.PrefetchScalarGridSpec(
        num_scalar_prefetch=2,
        grid=(bsz, NSA_KV_HEADS, NSA_TOPN // bps),
        in_specs=[pl.BlockSpec((None, 1, gw), lambda b, kh, j, t, i: (b, 0, kh))]
        + [pl.BlockSpec((None, NSA_BLOCK * NSA_KV_COLS, HEAD_DIM), cache_map(u)) for u in range(bps)]
        + [
            pl.BlockSpec((None, 1, HEAD_DIM), lambda b, kh, j, t, i: (b, 0, kh)),
            pl.BlockSpec((None, 1, HEAD_DIM), lambda b, kh, j, t, i: (b, 0, NSA_KV_HEADS + kh)),
            pl.BlockSpec((None, NSA_GROUP, REL_BUCKETS), lambda b, kh, j, t, i: (kh, 0, 0)),
            pl.BlockSpec((None, 1, LANES), lambda b, kh, j, t, i: (b, 0, 0)),
        ],
        out_specs=pl.BlockSpec((None, 1, gw), lambda b, kh, j, t, i: (b, 0, kh)),
        scratch_shapes=[pltpu.VMEM((NSA_GROUP, 1), F32), pltpu.VMEM((NSA_GROUP, 1), F32),
                        pltpu.VMEM((NSA_GROUP, HEAD_DIM), F32)],
    )
    return pl.pallas_call(
        functools.partial(_nsa_sel_step_kernel, t_pos=t_pos, nb=nb, bps=bps),
        grid_spec=grid_spec,
        out_shape=jax.ShapeDtypeStruct((bsz, 1, NSA_HEADS * HEAD_DIM), F32),
        compiler_params=_cparams(3),
        name="nsa_sel_step",
    )(tbl, idx, q3, *([cache_pages] * bps), sel_new3, sel_new3, rbt, gates3)


def _nsa_win_step_kernel(q_ref, k_ref, v_ref, kn_ref, vn_ref, rbt_ref, gates_ref, o_ref, *, pw):
    kh = pl.program_id(1)
    q4 = _group_rows(q_ref, NSA_GROUP)
    rbt = rbt_ref[...]
    col = lambda bk: rbt[:, bk:bk + 1]
    s = lax.dot_general(q4.astype(BF16), k_ref[...].astype(BF16), (((1,), (1,)), ((), ())),
                        preferred_element_type=F32)
    dist = pw - lax.broadcasted_iota(jnp.int32, (NSA_GROUP, pw), 1)
    mask = dist <= NSA_WINDOW
    s = jnp.where(mask, s + _bucket_bias(dist, col), NEG_BIG)
    s_self = jnp.sum(q4 * kn_ref[...], axis=-1, keepdims=True) + col(0)
    m = jnp.maximum(jnp.max(s, axis=-1, keepdims=True), s_self)
    p = jnp.where(mask, jnp.exp(s - m), 0.0)
    p_self = jnp.exp(s_self - m)
    l = jnp.sum(p, axis=-1, keepdims=True) + p_self
    o = jnp.dot(p.astype(BF16), v_ref[...].astype(BF16), preferred_element_type=F32) + p_self * vn_ref[...]
    gates = gates_ref[...]
    gcol = jnp.concatenate([_lane_column(gates, (kh * NSA_GROUP + g) * 3 + 2) for g in range(NSA_GROUP)], axis=0)
    _store_group_rows(o_ref, o / l * gcol)


def _nsa_win_step(q3, win_cache, lead, win_new3, rbt, gates3):
    bsz = q3.shape[0]
    pw = win_cache.shape[1]
    gw = NSA_GROUP * HEAD_DIM
    return pl.pallas_call(
        functools.partial(_nsa_win_step_kernel, pw=pw),
        grid=(bsz, NSA_KV_HEADS),
        in_specs=[
            pl.BlockSpec((None, 1, gw), lambda b, kh: (b, 0, kh)),
            pl.BlockSpec((None, pw, HEAD_DIM), lambda b, kh: (lead + b, 0, kh)),
            pl.BlockSpec((None, pw, HEAD_DIM), lambda b, kh: (lead + b, 0, NSA_KV_HEADS + kh)),
            pl.BlockSpec((None, 1, HEAD_DIM), lambda b, kh: (b, 0, kh)),
            pl.BlockSpec((None, 1, HEAD_DIM), lambda b, kh: (b, 0, NSA_KV_HEADS + kh)),
            pl.BlockSpec((None, NSA_GROUP, REL_BUCKETS), lambda b, kh: (kh, 0, 0)),
            pl.BlockSpec((None, 1, LANES), lambda b, kh: (b, 0, 0)),
        ],
        out_specs=pl.BlockSpec((None, 1, gw), lambda b, kh: (b, 0, kh)),
        out_shape=jax.ShapeDtypeStruct((bsz, 1, NSA_HEADS * HEAD_DIM), F32),
        compiler_params=_cparams(2),
        name="nsa_win_step",
    )(q3, win_cache, win_cache, win_new3, win_new3, rbt, gates3)


def _even_mixer(x2d, h, bsz, t_len, j, w, tabs, past):
    m, d = x2d.shape
    qw = NSA_HEADS * HEAD_DIM
    kv_w = 2 * NSA_KV_HEADS * HEAD_DIM
    n_main = qw + 3 * kv_w
    n_gate = 3 * NSA_HEADS
    w_in = w["w_in_even"]
    pa = _dense([h], w_in, j, 0, n_main, tn=512, name="even_in_attn")
    w_gate = jnp.pad(w_in[j, :, n_main:n_main + n_gate], ((0, 0), (0, LANES - n_gate)))
    graw = _dense([h], w_gate, None, 0, LANES, tn=LANES, name="even_in_gate")
    w_hg = w_in[j, :, n_main + n_gate:]
    ph = _dense([h], w_hg, None, 0, w_hg.shape[1], tn=512, name="even_in_hgrn")
    q, cmp_new, sel_new, win_new, selb, winb, gates = _even_post(pa, graw, w["nsa_q_norm"][j], w["nsa_k_norm"][j])
    q3 = q.reshape(bsz, t_len, qw)
    gates3 = gates.reshape(bsz, t_len, LANES)
    rel_bias = w["rel_bias"]
    k_norm = w["nsa_k_norm"][j]
    if past is None:
        q_start = 0
        src, tbl = cmp_new.reshape(bsz, t_len, kv_w), None
        s0 = jnp.zeros((bsz, HG_HEADS, HEAD_DIM, HEAD_DIM), F32)
    else:
        src, sel_pages, tbl, win_cache, s0 = past
        q_start = tbl.shape[1] * PAGE_ROWS
    kvc = _compress(src, tbl, w["nsa_phi_pos"], tabs["phi_w1"], w["nsa_phi_w2"], k_norm, j)
    nb = kvc.shape[2]
    bias_c = _bias_cmp_table(rel_bias, q_start, t_len, nb)
    o_c, sel = _nsa_cmp(q3, kvc, bias_c, gates3, q_start=q_start, extra=past is not None)
    if past is None:
        selb3 = selb.reshape(bsz, t_len, kv_w)
        winb3 = winb.reshape(bsz, t_len, kv_w)
        o_s = _flash("nsa_sel", q3, selb3, 0, NSA_KV_HEADS, NSA_KV_HEADS, bias_tiles=tabs["tiles"], bm=sel,
                     gates3=gates3, blk=NSA_BLOCK, gate_branch=1)
        o_w = _flash("nsa_win", q3, winb3, 0, NSA_KV_HEADS, NSA_KV_HEADS, bias_tiles=tabs["tiles"],
                     gates3=gates3, gate_branch=2)
        keep = min(NSA_WINDOW, t_len)
        win_buf = win_new.reshape(bsz, t_len, kv_w)[:, t_len - keep:]
    else:
        rbt = rel_bias.T.reshape(NSA_KV_HEADS, NSA_GROUP, REL_BUCKETS)
        sel_new3 = sel_new.reshape(bsz, t_len, kv_w)
        win_new3 = win_new.reshape(bsz, t_len, kv_w)
        idx = sel.reshape(bsz * NSA_KV_HEADS, NSA_TOPN)
        o_s = _nsa_sel_step(q3, sel_pages, tbl, idx, sel_new3, rbt, gates3, t_pos=q_start)
        wc = win_cache.reshape(win_cache.shape[0] * win_cache.shape[1], win_cache.shape[2], kv_w)
        o_w = _nsa_win_step(q3, wc, j * bsz, win_new3, rbt, gates3)
        win_all = jnp.concatenate([wc[j * bsz:(j + 1) * bsz], win_new3], axis=1)
        keep = min(NSA_WINDOW, win_all.shape[1])
        win_buf = win_all[:, win_all.shape[1] - keep:]
    a_nsa = _sum3_cast(o_c.reshape(m, qw), o_s.reshape(m, qw), o_w.reshape(m, qw))
    o_hg, s_new = _hgrn(ph.reshape(bsz, t_len, ph.shape[1]), tabs["lb_rows"][j], w["hg_o_norm"][j], s0)
    x_new = _dense([a_nsa, o_hg.reshape(m, HG_HEADS * HEAD_DIM)], w["w_out_even"], j, 0, d, tn=min(d, 512),
                   mode="res", res=x2d, scale=1.0, name="even_out")
    kv_shape = (bsz, t_len, 2, NSA_KV_HEADS, HEAD_DIM)
    return (x_new, cmp_new.reshape(kv_shape), sel_new.reshape(kv_shape),
            win_buf.reshape(bsz, win_buf.shape[1], 2, NSA_KV_HEADS, HEAD_DIM), s_new)


ODD_HEADS = MOBA_HEADS
ODD_QW = ODD_HEADS * HEAD_DIM
ODD_KVW = 2 * ODD_HEADS * HEAD_DIM


def _odd_post_kernel(pm_ref, fz_ref, mg_ref, fg_ref, fb_ref, qm_ref, moba_ref, mobab_ref, qf_ref, fox_ref,
                     foxb_ref, logf_ref):
    scale = HEAD_DIM ** -0.5
    off = 0
    for g_ref, q_ref, kv_ref, kvb_ref in ((mg_ref, qm_ref, moba_ref, mobab_ref), (fg_ref, qf_ref, fox_ref, foxb_ref)):
        qg = g_ref[0:1, :]
        kg = g_ref[1:2, :]
        for h in range(ODD_HEADS):
            sl = slice(h * HEAD_DIM, (h + 1) * HEAD_DIM)
            q_ref[:, sl] = (_rms_rows(pm_ref[:, off + h * HEAD_DIM: off + (h + 1) * HEAD_DIM], qg) * scale
                            ).astype(q_ref.dtype)
        off += ODD_QW
        for c in range(2 * ODD_HEADS):
            src = pm_ref[:, off + c * HEAD_DIM: off + (c + 1) * HEAD_DIM]
            val = _rms_rows(src, kg) if c < ODD_HEADS else src
            kv_ref[:, c * HEAD_DIM:(c + 1) * HEAD_DIM] = val
            kvb_ref[:, c * HEAD_DIM:(c + 1) * HEAD_DIM] = val.astype(BF16)
        off += ODD_KVW
    logf_ref[...] = _log_sigmoid(fz_ref[...] + fb_ref[...])


def _odd_post(pm, fz, moba_qk, fox_qk, fb):
    m = pm.shape[0]
    tm = min(m, ROW_TILE // 2)
    row = lambda w: pl.BlockSpec((tm, w), lambda i: (i, 0))
    full = lambda a: pl.BlockSpec(a.shape, lambda i: (0,) * a.ndim)
    return pl.pallas_call(
        _odd_post_kernel,
        grid=(m // tm,),
        in_specs=[row(pm.shape[1]), row(LANES), full(moba_qk), full(fox_qk), full(fb)],
        out_specs=[row(ODD_QW), row(ODD_KVW), row(ODD_KVW), row(ODD_QW), row(ODD_KVW), row(ODD_KVW), row(LANES)],
        out_shape=[jax.ShapeDtypeStruct((m, ODD_QW), BF16), jax.ShapeDtypeStruct((m, ODD_KVW), F32),
                   jax.ShapeDtypeStruct((m, ODD_KVW), BF16), jax.ShapeDtypeStruct((m, ODD_QW), BF16),
                   jax.ShapeDtypeStruct((m, ODD_KVW), F32), jax.ShapeDtypeStruct((m, ODD_KVW), BF16),
                   jax.ShapeDtypeStruct((m, LANES), F32)],
        compiler_params=_cparams(1),
        name="odd_post",
    )(pm, fz, moba_qk, fox_qk, fb)


def _topk_rank(score, n):
    idx = lax.broadcasted_iota(jnp.int32, (1, n), 1)
    rank = jnp.zeros(score.shape, jnp.int32)
    for mcol in range(n):
        sm = score[:, mcol:mcol + 1]
        ahead = (sm > score) | ((sm == score) & (mcol < idx))
        rank = rank + ahead.astype(jnp.int32)
    return rank


def _moba_gate_kernel(q_ref, k_ref, bm_ref, *, t_len, nbl):
    k_mean = jnp.mean(k_ref[...].reshape(nbl, MOBA_BLOCK, HEAD_DIM), axis=1)
    gate = lax.dot_general(q_ref[...], k_mean.astype(BF16), (((1,), (1,)), ((), ())), preferred_element_type=F32)
    cur = lax.broadcasted_iota(jnp.int32, (t_len, 1), 0) // MOBA_BLOCK
    blk = lax.broadcasted_iota(jnp.int32, (1, nbl), 1)
    past_ok = blk < cur
    gate = jnp.where(past_ok, gate, -jnp.inf)
    rank = _topk_rank(gate, nbl)
    sel = (rank < MOBA_TOPK) & past_ok & (jnp.abs(gate) < jnp.inf)
    bm_ref[...] = (sel | (blk == cur)).astype(F32)


def _moba_gate(qm3, moba_new3):
    bsz, t_len, _ = qm3.shape
    assert t_len % MOBA_BLOCK == 0
    nbl = t_len // MOBA_BLOCK
    return pl.pallas_call(
        functools.partial(_moba_gate_kernel, t_len=t_len, nbl=nbl),
        grid=(bsz, MOBA_HEADS),
        in_specs=[pl.BlockSpec((None, t_len, HEAD_DIM), lambda b, h: (b, 0, h)),
                  pl.BlockSpec((None, t_len, HEAD_DIM), lambda b, h: (b, 0, h))],
        out_specs=pl.BlockSpec((None, None, t_len, nbl), lambda b, h: (b, h, 0, 0)),
        out_shape=jax.ShapeDtypeStruct((bsz, MOBA_HEADS, t_len, nbl), F32),
        compiler_params=_cparams(2),
        name="moba_gate",
    )(qm3, moba_new3)


def _cumsum_kernel(x_ref, o_ref, *, t_len):
    n = Q_TILE
    upper = (lax.broadcasted_iota(jnp.int32, (n, n), 0) <= lax.broadcasted_iota(jnp.int32, (n, n), 1)).astype(F32)
    carry = jnp.zeros((FOX_HEADS, 1), F32)
    for c in range(t_len // n):
        xt = x_ref[c * n:(c + 1) * n, :].T[0:FOX_HEADS, :]
        cum = jnp.dot(xt, upper, preferred_element_type=F32, precision=lax.Precision.HIGHEST) + carry
        o_ref[:, c * n:(c + 1) * n] = cum
        carry = cum[:, n - 1:n]


def _cumsum_heads(logf3):
    bsz, t_len, _ = logf3.shape
    assert t_len % Q_TILE == 0
    return pl.pallas_call(
        functools.partial(_cumsum_kernel, t_len=t_len),
        grid=(bsz,),
        in_specs=[pl.BlockSpec((None, t_len, LANES), lambda b: (b, 0, 0))],
        out_specs=pl.BlockSpec((None, FOX_HEADS, t_len), lambda b: (b, 0, 0)),
        out_shape=jax.ShapeDtypeStruct((bsz, FOX_HEADS, t_len), F32),
        compiler_params=_cparams(1),
        name="fox_cumsum",
    )(logf3)


MOBA_GATE_PAGES_PER_STEP = 8


def _moba_gate_step_kernel(*refs, pps, nblk, ppb):
    q_ref = refs[1]
    k_refs = refs[2:2 + pps]
    idx_ref, gate_ref = refs[2 + pps:]
    st = pl.program_id(1)

    @pl.when(st == 0)
    def _():
        gate_ref[...] = jnp.zeros_like(gate_ref)

    q8 = _group_rows(q_ref, MOBA_HEADS)
    lane = lax.broadcasted_iota(jnp.int32, (1, nblk), 1)
    for blk_in_step in range(pps // ppb):
        ksum = jnp.zeros((MOBA_HEADS, HEAD_DIM), F32)
        for p in range(ppb):
            ksum = ksum + jnp.sum(k_refs[blk_in_step * ppb + p][...], axis=0)
        gcol = jnp.sum(q8 * (ksum * (1.0 / MOBA_BLOCK)), axis=-1, keepdims=True)
        gate_ref[...] = jnp.where(lane == st * (pps // ppb) + blk_in_step, gcol, gate_ref[...])

    @pl.when(st == pl.num_programs(1) - 1)
    def _():
        gate = gate_ref[...]
        rank = _topk_rank(gate, nblk)
        blk = lax.broadcasted_iota(jnp.int32, (1, nblk), 1)
        lane = lax.broadcasted_iota(jnp.int32, (MOBA_HEADS, MOBA_TOPK), 1)
        out = jnp.full((MOBA_HEADS, MOBA_TOPK), -1, jnp.int32)
        for r in range(min(MOBA_TOPK, nblk)):
            hit = (rank == r) & (jnp.abs(gate) < jnp.inf)
            idx_r = jnp.sum(jnp.where(hit, blk + 1, 0), axis=-1, keepdims=True) - 1
            out = jnp.where(lane == r, idx_r, out)
        idx_ref[...] = out


def _moba_gate_step(qm3, pages4, tbl):
    bsz, npg = tbl.shape
    ppb = MOBA_BLOCK // PAGE_ROWS
    pps = MOBA_GATE_PAGES_PER_STEP
    assert npg % pps == 0 and pps % ppb == 0

    def key_spec(p):
        return pl.BlockSpec((None, PAGE_ROWS, MOBA_HEADS, HEAD_DIM), lambda b, st, t: (t[b, st * pps + p], 0, 0, 0))

    grid_spec = pltpu.PrefetchScalarGridSpec(
        num_scalar_prefetch=1,
        grid=(bsz, npg // pps),
        in_specs=[pl.BlockSpec((None, 1, ODD_QW), lambda b, st, t: (b, 0, 0))] + [key_spec(p) for p in range(pps)],
        out_specs=pl.BlockSpec((None, MOBA_HEADS, MOBA_TOPK), lambda b, st, t: (b, 0, 0)),
        scratch_shapes=[pltpu.VMEM((MOBA_HEADS, npg // ppb), F32)],
    )
    return pl.pallas_call(
        functools.partial(_moba_gate_step_kernel, pps=pps, nblk=npg // ppb, ppb=ppb),
        grid_spec=grid_spec,
        out_shape=jax.ShapeDtypeStruct((bsz, MOBA_HEADS, MOBA_TOPK), jnp.int32),
        compiler_params=_cparams(2),
        name="moba_gate_step",
    )(tbl, qm3, *([pages4] * pps))


def _moba_attn_step_kernel(*refs, t_pos, ppb):
    idx_ref, q_ref = refs[1], refs[2]
    kv_refs = refs[3:3 + ppb]
    kn_ref, vn_ref, rbt_ref, o_ref, m_ref, l_ref, acc_ref = refs[3 + ppb:]
    b = pl.program_id(0)
    h = pl.program_id(1)
    s_id = pl.program_id(2)
    sub = 8
    q = q_ref[...].astype(F32)
    rbt = rbt_ref[...]
    col = lambda bk: rbt[:, bk:bk + 1]

    @pl.when(s_id == 0)
    def _():
        s_self = jnp.sum(q * kn_ref[...], axis=-1, keepdims=True) + col(0)
        m_ref[...] = jnp.broadcast_to(s_self, m_ref.shape)
        l_ref[...] = jnp.ones_like(l_ref)
        acc_ref[...] = jnp.broadcast_to(vn_ref[...], acc_ref.shape)

    blk_id = idx_ref[b * MOBA_HEADS + h, s_id]

    @pl.when(blk_id >= 0)
    def _():
        q8 = jnp.broadcast_to(q, (sub, HEAD_DIM)).astype(BF16)
        for pg in range(ppb):
            kt = kv_refs[pg][pl.ds(h, PAGE_ROWS, stride=2 * MOBA_HEADS), :].astype(BF16)
            vt = kv_refs[pg][pl.ds(MOBA_HEADS + h, PAGE_ROWS, stride=2 * MOBA_HEADS), :].astype(BF16)
            s = lax.dot_general(q8, kt, (((1,), (1,)), ((), ())), preferred_element_type=F32)
            key_pos = blk_id * MOBA_BLOCK + pg * PAGE_ROWS + lax.broadcasted_iota(jnp.int32, (sub, PAGE_ROWS), 1)
            s = s + _bucket_bias(t_pos - key_pos, col)
            m_old = m_ref[...]
            m_new = jnp.maximum(m_old, jnp.max(s, axis=-1, keepdims=True))
            p = jnp.exp(s - m_new)
            alpha = jnp.exp(m_old - m_new)
            l_ref[...] = alpha * l_ref[...] + jnp.sum(p, axis=-1, keepdims=True)
            acc_ref[...] = alpha * acc_ref[...] + jnp.dot(p.astype(BF16), vt, preferred_element_type=F32)
            m_ref[...] = m_new

    @pl.when(s_id == pl.num_programs(2) - 1)
    def _():
        o_ref[...] = (acc_ref[0:1, :] / l_ref[0:1, :]).astype(o_ref.dtype)


def _moba_attn_step(qm3, pages3, tbl, idx, moba_new3, rbt3, *, t_pos):
    bsz = qm3.shape[0]
    ppb = MOBA_BLOCK // PAGE_ROWS

    def cache_map(pg):
        def index_map(b, h, s, tbl_ref, idx_ref):
            blk = jnp.maximum(idx_ref[b * MOBA_HEADS + h, s], 0)
            return (tbl_ref[b, blk * ppb + pg], 0, 0)
        return index_map

    head = lambda off: pl.BlockSpec((None, 1, HEAD_DIM), lambda b, h, s, t, i: (b, 0, off + h))
    grid_spec = pltpu.PrefetchScalarGridSpec(
        num_scalar_prefetch=2,
        grid=(bsz, MOBA_HEADS, MOBA_TOPK),
        in_specs=[head(0)]
        + [pl.BlockSpec((None, PAGE_ROWS * 2 * MOBA_HEADS, HEAD_DIM), cache_map(pg)) for pg in range(ppb)]
        + [head(0), head(MOBA_HEADS),
           pl.BlockSpec((None, 1, REL_BUCKETS), lambda b, h, s, t, i: (h, 0, 0))],
        out_specs=head(0),
        scratch_shapes=[pltpu.VMEM((8, 1), F32), pltpu.VMEM((8, 1), F32), pltpu.VMEM((8, HEAD_DIM), F32)],
    )
    return pl.pallas_call(
        functools.partial(_moba_attn_step_kernel, t_pos=t_pos, ppb=ppb),
        grid_spec=grid_spec,
        out_shape=jax.ShapeDtypeStruct((bsz, 1, ODD_QW), BF16),
        compiler_params=_cparams(3),
        name="moba_attn_step",
    )(tbl, idx, qm3, *([pages3] * ppb), moba_new3, moba_new3, rbt3)


FOX_PAGES_PER_STEP = 4


def _fox_step_kernel(*refs, pps):
    q_ref = refs[1]
    k_refs = refs[2:2 + pps]
    v_refs = refs[2 + pps:2 + 2 * pps]
    lf_refs = refs[2 + 2 * pps:2 + 3 * pps]
    kn_ref, vn_ref, lfn_ref, o_ref, m_ref, l_ref, acc_ref, carry_ref = refs[2 + 3 * pps:]
    st = pl.program_id(1)
    n = PAGE_ROWS
    nh = FOX_HEADS
    q8 = _group_rows(q_ref, nh)
    lane8 = lax.broadcasted_iota(jnp.int32, (nh, LANES), 1)
    row8 = lax.broadcasted_iota(jnp.int32, (nh, LANES), 0)

    @pl.when(st == 0)
    def _():
        m_ref[...] = jnp.sum(q8 * _group_rows(kn_ref, nh), axis=-1, keepdims=True)
        l_ref[...] = jnp.ones_like(l_ref)
        acc_ref[...] = _group_rows(vn_ref, nh)
        carry_ref[...] = jnp.sum(jnp.where(lane8 == row8, lfn_ref[...], 0.0), axis=-1, keepdims=True)

    ones = jnp.ones((HEAD_DIM, LANES), BF16)
    after_t = (lax.broadcasted_iota(jnp.int32, (n, n), 0) > lax.broadcasted_iota(jnp.int32, (n, n), 1)).astype(F32)
    on_diag = (lax.broadcasted_iota(jnp.int32, (n, nh, LANES), 0)
               == lax.broadcasted_iota(jnp.int32, (n, nh, LANES), 2))
    for p in range(pps):
        lf = lf_refs[p][...]
        bias = carry_ref[...] + jnp.dot(lf, after_t, preferred_element_type=F32, precision=lax.Precision.HIGHEST)
        carry_ref[...] += jnp.sum(lf, axis=-1, keepdims=True)
        prod = (k_refs[p][...] * q8[None]).reshape(n * nh, HEAD_DIM).astype(BF16)
        qk_rep = jnp.dot(prod, ones, preferred_element_type=F32).reshape(n, nh, LANES)
        s = jnp.sum(jnp.where(on_diag, qk_rep, 0.0), axis=0) + bias
        m_old = m_ref[...]
        m_new = jnp.maximum(m_old, jnp.max(s, axis=-1, keepdims=True))
        prob = jnp.exp(s - m_new)
        alpha = jnp.exp(m_old - m_new)
        l_ref[...] = alpha * l_ref[...] + jnp.sum(prob, axis=-1, keepdims=True)
        m_ref[...] = m_new
        spread = jnp.where(on_diag, prob[None], 0.0).reshape(n * nh, LANES).astype(BF16)
        p_rep = jnp.dot(spread, ones, preferred_element_type=F32).reshape(n, nh, HEAD_DIM)
        acc_ref[...] = alpha * acc_ref[...] + jnp.sum(p_rep * v_refs[p][...], axis=0)

    @pl.when(st == pl.num_programs(1) - 1)
    def _():
        _store_group_rows(o_ref, acc_ref[...] / l_ref[...])


def _fox_step(qf3, pages4, logf_pages, tbl, fox_new3, logf_new3):
    bsz, npg = tbl.shape
    pps = math.gcd(FOX_PAGES_PER_STEP, npg)
    page = lambda p: (lambda b, st, t: t[b, npg - 1 - (st * pps + p)])
    kv_spec = lambda p, part: pl.BlockSpec((None, PAGE_ROWS, FOX_HEADS, HEAD_DIM),
                                           lambda b, st, t: (page(p)(b, st, t), 0, part, 0))
    lf_spec = lambda p: pl.BlockSpec((None, FOX_HEADS, PAGE_ROWS), lambda b, st, t: (page(p)(b, st, t), 0, 0))
    new = lambda col: pl.BlockSpec((None, 1, ODD_QW), lambda b, st, t: (b, 0, col))
    grid_spec = pltpu.PrefetchScalarGridSpec(
        num_scalar_prefetch=1,
        grid=(bsz, npg // pps),
        in_specs=[new(0)] + [kv_spec(p, 0) for p in range(pps)] + [kv_spec(p, 1) for p in range(pps)]
        + [lf_spec(p) for p in range(pps)]
        + [new(0), new(1), pl.BlockSpec((None, 1, LANES), lambda b, st, t: (b, 0, 0))],
        out_specs=new(0),
        scratch_shapes=[pltpu.VMEM((FOX_HEADS, 1), F32), pltpu.VMEM((FOX_HEADS, 1), F32),
                        pltpu.VMEM((FOX_HEADS, HEAD_DIM), F32), pltpu.VMEM((FOX_HEADS, 1), F32)],
    )
    return pl.pallas_call(
        functools.partial(_fox_step_kernel, pps=pps),
        grid_spec=grid_spec,
        out_shape=jax.ShapeDtypeStruct((bsz, 1, ODD_QW), BF16),
        compiler_params=_cparams(2),
        name="fox_step",
    )(tbl, qf3, *([pages4] * (2 * pps)), *([logf_pages] * pps), fox_new3, fox_new3, logf_new3)


def _odd_mixer(x2d, h, bsz, t_len, j, w, tabs, past):
    m, d = x2d.shape
    n_main = 2 * (ODD_QW + ODD_KVW)
    w_in = w["w_in_odd"]
    pm = _dense([h], w_in, j, 0, n_main, tn=512, name="odd_in")
    w_fz = jnp.pad(w_in[j, :, n_main:n_main + FOX_HEADS], ((0, 0), (0, LANES - FOX_HEADS)))
    fz = _dense([h], w_fz, None, 0, LANES, tn=LANES, name="odd_in_forget")
    fb = jnp.pad(w["fox_f_bias"][j].astype(F32), (0, LANES - FOX_HEADS)).reshape(1, LANES)
    qm, moba_new, mobab, qf, fox_new, foxb, logf = _odd_post(pm, fz, w["moba_qk_norm"][j], w["fox_qk_norm"][j], fb)
    qm3 = qm.reshape(bsz, t_len, ODD_QW)
    qf3 = qf.reshape(bsz, t_len, ODD_QW)
    logf3 = logf.reshape(bsz, t_len, LANES)
    if past is None:
        bm = _moba_gate(qm3, moba_new.reshape(bsz, t_len, ODD_KVW))
        o_m = _flash("moba", qm3, mobab.reshape(bsz, t_len, ODD_KVW), 0, MOBA_HEADS, MOBA_HEADS,
                     bias_tiles=tabs["tiles"], bm=bm, blk=MOBA_BLOCK, out_dtype=BF16)
        cum = _cumsum_heads(logf3).reshape(bsz, FOX_HEADS, 1, t_len)
        o_f = _flash("fox", qf3, foxb.reshape(bsz, t_len, ODD_KVW), 0, FOX_HEADS, FOX_HEADS, cum=cum,
                     out_dtype=BF16)
    else:
        moba_pages4, fox_pages, logf_pages, tbl = past
        t_pos = tbl.shape[1] * PAGE_ROWS
        idx = _moba_gate_step(qm3, moba_pages4, tbl)
        rbt3 = w["rel_bias"].T.reshape(MOBA_HEADS, 1, REL_BUCKETS)
        moba_pages3 = moba_pages4.reshape(moba_pages4.shape[0], PAGE_ROWS * 2 * MOBA_HEADS, HEAD_DIM)
        o_m = _moba_attn_step(qm3, moba_pages3, tbl, idx.reshape(bsz * MOBA_HEADS, MOBA_TOPK),
                              moba_new.reshape(bsz, t_len, ODD_KVW), rbt3, t_pos=t_pos)
        o_f = _fox_step(qf3, fox_pages, logf_pages, tbl, fox_new.reshape(bsz, t_len, ODD_KVW), logf3)
    x_new = _dense([o_m.reshape(m, ODD_QW), o_f.reshape(m, ODD_QW)], w["w_out_odd"], j, 0, d, tn=min(d, 512),
                   mode="res", res=x2d, scale=1.0, name="odd_out")
    kv_shape = (bsz, t_len, 2, ODD_HEADS, HEAD_DIM)
    return (x_new, moba_new.reshape(kv_shape), fox_new.reshape(kv_shape),
            logf3[:, :, :FOX_HEADS])


def _trunk(x, w, tabs, caches):
    bsz, t_len, d = x.shape
    depth = w["norm_mix"].shape[0]
    x2d = x.reshape(bsz * t_len, d)
    even_new, odd_new = [], []
    for layer in range(depth):
        x2d = _ffn(x2d, w["norm_ffn1"], w["w_ffn1_in"], w["w_ffn1_out"], layer)
        h = _rms_cast(x2d, w["norm_mix"], layer)
        j = layer // 2
        if layer % 2 == 0:
            past = None
            if caches is not None:
                past = (caches["cmp"], caches["sel"], caches["tbl"] + j * caches["n_phys"], caches["win"],
                        caches["hgrn"][j])
            x2d, *new = _even_mixer(x2d, h, bsz, t_len, j, w, tabs, past)
            even_new.append(new)
        else:
            past = None
            if caches is not None:
                past = (caches["moba"], caches["fox"], caches["logf"], caches["tbl"] + j * caches["n_phys"])
            x2d, *new = _odd_mixer(x2d, h, bsz, t_len, j, w, tabs, past)
            odd_new.append(new)
        x2d = _ffn(x2d, w["norm_ffn2"], w["w_ffn2_in"], w["w_ffn2_out"], layer)
    return x2d.reshape(bsz, t_len, d), even_new, odd_new


def kernel(x_prompt, x_sample, cache_nsa_cmp, cache_nsa_sel, cache_moba, cache_fox, cache_fox_logf,
           cache_nsa_win, state_hgrn, page_table, norm_ffn1, w_ffn1_in, w_ffn1_out, norm_mix, norm_ffn2,
           w_ffn2_in, w_ffn2_out, rel_bias, w_in_even, w_out_even, nsa_q_norm, nsa_k_norm, nsa_phi_pos,
           nsa_phi_w1, nsa_phi_w2, hg_lb, hg_o_norm, w_in_odd, w_out_odd, fox_f_bias, moba_qk_norm,
           fox_qk_norm):
    w = dict(norm_ffn1=norm_ffn1, w_ffn1_in=w_ffn1_in, w_ffn1_out=w_ffn1_out, norm_mix=norm_mix,
             norm_ffn2=norm_ffn2, w_ffn2_in=w_ffn2_in, w_ffn2_out=w_ffn2_out, rel_bias=rel_bias,
             w_in_even=w_in_even, w_out_even=w_out_even, nsa_q_norm=nsa_q_norm, nsa_k_norm=nsa_k_norm,
             nsa_phi_pos=nsa_phi_pos, nsa_phi_w1=nsa_phi_w1, nsa_phi_w2=nsa_phi_w2, hg_lb=hg_lb,
             hg_o_norm=hg_o_norm, w_in_odd=w_in_odd, w_out_odd=w_out_odd, fox_f_bias=fox_f_bias,
             moba_qk_norm=moba_qk_norm, fox_qk_norm=fox_qk_norm)
    tabs = _tables(w)
    y_prompt, pe, po = _trunk(x_prompt, w, tabs, None)

    n_phys = cache_nsa_cmp.shape[1]
    n_layers = cache_nsa_cmp.shape[0]
    rows3 = lambda pool: pool.reshape(n_layers * n_phys, -1, HEAD_DIM)
    rows4 = lambda pool: pool.reshape(n_layers * n_phys, PAGE_ROWS, -1, HEAD_DIM)
    logf_t = jnp.swapaxes(cache_fox_logf.reshape(n_layers * n_phys, PAGE_ROWS, FOX_HEADS), 1, 2)
    caches = dict(cmp=rows3(cache_nsa_cmp), sel=rows3(cache_nsa_sel), moba=rows4(cache_moba),
                  fox=rows4(cache_fox), logf=logf_t, win=cache_nsa_win, hgrn=state_hgrn,
                  tbl=page_table.astype(jnp.int32), n_phys=n_phys)
    y_sample, se, so = _trunk(x_sample, w, tabs, caches)

    stack = lambda items, i: jnp.stack([it[i] for it in items])
    return (y_prompt, y_sample,
            stack(pe, 0), stack(pe, 1), stack(pe, 2), stack(pe, 3),
            stack(po, 0), stack(po, 1), stack(po, 2),
            stack(se, 0), stack(se, 1), stack(se, 2), stack(se, 3),
            stack(so, 0), stack(so, 1), stack(so, 2))
```

```python
import functools
import math

import jax
import jax.numpy as jnp
import numpy as np
from jax import lax
from jax.experimental import pallas as pl
from jax.experimental.pallas import tpu as pltpu

F32 = jnp.float32
BF16 = jnp.bfloat16

HEAD_DIM = 128
NSA_HEADS = 8
NSA_KV_HEADS = 2
NSA_GROUP = NSA_HEADS // NSA_KV_HEADS
NSA_BLOCK = 64
NSA_TOPN = 16
NSA_WINDOW = 512
NSA_PHI_HIDDEN = 2 * HEAD_DIM
NSA_FORCE_SCORE = 1.0e4
HG_HEADS = 8
MOBA_HEADS = 8
MOBA_BLOCK = 256
MOBA_TOPK = 3
FOX_HEADS = 8
REL_BUCKETS = 32
REL_MAX_DIST = 128
EPS = 1e-6
PAGE_ROWS = 128

LANES = 128
VMEM_LIMIT_BYTES = 56 * 1024 * 1024
ROW_TILE = 512
DEEP_K = 2048
Q_TILE = 128
FLASH_TILE = 256
FLASH_HEADS_PER_STEP = 4
NEG_BIG = -1e30


def _cparams(n_axes):
    return pltpu.CompilerParams(dimension_semantics=("arbitrary",) * n_axes,
                                vmem_limit_bytes=VMEM_LIMIT_BYTES)


def _rel_bucket(dist):
    n = jnp.maximum(dist, 0)
    exact = REL_BUCKETS // 2
    nf = jnp.maximum(n, 1).astype(F32)
    big = exact + (jnp.log(nf / exact) / math.log(REL_MAX_DIST / exact) * (REL_BUCKETS - exact)).astype(jnp.int32)
    return jnp.where(n < exact, n, jnp.minimum(big, REL_BUCKETS - 1))


def _bucket_bias(dist, table_rows):
    bucket = _rel_bucket(dist)
    out = jnp.zeros(dist.shape, F32) + table_rows(0)
    for b in range(1, REL_BUCKETS):
        out = jnp.where(bucket == b, table_rows(b), out)
    return out


def _rms_rows(x, g):
    return x * lax.rsqrt(jnp.mean(x * x, axis=-1, keepdims=True) + EPS) * g


def _log_sigmoid(z):
    return jnp.minimum(z, 0.0) - jnp.log1p(jnp.exp(-jnp.abs(z)))


def _sigmoid(z):
    return 1.0 / (1.0 + jnp.exp(-z))


def _rms_cast_kernel(x_ref, g_ref, o_ref):
    o_ref[...] = _rms_rows(x_ref[...], g_ref[...]).astype(o_ref.dtype)


def _rms_cast(x2d, g_stack, layer):
    m, d = x2d.shape
    tm = min(m, ROW_TILE)
    g3 = g_stack.reshape(g_stack.shape[0], 1, d)
    return pl.pallas_call(
        _rms_cast_kernel,
        grid=(m // tm,),
        in_specs=[pl.BlockSpec((tm, d), lambda i: (i, 0)),
                  pl.BlockSpec((None, 1, d), lambda i: (layer, 0, 0))],
        out_specs=pl.BlockSpec((tm, d), lambda i: (i, 0)),
        out_shape=jax.ShapeDtypeStruct((m, d), BF16),
        compiler_params=_cparams(1),
        name="rms_cast",
    )(x2d, g3)


def _dense_kernel(*refs, n_a, mode, scale, rider):
    pos = 0
    a_refs = refs[pos:pos + n_a]
    pos += n_a
    ra_refs = refs[pos:pos + n_a] if rider else ()
    pos += len(ra_refs)
    w_ref = refs[pos]
    pos += 1
    w2_ref = res_ref = rres_ref = ro_ref = None
    if mode == "swiglu":
        w2_ref = refs[pos]
        pos += 1
    if mode == "res":
        res_ref = refs[pos]
        pos += 1
        if rider:
            rres_ref = refs[pos]
            pos += 1
    o_ref = refs[pos]
    pos += 1
    if rider:
        ro_ref = refs[pos]
        pos += 1
    wb_ref = refs[pos]
    pos += 1
    wb2_ref = refs[pos] if mode == "swiglu" else None

    def apply(in_refs, r_ref, out_ref):
        a = in_refs[0][...] if n_a == 1 else jnp.concatenate([r[...] for r in in_refs], axis=-1)
        y = jnp.dot(a, wb_ref[...], preferred_element_type=F32)
        if mode == "swiglu":
            y2 = jnp.dot(a, wb2_ref[...], preferred_element_type=F32)
            y = y * _sigmoid(y) * y2
        elif mode == "res":
            y = r_ref[...] + scale * y
        out_ref[...] = y.astype(out_ref.dtype)

    @pl.when(pl.program_id(1) == 0)
    def _():
        wb_ref[...] = w_ref[...].astype(BF16)
        if mode == "swiglu":
            wb2_ref[...] = w2_ref[...].astype(BF16)
        if rider:
            apply(ra_refs, rres_ref, ro_ref)

    apply(a_refs, res_ref, o_ref)


def _dense(a_parts, w, lead, col0, n_out, *, tn, mode="plain", res=None, scale=1.0,
           out_dtype=F32, col0_b=None, rider=None, name="dense"):
    m = a_parts[0].shape[0]
    k = sum(a.shape[1] for a in a_parts)
    tm = min(m, 2 * ROW_TILE if k <= DEEP_K else ROW_TILE)
    assert m % tm == 0 and n_out % tn == 0 and col0 % tn == 0
    if w.ndim == 3:
        wblock = (None, k, tn)

        def wmap(off):
            return lambda j, i: (lead, 0, j + off)
    else:
        wblock = (k, tn)

        def wmap(off):
            return lambda j, i: (0, j + off)
    in_specs = [pl.BlockSpec((tm, a.shape[1]), lambda j, i: (i, 0)) for a in a_parts]
    args = list(a_parts)
    ms = 0
    if rider is not None:
        r_parts, r_res = rider
        ms = r_parts[0].shape[0]
        assert [a.shape[1] for a in r_parts] == [a.shape[1] for a in a_parts]
        in_specs += [pl.BlockSpec((ms, a.shape[1]), lambda j, i: (0, 0)) for a in r_parts]
        args += list(r_parts)
    in_specs.append(pl.BlockSpec(wblock, wmap(col0 // tn)))
    args.append(w)
    scratch = [pltpu.VMEM((k, tn), BF16)]
    if mode == "swiglu":
        assert col0_b % tn == 0
        in_specs.append(pl.BlockSpec(wblock, wmap(col0_b // tn)))
        args.append(w)
        scratch.append(pltpu.VMEM((k, tn), BF16))
    if mode == "res":
        in_specs.append(pl.BlockSpec((tm, tn), lambda j, i: (i, j)))
        args.append(res)
        if rider is not None:
            in_specs.append(pl.BlockSpec((ms, tn), lambda j, i: (0, j)))
            args.append(r_res)
    out_specs = pl.BlockSpec((tm, tn), lambda j, i: (i, j))
    out_shape = jax.ShapeDtypeStruct((m, n_out), out_dtype)
    if rider is not None:
        out_specs = [out_specs, pl.BlockSpec((ms, tn), lambda j, i: (0, j))]
        out_shape = [out_shape, jax.ShapeDtypeStruct((ms, n_out), out_dtype)]
    return pl.pallas_call(
        functools.partial(_dense_kernel, n_a=len(a_parts), mode=mode, scale=scale, rider=rider is not None),
        grid=(n_out // tn, m // tm),
        in_specs=in_specs,
        out_specs=out_specs,
        out_shape=out_shape,
        scratch_shapes=scratch,
        compiler_params=_cparams(2),
        name=name,
    )(*args)


def _dense2(a_parts, r_parts, *args, res=None, r_res=None, **kw):
    if r_parts is None:
        return _dense(a_parts, *args, res=res, **kw), None
    return _dense(a_parts, *args, res=res, rider=(r_parts, r_res), **kw)


def _ffn(x2d, xr2d, norm_g, w_in, w_out, layer):
    d_ff = w_out.shape[1]
    xn = _rms_cast(x2d, norm_g, layer)
    xrn = None if xr2d is None else [_rms_cast(xr2d, norm_g, layer)]
    h, hr = _dense2([xn], xrn, w_in, layer, 0, d_ff, tn=512, mode="swiglu", col0_b=d_ff,
                    out_dtype=BF16, name="ffn_in")
    return _dense2([h], None if hr is None else [hr], w_out, layer, 0, x2d.shape[1], tn=512, mode="res",
                   res=x2d, r_res=xr2d, scale=0.5, name="ffn_out")


def _bias_tiles(rel_bias):
    i = jnp.arange(FLASH_TILE)[:, None]
    j = jnp.arange(FLASH_TILE)[None, :]
    dist = jnp.stack([i - j, FLASH_TILE + i - j])
    onehot = (_rel_bucket(dist)[..., None] == jnp.arange(REL_BUCKETS)).astype(F32)
    tiles = jnp.einsum("ktsb,bh->hkts", onehot, rel_bias.astype(F32), precision=lax.Precision.HIGHEST)
    return jnp.where((dist < 0)[None], NEG_BIG, tiles)


def _tables(w):
    lb_all = jnp.cumsum(jax.nn.softmax(w["hg_lb"].astype(F32), axis=0), axis=0)
    lb_all = lb_all - lb_all[0:1]
    lbh = lb_all.reshape(lb_all.shape[0], HG_HEADS, HEAD_DIM)
    lb_rows = jnp.stack([jnp.log(lbh), jnp.log1p(-lbh), 1.0 - lbh], axis=2)
    return {"tiles": _bias_tiles(w["rel_bias"]), "lb_rows": lb_rows, "phi_w1": w["nsa_phi_w1"].astype(BF16)}


def _bias_cmp_table(rel_bias, q_start, t_len, nb):
    t_pos = q_start + jnp.arange(t_len)
    dist = t_pos[:, None] - (jnp.arange(nb) * NSA_BLOCK + NSA_BLOCK - 1)[None, :]
    onehot = (_rel_bucket(dist)[..., None] == jnp.arange(REL_BUCKETS)).astype(F32)
    return jnp.einsum("tnb,bh->htn", onehot, rel_bias.astype(F32), precision=lax.Precision.HIGHEST)


def _even_post_kernel(pa_ref, graw_ref, qg_ref, kg_ref, q_ref, cmp_ref, sel_ref, win_ref,
                      selb_ref, winb_ref, gates_ref):
    scale = HEAD_DIM ** -0.5
    qg = qg_ref[...]
    for h in range(NSA_HEADS):
        sl = slice(h * HEAD_DIM, (h + 1) * HEAD_DIM)
        q_ref[:, sl] = (_rms_rows(pa_ref[:, sl], qg) * scale).astype(q_ref.dtype)
    base = NSA_HEADS * HEAD_DIM
    kv_w = 2 * NSA_KV_HEADS * HEAD_DIM
    cmp_ref[...] = pa_ref[:, base:base + kv_w]
    for which, (o_ref, ob_ref) in enumerate(((sel_ref, selb_ref), (win_ref, winb_ref))):
        off = base + (which + 1) * kv_w
        kg = kg_ref[which + 1:which + 2, :]
        for c in range(2 * NSA_KV_HEADS):
            src = pa_ref[:, off + c * HEAD_DIM: off + (c + 1) * HEAD_DIM]
            val = _rms_rows(src, kg) if c < NSA_KV_HEADS else src
            o_ref[:, c * HEAD_DIM:(c + 1) * HEAD_DIM] = val
            ob_ref[:, c * HEAD_DIM:(c + 1) * HEAD_DIM] = val.astype(BF16)
    gates_ref[...] = _sigmoid(graw_ref[...])


def _even_post(pa, graw, q_norm, k_norm):
    m = pa.shape[0]
    tm = min(m, ROW_TILE)
    kv_w = 2 * NSA_KV_HEADS * HEAD_DIM
    qw = NSA_HEADS * HEAD_DIM
    row = lambda w: pl.BlockSpec((tm, w), lambda i: (i, 0))
    full = lambda a: pl.BlockSpec(a.shape, lambda i: (0,) * a.ndim)
    qg = q_norm.reshape(1, HEAD_DIM)
    return pl.pallas_call(
        _even_post_kernel,
        grid=(m // tm,),
        in_specs=[row(pa.shape[1]), row(LANES), full(qg), full(k_norm)],
        out_specs=[row(qw), row(kv_w), row(kv_w), row(kv_w), row(kv_w), row(kv_w), row(LANES)],
        out_shape=[jax.ShapeDtypeStruct((m, qw), BF16)] + [jax.ShapeDtypeStruct((m, kv_w), F32)] * 3
        + [jax.ShapeDtypeStruct((m, kv_w), BF16)] * 2 + [jax.ShapeDtypeStruct((m, LANES), F32)],
        compiler_params=_cparams(1),
        name="even_post",
    )(pa, graw, qg, k_norm)


def _gelu_tanh(x):
    return 0.5 * x * (1.0 + jnp.tanh(math.sqrt(2.0 / math.pi) * (x + 0.044715 * (x * x * x))))


NSA_KV_COLS = 2 * NSA_KV_HEADS
CMP_PAGES_PER_STEP = 8
CMP_PAGES_PER_GROUP = 64


def _compress_mlp(x_of, nblk, pos_ref, w1_ref, w2_ref, kg_ref, o_ref, acc_ref):
    acc_ref[...] = jnp.zeros_like(acc_ref)

    def body(i2, carry):
        for c in range(NSA_KV_COLS):
            w = c // NSA_KV_HEADS
            xa = x_of(2 * i2, c) + pos_ref[w, pl.ds(2 * i2, 1), :]
            xb = x_of(2 * i2 + 1, c) + pos_ref[w, pl.ds(2 * i2 + 1, 1), :]
            x = jnp.concatenate([xa, xb], axis=-1).astype(BF16)
            wi = w1_ref[w, pl.ds(pl.multiple_of(i2 * 2 * HEAD_DIM, 2 * HEAD_DIM), 2 * HEAD_DIM), :]
            acc_ref[c] += jnp.dot(x, wi, preferred_element_type=F32)
        return carry

    lax.fori_loop(0, NSA_BLOCK // 2, body, 0)
    for c in range(NSA_KV_COLS):
        w, kh = divmod(c, NSA_KV_HEADS)
        hid = _gelu_tanh(acc_ref[c]).astype(BF16)
        y = jnp.dot(hid, w2_ref[w].astype(BF16), preferred_element_type=F32)
        if w == 0:
            y = _rms_rows(y, kg_ref[0:1, :])
        o_ref[w, :, kh * HEAD_DIM:(kh + 1) * HEAD_DIM] = y


def _compress_rows_kernel(*refs, nblk):
    x_refs = refs[:NSA_KV_COLS]
    pos_ref, w1_ref, w2_ref, kg_ref, o_ref, acc_ref = refs[NSA_KV_COLS:]

    def x_of(i, c):
        return x_refs[c][pl.ds(i, nblk, stride=NSA_BLOCK), :]

    _compress_mlp(x_of, nblk, pos_ref, w1_ref, w2_ref, kg_ref, o_ref, acc_ref)


def _compress_pages_kernel(*refs, pps, nblk):
    tbl_ref = refs[0]
    page_refs = refs[1:1 + pps]
    pos_ref, w1_ref, w2_ref, kg_ref, o_ref, xs_ref, acc_ref = refs[1 + pps:]
    st = pl.program_id(2)
    page_rows = PAGE_ROWS * NSA_KV_COLS
    for p in range(pps):
        xs_ref[pl.ds(pl.multiple_of((st * pps + p) * page_rows, page_rows), page_rows), :] = page_refs[p][...]

    @pl.when(st == pl.num_programs(2) - 1)
    def _():
        def x_of(i, c):
            return xs_ref[pl.ds(i * NSA_KV_COLS + c, nblk, stride=NSA_BLOCK * NSA_KV_COLS), :]

        _compress_mlp(x_of, nblk, pos_ref, w1_ref, w2_ref, kg_ref, o_ref, acc_ref)


def _compress(src, tbl, pos, w1b, w2, k_norm, j):
    weight_specs = lambda nidx: [
        pl.BlockSpec((None, 2, NSA_BLOCK, HEAD_DIM), lambda *a: (j, 0, 0, 0)),
        pl.BlockSpec((None, 2, NSA_BLOCK * HEAD_DIM, NSA_PHI_HIDDEN), lambda *a: (j, 0, 0, 0),
                     pipeline_mode=pl.Buffered(1)),
        pl.BlockSpec((None, 2, NSA_PHI_HIDDEN, HEAD_DIM), lambda *a: (j, 0, 0, 0)),
        pl.BlockSpec(k_norm.shape, lambda *a: (0, 0)),
    ]
    kvw = NSA_KV_HEADS * HEAD_DIM
    if tbl is None:
        bsz, t_len, _ = src.shape
        nblk = t_len // NSA_BLOCK
        return pl.pallas_call(
            functools.partial(_compress_rows_kernel, nblk=nblk),
            grid=(bsz,),
            in_specs=[pl.BlockSpec((None, t_len, HEAD_DIM), lambda b, c=c: (b, 0, c)) for c in range(NSA_KV_COLS)]
            + weight_specs(1),
            out_specs=pl.BlockSpec((2, None, nblk, kvw), lambda b: (0, b, 0, 0)),
            out_shape=jax.ShapeDtypeStruct((2, bsz, nblk, kvw), F32),
            scratch_shapes=[pltpu.VMEM((NSA_KV_COLS, nblk, NSA_PHI_HIDDEN), F32)],
            compiler_params=_cparams(1),
            name="nsa_compress_rows",
        )(*([src] * NSA_KV_COLS), pos, w1b, w2, k_norm)
    bsz, npg = tbl.shape
    pps = math.gcd(CMP_PAGES_PER_STEP, npg)
    ppg = math.gcd(CMP_PAGES_PER_GROUP, npg)
    bpp = PAGE_ROWS // NSA_BLOCK
    nblk = ppg * bpp
    page_rows = PAGE_ROWS * NSA_KV_COLS

    def page_spec(p):
        return pl.BlockSpec((None, page_rows, HEAD_DIM),
                            lambda b, grp, st, t: (t[b, grp * ppg + st * pps + p], 0, 0))

    grid_spec = pltpu.PrefetchScalarGridSpec(
        num_scalar_prefetch=1,
        grid=(bsz, npg // ppg, ppg // pps),
        in_specs=[page_spec(p) for p in range(pps)] + weight_specs(4),
        out_specs=pl.BlockSpec((2, None, nblk, kvw), lambda b, grp, st, t: (0, b, grp, 0)),
        scratch_shapes=[pltpu.VMEM((ppg * page_rows, HEAD_DIM), F32),
                        pltpu.VMEM((NSA_KV_COLS, nblk, NSA_PHI_HIDDEN), F32)],
    )
    return pl.pallas_call(
        functools.partial(_compress_pages_kernel, pps=pps, nblk=nblk),
        grid_spec=grid_spec,
        out_shape=jax.ShapeDtypeStruct((2, bsz, npg * bpp, kvw), F32),
        compiler_params=_cparams(3),
        name="nsa_compress_pages",
    )(tbl, *([src] * pps), pos, w1b, w2, k_norm)


def _nsa_cmp_kernel(q_ref, kc_ref, vc_ref, bias_ref, gates_ref, oc_ref, sel_ref, *, q_start, tq, nb, extra):
    kh = pl.program_id(1)
    qi = pl.program_id(2)
    t_pos = q_start + qi * tq + lax.broadcasted_iota(jnp.int32, (tq, 1), 0)
    blk = lax.broadcasted_iota(jnp.int32, (1, nb), 1)
    valid = t_pos >= blk * NSA_BLOCK + (NSA_BLOCK - 1)
    kc = kc_ref[...].astype(BF16)
    vc = vc_ref[...].astype(BF16)
    gates = gates_ref[...]
    imp = jnp.zeros((tq, nb), F32)
    for g in range(NSA_GROUP):
        qg = q_ref[:, g * HEAD_DIM:(g + 1) * HEAD_DIM]
        s = lax.dot_general(qg, kc, (((1,), (1,)), ((), ())), preferred_element_type=F32) + bias_ref[g]
        s = jnp.where(valid, s, NEG_BIG)
        m = jnp.max(s, axis=-1, keepdims=True)
        p = jnp.where(valid, jnp.exp(s - m), 0.0)
        l = jnp.sum(p, axis=-1, keepdims=True)
        p = p / jnp.where(l > 0, l, 1.0)
        imp = imp + p
        o = jnp.dot(p.astype(BF16), vc, preferred_element_type=F32)
        onehot = lax.broadcasted_iota(jnp.int32, (1, LANES), 1) == (kh * NSA_GROUP + g) * 3
        gcol = jnp.sum(jnp.where(onehot, gates, 0.0), axis=-1, keepdims=True)
        oc_ref[:, g * HEAD_DIM:(g + 1) * HEAD_DIM] = o * gcol
    cur = t_pos // NSA_BLOCK
    forced = (blk == 0) | (blk == cur) | (blk == cur - 1)
    score = jnp.where(forced, NSA_FORCE_SCORE, jnp.where(blk <= cur, imp, -1.0))
    rank = jnp.zeros((tq, nb), jnp.int32)
    for mcol in range(nb):
        sm = score[:, mcol:mcol + 1]
        ahead = (sm > score) | ((sm == score) & (mcol < blk))
        rank = rank + ahead.astype(jnp.int32)
    if not extra:
        sel_ref[...] = (rank < NSA_TOPN).astype(F32)
    else:
        rank = rank + (score < NSA_FORCE_SCORE).astype(jnp.int32)
        rank_extra = jnp.sum((score >= NSA_FORCE_SCORE).astype(jnp.int32), axis=-1, keepdims=True)
        lane = lax.broadcasted_iota(jnp.int32, (tq, NSA_TOPN), 1)
        out = jnp.zeros((tq, NSA_TOPN), jnp.int32)
        for r in range(NSA_TOPN):
            idx_r = jnp.sum(jnp.where(rank == r, blk, 0), axis=-1, keepdims=True)
            idx_r = idx_r + jnp.where(rank_extra == r, nb, 0)
            out = jnp.where(lane == r, idx_r, out)
        sel_ref[...] = out


def _nsa_cmp(q3, kvc, bias_c, gates3, *, q_start, extra):
    bsz, t_len, _ = q3.shape
    nb = kvc.shape[2]
    tq = min(t_len, Q_TILE)
    gw = NSA_GROUP * HEAD_DIM
    if extra:
        assert t_len == 1 and q_start // NSA_BLOCK == nb
        sel_shape = jax.ShapeDtypeStruct((bsz, NSA_KV_HEADS, t_len, NSA_TOPN), jnp.int32)
        sel_spec = pl.BlockSpec((None, None, tq, NSA_TOPN), lambda b, kh, qi: (b, kh, qi, 0))
    else:
        assert (q_start + t_len) == nb * NSA_BLOCK
        sel_shape = jax.ShapeDtypeStruct((bsz, NSA_KV_HEADS, t_len, nb), F32)
        sel_spec = pl.BlockSpec((None, None, tq, nb), lambda b, kh, qi: (b, kh, qi, 0))
    return pl.pallas_call(
        functools.partial(_nsa_cmp_kernel, q_start=q_start, tq=tq, nb=nb, extra=extra),
        grid=(bsz, NSA_KV_HEADS, t_len // tq),
        in_specs=[
            pl.BlockSpec((None, tq, gw), lambda b, kh, qi: (b, qi, kh)),
            pl.BlockSpec((None, None, nb, HEAD_DIM), lambda b, kh, qi: (0, b, 0, kh)),
            pl.BlockSpec((None, None, nb, HEAD_DIM), lambda b, kh, qi: (1, b, 0, kh)),
            pl.BlockSpec((NSA_GROUP, tq, nb), lambda b, kh, qi: (kh, qi, 0)),
            pl.BlockSpec((None, tq, LANES), lambda b, kh, qi: (b, qi, 0)),
        ],
        out_specs=[pl.BlockSpec((None, tq, gw), lambda b, kh, qi: (b, qi, kh)), sel_spec],
        out_shape=[jax.ShapeDtypeStruct((bsz, t_len, NSA_HEADS * HEAD_DIM), F32), sel_shape],
        compiler_params=_cparams(3),
        name="nsa_cmp_attn",
    )(q3, kvc, kvc, bias_c, gates3)


def _lane_column(x, col):
    onehot = lax.broadcasted_iota(jnp.int32, (1, x.shape[1]), 1) == col
    return jnp.sum(jnp.where(onehot, x, 0.0), axis=-1, keepdims=True)


def _row_to_column(row):
    n = row.shape[1]
    eye = lax.broadcasted_iota(jnp.int32, (n, n), 0) == lax.broadcasted_iota(jnp.int32, (n, n), 1)
    return jnp.sum(jnp.where(eye, row, 0.0), axis=-1, keepdims=True)


def _flash_kernel(*refs, kind, hps, shared_kv, tq, blk, gate_branch):
    q_ref, k_ref, v_ref = refs[:3]
    pos = 3
    bias_ref = bm_ref = gates_ref = c_ref = None
    if kind in ("nsa_sel", "nsa_win", "moba"):
        bias_ref = refs[pos]
        pos += 1
    if kind in ("nsa_sel", "moba"):
        bm_ref = refs[pos]
        pos += 1
    if kind in ("nsa_sel", "nsa_win"):
        gates_ref = refs[pos]
        pos += 1
    if kind == "fox":
        c_ref = refs[pos]
        pos += 1
    o_ref = refs[pos]
    m_refs = refs[pos + 1:pos + 1 + hps]
    l_refs = refs[pos + 1 + hps:pos + 1 + 2 * hps]
    acc_refs = refs[pos + 1 + 2 * hps:pos + 1 + 3 * hps]
    tk = tq
    step = pl.program_id(1)
    qi = pl.program_id(2)
    q0 = pl.multiple_of(qi * tq, tq)
    ii = lax.broadcasted_iota(jnp.int32, (tq, tk), 0)
    jj = lax.broadcasted_iota(jnp.int32, (tq, tk), 1)
    n_back = NSA_WINDOW // tk
    lo = jnp.maximum(qi - n_back, 0) if kind == "nsa_win" else 0
    for g in range(hps):
        m_refs[g][...] = jnp.full(m_refs[g].shape, NEG_BIG, F32)
        l_refs[g][...] = jnp.zeros(l_refs[g].shape, F32)
        acc_refs[g][...] = jnp.zeros(acc_refs[g].shape, F32)

    def head_cols(g):
        return slice(g * HEAD_DIM, (g + 1) * HEAD_DIM)

    def tile_step(ki, diag):
        k0 = pl.multiple_of(ki * tk, tk)
        delta = qi - ki
        shared_add = None
        if kind == "nsa_sel":
            bm = bm_ref[...]
            shared_add = (_lane_column(bm, k0 // blk) - 1.0) * (-NEG_BIG)
            for sub in range(1, tk // blk):
                shared_add = jnp.where(jj >= sub * blk, (_lane_column(bm, k0 // blk + sub) - 1.0) * (-NEG_BIG),
                                       shared_add)
        elif kind == "nsa_win" and not diag:
            shared_add = jnp.where(delta == n_back, jnp.where(jj < ii, NEG_BIG, 0.0), 0.0)
        elif kind == "fox" and diag:
            shared_add = jnp.where(jj <= ii, 0.0, NEG_BIG)
        for g in range(hps):
            kv_cols = slice(0, HEAD_DIM) if shared_kv else head_cols(g)
            kt = k_ref[pl.ds(k0, tk), kv_cols]
            vt = v_ref[pl.ds(k0, tk), kv_cols]
            s = lax.dot_general(q_ref[:, head_cols(g)], kt, (((1,), (1,)), ((), ())), preferred_element_type=F32)
            if kind == "fox":
                s = s + (c_ref[g, :, pl.ds(q0, LANES)][:, 0:1] - c_ref[g, :, pl.ds(k0, tk)])
            elif diag:
                s = s + bias_ref[g, 0]
            else:
                near = bias_ref[g, 1]
                s = s + jnp.where(delta == 1, near, near[tq - 1:tq, 0:1])
            if shared_add is not None:
                s = s + shared_add
            if kind == "moba":
                s = s + (_lane_column(bm_ref[g], k0 // blk) - 1.0) * (-NEG_BIG)
            m_old = m_refs[g][...]
            m_new = jnp.maximum(m_old, jnp.max(s, axis=-1, keepdims=True))
            alpha = jnp.exp(m_old - m_new)
            parts = [jnp.exp(s[:, c * LANES:(c + 1) * LANES] - m_new) for c in range(tk // LANES)]
            l_refs[g][...] = alpha * l_refs[g][...] + functools.reduce(lambda a, b: a + b, parts)
            p = jnp.concatenate(parts, axis=-1).astype(BF16)
            acc_refs[g][...] = alpha * acc_refs[g][...] + jnp.dot(p, vt, preferred_element_type=F32)
            m_refs[g][...] = m_new

    def body(ki, carry):
        tile_step(ki, False)
        return carry

    lax.fori_loop(lo, qi, body, 0)
    tile_step(qi, True)
    for g in range(hps):
        o = acc_refs[g][...] / jnp.sum(l_refs[g][...], axis=-1, keepdims=True)
        if gates_ref is not None:
            o = o * _lane_column(gates_ref[...], (step * hps + g) * 3 + gate_branch)
        o_ref[:, head_cols(g)] = o.astype(o_ref.dtype)


def _flash(kind, q3, kv3, k_col, v_col, n_kv_heads, *, bias_tiles=None, bm=None, gates3=None, cum=None,
           blk=0, gate_branch=0, out_dtype=F32):
    bsz, t_len, qw = q3.shape
    n_heads = qw // HEAD_DIM
    shared_kv = n_kv_heads < n_heads
    hps = n_heads // n_kv_heads if shared_kv else FLASH_HEADS_PER_STEP
    tq = FLASH_TILE
    assert t_len % tq == 0 and (shared_kv or (k_col % hps == 0 and v_col % hps == 0))
    gw = hps * HEAD_DIM
    if shared_kv:
        kv_spec = lambda c0: pl.BlockSpec((None, t_len, HEAD_DIM), lambda b, h, qi: (b, 0, c0 + h))
    else:
        kv_spec = lambda c0: pl.BlockSpec((None, t_len, gw), lambda b, h, qi: (b, 0, c0 // hps + h))
    in_specs = [pl.BlockSpec((None, tq, gw), lambda b, h, qi: (b, qi, h)), kv_spec(k_col), kv_spec(v_col)]
    args = [q3, kv3, kv3]
    if bias_tiles is not None:
        in_specs.append(pl.BlockSpec((hps, 2, tq, tq), lambda b, h, qi: (h, 0, 0, 0)))
        args.append(bias_tiles)
    if bm is not None:
        nb = bm.shape[-1]
        if shared_kv:
            in_specs.append(pl.BlockSpec((None, None, tq, nb), lambda b, h, qi: (b, h, qi, 0)))
        else:
            in_specs.append(pl.BlockSpec((None, hps, tq, nb), lambda b, h, qi: (b, h, qi, 0)))
        args.append(bm)
    if gates3 is not None:
        in_specs.append(pl.BlockSpec((None, tq, LANES), lambda b, h, qi: (b, qi, 0)))
        args.append(gates3)
    if cum is not None:
        in_specs.append(pl.BlockSpec((None, hps, 1, t_len), lambda b, h, qi: (b, h, 0, 0)))
        args.append(cum)
    return pl.pallas_call(
        functools.partial(_flash_kernel, kind=kind, hps=hps, shared_kv=shared_kv, tq=tq, blk=blk,
                          gate_branch=gate_branch),
        grid=(bsz, n_heads // hps, t_len // tq),
        in_specs=in_specs,
        out_specs=pl.BlockSpec((None, tq, gw), lambda b, h, qi: (b, qi, h)),
        out_shape=jax.ShapeDtypeStruct((bsz, t_len, qw), out_dtype),
        scratch_shapes=[pltpu.VMEM((tq, LANES), F32)] * (2 * hps) + [pltpu.VMEM((tq, HEAD_DIM), F32)] * hps,
        compiler_params=_cparams(3),
        name="flash_" + kind,
    )(*args)


def _sum3_kernel(a_ref, b_ref, c_ref, o_ref):
    o_ref[...] = (a_ref[...] + b_ref[...] + c_ref[...]).astype(o_ref.dtype)


def _sum3_cast(a, b, c):
    m, n = a.shape
    tm = min(m, ROW_TILE)
    spec = pl.BlockSpec((tm, n), lambda i: (i, 0))
    return pl.pallas_call(
        _sum3_kernel, grid=(m // tm,), in_specs=[spec] * 3, out_specs=spec,
        out_shape=jax.ShapeDtypeStruct((m, n), BF16), compiler_params=_cparams(1), name="nsa_sum",
    )(a, b, c)


def _hgrn_kernel(q_ref, z_ref, v_ref, g_ref, lb_ref, on_ref, s0_ref, o_ref, s_ref, *st_refs, t_len, chunk, hps):
    on = on_ref[...]
    ri = lax.broadcasted_iota(jnp.int32, (chunk, chunk), 0)
    ci = lax.broadcasted_iota(jnp.int32, (chunk, chunk), 1)
    tril = (ci <= ri).astype(F32)
    rows = lax.broadcasted_iota(jnp.int32, (chunk, 1), 0)

    single = t_len < chunk

    def load(ref, r0, cols):
        if single:
            return jnp.broadcast_to(ref[0:1, cols], (chunk, HEAD_DIM))
        return ref[pl.ds(r0, chunk), cols]

    for g in range(hps):
        st_refs[g][...] = s0_ref[g].T

    def body(c, carry):
        r0 = pl.multiple_of(c * chunk, chunk)
        for g in range(hps):
            cols = slice(g * HEAD_DIM, (g + 1) * HEAD_DIM)
            q = load(q_ref, r0, cols)
            z = load(z_ref, r0, cols)
            v = load(v_ref, r0, cols)
            a_term = lb_ref[g, 0:1, :]
            b_term = lb_ref[g, 1:2, :] + _log_sigmoid(z)
            logf = jnp.maximum(a_term, b_term) + jnp.log1p(jnp.exp(-jnp.abs(a_term - b_term)))
            k = lb_ref[g, 2:3, :] * _sigmoid(-z)
            if single:
                logf = jnp.where(rows < t_len, logf, 0.0)
                k = jnp.where(rows < t_len, k, 0.0)
            cum = jnp.dot(tril, logf, preferred_element_type=F32, precision=lax.Precision.HIGHEST)
            a_last = cum[chunk - 1:chunk, :]
            o = jnp.zeros((chunk, HEAD_DIM), F32)
            for s_row in range(chunk):
                diff = jnp.where(rows >= s_row, cum - cum[s_row:s_row + 1, :], -jnp.inf)
                wgt = q * jnp.exp(diff) * k[s_row:s_row + 1, :]
                o = o + jnp.sum(wgt, axis=-1, keepdims=True) * v[s_row:s_row + 1, :]
            st = st_refs[g][...]
            qa = (q * jnp.exp(cum)).astype(BF16)
            o = o + lax.dot_general(qa, st.astype(BF16), (((1,), (1,)), ((), ())), preferred_element_type=F32)
            kd = (k * jnp.exp(a_last - cum)).astype(BF16)
            st_refs[g][...] = jnp.exp(a_last) * st + lax.dot_general(v.astype(BF16), kd, (((0,), (0,)), ((), ())),
                                                                     preferred_element_type=F32)
            gate = load(g_ref, r0, cols)
            o = _rms_rows(o, on) * (gate * _sigmoid(gate))
            if single:
                o_ref[:, cols] = o[0:t_len, :].astype(o_ref.dtype)
            else:
                o_ref[pl.ds(r0, chunk), cols] = o.astype(o_ref.dtype)
        return carry

    lax.fori_loop(0, max(t_len // chunk, 1), body, 0)
    for g in range(hps):
        s_ref[g] = st_refs[g][...].T


HGRN_HEADS_PER_STEP = 4


def _hgrn(ph3, lb_rows, o_norm, s0):
    bsz, t_len, _ = ph3.shape
    chunk = 16
    hps = HGRN_HEADS_PER_STEP
    assert t_len % chunk == 0 or t_len == 1
    gw = hps * HEAD_DIM
    n_steps = HG_HEADS // hps
    col = lambda part: pl.BlockSpec((None, t_len, gw), lambda b, h: (b, 0, part * n_steps + h))
    on = o_norm.reshape(1, HEAD_DIM)
    state_spec = pl.BlockSpec((None, hps, HEAD_DIM, HEAD_DIM), lambda b, h: (b, h, 0, 0))
    return pl.pallas_call(
        functools.partial(_hgrn_kernel, t_len=t_len, chunk=chunk, hps=hps),
        grid=(bsz, n_steps),
        in_specs=[col(0), col(1), col(2), col(3),
                  pl.BlockSpec((hps, 3, HEAD_DIM), lambda b, h: (h, 0, 0)),
                  pl.BlockSpec((1, HEAD_DIM), lambda b, h: (0, 0)),
                  state_spec],
        out_specs=[pl.BlockSpec((None, t_len, gw), lambda b, h: (b, 0, h)), state_spec],
        out_shape=[jax.ShapeDtypeStruct((bsz, t_len, HG_HEADS * HEAD_DIM), BF16),
                   jax.ShapeDtypeStruct((bsz, HG_HEADS, HEAD_DIM, HEAD_DIM), F32)],
        scratch_shapes=[pltpu.VMEM((HEAD_DIM, HEAD_DIM), F32)] * hps,
        compiler_params=_cparams(2),
        name="hgrn2",
    )(ph3, ph3, ph3, ph3, lb_rows, on, s0)


def _group_rows(q_ref, group):
    return jnp.concatenate([q_ref[:, g * HEAD_DIM:(g + 1) * HEAD_DIM].astype(F32) for g in range(group)], axis=0)


def _store_group_rows(o_ref, o):
    for g in range(o.shape[0]):
        o_ref[:, g * HEAD_DIM:(g + 1) * HEAD_DIM] = o[g:g + 1, :].astype(o_ref.dtype)


NSA_SEL_BLOCKS_PER_STEP = 4


def _nsa_sel_step_kernel(*refs, t_pos, nb, bps):
    idx_ref, q_ref = refs[1], refs[2]
    kv_refs = refs[3:3 + bps]
    kn_ref, vn_ref, rbt_ref, gates_ref, o_ref, m_ref, l_ref, acc_ref = refs[3 + bps:]
    b = pl.program_id(0)
    kh = pl.program_id(1)
    j = pl.program_id(2)
    q4 = _group_rows(q_ref, NSA_GROUP)
    rbt = rbt_ref[...]
    col = lambda bk: rbt[:, bk:bk + 1]

    @pl.when(j == 0)
    def _():
        m_ref[...] = jnp.sum(q4 * kn_ref[...], axis=-1, keepdims=True) + col(0)
        l_ref[...] = jnp.ones_like(l_ref)
        acc_ref[...] = jnp.broadcast_to(vn_ref[...], acc_ref.shape)

    for u in range(bps):
        blk_id = idx_ref[b * NSA_KV_HEADS + kh, j * bps + u]

        @pl.when(blk_id < nb)
        def _(u=u, blk_id=blk_id):
            kt = kv_refs[u][pl.ds(kh, NSA_BLOCK, stride=NSA_KV_COLS), :].astype(BF16)
            vt = kv_refs[u][pl.ds(NSA_KV_HEADS + kh, NSA_BLOCK, stride=NSA_KV_COLS), :].astype(BF16)
            s = lax.dot_general(q4.astype(BF16), kt, (((1,), (1,)), ((), ())), preferred_element_type=F32)
            dist = t_pos - (blk_id * NSA_BLOCK + lax.broadcasted_iota(jnp.int32, (1, NSA_BLOCK), 1))
            s = s + _bucket_bias(jnp.broadcast_to(dist, s.shape), col)
            m_old = m_ref[...]
            m_new = jnp.maximum(m_old, jnp.max(s, axis=-1, keepdims=True))
            p = jnp.exp(s - m_new)
            alpha = jnp.exp(m_old - m_new)
            l_ref[...] = alpha * l_ref[...] + jnp.sum(p, axis=-1, keepdims=True)
            acc_ref[...] = alpha * acc_ref[...] + jnp.dot(p.astype(BF16), vt, preferred_element_type=F32)
            m_ref[...] = m_new

    @pl.when(j == pl.num_programs(2) - 1)
    def _():
        o = acc_ref[...] / l_ref[...]
        gates = gates_ref[...]
        gcol = jnp.concatenate([_lane_column(gates, (kh * NSA_GROUP + g) * 3 + 1) for g in range(NSA_GROUP)], axis=0)
        _store_group_rows(o_ref, o * gcol)


def _nsa_sel_step(q3, cache_pages, tbl, idx, sel_new3, rbt, gates3, *, t_pos):
    bsz = q3.shape[0]
    nb = tbl.shape[1] * (PAGE_ROWS // NSA_BLOCK)
    gw = NSA_GROUP * HEAD_DIM
    halves = PAGE_ROWS // NSA_BLOCK

    bps = NSA_SEL_BLOCKS_PER_STEP
    assert NSA_TOPN % bps == 0

    def cache_map(u):
        def index_map(b, kh, j, tbl_ref, idx_ref):
            blk = jnp.minimum(idx_ref[b * NSA_KV_HEADS + kh, j * bps + u], nb - 1)
            return (tbl_ref[b, blk // halves], blk % halves, 0)
        return index_map

    grid_spec = pltpu.PrefetchScalarGridSpec(
        num_scalar_prefetch=2,
        grid=(bsz, NSA_KV_HEADS, NSA_TOPN // bps),
        in_specs=[pl.BlockSpec((None, 1, gw), lambda b, kh, j, t, i: (b, 0, kh))]
        + [pl.BlockSpec((None, NSA_BLOCK * NSA_KV_COLS, HEAD_DIM), cache_map(u)) for u in range(bps)]
        + [
            pl.BlockSpec((None, 1, HEAD_DIM), lambda b, kh, j, t, i: (b, 0, kh)),
            pl.BlockSpec((None, 1, HEAD_DIM), lambda b, kh, j, t, i: (b, 0, NSA_KV_HEADS + kh)),
            pl.BlockSpec((None, NSA_GROUP, REL_BUCKETS), lambda b, kh, j, t, i: (kh, 0, 0)),
            pl.BlockSpec((None, 1, LANES), lambda b, kh, j, t, i: (b, 0, 0)),
        ],
        out_specs=pl.BlockSpec((None, 1, gw), lambda b, kh, j, t, i: (b, 0, kh)),
        scratch_shapes=[pltpu.VMEM((NSA_GROUP, 1), F32), pltpu.VMEM((NSA_GROUP, 1), F32),
                        pltpu.VMEM((NSA_GROUP, HEAD_DIM), F32)],
    )
    return pl.pallas_call(
        functools.partial(_nsa_sel_step_kernel, t_pos=t_pos, nb=nb, bps=bps),
        grid_spec=grid_spec,
        out_shape=jax.ShapeDtypeStruct((bsz, 1, NSA_HEADS * HEAD_DIM), F32),
        compiler_params=_cparams(3),
        name="nsa_sel_step",
    )(tbl, idx, q3, *([cache_pages] * bps), sel_new3, sel_new3, rbt, gates3)


def _nsa_win_step_kernel(q_ref, k_ref, v_ref, kn_ref, vn_ref, rbt_ref, gates_ref, o_ref, *, pw):
    kh = pl.program_id(1)
    q4 = _group_rows(q_ref, NSA_GROUP)
    rbt = rbt_ref[...]
    col = lambda bk: rbt[:, bk:bk + 1]
    s = lax.dot_general(q4.astype(BF16), k_ref[...].astype(BF16), (((1,), (1,)), ((), ())),
                        preferred_element_type=F32)
    dist = pw - lax.broadcasted_iota(jnp.int32, (NSA_GROUP, pw), 1)
    mask = dist <= NSA_WINDOW
    s = jnp.where(mask, s + _bucket_bias(dist, col), NEG_BIG)
    s_self = jnp.sum(q4 * kn_ref[...], axis=-1, keepdims=True) + col(0)
    m = jnp.maximum(jnp.max(s, axis=-1, keepdims=True), s_self)
    p = jnp.where(mask, jnp.exp(s - m), 0.0)
    p_self = jnp.exp(s_self - m)
    l = jnp.sum(p, axis=-1, keepdims=True) + p_self
    o = jnp.dot(p.astype(BF16), v_ref[...].astype(BF16), preferred_element_type=F32) + p_self * vn_ref[...]
    gates = gates_ref[...]
    gcol = jnp.concatenate([_lane_column(gates, (kh * NSA_GROUP + g) * 3 + 2) for g in range(NSA_GROUP)], axis=0)
    _store_group_rows(o_ref, o / l * gcol)


def _nsa_win_step(q3, win_cache, lead, win_new3, rbt, gates3):
    bsz = q3.shape[0]
    pw = win_cache.shape[1]
    gw = NSA_GROUP * HEAD_DIM
    return pl.pallas_call(
        functools.partial(_nsa_win_step_kernel, pw=pw),
        grid=(bsz, NSA_KV_HEADS),
        in_specs=[
            pl.BlockSpec((None, 1, gw), lambda b, kh: (b, 0, kh)),
            pl.BlockSpec((None, pw, HEAD_DIM), lambda b, kh: (lead + b, 0, kh)),
            pl.BlockSpec((None, pw, HEAD_DIM), lambda b, kh: (lead + b, 0, NSA_KV_HEADS + kh)),
            pl.BlockSpec((None, 1, HEAD_DIM), lambda b, kh: (b, 0, kh)),
            pl.BlockSpec((None, 1, HEAD_DIM), lambda b, kh: (b, 0, NSA_KV_HEADS + kh)),
            pl.BlockSpec((None, NSA_GROUP, REL_BUCKETS), lambda b, kh: (kh, 0, 0)),
            pl.BlockSpec((None, 1, LANES), lambda b, kh: (b, 0, 0)),
        ],
        out_specs=pl.BlockSpec((None, 1, gw), lambda b, kh: (b, 0, kh)),
        out_shape=jax.ShapeDtypeStruct((bsz, 1, NSA_HEADS * HEAD_DIM), F32),
        compiler_params=_cparams(2),
        name="nsa_win_step",
    )(q3, win_cache, win_cache, win_new3, win_new3, rbt, gates3)


def _even_proj(h, hr, w, j):
    qw = NSA_HEADS * HEAD_DIM
    kv_w = 2 * NSA_KV_HEADS * HEAD_DIM
    n_main = qw + 3 * kv_w
    n_gate = 3 * NSA_HEADS
    w_in = w["w_in_even"]
    hr_parts = None if hr is None else [hr]
    pa, par = _dense2([h], hr_parts, w_in, j, 0, n_main, tn=512, name="even_in_attn")
    w_gate = jnp.pad(w_in[j, :, n_main:n_main + n_gate], ((0, 0), (0, LANES - n_gate)))
    graw, grawr = _dense2([h], hr_parts, w_gate, None, 0, LANES, tn=LANES, name="even_in_gate")
    w_hg = w_in[j, :, n_main + n_gate:]
    ph, phr = _dense2([h], hr_parts, w_hg, None, 0, w_hg.shape[1], tn=512, name="even_in_hgrn")
    return (pa, graw, ph), (None if hr is None else (par, grawr, phr))


def _even_out(x2d, a_parts, xr2d, ar_parts, w, j):
    d = x2d.shape[1]
    return _dense2(a_parts, ar_parts, w["w_out_even"], j, 0, d, tn=min(d, 512), mode="res", res=x2d,
                   r_res=xr2d, scale=1.0, name="even_out")


def _even_mixer(x2d, h, bsz, t_len, j, w, tabs, past):
    proj, _ = _even_proj(h, None, w, j)
    a_parts, new = _even_core(proj, bsz, t_len, j, w, tabs, past)
    x_new, _ = _even_out(x2d, a_parts, None, None, w, j)
    return (x_new,) + new


def _even_core(proj, bsz, t_len, j, w, tabs, past):
    pa, graw, ph = proj
    m = pa.shape[0]
    qw = NSA_HEADS * HEAD_DIM
    kv_w = 2 * NSA_KV_HEADS * HEAD_DIM
    q, cmp_new, sel_new, win_new, selb, winb, gates = _even_post(pa, graw, w["nsa_q_norm"][j], w["nsa_k_norm"][j])
    q3 = q.reshape(bsz, t_len, qw)
    gates3 = gates.reshape(bsz, t_len, LANES)
    rel_bias = w["rel_bias"]
    k_norm = w["nsa_k_norm"][j]
    if past is None:
        q_start = 0
        src, tbl = cmp_new.reshape(bsz, t_len, kv_w), None
        s0 = jnp.zeros((bsz, HG_HEADS, HEAD_DIM, HEAD_DIM), F32)
    else:
        src, sel_pages, tbl, win_cache, s0 = past
        q_start = tbl.shape[1] * PAGE_ROWS
    kvc = _compress(src, tbl, w["nsa_phi_pos"], tabs["phi_w1"], w["nsa_phi_w2"], k_norm, j)
    nb = kvc.shape[2]
    bias_c = _bias_cmp_table(rel_bias, q_start, t_len, nb)
    o_c, sel = _nsa_cmp(q3, kvc, bias_c, gates3, q_start=q_start, extra=past is not None)
    if past is None:
        selb3 = selb.reshape(bsz, t_len, kv_w)
        winb3 = winb.reshape(bsz, t_len, kv_w)
        o_s = _flash("nsa_sel", q3, selb3, 0, NSA_KV_HEADS, NSA_KV_HEADS, bias_tiles=tabs["tiles"], bm=sel,
                     gates3=gates3, blk=NSA_BLOCK, gate_branch=1)
        o_w = _flash("nsa_win", q3, winb3, 0, NSA_KV_HEADS, NSA_KV_HEADS, bias_tiles=tabs["tiles"],
                     gates3=gates3, gate_branch=2)
        keep = min(NSA_WINDOW, t_len)
        win_buf = win_new.reshape(bsz, t_len, kv_w)[:, t_len - keep:]
    else:
        rbt = rel_bias.T.reshape(NSA_KV_HEADS, NSA_GROUP, REL_BUCKETS)
        sel_new3 = sel_new.reshape(bsz, t_len, kv_w)
        win_new3 = win_new.reshape(bsz, t_len, kv_w)
        idx = sel.reshape(bsz * NSA_KV_HEADS, NSA_TOPN)
        o_s = _nsa_sel_step(q3, sel_pages, tbl, idx, sel_new3, rbt, gates3, t_pos=q_start)
        wc = win_cache.reshape(win_cache.shape[0] * win_cache.shape[1], win_cache.shape[2], kv_w)
        o_w = _nsa_win_step(q3, wc, j * bsz, win_new3, rbt, gates3)
        win_all = jnp.concatenate([wc[j * bsz:(j + 1) * bsz], win_new3], axis=1)
        keep = min(NSA_WINDOW, win_all.shape[1])
        win_buf = win_all[:, win_all.shape[1] - keep:]
    a_nsa = _sum3_cast(o_c.reshape(m, qw), o_s.reshape(m, qw), o_w.reshape(m, qw))
    o_hg, s_new = _hgrn(ph.reshape(bsz, t_len, ph.shape[1]), tabs["lb_rows"][j], w["hg_o_norm"][j], s0)
    kv_shape = (bsz, t_len, 2, NSA_KV_HEADS, HEAD_DIM)
    new = (cmp_new.reshape(kv_shape), sel_new.reshape(kv_shape),
           win_buf.reshape(bsz, win_buf.shape[1], 2, NSA_KV_HEADS, HEAD_DIM), s_new)
    return [a_nsa, o_hg.reshape(m, HG_HEADS * HEAD_DIM)], new


ODD_HEADS = MOBA_HEADS
ODD_QW = ODD_HEADS * HEAD_DIM
ODD_KVW = 2 * ODD_HEADS * HEAD_DIM


def _odd_post_kernel(pm_ref, fz_ref, mg_ref, fg_ref, fb_ref, qm_ref, moba_ref, mobab_ref, qf_ref, fox_ref,
                     foxb_ref, logf_ref):
    scale = HEAD_DIM ** -0.5
    off = 0
    for g_ref, q_ref, kv_ref, kvb_ref in ((mg_ref, qm_ref, moba_ref, mobab_ref), (fg_ref, qf_ref, fox_ref, foxb_ref)):
        qg = g_ref[0:1, :]
        kg = g_ref[1:2, :]
        for h in range(ODD_HEADS):
            sl = slice(h * HEAD_DIM, (h + 1) * HEAD_DIM)
            q_ref[:, sl] = (_rms_rows(pm_ref[:, off + h * HEAD_DIM: off + (h + 1) * HEAD_DIM], qg) * scale
                            ).astype(q_ref.dtype)
        off += ODD_QW
        for c in range(2 * ODD_HEADS):
            src = pm_ref[:, off + c * HEAD_DIM: off + (c + 1) * HEAD_DIM]
            val = _rms_rows(src, kg) if c < ODD_HEADS else src
            kv_ref[:, c * HEAD_DIM:(c + 1) * HEAD_DIM] = val
            kvb_ref[:, c * HEAD_DIM:(c + 1) * HEAD_DIM] = val.astype(BF16)
        off += ODD_KVW
    logf_ref[...] = _log_sigmoid(fz_ref[...] + fb_ref[...])


def _odd_post(pm, fz, moba_qk, fox_qk, fb):
    m = pm.shape[0]
    tm = min(m, ROW_TILE // 2)
    row = lambda w: pl.BlockSpec((tm, w), lambda i: (i, 0))
    full = lambda a: pl.BlockSpec(a.shape, lambda i: (0,) * a.ndim)
    return pl.pallas_call(
        _odd_post_kernel,
        grid=(m // tm,),
        in_specs=[row(pm.shape[1]), row(LANES), full(moba_qk), full(fox_qk), full(fb)],
        out_specs=[row(ODD_QW), row(ODD_KVW), row(ODD_KVW), row(ODD_QW), row(ODD_KVW), row(ODD_KVW), row(LANES)],
        out_shape=[jax.ShapeDtypeStruct((m, ODD_QW), BF16), jax.ShapeDtypeStruct((m, ODD_KVW), F32),
                   jax.ShapeDtypeStruct((m, ODD_KVW), BF16), jax.ShapeDtypeStruct((m, ODD_QW), BF16),
                   jax.ShapeDtypeStruct((m, ODD_KVW), F32), jax.ShapeDtypeStruct((m, ODD_KVW), BF16),
                   jax.ShapeDtypeStruct((m, LANES), F32)],
        compiler_params=_cparams(1),
        name="odd_post",
    )(pm, fz, moba_qk, fox_qk, fb)


def _topk_rank(score, n):
    idx = lax.broadcasted_iota(jnp.int32, (1, n), 1)
    rank = jnp.zeros(score.shape, jnp.int32)
    for mcol in range(n):
        sm = score[:, mcol:mcol + 1]
        ahead = (sm > score) | ((sm == score) & (mcol < idx))
        rank = rank + ahead.astype(jnp.int32)
    return rank


def _moba_gate_kernel(q_ref, k_ref, bm_ref, *, t_len, nbl):
    k_mean = jnp.mean(k_ref[...].reshape(nbl, MOBA_BLOCK, HEAD_DIM), axis=1)
    gate = lax.dot_general(q_ref[...], k_mean.astype(BF16), (((1,), (1,)), ((), ())), preferred_element_type=F32)
    cur = lax.broadcasted_iota(jnp.int32, (t_len, 1), 0) // MOBA_BLOCK
    blk = lax.broadcasted_iota(jnp.int32, (1, nbl), 1)
    past_ok = blk < cur
    gate = jnp.where(past_ok, gate, -jnp.inf)
    rank = _topk_rank(gate, nbl)
    sel = (rank < MOBA_TOPK) & past_ok & (jnp.abs(gate) < jnp.inf)
    bm_ref[...] = (sel | (blk == cur)).astype(F32)


def _moba_gate(qm3, moba_new3):
    bsz, t_len, _ = qm3.shape
    assert t_len % MOBA_BLOCK == 0
    nbl = t_len // MOBA_BLOCK
    return pl.pallas_call(
        functools.partial(_moba_gate_kernel, t_len=t_len, nbl=nbl),
        grid=(bsz, MOBA_HEADS),
        in_specs=[pl.BlockSpec((None, t_len, HEAD_DIM), lambda b, h: (b, 0, h)),
                  pl.BlockSpec((None, t_len, HEAD_DIM), lambda b, h: (b, 0, h))],
        out_specs=pl.BlockSpec((None, None, t_len, nbl), lambda b, h: (b, h, 0, 0)),
        out_shape=jax.ShapeDtypeStruct((bsz, MOBA_HEADS, t_len, nbl), F32),
        compiler_params=_cparams(2),
        name="moba_gate",
    )(qm3, moba_new3)


def _cumsum_kernel(x_ref, o_ref, *, t_len):
    n = Q_TILE
    upper = (lax.broadcasted_iota(jnp.int32, (n, n), 0) <= lax.broadcasted_iota(jnp.int32, (n, n), 1)).astype(F32)
    carry = jnp.zeros((FOX_HEADS, 1), F32)
    for c in range(t_len // n):
        xt = x_ref[c * n:(c + 1) * n, :].T[0:FOX_HEADS, :]
        cum = jnp.dot(xt, upper, preferred_element_type=F32, precision=lax.Precision.HIGHEST) + carry
        o_ref[:, c * n:(c + 1) * n] = cum
        carry = cum[:, n - 1:n]


def _cumsum_heads(logf3):
    bsz, t_len, _ = logf3.shape
    assert t_len % Q_TILE == 0
    return pl.pallas_call(
        functools.partial(_cumsum_kernel, t_len=t_len),
        grid=(bsz,),
        in_specs=[pl.BlockSpec((None, t_len, LANES), lambda b: (b, 0, 0))],
        out_specs=pl.BlockSpec((None, FOX_HEADS, t_len), lambda b: (b, 0, 0)),
        out_shape=jax.ShapeDtypeStruct((bsz, FOX_HEADS, t_len), F32),
        compiler_params=_cparams(1),
        name="fox_cumsum",
    )(logf3)


MOBA_GATE_PAGES_PER_STEP = 8


def _moba_gate_step_kernel(*refs, pps, nblk, ppb):
    q_ref = refs[1]
    k_refs = refs[2:2 + pps]
    idx_ref, gate_ref = refs[2 + pps:]
    st = pl.program_id(1)

    @pl.when(st == 0)
    def _():
        gate_ref[...] = jnp.zeros_like(gate_ref)

    q8 = _group_rows(q_ref, MOBA_HEADS)
    lane = lax.broadcasted_iota(jnp.int32, (1, nblk), 1)
    for blk_in_step in range(pps // ppb):
        ksum = jnp.zeros((MOBA_HEADS, HEAD_DIM), F32)
        for p in range(ppb):
            ksum = ksum + jnp.sum(k_refs[blk_in_step * ppb + p][...], axis=0)
        gcol = jnp.sum(q8 * (ksum * (1.0 / MOBA_BLOCK)), axis=-1, keepdims=True)
        gate_ref[...] = jnp.where(lane == st * (pps // ppb) + blk_in_step, gcol, gate_ref[...])

    @pl.when(st == pl.num_programs(1) - 1)
    def _():
        gate = gate_ref[...]
        rank = _topk_rank(gate, nblk)
        blk = lax.broadcasted_iota(jnp.int32, (1, nblk), 1)
        lane = lax.broadcasted_iota(jnp.int32, (MOBA_HEADS, MOBA_TOPK), 1)
        out = jnp.full((MOBA_HEADS, MOBA_TOPK), -1, jnp.int32)
        for r in range(min(MOBA_TOPK, nblk)):
            hit = (rank == r) & (jnp.abs(gate) < jnp.inf)
            idx_r = jnp.sum(jnp.where(hit, blk + 1, 0), axis=-1, keepdims=True) - 1
            out = jnp.where(lane == r, idx_r, out)
        idx_ref[...] = out


def _moba_gate_step(qm3, pages4, tbl):
    bsz, npg = tbl.shape
    ppb = MOBA_BLOCK // PAGE_ROWS
    pps = MOBA_GATE_PAGES_PER_STEP
    assert npg % pps == 0 and pps % ppb == 0

    def key_spec(p):
        return pl.BlockSpec((None, PAGE_ROWS, MOBA_HEADS, HEAD_DIM), lambda b, st, t: (t[b, st * pps + p], 0, 0, 0))

    grid_spec = pltpu.PrefetchScalarGridSpec(
        num_scalar_prefetch=1,
        grid=(bsz, npg // pps),
        in_specs=[pl.BlockSpec((None, 1, ODD_QW), lambda b, st, t: (b, 0, 0))] + [key_spec(p) for p in range(pps)],
        out_specs=pl.BlockSpec((None, MOBA_HEADS, MOBA_TOPK), lambda b, st, t: (b, 0, 0)),
        scratch_shapes=[pltpu.VMEM((MOBA_HEADS, npg // ppb), F32)],
    )
    return pl.pallas_call(
        functools.partial(_moba_gate_step_kernel, pps=pps, nblk=npg // ppb, ppb=ppb),
        grid_spec=grid_spec,
        out_shape=jax.ShapeDtypeStruct((bsz, MOBA_HEADS, MOBA_TOPK), jnp.int32),
        compiler_params=_cparams(2),
        name="moba_gate_step",
    )(tbl, qm3, *([pages4] * pps))


def _moba_attn_step_kernel(*refs, t_pos, ppb):
    idx_ref, q_ref = refs[1], refs[2]
    kv_refs = refs[3:3 + ppb]
    kn_ref, vn_ref, rbt_ref, o_ref, m_ref, l_ref, acc_ref = refs[3 + ppb:]
    b = pl.program_id(0)
    h = pl.program_id(1)
    s_id = pl.program_id(2)
    sub = 8
    q = q_ref[...].astype(F32)
    rbt = rbt_ref[...]
    col = lambda bk: rbt[:, bk:bk + 1]

    @pl.when(s_id == 0)
    def _():
        s_self = jnp.sum(q * kn_ref[...], axis=-1, keepdims=True) + col(0)
        m_ref[...] = jnp.broadcast_to(s_self, m_ref.shape)
        l_ref[...] = jnp.ones_like(l_ref)
        acc_ref[...] = jnp.broadcast_to(vn_ref[...], acc_ref.shape)

    blk_id = idx_ref[b * MOBA_HEADS + h, s_id]

    @pl.when(blk_id >= 0)
    def _():
        q8 = jnp.broadcast_to(q, (sub, HEAD_DIM)).astype(BF16)
        for pg in range(ppb):
            kt = kv_refs[pg][pl.ds(h, PAGE_ROWS, stride=2 * MOBA_HEADS), :].astype(BF16)
            vt = kv_refs[pg][pl.ds(MOBA_HEADS + h, PAGE_ROWS, stride=2 * MOBA_HEADS), :].astype(BF16)
            s = lax.dot_general(q8, kt, (((1,), (1,)), ((), ())), preferred_element_type=F32)
            key_pos = blk_id * MOBA_BLOCK + pg * PAGE_ROWS + lax.broadcasted_iota(jnp.int32, (sub, PAGE_ROWS), 1)
            s = s + _bucket_bias(t_pos - key_pos, col)
            m_old = m_ref[...]
            m_new = jnp.maximum(m_old, jnp.max(s, axis=-1, keepdims=True))
            p = jnp.exp(s - m_new)
            alpha = jnp.exp(m_old - m_new)
            l_ref[...] = alpha * l_ref[...] + jnp.sum(p, axis=-1, keepdims=True)
            acc_ref[...] = alpha * acc_ref[...] + jnp.dot(p.astype(BF16), vt, preferred_element_type=F32)
            m_ref[...] = m_new

    @pl.when(s_id == pl.num_programs(2) - 1)
    def _():
        o_ref[...] = (acc_ref[0:1, :] / l_ref[0:1, :]).astype(o_ref.dtype)


def _moba_attn_step(qm3, pages3, tbl, idx, moba_new3, rbt3, *, t_pos):
    bsz = qm3.shape[0]
    ppb = MOBA_BLOCK // PAGE_ROWS

    def cache_map(pg):
        def index_map(b, h, s, tbl_ref, idx_ref):
            blk = jnp.maximum(idx_ref[b * MOBA_HEADS + h, s], 0)
            return (tbl_ref[b, blk * ppb + pg], 0, 0)
        return index_map

    head = lambda off: pl.BlockSpec((None, 1, HEAD_DIM), lambda b, h, s, t, i: (b, 0, off + h))
    grid_spec = pltpu.PrefetchScalarGridSpec(
        num_scalar_prefetch=2,
        grid=(bsz, MOBA_HEADS, MOBA_TOPK),
        in_specs=[head(0)]
        + [pl.BlockSpec((None, PAGE_ROWS * 2 * MOBA_HEADS, HEAD_DIM), cache_map(pg)) for pg in range(ppb)]
        + [head(0), head(MOBA_HEADS),
           pl.BlockSpec((None, 1, REL_BUCKETS), lambda b, h, s, t, i: (h, 0, 0))],
        out_specs=head(0),
        scratch_shapes=[pltpu.VMEM((8, 1), F32), pltpu.VMEM((8, 1), F32), pltpu.VMEM((8, HEAD_DIM), F32)],
    )
    return pl.pallas_call(
        functools.partial(_moba_attn_step_kernel, t_pos=t_pos, ppb=ppb),
        grid_spec=grid_spec,
        out_shape=jax.ShapeDtypeStruct((bsz, 1, ODD_QW), BF16),
        compiler_params=_cparams(3),
        name="moba_attn_step",
    )(tbl, idx, qm3, *([pages3] * ppb), moba_new3, moba_new3, rbt3)


FOX_PAGES_PER_STEP = 4


def _fox_step_kernel(*refs, pps):
    q_ref = refs[1]
    k_refs = refs[2:2 + pps]
    v_refs = refs[2 + pps:2 + 2 * pps]
    lf_refs = refs[2 + 2 * pps:2 + 3 * pps]
    kn_ref, vn_ref, lfn_ref, o_ref, m_ref, l_ref, acc_ref, carry_ref = refs[2 + 3 * pps:]
    st = pl.program_id(1)
    n = PAGE_ROWS
    nh = FOX_HEADS
    q8 = _group_rows(q_ref, nh)
    lane8 = lax.broadcasted_iota(jnp.int32, (nh, LANES), 1)
    row8 = lax.broadcasted_iota(jnp.int32, (nh, LANES), 0)

    @pl.when(st == 0)
    def _():
        m_ref[...] = jnp.sum(q8 * _group_rows(kn_ref, nh), axis=-1, keepdims=True)
        l_ref[...] = jnp.ones_like(l_ref)
        acc_ref[...] = _group_rows(vn_ref, nh)
        carry_ref[...] = jnp.sum(jnp.where(lane8 == row8, lfn_ref[...], 0.0), axis=-1, keepdims=True)

    ones = jnp.ones((HEAD_DIM, LANES), BF16)
    after_t = (lax.broadcasted_iota(jnp.int32, (n, n), 0) > lax.broadcasted_iota(jnp.int32, (n, n), 1)).astype(F32)
    on_diag = (lax.broadcasted_iota(jnp.int32, (n, nh, LANES), 0)
               == lax.broadcasted_iota(jnp.int32, (n, nh, LANES), 2))
    for p in range(pps):
        lf = lf_refs[p][...]
        bias = carry_ref[...] + jnp.dot(lf, after_t, preferred_element_type=F32, precision=lax.Precision.HIGHEST)
        carry_ref[...] += jnp.sum(lf, axis=-1, keepdims=True)
        prod = (k_refs[p][...] * q8[None]).reshape(n * nh, HEAD_DIM).astype(BF16)
        qk_rep = jnp.dot(prod, ones, preferred_element_type=F32).reshape(n, nh, LANES)
        s = jnp.sum(jnp.where(on_diag, qk_rep, 0.0), axis=0) + bias
        m_old = m_ref[...]
        m_new = jnp.maximum(m_old, jnp.max(s, axis=-1, keepdims=True))
        prob = jnp.exp(s - m_new)
        alpha = jnp.exp(m_old - m_new)
        l_ref[...] = alpha * l_ref[...] + jnp.sum(prob, axis=-1, keepdims=True)
        m_ref[...] = m_new
        spread = jnp.where(on_diag, prob[None], 0.0).reshape(n * nh, LANES).astype(BF16)
        p_rep = jnp.dot(spread, ones, preferred_element_type=F32).reshape(n, nh, HEAD_DIM)
        acc_ref[...] = alpha * acc_ref[...] + jnp.sum(p_rep * v_refs[p][...], axis=0)

    @pl.when(st == pl.num_programs(1) - 1)
    def _():
        _store_group_rows(o_ref, acc_ref[...] / l_ref[...])


def _fox_step(qf3, pages4, logf_pages, tbl, fox_new3, logf_new3):
    bsz, npg = tbl.shape
    pps = math.gcd(FOX_PAGES_PER_STEP, npg)
    page = lambda p: (lambda b, st, t: t[b, npg - 1 - (st * pps + p)])
    kv_spec = lambda p, part: pl.BlockSpec((None, PAGE_ROWS, FOX_HEADS, HEAD_DIM),
                                           lambda b, st, t: (page(p)(b, st, t), 0, part, 0))
    lf_spec = lambda p: pl.BlockSpec((None, FOX_HEADS, PAGE_ROWS), lambda b, st, t: (page(p)(b, st, t), 0, 0))
    new = lambda col: pl.BlockSpec((None, 1, ODD_QW), lambda b, st, t: (b, 0, col))
    grid_spec = pltpu.PrefetchScalarGridSpec(
        num_scalar_prefetch=1,
        grid=(bsz, npg // pps),
        in_specs=[new(0)] + [kv_spec(p, 0) for p in range(pps)] + [kv_spec(p, 1) for p in range(pps)]
        + [lf_spec(p) for p in range(pps)]
        + [new(0), new(1), pl.BlockSpec((None, 1, LANES), lambda b, st, t: (b, 0, 0))],
        out_specs=new(0),
        scratch_shapes=[pltpu.VMEM((FOX_HEADS, 1), F32), pltpu.VMEM((FOX_HEADS, 1), F32),
                        pltpu.VMEM((FOX_HEADS, HEAD_DIM), F32), pltpu.VMEM((FOX_HEADS, 1), F32)],
    )
    return pl.pallas_call(
        functools.partial(_fox_step_kernel, pps=pps),
        grid_spec=grid_spec,
        out_shape=jax.ShapeDtypeStruct((bsz, 1, ODD_QW), BF16),
        compiler_params=_cparams(2),
        name="fox_step",
    )(tbl, qf3, *([pages4] * (2 * pps)), *([logf_pages] * pps), fox_new3, fox_new3, logf_new3)


def _odd_proj(h, hr, w, j):
    n_main = 2 * (ODD_QW + ODD_KVW)
    w_in = w["w_in_odd"]
    hr_parts = None if hr is None else [hr]
    pm, pmr = _dense2([h], hr_parts, w_in, j, 0, n_main, tn=512, name="odd_in")
    w_fz = jnp.pad(w_in[j, :, n_main:n_main + FOX_HEADS], ((0, 0), (0, LANES - FOX_HEADS)))
    fz, fzr = _dense2([h], hr_parts, w_fz, None, 0, LANES, tn=LANES, name="odd_in_forget")
    return (pm, fz), (None if hr is None else (pmr, fzr))


def _odd_out(x2d, a_parts, xr2d, ar_parts, w, j):
    d = x2d.shape[1]
    return _dense2(a_parts, ar_parts, w["w_out_odd"], j, 0, d, tn=min(d, 512), mode="res", res=x2d,
                   r_res=xr2d, scale=1.0, name="odd_out")


def _odd_mixer(x2d, h, bsz, t_len, j, w, tabs, past):
    proj, _ = _odd_proj(h, None, w, j)
    a_parts, new = _odd_core(proj, bsz, t_len, j, w, tabs, past)
    x_new, _ = _odd_out(x2d, a_parts, None, None, w, j)
    return (x_new,) + new


def _odd_core(proj, bsz, t_len, j, w, tabs, past):
    pm, fz = proj
    m = pm.shape[0]
    fb = jnp.pad(w["fox_f_bias"][j].astype(F32), (0, LANES - FOX_HEADS)).reshape(1, LANES)
    qm, moba_new, mobab, qf, fox_new, foxb, logf = _odd_post(pm, fz, w["moba_qk_norm"][j], w["fox_qk_norm"][j], fb)
    qm3 = qm.reshape(bsz, t_len, ODD_QW)
    qf3 = qf.reshape(bsz, t_len, ODD_QW)
    logf3 = logf.reshape(bsz, t_len, LANES)
    if past is None:
        bm = _moba_gate(qm3, moba_new.reshape(bsz, t_len, ODD_KVW))
        o_m = _flash("moba", qm3, mobab.reshape(bsz, t_len, ODD_KVW), 0, MOBA_HEADS, MOBA_HEADS,
                     bias_tiles=tabs["tiles"], bm=bm, blk=MOBA_BLOCK, out_dtype=BF16)
        cum = _cumsum_heads(logf3).reshape(bsz, FOX_HEADS, 1, t_len)
        o_f = _flash("fox", qf3, foxb.reshape(bsz, t_len, ODD_KVW), 0, FOX_HEADS, FOX_HEADS, cum=cum,
                     out_dtype=BF16)
    else:
        moba_pages4, fox_pages, logf_pages, tbl = past
        t_pos = tbl.shape[1] * PAGE_ROWS
        idx = _moba_gate_step(qm3, moba_pages4, tbl)
        rbt3 = w["rel_bias"].T.reshape(MOBA_HEADS, 1, REL_BUCKETS)
        moba_pages3 = moba_pages4.reshape(moba_pages4.shape[0], PAGE_ROWS * 2 * MOBA_HEADS, HEAD_DIM)
        o_m = _moba_attn_step(qm3, moba_pages3, tbl, idx.reshape(bsz * MOBA_HEADS, MOBA_TOPK),
                              moba_new.reshape(bsz, t_len, ODD_KVW), rbt3, t_pos=t_pos)
        o_f = _fox_step(qf3, fox_pages, logf_pages, tbl, fox_new.reshape(bsz, t_len, ODD_KVW), logf3)
    kv_shape = (bsz, t_len, 2, ODD_HEADS, HEAD_DIM)
    new = (moba_new.reshape(kv_shape), fox_new.reshape(kv_shape), logf3[:, :, :FOX_HEADS])
    return [o_m.reshape(m, ODD_QW), o_f.reshape(m, ODD_QW)], new


def _trunk(x_prompt, x_sample, w, tabs, caches):
    bp, tp, d = x_prompt.shape
    bs, ts, _ = x_sample.shape
    depth = w["norm_mix"].shape[0]
    xp = x_prompt.reshape(bp * tp, d)
    xs = x_sample.reshape(bs * ts, d)
    pe, po, se, so = [], [], [], []
    for layer in range(depth):
        xp, xs = _ffn(xp, xs, w["norm_ffn1"], w["w_ffn1_in"], w["w_ffn1_out"], layer)
        hp = _rms_cast(xp, w["norm_mix"], layer)
        hs = _rms_cast(xs, w["norm_mix"], layer)
        j = layer // 2
        tbl = caches["tbl"] + j * caches["n_phys"]
        if layer % 2 == 0:
            proj_p, proj_s = _even_proj(hp, hs, w, j)
            ap, new_p = _even_core(proj_p, bp, tp, j, w, tabs, None)
            past = (caches["cmp"], caches["sel"], tbl, caches["win"], caches["hgrn"][j])
            a_s, new_s = _even_core(proj_s, bs, ts, j, w, tabs, past)
            xp, xs = _even_out(xp, ap, xs, a_s, w, j)
            pe.append(new_p)
            se.append(new_s)
        else:
            proj_p, proj_s = _odd_proj(hp, hs, w, j)
            ap, new_p = _odd_core(proj_p, bp, tp, j, w, tabs, None)
            past = (caches["moba"], caches["fox"], caches["logf"], tbl)
            a_s, new_s = _odd_core(proj_s, bs, ts, j, w, tabs, past)
            xp, xs = _odd_out(xp, ap, xs, a_s, w, j)
            po.append(new_p)
            so.append(new_s)
        xp, xs = _ffn(xp, xs, w["norm_ffn2"], w["w_ffn2_in"], w["w_ffn2_out"], layer)
    return xp.reshape(bp, tp, d), xs.reshape(bs, ts, d), pe, po, se, so


def kernel(x_prompt, x_sample, cache_nsa_cmp, cache_nsa_sel, cache_moba, cache_fox, cache_fox_logf,
           cache_nsa_win, state_hgrn, page_table, norm_ffn1, w_ffn1_in, w_ffn1_out, norm_mix, norm_ffn2,
           w_ffn2_in, w_ffn2_out, rel_bias, w_in_even, w_out_even, nsa_q_norm, nsa_k_norm, nsa_phi_pos,
           nsa_phi_w1, nsa_phi_w2, hg_lb, hg_o_norm, w_in_odd, w_out_odd, fox_f_bias, moba_qk_norm,
           fox_qk_norm):
    w = dict(norm_ffn1=norm_ffn1, w_ffn1_in=w_ffn1_in, w_ffn1_out=w_ffn1_out, norm_mix=norm_mix,
             norm_ffn2=norm_ffn2, w_ffn2_in=w_ffn2_in, w_ffn2_out=w_ffn2_out, rel_bias=rel_bias,
             w_in_even=w_in_even, w_out_even=w_out_even, nsa_q_norm=nsa_q_norm, nsa_k_norm=nsa_k_norm,
             nsa_phi_pos=nsa_phi_pos, nsa_phi_w1=nsa_phi_w1, nsa_phi_w2=nsa_phi_w2, hg_lb=hg_lb,
             hg_o_norm=hg_o_norm, w_in_odd=w_in_odd, w_out_odd=w_out_odd, fox_f_bias=fox_f_bias,
             moba_qk_norm=moba_qk_norm, fox_qk_norm=fox_qk_norm)
    tabs = _tables(w)
    n_phys = cache_nsa_cmp.shape[1]
    n_layers = cache_nsa_cmp.shape[0]
    rows3 = lambda pool: pool.reshape(n_layers * n_phys, -1, HEAD_DIM)
    rows4 = lambda pool: pool.reshape(n_layers * n_phys, PAGE_ROWS, -1, HEAD_DIM)
    logf_t = jnp.swapaxes(cache_fox_logf.reshape(n_layers * n_phys, PAGE_ROWS, FOX_HEADS), 1, 2)
    caches = dict(cmp=rows3(cache_nsa_cmp), sel=rows3(cache_nsa_sel), moba=rows4(cache_moba),
                  fox=rows4(cache_fox), logf=logf_t, win=cache_nsa_win, hgrn=state_hgrn,
                  tbl=page_table.astype(jnp.int32), n_phys=n_phys)
    y_prompt, y_sample, pe, po, se, so = _trunk(x_prompt, x_sample, w, tabs, caches)

    stack = lambda items, i: jnp.stack([it[i] for it in items])
    return (y_prompt, y_sample,
            stack(pe, 0), stack(pe, 1), stack(pe, 2), stack(pe, 3),
            stack(po, 0), stack(po, 1), stack(po, 2),
            stack(se, 0), stack(se, 1), stack(se, 2), stack(se, 3),
            stack(so, 0), stack(so, 1), stack(so, 2))
```

```python
import functools
import math

import jax
import jax.numpy as jnp
import numpy as np
from jax import lax
from jax.experimental import pallas as pl
from jax.experimental.pallas import tpu as pltpu

F32 = jnp.float32
BF16 = jnp.bfloat16

HEAD_DIM = 128
NSA_HEADS = 8
NSA_KV_HEADS = 2
NSA_GROUP = NSA_HEADS // NSA_KV_HEADS
NSA_BLOCK = 64
NSA_TOPN = 16
NSA_WINDOW = 512
NSA_PHI_HIDDEN = 2 * HEAD_DIM
NSA_FORCE_SCORE = 1.0e4
HG_HEADS = 8
MOBA_HEADS = 8
MOBA_BLOCK = 256
MOBA_TOPK = 3
FOX_HEADS = 8
REL_BUCKETS = 32
REL_MAX_DIST = 128
EPS = 1e-6
PAGE_ROWS = 128

LANES = 128
VMEM_LIMIT_BYTES = 56 * 1024 * 1024
ROW_TILE = 512
DEEP_K = 2048
Q_TILE = 128
FLASH_TILE = 256
FLASH_HEADS_PER_STEP = 4
NEG_BIG = -1e30


def _cparams(n_axes):
    return pltpu.CompilerParams(dimension_semantics=("arbitrary",) * n_axes,
                                vmem_limit_bytes=VMEM_LIMIT_BYTES)


def _rel_bucket(dist):
    n = jnp.maximum(dist, 0)
    exact = REL_BUCKETS // 2
    nf = jnp.maximum(n, 1).astype(F32)
    big = exact + (jnp.log(nf / exact) / math.log(REL_MAX_DIST / exact) * (REL_BUCKETS - exact)).astype(jnp.int32)
    return jnp.where(n < exact, n, jnp.minimum(big, REL_BUCKETS - 1))


def _bucket_bias(dist, table_rows):
    bucket = _rel_bucket(dist)
    out = jnp.zeros(dist.shape, F32) + table_rows(0)
    for b in range(1, REL_BUCKETS):
        out = jnp.where(bucket == b, table_rows(b), out)
    return out


def _rms_rows(x, g):
    return x * lax.rsqrt(jnp.mean(x * x, axis=-1, keepdims=True) + EPS) * g


def _log_sigmoid(z):
    return jnp.minimum(z, 0.0) - jnp.log1p(jnp.exp(-jnp.abs(z)))


def _sigmoid(z):
    return 1.0 / (1.0 + jnp.exp(-z))


def _rms_cast_kernel(x_ref, g_ref, o_ref):
    o_ref[...] = _rms_rows(x_ref[...], g_ref[...]).astype(o_ref.dtype)


def _rms_cast(x2d, g_stack, layer):
    m, d = x2d.shape
    tm = min(m, ROW_TILE)
    g3 = g_stack.reshape(g_stack.shape[0], 1, d)
    return pl.pallas_call(
        _rms_cast_kernel,
        grid=(m // tm,),
        in_specs=[pl.BlockSpec((tm, d), lambda i: (i, 0)),
                  pl.BlockSpec((None, 1, d), lambda i: (layer, 0, 0))],
        out_specs=pl.BlockSpec((tm, d), lambda i: (i, 0)),
        out_shape=jax.ShapeDtypeStruct((m, d), BF16),
        compiler_params=_cparams(1),
        name="rms_cast",
    )(x2d, g3)


def _dense_kernel(*refs, n_a, mode, scale, rider):
    pos = 0
    a_refs = refs[pos:pos + n_a]
    pos += n_a
    ra_refs = refs[pos:pos + n_a] if rider else ()
    pos += len(ra_refs)
    w_ref = refs[pos]
    pos += 1
    w2_ref = res_ref = rres_ref = ro_ref = None
    if mode == "swiglu":
        w2_ref = refs[pos]
        pos += 1
    if mode == "res":
        res_ref = refs[pos]
        pos += 1
        if rider:
            rres_ref = refs[pos]
            pos += 1
    o_ref = refs[pos]
    pos += 1
    if rider:
        ro_ref = refs[pos]
        pos += 1
    wb_ref = refs[pos]
    pos += 1
    wb2_ref = refs[pos] if mode == "swiglu" else None

    def apply(in_refs, r_ref, out_ref):
        a = in_refs[0][...] if n_a == 1 else jnp.concatenate([r[...] for r in in_refs], axis=-1)
        y = jnp.dot(a, wb_ref[...], preferred_element_type=F32)
        if mode == "swiglu":
            y2 = jnp.dot(a, wb2_ref[...], preferred_element_type=F32)
            y = y * _sigmoid(y) * y2
        elif mode == "res":
            y = r_ref[...] + scale * y
        out_ref[...] = y.astype(out_ref.dtype)

    @pl.when(pl.program_id(1) == 0)
    def _():
        wb_ref[...] = w_ref[...].astype(BF16)
        if mode == "swiglu":
            wb2_ref[...] = w2_ref[...].astype(BF16)
        if rider:
            apply(ra_refs, rres_ref, ro_ref)

    apply(a_refs, res_ref, o_ref)


def _dense(a_parts, w, lead, col0, n_out, *, tn, mode="plain", res=None, scale=1.0,
           out_dtype=F32, col0_b=None, rider=None, name="dense"):
    m = a_parts[0].shape[0]
    k = sum(a.shape[1] for a in a_parts)
    tm = min(m, 2 * ROW_TILE if k <= DEEP_K else ROW_TILE)
    assert m % tm == 0 and n_out % tn == 0 and col0 % tn == 0
    if w.ndim == 3:
        wblock = (None, k, tn)

        def wmap(off):
            return lambda j, i: (lead, 0, j + off)
    else:
        wblock = (k, tn)

        def wmap(off):
            return lambda j, i: (0, j + off)
    in_specs = [pl.BlockSpec((tm, a.shape[1]), lambda j, i: (i, 0)) for a in a_parts]
    args = list(a_parts)
    ms = 0
    if rider is not None:
        r_parts, r_res = rider
        ms = r_parts[0].shape[0]
        assert [a.shape[1] for a in r_parts] == [a.shape[1] for a in a_parts]
        in_specs += [pl.BlockSpec((ms, a.shape[1]), lambda j, i: (0, 0)) for a in r_parts]
        args += list(r_parts)
    in_specs.append(pl.BlockSpec(wblock, wmap(col0 // tn)))
    args.append(w)
    scratch = [pltpu.VMEM((k, tn), BF16)]
    if mode == "swiglu":
        assert col0_b % tn == 0
        in_specs.append(pl.BlockSpec(wblock, wmap(col0_b // tn)))
        args.append(w)
        scratch.append(pltpu.VMEM((k, tn), BF16))
    if mode == "res":
        in_specs.append(pl.BlockSpec((tm, tn), lambda j, i: (i, j)))
        args.append(res)
        if rider is not None:
            in_specs.append(pl.BlockSpec((ms, tn), lambda j, i: (0, j)))
            args.append(r_res)
    out_specs = pl.BlockSpec((tm, tn), lambda j, i: (i, j))
    out_shape = jax.ShapeDtypeStruct((m, n_out), out_dtype)
    if rider is not None:
        out_specs = [out_specs, pl.BlockSpec((ms, tn), lambda j, i: (0, j))]
        out_shape = [out_shape, jax.ShapeDtypeStruct((ms, n_out), out_dtype)]
    return pl.pallas_call(
        functools.partial(_dense_kernel, n_a=len(a_parts), mode=mode, scale=scale, rider=rider is not None),
        grid=(n_out // tn, m // tm),
        in_specs=in_specs,
        out_specs=out_specs,
        out_shape=out_shape,
        scratch_shapes=scratch,
        compiler_params=_cparams(2),
        name=name,
    )(*args)


def _dense2(a_parts, r_parts, *args, res=None, r_res=None, **kw):
    if r_parts is None:
        return _dense(a_parts, *args, res=res, **kw), None
    return _dense(a_parts, *args, res=res, rider=(r_parts, r_res), **kw)


def _ffn(x2d, xr2d, norm_g, w_in, w_out, layer):
    d_ff = w_out.shape[1]
    xn = _rms_cast(x2d, norm_g, layer)
    xrn = None if xr2d is None else [_rms_cast(xr2d, norm_g, layer)]
    h, hr = _dense2([xn], xrn, w_in, layer, 0, d_ff, tn=512, mode="swiglu", col0_b=d_ff,
                    out_dtype=BF16, name="ffn_in")
    return _dense2([h], None if hr is None else [hr], w_out, layer, 0, x2d.shape[1], tn=512, mode="res",
                   res=x2d, r_res=xr2d, scale=0.5, name="ffn_out")


def _bias_tiles(rel_bias):
    i = jnp.arange(FLASH_TILE)[:, None]
    j = jnp.arange(FLASH_TILE)[None, :]
    dist = jnp.stack([i - j, FLASH_TILE + i - j])
    onehot = (_rel_bucket(dist)[..., None] == jnp.arange(REL_BUCKETS)).astype(F32)
    tiles = jnp.einsum("ktsb,bh->hkts", onehot, rel_bias.astype(F32), precision=lax.Precision.HIGHEST)
    return jnp.where((dist < 0)[None], NEG_BIG, tiles)


def _tables(w):
    lb_all = jnp.cumsum(jax.nn.softmax(w["hg_lb"].astype(F32), axis=0), axis=0)
    lb_all = lb_all - lb_all[0:1]
    lbh = lb_all.reshape(lb_all.shape[0], HG_HEADS, HEAD_DIM)
    lb_rows = jnp.stack([jnp.log(lbh), jnp.log1p(-lbh), 1.0 - lbh], axis=2)
    return {"tiles": _bias_tiles(w["rel_bias"]), "lb_rows": lb_rows, "phi_w1": w["nsa_phi_w1"].astype(BF16)}


def _bias_cmp_table(rel_bias, q_start, t_len, nb):
    t_pos = q_start + jnp.arange(t_len)
    dist = t_pos[:, None] - (jnp.arange(nb) * NSA_BLOCK + NSA_BLOCK - 1)[None, :]
    onehot = (_rel_bucket(dist)[..., None] == jnp.arange(REL_BUCKETS)).astype(F32)
    return jnp.einsum("tnb,bh->htn", onehot, rel_bias.astype(F32), precision=lax.Precision.HIGHEST)


def _even_post_kernel(pa_ref, graw_ref, qg_ref, kg_ref, q_ref, cmp_ref, sel_ref, win_ref,
                      selb_ref, winb_ref, gates_ref):
    scale = HEAD_DIM ** -0.5
    qg = qg_ref[...]
    for h in range(NSA_HEADS):
        sl = slice(h * HEAD_DIM, (h + 1) * HEAD_DIM)
        q_ref[:, sl] = (_rms_rows(pa_ref[:, sl], qg) * scale).astype(q_ref.dtype)
    base = NSA_HEADS * HEAD_DIM
    kv_w = 2 * NSA_KV_HEADS * HEAD_DIM
    cmp_ref[...] = pa_ref[:, base:base + kv_w]
    for which, (o_ref, ob_ref) in enumerate(((sel_ref, selb_ref), (win_ref, winb_ref))):
        off = base + (which + 1) * kv_w
        kg = kg_ref[which + 1:which + 2, :]
        for c in range(2 * NSA_KV_HEADS):
            src = pa_ref[:, off + c * HEAD_DIM: off + (c + 1) * HEAD_DIM]
            val = _rms_rows(src, kg) if c < NSA_KV_HEADS else src
            o_ref[:, c * HEAD_DIM:(c + 1) * HEAD_DIM] = val
            ob_ref[:, c * HEAD_DIM:(c + 1) * HEAD_DIM] = val.astype(BF16)
    gates_ref[...] = _sigmoid(graw_ref[...])


def _even_post(pa, graw, q_norm, k_norm):
    m = pa.shape[0]
    tm = min(m, ROW_TILE)
    kv_w = 2 * NSA_KV_HEADS * HEAD_DIM
    qw = NSA_HEADS * HEAD_DIM
    row = lambda w: pl.BlockSpec((tm, w), lambda i: (i, 0))
    full = lambda a: pl.BlockSpec(a.shape, lambda i: (0,) * a.ndim)
    qg = q_norm.reshape(1, HEAD_DIM)
    return pl.pallas_call(
        _even_post_kernel,
        grid=(m // tm,),
        in_specs=[row(pa.shape[1]), row(LANES), full(qg), full(k_norm)],
        out_specs=[row(qw), row(kv_w), row(kv_w), row(kv_w), row(kv_w), row(kv_w), row(LANES)],
        out_shape=[jax.ShapeDtypeStruct((m, qw), BF16)] + [jax.ShapeDtypeStruct((m, kv_w), F32)] * 3
        + [jax.ShapeDtypeStruct((m, kv_w), BF16)] * 2 + [jax.ShapeDtypeStruct((m, LANES), F32)],
        compiler_params=_cparams(1),
        name="even_post",
    )(pa, graw, qg, k_norm)


def _gelu_tanh(x):
    return 0.5 * x * (1.0 + jnp.tanh(math.sqrt(2.0 / math.pi) * (x + 0.044715 * (x * x * x))))


NSA_KV_COLS = 2 * NSA_KV_HEADS
CMP_PAGES_PER_STEP = 8
CMP_PAGES_PER_GROUP = 64


def _compress_mlp(x_of, nblk, pos_ref, w1_ref, w2_ref, kg_ref, o_ref, acc_ref):
    acc_ref[...] = jnp.zeros_like(acc_ref)

    def body(i2, carry):
        for c in range(NSA_KV_COLS):
            w = c // NSA_KV_HEADS
            xa = x_of(2 * i2, c) + pos_ref[w, pl.ds(2 * i2, 1), :]
            xb = x_of(2 * i2 + 1, c) + pos_ref[w, pl.ds(2 * i2 + 1, 1), :]
            x = jnp.concatenate([xa, xb], axis=-1).astype(BF16)
            wi = w1_ref[w, pl.ds(pl.multiple_of(i2 * 2 * HEAD_DIM, 2 * HEAD_DIM), 2 * HEAD_DIM), :]
            acc_ref[c] += jnp.dot(x, wi, preferred_element_type=F32)
        return carry

    lax.fori_loop(0, NSA_BLOCK // 2, body, 0)
    for c in range(NSA_KV_COLS):
        w, kh = divmod(c, NSA_KV_HEADS)
        hid = _gelu_tanh(acc_ref[c]).astype(BF16)
        y = jnp.dot(hid, w2_ref[w].astype(BF16), preferred_element_type=F32)
        if w == 0:
            y = _rms_rows(y, kg_ref[0:1, :])
        o_ref[w, :, kh * HEAD_DIM:(kh + 1) * HEAD_DIM] = y


def _compress_rows_kernel(*refs, nblk):
    x_refs = refs[:NSA_KV_COLS]
    pos_ref, w1_ref, w2_ref, kg_ref, o_ref, acc_ref = refs[NSA_KV_COLS:]

    def x_of(i, c):
        return x_refs[c][pl.ds(i, nblk, stride=NSA_BLOCK), :]

    _compress_mlp(x_of, nblk, pos_ref, w1_ref, w2_ref, kg_ref, o_ref, acc_ref)


def _compress_pages_kernel(*refs, pps, nblk):
    tbl_ref = refs[0]
    page_refs = refs[1:1 + pps]
    pos_ref, w1_ref, w2_ref, kg_ref, o_ref, xs_ref, acc_ref = refs[1 + pps:]
    st = pl.program_id(2)
    page_rows = PAGE_ROWS * NSA_KV_COLS
    for p in range(pps):
        xs_ref[pl.ds(pl.multiple_of((st * pps + p) * page_rows, page_rows), page_rows), :] = page_refs[p][...]

    @pl.when(st == pl.num_programs(2) - 1)
    def _():
        def x_of(i, c):
            return xs_ref[pl.ds(i * NSA_KV_COLS + c, nblk, stride=NSA_BLOCK * NSA_KV_COLS), :]

        _compress_mlp(x_of, nblk, pos_ref, w1_ref, w2_ref, kg_ref, o_ref, acc_ref)


def _compress(src, tbl, pos, w1b, w2, k_norm, j):
    weight_specs = lambda nidx: [
        pl.BlockSpec((None, 2, NSA_BLOCK, HEAD_DIM), lambda *a: (j, 0, 0, 0)),
        pl.BlockSpec((None, 2, NSA_BLOCK * HEAD_DIM, NSA_PHI_HIDDEN), lambda *a: (j, 0, 0, 0),
                     pipeline_mode=pl.Buffered(1)),
        pl.BlockSpec((None, 2, NSA_PHI_HIDDEN, HEAD_DIM), lambda *a: (j, 0, 0, 0)),
        pl.BlockSpec(k_norm.shape, lambda *a: (0, 0)),
    ]
    kvw = NSA_KV_HEADS * HEAD_DIM
    if tbl is None:
        bsz, t_len, _ = src.shape
        nblk = t_len // NSA_BLOCK
        return pl.pallas_call(
            functools.partial(_compress_rows_kernel, nblk=nblk),
            grid=(bsz,),
            in_specs=[pl.BlockSpec((None, t_len, HEAD_DIM), lambda b, c=c: (b, 0, c)) for c in range(NSA_KV_COLS)]
            + weight_specs(1),
            out_specs=pl.BlockSpec((2, None, nblk, kvw), lambda b: (0, b, 0, 0)),
            out_shape=jax.ShapeDtypeStruct((2, bsz, nblk, kvw), F32),
            scratch_shapes=[pltpu.VMEM((NSA_KV_COLS, nblk, NSA_PHI_HIDDEN), F32)],
            compiler_params=_cparams(1),
            name="nsa_compress_rows",
        )(*([src] * NSA_KV_COLS), pos, w1b, w2, k_norm)
    bsz, npg = tbl.shape
    pps = math.gcd(CMP_PAGES_PER_STEP, npg)
    ppg = math.gcd(CMP_PAGES_PER_GROUP, npg)
    bpp = PAGE_ROWS // NSA_BLOCK
    nblk = ppg * bpp
    page_rows = PAGE_ROWS * NSA_KV_COLS

    def page_spec(p):
        return pl.BlockSpec((None, page_rows, HEAD_DIM),
                            lambda b, grp, st, t: (t[b, grp * ppg + st * pps + p], 0, 0))

    grid_spec = pltpu.PrefetchScalarGridSpec(
        num_scalar_prefetch=1,
        grid=(bsz, npg // ppg, ppg // pps),
        in_specs=[page_spec(p) for p in range(pps)] + weight_specs(4),
        out_specs=pl.BlockSpec((2, None, nblk, kvw), lambda b, grp, st, t: (0, b, grp, 0)),
        scratch_shapes=[pltpu.VMEM((ppg * page_rows, HEAD_DIM), F32),
                        pltpu.VMEM((NSA_KV_COLS, nblk, NSA_PHI_HIDDEN), F32)],
    )
    return pl.pallas_call(
        functools.partial(_compress_pages_kernel, pps=pps, nblk=nblk),
        grid_spec=grid_spec,
        out_shape=jax.ShapeDtypeStruct((2, bsz, npg * bpp, kvw), F32),
        compiler_params=_cparams(3),
        name="nsa_compress_pages",
    )(tbl, *([src] * pps), pos, w1b, w2, k_norm)


def _nsa_cmp_kernel(q_ref, kc_ref, vc_ref, bias_ref, gates_ref, oc_ref, sel_ref, *, q_start, tq, nb, extra):
    kh = pl.program_id(1)
    qi = pl.program_id(2)
    t_pos = q_start + qi * tq + lax.broadcasted_iota(jnp.int32, (tq, 1), 0)
    blk = lax.broadcasted_iota(jnp.int32, (1, nb), 1)
    valid = t_pos >= blk * NSA_BLOCK + (NSA_BLOCK - 1)
    kc = kc_ref[...].astype(BF16)
    vc = vc_ref[...].astype(BF16)
    gates = gates_ref[...]
    imp = jnp.zeros((tq, nb), F32)
    for g in range(NSA_GROUP):
        qg = q_ref[:, g * HEAD_DIM:(g + 1) * HEAD_DIM]
        s = lax.dot_general(qg, kc, (((1,), (1,)), ((), ())), preferred_element_type=F32) + bias_ref[g]
        s = jnp.where(valid, s, NEG_BIG)
        m = jnp.max(s, axis=-1, keepdims=True)
        p = jnp.where(valid, jnp.exp(s - m), 0.0)
        l = jnp.sum(p, axis=-1, keepdims=True)
        p = p / jnp.where(l > 0, l, 1.0)
        imp = imp + p
        o = jnp.dot(p.astype(BF16), vc, preferred_element_type=F32)
        onehot = lax.broadcasted_iota(jnp.int32, (1, LANES), 1) == (kh * NSA_GROUP + g) * 3
        gcol = jnp.sum(jnp.where(onehot, gates, 0.0), axis=-1, keepdims=True)
        oc_ref[:, g * HEAD_DIM:(g + 1) * HEAD_DIM] = o * gcol
    cur = t_pos // NSA_BLOCK
    forced = (blk == 0) | (blk == cur) | (blk == cur - 1)
    score = jnp.where(forced, NSA_FORCE_SCORE, jnp.where(blk <= cur, imp, -1.0))
    if not extra and tq == LANES and nb <= LANES:
        score_t = jnp.concatenate([score, jnp.zeros((tq, LANES - nb), F32)], axis=1).T[0:nb, :]
        blk_t = lax.broadcasted_iota(jnp.int32, (nb, 1), 0)
        rank_t = jnp.zeros((nb, tq), jnp.int32)
        for mrow in range(nb):
            sm = score_t[mrow:mrow + 1, :]
            ahead = (sm > score_t) | ((sm == score_t) & (mrow < blk_t))
            rank_t = rank_t + ahead.astype(jnp.int32)
        sel_t = (rank_t < NSA_TOPN).astype(F32)
        sel_ref[...] = jnp.concatenate([sel_t, jnp.zeros((LANES - nb, tq), F32)], axis=0).T[:, 0:nb]
        return
    rank = jnp.zeros((tq, nb), jnp.int32)
    for mcol in range(nb):
        sm = score[:, mcol:mcol + 1]
        ahead = (sm > score) | ((sm == score) & (mcol < blk))
        rank = rank + ahead.astype(jnp.int32)
    if not extra:
        sel_ref[...] = (rank < NSA_TOPN).astype(F32)
    else:
        rank = rank + (score < NSA_FORCE_SCORE).astype(jnp.int32)
        rank_extra = jnp.sum((score >= NSA_FORCE_SCORE).astype(jnp.int32), axis=-1, keepdims=True)
        lane = lax.broadcasted_iota(jnp.int32, (tq, NSA_TOPN), 1)
        out = jnp.zeros((tq, NSA_TOPN), jnp.int32)
        for r in range(NSA_TOPN):
            idx_r = jnp.sum(jnp.where(rank == r, blk, 0), axis=-1, keepdims=True)
            idx_r = idx_r + jnp.where(rank_extra == r, nb, 0)
            out = jnp.where(lane == r, idx_r, out)
        sel_ref[...] = out


def _nsa_cmp(q3, kvc, bias_c, gates3, *, q_start, extra):
    bsz, t_len, _ = q3.shape
    nb = kvc.shape[2]
    tq = min(t_len, Q_TILE)
    gw = NSA_GROUP * HEAD_DIM
    if extra:
        assert t_len == 1 and q_start // NSA_BLOCK == nb
        sel_shape = jax.ShapeDtypeStruct((bsz, NSA_KV_HEADS, t_len, NSA_TOPN), jnp.int32)
        sel_spec = pl.BlockSpec((None, None, tq, NSA_TOPN), lambda b, kh, qi: (b, kh, qi, 0))
    else:
        assert (q_start + t_len) == nb * NSA_BLOCK
        sel_shape = jax.ShapeDtypeStruct((bsz, NSA_KV_HEADS, t_len, nb), F32)
        sel_spec = pl.BlockSpec((None, None, tq, nb), lambda b, kh, qi: (b, kh, qi, 0))
    return pl.pallas_call(
        functools.partial(_nsa_cmp_kernel, q_start=q_start, tq=tq, nb=nb, extra=extra),
        grid=(bsz, NSA_KV_HEADS, t_len // tq),
        in_specs=[
            pl.BlockSpec((None, tq, gw), lambda b, kh, qi: (b, qi, kh)),
            pl.BlockSpec((None, None, nb, HEAD_DIM), lambda b, kh, qi: (0, b, 0, kh)),
            pl.BlockSpec((None, None, nb, HEAD_DIM), lambda b, kh, qi: (1, b, 0, kh)),
            pl.BlockSpec((NSA_GROUP, tq, nb), lambda b, kh, qi: (kh, qi, 0)),
            pl.BlockSpec((None, tq, LANES), lambda b, kh, qi: (b, qi, 0)),
        ],
        out_specs=[pl.BlockSpec((None, tq, gw), lambda b, kh, qi: (b, qi, kh)), sel_spec],
        out_shape=[jax.ShapeDtypeStruct((bsz, t_len, NSA_HEADS * HEAD_DIM), F32), sel_shape],
        compiler_params=_cparams(3),
        name="nsa_cmp_attn",
    )(q3, kvc, kvc, bias_c, gates3)


def _lane_column(x, col):
    onehot = lax.broadcasted_iota(jnp.int32, (1, x.shape[1]), 1) == col
    return jnp.sum(jnp.where(onehot, x, 0.0), axis=-1, keepdims=True)


def _row_to_column(row):
    n = row.shape[1]
    eye = lax.broadcasted_iota(jnp.int32, (n, n), 0) == lax.broadcasted_iota(jnp.int32, (n, n), 1)
    return jnp.sum(jnp.where(eye, row, 0.0), axis=-1, keepdims=True)


def _flash_kernel(*refs, kind, hps, shared_kv, tq, blk, gate_branch):
    q_ref, k_ref, v_ref = refs[:3]
    pos = 3
    bias_ref = bm_ref = gates_ref = c_ref = None
    if kind in ("nsa_sel", "nsa_win", "moba"):
        bias_ref = refs[pos]
        pos += 1
    if kind in ("nsa_sel", "moba"):
        bm_ref = refs[pos]
        pos += 1
    if kind in ("nsa_sel", "nsa_win"):
        gates_ref = refs[pos]
        pos += 1
    if kind == "fox":
        c_ref = refs[pos]
        pos += 1
    o_ref = refs[pos]
    m_refs = refs[pos + 1:pos + 1 + hps]
    l_refs = refs[pos + 1 + hps:pos + 1 + 2 * hps]
    acc_refs = refs[pos + 1 + 2 * hps:pos + 1 + 3 * hps]
    tk = tq
    step = pl.program_id(1)
    qi = pl.program_id(2)
    q0 = pl.multiple_of(qi * tq, tq)
    ii = lax.broadcasted_iota(jnp.int32, (tq, tk), 0)
    jj = lax.broadcasted_iota(jnp.int32, (tq, tk), 1)
    n_back = NSA_WINDOW // tk
    lo = jnp.maximum(qi - n_back, 0) if kind == "nsa_win" else 0
    for g in range(hps):
        m_refs[g][...] = jnp.full(m_refs[g].shape, NEG_BIG, F32)
        l_refs[g][...] = jnp.zeros(l_refs[g].shape, F32)
        acc_refs[g][...] = jnp.zeros(acc_refs[g].shape, F32)

    def head_cols(g):
        return slice(g * HEAD_DIM, (g + 1) * HEAD_DIM)

    def tile_step(ki, diag):
        k0 = pl.multiple_of(ki * tk, tk)
        delta = qi - ki
        shared_add = None
        if kind == "nsa_sel":
            bm = bm_ref[...]
            shared_add = (_lane_column(bm, k0 // blk) - 1.0) * (-NEG_BIG)
            for sub in range(1, tk // blk):
                shared_add = jnp.where(jj >= sub * blk, (_lane_column(bm, k0 // blk + sub) - 1.0) * (-NEG_BIG),
                                       shared_add)
        elif kind == "nsa_win" and not diag:
            shared_add = jnp.where(delta == n_back, jnp.where(jj < ii, NEG_BIG, 0.0), 0.0)
        elif kind == "fox" and diag:
            shared_add = jnp.where(jj <= ii, 0.0, NEG_BIG)
        for g in range(hps):
            kv_cols = slice(0, HEAD_DIM) if shared_kv else head_cols(g)
            kt = k_ref[pl.ds(k0, tk), kv_cols]
            vt = v_ref[pl.ds(k0, tk), kv_cols]
            s = lax.dot_general(q_ref[:, head_cols(g)], kt, (((1,), (1,)), ((), ())), preferred_element_type=F32)
            if kind == "fox":
                s = s + (c_ref[g, :, pl.ds(q0, LANES)][:, 0:1] - c_ref[g, :, pl.ds(k0, tk)])
            elif diag:
                s = s + bias_ref[g, 0]
            else:
                near = bias_ref[g, 1]
                s = s + jnp.where(delta == 1, near, near[tq - 1:tq, 0:1])
            if shared_add is not None:
                s = s + shared_add
            if kind == "moba":
                s = s + (_lane_column(bm_ref[g], k0 // blk) - 1.0) * (-NEG_BIG)
            m_old = m_refs[g][...]
            m_new = jnp.maximum(m_old, jnp.max(s, axis=-1, keepdims=True))
            alpha = jnp.exp(m_old - m_new)
            parts = [jnp.exp(s[:, c * LANES:(c + 1) * LANES] - m_new) for c in range(tk // LANES)]
            l_refs[g][...] = alpha * l_refs[g][...] + functools.reduce(lambda a, b: a + b, parts)
            p = jnp.concatenate(parts, axis=-1).astype(BF16)
            acc_refs[g][...] = alpha * acc_refs[g][...] + jnp.dot(p, vt, preferred_element_type=F32)
            m_refs[g][...] = m_new

    def body(ki, carry):
        tile_step(ki, False)
        return carry

    lax.fori_loop(lo, qi, body, 0)
    tile_step(qi, True)
    for g in range(hps):
        o = acc_refs[g][...] / jnp.sum(l_refs[g][...], axis=-1, keepdims=True)
        if gates_ref is not None:
            o = o * _lane_column(gates_ref[...], (step * hps + g) * 3 + gate_branch)
        o_ref[:, head_cols(g)] = o.astype(o_ref.dtype)


def _flash(kind, q3, kv3, k_col, v_col, n_kv_heads, *, bias_tiles=None, bm=None, gates3=None, cum=None,
           blk=0, gate_branch=0, out_dtype=F32):
    bsz, t_len, qw = q3.shape
    n_heads = qw // HEAD_DIM
    shared_kv = n_kv_heads < n_heads
    hps = n_heads // n_kv_heads if shared_kv else FLASH_HEADS_PER_STEP
    tq = FLASH_TILE
    assert t_len % tq == 0 and (shared_kv or (k_col % hps == 0 and v_col % hps == 0))
    gw = hps * HEAD_DIM
    if shared_kv:
        kv_spec = lambda c0: pl.BlockSpec((None, t_len, HEAD_DIM), lambda b, h, qi: (b, 0, c0 + h))
    else:
        kv_spec = lambda c0: pl.BlockSpec((None, t_len, gw), lambda b, h, qi: (b, 0, c0 // hps + h))
    in_specs = [pl.BlockSpec((None, tq, gw), lambda b, h, qi: (b, qi, h)), kv_spec(k_col), kv_spec(v_col)]
    args = [q3, kv3, kv3]
    if bias_tiles is not None:
        in_specs.append(pl.BlockSpec((hps, 2, tq, tq), lambda b, h, qi: (h, 0, 0, 0)))
        args.append(bias_tiles)
    if bm is not None:
        nb = bm.shape[-1]
        if shared_kv:
            in_specs.append(pl.BlockSpec((None, None, tq, nb), lambda b, h, qi: (b, h, qi, 0)))
        else:
            in_specs.append(pl.BlockSpec((None, hps, tq, nb), lambda b, h, qi: (b, h, qi, 0)))
        args.append(bm)
    if gates3 is not None:
        in_specs.append(pl.BlockSpec((None, tq, LANES), lambda b, h, qi: (b, qi, 0)))
        args.append(gates3)
    if cum is not None:
        in_specs.append(pl.BlockSpec((None, hps, 1, t_len), lambda b, h, qi: (b, h, 0, 0)))
        args.append(cum)
    return pl.pallas_call(
        functools.partial(_flash_kernel, kind=kind, hps=hps, shared_kv=shared_kv, tq=tq, blk=blk,
                          gate_branch=gate_branch),
        grid=(bsz, n_heads // hps, t_len // tq),
        in_specs=in_specs,
        out_specs=pl.BlockSpec((None, tq, gw), lambda b, h, qi: (b, qi, h)),
        out_shape=jax.ShapeDtypeStruct((bsz, t_len, qw), out_dtype),
        scratch_shapes=[pltpu.VMEM((tq, LANES), F32)] * (2 * hps) + [pltpu.VMEM((tq, HEAD_DIM), F32)] * hps,
        compiler_params=_cparams(3),
        name="flash_" + kind,
    )(*args)


def _sum3_kernel(a_ref, b_ref, c_ref, o_ref):
    o_ref[...] = (a_ref[...] + b_ref[...] + c_ref[...]).astype(o_ref.dtype)


def _sum3_cast(a, b, c):
    m, n = a.shape
    tm = min(m, ROW_TILE)
    spec = pl.BlockSpec((tm, n), lambda i: (i, 0))
    return pl.pallas_call(
        _sum3_kernel, grid=(m // tm,), in_specs=[spec] * 3, out_specs=spec,
        out_shape=jax.ShapeDtypeStruct((m, n), BF16), compiler_params=_cparams(1), name="nsa_sum",
    )(a, b, c)


def _hgrn_kernel(q_ref, z_ref, v_ref, g_ref, lb_ref, on_ref, s0_ref, o_ref, s_ref, *st_refs, t_len, chunk, hps):
    on = on_ref[...]
    ri = lax.broadcasted_iota(jnp.int32, (chunk, chunk), 0)
    ci = lax.broadcasted_iota(jnp.int32, (chunk, chunk), 1)
    tril = (ci <= ri).astype(F32)
    rows = lax.broadcasted_iota(jnp.int32, (chunk, 1), 0)

    single = t_len < chunk

    def load(ref, r0, cols):
        if single:
            return jnp.broadcast_to(ref[0:1, cols], (chunk, HEAD_DIM))
        return ref[pl.ds(r0, chunk), cols]

    for g in range(hps):
        st_refs[g][...] = s0_ref[g].T

    def body(c, carry):
        r0 = pl.multiple_of(c * chunk, chunk)
        for g in range(hps):
            cols = slice(g * HEAD_DIM, (g + 1) * HEAD_DIM)
            q = load(q_ref, r0, cols)
            z = load(z_ref, r0, cols)
            v = load(v_ref, r0, cols)
            a_term = lb_ref[g, 0:1, :]
            b_term = lb_ref[g, 1:2, :] + _log_sigmoid(z)
            logf = jnp.maximum(a_term, b_term) + jnp.log1p(jnp.exp(-jnp.abs(a_term - b_term)))
            k = lb_ref[g, 2:3, :] * _sigmoid(-z)
            if single:
                logf = jnp.where(rows < t_len, logf, 0.0)
                k = jnp.where(rows < t_len, k, 0.0)
            cum = jnp.dot(tril, logf, preferred_element_type=F32, precision=lax.Precision.HIGHEST)
            a_last = cum[chunk - 1:chunk, :]
            o = jnp.zeros((chunk, HEAD_DIM), F32)
            for s_row in range(chunk):
                diff = jnp.where(rows >= s_row, cum - cum[s_row:s_row + 1, :], -jnp.inf)
                wgt = q * jnp.exp(diff) * k[s_row:s_row + 1, :]
                o = o + jnp.sum(wgt, axis=-1, keepdims=True) * v[s_row:s_row + 1, :]
            st = st_refs[g][...]
            qa = (q * jnp.exp(cum)).astype(BF16)
            o = o + lax.dot_general(qa, st.astype(BF16), (((1,), (1,)), ((), ())), preferred_element_type=F32)
            kd = (k * jnp.exp(a_last - cum)).astype(BF16)
            st_refs[g][...] = jnp.exp(a_last) * st + lax.dot_general(v.astype(BF16), kd, (((0,), (0,)), ((), ())),
                                                                     preferred_element_type=F32)
            gate = load(g_ref, r0, cols)
            o = _rms_rows(o, on) * (gate * _sigmoid(gate))
            if single:
                o_ref[:, cols] = o[0:t_len, :].astype(o_ref.dtype)
            else:
                o_ref[pl.ds(r0, chunk), cols] = o.astype(o_ref.dtype)
        return carry

    lax.fori_loop(0, max(t_len // chunk, 1), body, 0)
    for g in range(hps):
        s_ref[g] = st_refs[g][...].T


HGRN_HEADS_PER_STEP = 4


def _hgrn(ph3, lb_rows, o_norm, s0):
    bsz, t_len, _ = ph3.shape
    chunk = 16
    hps = HGRN_HEADS_PER_STEP
    assert t_len % chunk == 0 or t_len == 1
    gw = hps * HEAD_DIM
    n_steps = HG_HEADS // hps
    col = lambda part: pl.BlockSpec((None, t_len, gw), lambda b, h: (b, 0, part * n_steps + h))
    on = o_norm.reshape(1, HEAD_DIM)
    state_spec = pl.BlockSpec((None, hps, HEAD_DIM, HEAD_DIM), lambda b, h: (b, h, 0, 0))
    return pl.pallas_call(
        functools.partial(_hgrn_kernel, t_len=t_len, chunk=chunk, hps=hps),
        grid=(bsz, n_steps),
        in_specs=[col(0), col(1), col(2), col(3),
                  pl.BlockSpec((hps, 3, HEAD_DIM), lambda b, h: (h, 0, 0)),
                  pl.BlockSpec((1, HEAD_DIM), lambda b, h: (0, 0)),
                  state_spec],
        out_specs=[pl.BlockSpec((None, t_len, gw), lambda b, h: (b, 0, h)), state_spec],
        out_shape=[jax.ShapeDtypeStruct((bsz, t_len, HG_HEADS * HEAD_DIM), BF16),
                   jax.ShapeDtypeStruct((bsz, HG_HEADS, HEAD_DIM, HEAD_DIM), F32)],
        scratch_shapes=[pltpu.VMEM((HEAD_DIM, HEAD_DIM), F32)] * hps,
        compiler_params=_cparams(2),
        name="hgrn2",
    )(ph3, ph3, ph3, ph3, lb_rows, on, s0)


def _group_rows(q_ref, group):
    return jnp.concatenate([q_ref[:, g * HEAD_DIM:(g + 1) * HEAD_DIM].astype(F32) for g in range(group)], axis=0)


def _store_group_rows(o_ref, o):
    for g in range(o.shape[0]):
        o_ref[:, g * HEAD_DIM:(g + 1) * HEAD_DIM] = o[g:g + 1, :].astype(o_ref.dtype)


NSA_SEL_BLOCKS_PER_STEP = 4


def _nsa_sel_step_kernel(*refs, t_pos, nb, bps):
    idx_ref, q_ref = refs[1], refs[2]
    kv_refs = refs[3:3 + bps]
    kn_ref, vn_ref, rbt_ref, gates_ref, o_ref, m_ref, l_ref, acc_ref = refs[3 + bps:]
    b = pl.program_id(0)
    kh = pl.program_id(1)
    j = pl.program_id(2)
    q4 = _group_rows(q_ref, NSA_GROUP)
    rbt = rbt_ref[...]
    col = lambda bk: rbt[:, bk:bk + 1]

    @pl.when(j == 0)
    def _():
        m_ref[...] = jnp.sum(q4 * kn_ref[...], axis=-1, keepdims=True) + col(0)
        l_ref[...] = jnp.ones_like(l_ref)
        acc_ref[...] = jnp.broadcast_to(vn_ref[...], acc_ref.shape)

    for u in range(bps):
        blk_id = idx_ref[b * NSA_KV_HEADS + kh, j * bps + u]

        @pl.when(blk_id < nb)
        def _(u=u, blk_id=blk_id):
            kt = kv_refs[u][pl.ds(kh, NSA_BLOCK, stride=NSA_KV_COLS), :].astype(BF16)
            vt = kv_refs[u][pl.ds(NSA_KV_HEADS + kh, NSA_BLOCK, stride=NSA_KV_COLS), :].astype(BF16)
            s = lax.dot_general(q4.astype(BF16), kt, (((1,), (1,)), ((), ())), preferred_element_type=F32)
            dist = t_pos - (blk_id * NSA_BLOCK + lax.broadcasted_iota(jnp.int32, (1, NSA_BLOCK), 1))
            s = s + _bucket_bias(jnp.broadcast_to(dist, s.shape), col)
            m_old = m_ref[...]
            m_new = jnp.maximum(m_old, jnp.max(s, axis=-1, keepdims=True))
            p = jnp.exp(s - m_new)
            alpha = jnp.exp(m_old - m_new)
            l_ref[...] = alpha * l_ref[...] + jnp.sum(p, axis=-1, keepdims=True)
            acc_ref[...] = alpha * acc_ref[...] + jnp.dot(p.astype(BF16), vt, preferred_element_type=F32)
            m_ref[...] = m_new

    @pl.when(j == pl.num_programs(2) - 1)
    def _():
        o = acc_ref[...] / l_ref[...]
        gates = gates_ref[...]
        gcol = jnp.concatenate([_lane_column(gates, (kh * NSA_GROUP + g) * 3 + 1) for g in range(NSA_GROUP)], axis=0)
        _store_group_rows(o_ref, o * gcol)


def _nsa_sel_step(q3, cache_pages, tbl, idx, sel_new3, rbt, gates3, *, t_pos):
    bsz = q3.shape[0]
    nb = tbl.shape[1] * (PAGE_ROWS // NSA_BLOCK)
    gw = NSA_GROUP * HEAD_DIM
    halves = PAGE_ROWS // NSA_BLOCK

    bps = NSA_SEL_BLOCKS_PER_STEP
    assert NSA_TOPN % bps == 0

    def cache_map(u):
        def index_map(b, kh, j, tbl_ref, idx_ref):
            blk = jnp.minimum(idx_ref[b * NSA_KV_HEADS + kh, j * bps + u], nb - 1)
            return (tbl_ref[b, blk // halves], blk % halves, 0)
        return index_map

    grid_spec = pltpu.PrefetchScalarGridSpec(
        num_scalar_prefetch=2,
        grid=(bsz, NSA_KV_HEADS, NSA_TOPN // bps),
        in_specs=[pl.BlockSpec((None, 1, gw), lambda b, kh, j, t, i: (b, 0, kh))]
        + [pl.BlockSpec((None, NSA_BLOCK * NSA_KV_COLS, HEAD_DIM), cache_map(u)) for u in range(bps)]
        + [
            pl.BlockSpec((None, 1, HEAD_DIM), lambda b, kh, j, t, i: (b, 0, kh)),
            pl.BlockSpec((None, 1, HEAD_DIM), lambda b, kh, j, t, i: (b, 0, NSA_KV_HEADS + kh)),
            pl.BlockSpec((None, NSA_GROUP, REL_BUCKETS), lambda b, kh, j, t, i: (kh, 0, 0)),
            pl.BlockSpec((None, 1, LANES), lambda b, kh, j, t, i: (b, 0, 0)),
        ],
        out_specs=pl.BlockSpec((None, 1, gw), lambda b, kh, j, t, i: (b, 0, kh)),
        scratch_shapes=[pltpu.VMEM((NSA_GROUP, 1), F32), pltpu.VMEM((NSA_GROUP, 1), F32),
                        pltpu.VMEM((NSA_GROUP, HEAD_DIM), F32)],
    )
    return pl.pallas_call(
        functools.partial(_nsa_sel_step_kernel, t_pos=t_pos, nb=nb, bps=bps),
        grid_spec=grid_spec,
        out_shape=jax.ShapeDtypeStruct((bsz, 1, NSA_HEADS * HEAD_DIM), F32),
        compiler_params=_cparams(3),
        name="nsa_sel_step",
    )(tbl, idx, q3, *([cache_pages] * bps), sel_new3, sel_new3, rbt, gates3)


def _nsa_win_step_kernel(q_ref, k_ref, v_ref, kn_ref, vn_ref, rbt_ref, gates_ref, o_ref, *, pw):
    kh = pl.program_id(1)
    q4 = _group_rows(q_ref, NSA_GROUP)
    rbt = rbt_ref[...]
    col = lambda bk: rbt[:, bk:bk + 1]
    s = lax.dot_general(q4.astype(BF16), k_ref[...].astype(BF16), (((1,), (1,)), ((), ())),
                        preferred_element_type=F32)
    dist = pw - lax.broadcasted_iota(jnp.int32, (NSA_GROUP, pw), 1)
    mask = dist <= NSA_WINDOW
    s = jnp.where(mask, s + _bucket_bias(dist, col), NEG_BIG)
    s_self = jnp.sum(q4 * kn_ref[...], axis=-1, keepdims=True) + col(0)
    m = jnp.maximum(jnp.max(s, axis=-1, keepdims=True), s_self)
    p = jnp.where(mask, jnp.exp(s - m), 0.0)
    p_self = jnp.exp(s_self - m)
    l = jnp.sum(p, axis=-1, keepdims=True) + p_self
    o = jnp.dot(p.astype(BF16), v_ref[...].astype(BF16), preferred_element_type=F32) + p_self * vn_ref[...]
    gates = gates_ref[...]
    gcol = jnp.concatenate([_lane_column(gates, (kh * NSA_GROUP + g) * 3 + 2) for g in range(NSA_GROUP)], axis=0)
    _store_group_rows(o_ref, o / l * gcol)


def _nsa_win_step(q3, win_cache, lead, win_new3, rbt, gates3):
    bsz = q3.shape[0]
    pw = win_cache.shape[1]
    gw = NSA_GROUP * HEAD_DIM
    return pl.pallas_call(
        functools.partial(_nsa_win_step_kernel, pw=pw),
        grid=(bsz, NSA_KV_HEADS),
        in_specs=[
            pl.BlockSpec((None, 1, gw), lambda b, kh: (b, 0, kh)),
            pl.BlockSpec((None, pw, HEAD_DIM), lambda b, kh: (lead + b, 0, kh)),
            pl.BlockSpec((None, pw, HEAD_DIM), lambda b, kh: (lead + b, 0, NSA_KV_HEADS + kh)),
            pl.BlockSpec((None, 1, HEAD_DIM), lambda b, kh: (b, 0, kh)),
            pl.BlockSpec((None, 1, HEAD_DIM), lambda b, kh: (b, 0, NSA_KV_HEADS + kh)),
            pl.BlockSpec((None, NSA_GROUP, REL_BUCKETS), lambda b, kh: (kh, 0, 0)),
            pl.BlockSpec((None, 1, LANES), lambda b, kh: (b, 0, 0)),
        ],
        out_specs=pl.BlockSpec((None, 1, gw), lambda b, kh: (b, 0, kh)),
        out_shape=jax.ShapeDtypeStruct((bsz, 1, NSA_HEADS * HEAD_DIM), F32),
        compiler_params=_cparams(2),
        name="nsa_win_step",
    )(q3, win_cache, win_cache, win_new3, win_new3, rbt, gates3)


def _even_proj(h, hr, w, j):
    qw = NSA_HEADS * HEAD_DIM
    kv_w = 2 * NSA_KV_HEADS * HEAD_DIM
    n_main = qw + 3 * kv_w
    n_gate = 3 * NSA_HEADS
    w_in = w["w_in_even"]
    hr_parts = None if hr is None else [hr]
    pa, par = _dense2([h], hr_parts, w_in, j, 0, n_main, tn=512, name="even_in_attn")
    w_gate = jnp.pad(w_in[j, :, n_main:n_main + n_gate], ((0, 0), (0, LANES - n_gate)))
    graw, grawr = _dense2([h], hr_parts, w_gate, None, 0, LANES, tn=LANES, name="even_in_gate")
    w_hg = w_in[j, :, n_main + n_gate:]
    ph, phr = _dense2([h], hr_parts, w_hg, None, 0, w_hg.shape[1], tn=512, name="even_in_hgrn")
    return (pa, graw, ph), (None if hr is None else (par, grawr, phr))


def _even_out(x2d, a_parts, xr2d, ar_parts, w, j):
    d = x2d.shape[1]
    return _dense2(a_parts, ar_parts, w["w_out_even"], j, 0, d, tn=min(d, 512), mode="res", res=x2d,
                   r_res=xr2d, scale=1.0, name="even_out")


def _even_mixer(x2d, h, bsz, t_len, j, w, tabs, past):
    proj, _ = _even_proj(h, None, w, j)
    a_parts, new = _even_core(proj, bsz, t_len, j, w, tabs, past)
    x_new, _ = _even_out(x2d, a_parts, None, None, w, j)
    return (x_new,) + new


def _even_core(proj, bsz, t_len, j, w, tabs, past):
    pa, graw, ph = proj
    m = pa.shape[0]
    qw = NSA_HEADS * HEAD_DIM
    kv_w = 2 * NSA_KV_HEADS * HEAD_DIM
    q, cmp_new, sel_new, win_new, selb, winb, gates = _even_post(pa, graw, w["nsa_q_norm"][j], w["nsa_k_norm"][j])
    q3 = q.reshape(bsz, t_len, qw)
    gates3 = gates.reshape(bsz, t_len, LANES)
    rel_bias = w["rel_bias"]
    k_norm = w["nsa_k_norm"][j]
    if past is None:
        q_start = 0
        src, tbl = cmp_new.reshape(bsz, t_len, kv_w), None
        s0 = jnp.zeros((bsz, HG_HEADS, HEAD_DIM, HEAD_DIM), F32)
    else:
        src, sel_pages, tbl, win_cache, s0 = past
        q_start = tbl.shape[1] * PAGE_ROWS
    kvc = _compress(src, tbl, w["nsa_phi_pos"], tabs["phi_w1"], w["nsa_phi_w2"], k_norm, j)
    nb = kvc.shape[2]
    bias_c = _bias_cmp_table(rel_bias, q_start, t_len, nb)
    o_c, sel = _nsa_cmp(q3, kvc, bias_c, gates3, q_start=q_start, extra=past is not None)
    if past is None:
        selb3 = selb.reshape(bsz, t_len, kv_w)
        winb3 = winb.reshape(bsz, t_len, kv_w)
        o_s = _flash("nsa_sel", q3, selb3, 0, NSA_KV_HEADS, NSA_KV_HEADS, bias_tiles=tabs["tiles"], bm=sel,
                     gates3=gates3, blk=NSA_BLOCK, gate_branch=1)
        o_w = _flash("nsa_win", q3, winb3, 0, NSA_KV_HEADS, NSA_KV_HEADS, bias_tiles=tabs["tiles"],
                     gates3=gates3, gate_branch=2)
        keep = min(NSA_WINDOW, t_len)
        win_buf = win_new.reshape(bsz, t_len, kv_w)[:, t_len - keep:]
    else:
        rbt = rel_bias.T.reshape(NSA_KV_HEADS, NSA_GROUP, REL_BUCKETS)
        sel_new3 = sel_new.reshape(bsz, t_len, kv_w)
        win_new3 = win_new.reshape(bsz, t_len, kv_w)
        idx = sel.reshape(bsz * NSA_KV_HEADS, NSA_TOPN)
        o_s = _nsa_sel_step(q3, sel_pages, tbl, idx, sel_new3, rbt, gates3, t_pos=q_start)
        wc = win_cache.reshape(win_cache.shape[0] * win_cache.shape[1], win_cache.shape[2], kv_w)
        o_w = _nsa_win_step(q3, wc, j * bsz, win_new3, rbt, gates3)
        win_all = jnp.concatenate([wc[j * bsz:(j + 1) * bsz], win_new3], axis=1)
        keep = min(NSA_WINDOW, win_all.shape[1])
        win_buf = win_all[:, win_all.shape[1] - keep:]
    a_nsa = _sum3_cast(o_c.reshape(m, qw), o_s.reshape(m, qw), o_w.reshape(m, qw))
    o_hg, s_new = _hgrn(ph.reshape(bsz, t_len, ph.shape[1]), tabs["lb_rows"][j], w["hg_o_norm"][j], s0)
    kv_shape = (bsz, t_len, 2, NSA_KV_HEADS, HEAD_DIM)
    new = (cmp_new.reshape(kv_shape), sel_new.reshape(kv_shape),
           win_buf.reshape(bsz, win_buf.shape[1], 2, NSA_KV_HEADS, HEAD_DIM), s_new)
    return [a_nsa, o_hg.reshape(m, HG_HEADS * HEAD_DIM)], new


ODD_HEADS = MOBA_HEADS
ODD_QW = ODD_HEADS * HEAD_DIM
ODD_KVW = 2 * ODD_HEADS * HEAD_DIM


def _odd_post_kernel(pm_ref, fz_ref, mg_ref, fg_ref, fb_ref, qm_ref, moba_ref, mobab_ref, qf_ref, fox_ref,
                     foxb_ref, logf_ref):
    scale = HEAD_DIM ** -0.5
    off = 0
    for g_ref, q_ref, kv_ref, kvb_ref in ((mg_ref, qm_ref, moba_ref, mobab_ref), (fg_ref, qf_ref, fox_ref, foxb_ref)):
        qg = g_ref[0:1, :]
        kg = g_ref[1:2, :]
        for h in range(ODD_HEADS):
            sl = slice(h * HEAD_DIM, (h + 1) * HEAD_DIM)
            q_ref[:, sl] = (_rms_rows(pm_ref[:, off + h * HEAD_DIM: off + (h + 1) * HEAD_DIM], qg) * scale
                            ).astype(q_ref.dtype)
        off += ODD_QW
        for c in range(2 * ODD_HEADS):
            src = pm_ref[:, off + c * HEAD_DIM: off + (c + 1) * HEAD_DIM]
            val = _rms_rows(src, kg) if c < ODD_HEADS else src
            kv_ref[:, c * HEAD_DIM:(c + 1) * HEAD_DIM] = val
            kvb_ref[:, c * HEAD_DIM:(c + 1) * HEAD_DIM] = val.astype(BF16)
        off += ODD_KVW
    logf_ref[...] = _log_sigmoid(fz_ref[...] + fb_ref[...])


def _odd_post(pm, fz, moba_qk, fox_qk, fb):
    m = pm.shape[0]
    tm = min(m, ROW_TILE // 2)
    row = lambda w: pl.BlockSpec((tm, w), lambda i: (i, 0))
    full = lambda a: pl.BlockSpec(a.shape, lambda i: (0,) * a.ndim)
    return pl.pallas_call(
        _odd_post_kernel,
        grid=(m // tm,),
        in_specs=[row(pm.shape[1]), row(LANES), full(moba_qk), full(fox_qk), full(fb)],
        out_specs=[row(ODD_QW), row(ODD_KVW), row(ODD_KVW), row(ODD_QW), row(ODD_KVW), row(ODD_KVW), row(LANES)],
        out_shape=[jax.ShapeDtypeStruct((m, ODD_QW), BF16), jax.ShapeDtypeStruct((m, ODD_KVW), F32),
                   jax.ShapeDtypeStruct((m, ODD_KVW), BF16), jax.ShapeDtypeStruct((m, ODD_QW), BF16),
                   jax.ShapeDtypeStruct((m, ODD_KVW), F32), jax.ShapeDtypeStruct((m, ODD_KVW), BF16),
                   jax.ShapeDtypeStruct((m, LANES), F32)],
        compiler_params=_cparams(1),
        name="odd_post",
    )(pm, fz, moba_qk, fox_qk, fb)


def _topk_rank(score, n):
    idx = lax.broadcasted_iota(jnp.int32, (1, n), 1)
    rank = jnp.zeros(score.shape, jnp.int32)
    for mcol in range(n):
        sm = score[:, mcol:mcol + 1]
        ahead = (sm > score) | ((sm == score) & (mcol < idx))
        rank = rank + ahead.astype(jnp.int32)
    return rank


def _moba_gate_kernel(q_ref, k_ref, bm_ref, *, t_len, nbl):
    k_mean = jnp.mean(k_ref[...].reshape(nbl, MOBA_BLOCK, HEAD_DIM), axis=1)
    gate = lax.dot_general(k_mean.astype(BF16), q_ref[...], (((1,), (1,)), ((), ())), preferred_element_type=F32)
    cur = lax.broadcasted_iota(jnp.int32, (1, t_len), 1) // MOBA_BLOCK
    blk = lax.broadcasted_iota(jnp.int32, (nbl, 1), 0)
    past_ok = blk < cur
    gate = jnp.where(past_ok, gate, -jnp.inf)
    rank = jnp.zeros(gate.shape, jnp.int32)
    for mrow in range(nbl):
        sm = gate[mrow:mrow + 1, :]
        ahead = (sm > gate) | ((sm == gate) & (mrow < blk))
        rank = rank + ahead.astype(jnp.int32)
    sel = (rank < MOBA_TOPK) & past_ok & (jnp.abs(gate) < jnp.inf)
    mask_t = (sel | (blk == cur)).astype(F32)
    eye = (lax.broadcasted_iota(jnp.int32, (nbl, nbl), 0) == lax.broadcasted_iota(jnp.int32, (nbl, nbl), 1))
    bm_ref[...] = lax.dot_general(mask_t, eye.astype(F32), (((0,), (0,)), ((), ())), preferred_element_type=F32)


def _moba_gate(qm3, moba_new3):
    bsz, t_len, _ = qm3.shape
    assert t_len % MOBA_BLOCK == 0
    nbl = t_len // MOBA_BLOCK
    return pl.pallas_call(
        functools.partial(_moba_gate_kernel, t_len=t_len, nbl=nbl),
        grid=(bsz, MOBA_HEADS),
        in_specs=[pl.BlockSpec((None, t_len, HEAD_DIM), lambda b, h: (b, 0, h)),
                  pl.BlockSpec((None, t_len, HEAD_DIM), lambda b, h: (b, 0, h))],
        out_specs=pl.BlockSpec((None, None, t_len, nbl), lambda b, h: (b, h, 0, 0)),
        out_shape=jax.ShapeDtypeStruct((bsz, MOBA_HEADS, t_len, nbl), F32),
        compiler_params=_cparams(2),
        name="moba_gate",
    )(qm3, moba_new3)


def _cumsum_kernel(x_ref, o_ref, *, t_len):
    n = Q_TILE
    upper = (lax.broadcasted_iota(jnp.int32, (n, n), 0) <= lax.broadcasted_iota(jnp.int32, (n, n), 1)).astype(F32)
    carry = jnp.zeros((FOX_HEADS, 1), F32)
    for c in range(t_len // n):
        xt = x_ref[c * n:(c + 1) * n, :].T[0:FOX_HEADS, :]
        cum = jnp.dot(xt, upper, preferred_element_type=F32, precision=lax.Precision.HIGHEST) + carry
        o_ref[:, c * n:(c + 1) * n] = cum
        carry = cum[:, n - 1:n]


def _cumsum_heads(logf3):
    bsz, t_len, _ = logf3.shape
    assert t_len % Q_TILE == 0
    return pl.pallas_call(
        functools.partial(_cumsum_kernel, t_len=t_len),
        grid=(bsz,),
        in_specs=[pl.BlockSpec((None, t_len, LANES), lambda b: (b, 0, 0))],
        out_specs=pl.BlockSpec((None, FOX_HEADS, t_len), lambda b: (b, 0, 0)),
        out_shape=jax.ShapeDtypeStruct((bsz, FOX_HEADS, t_len), F32),
        compiler_params=_cparams(1),
        name="fox_cumsum",
    )(logf3)


MOBA_GATE_PAGES_PER_STEP = 8


def _moba_gate_step_kernel(*refs, pps, nblk, ppb):
    q_ref = refs[1]
    k_refs = refs[2:2 + pps]
    idx_ref, gate_ref = refs[2 + pps:]
    st = pl.program_id(1)

    @pl.when(st == 0)
    def _():
        gate_ref[...] = jnp.zeros_like(gate_ref)

    q8 = _group_rows(q_ref, MOBA_HEADS)
    lane = lax.broadcasted_iota(jnp.int32, (1, nblk), 1)
    for blk_in_step in range(pps // ppb):
        ksum = jnp.zeros((MOBA_HEADS, HEAD_DIM), F32)
        for p in range(ppb):
            ksum = ksum + jnp.sum(k_refs[blk_in_step * ppb + p][...], axis=0)
        gcol = jnp.sum(q8 * (ksum * (1.0 / MOBA_BLOCK)), axis=-1, keepdims=True)
        gate_ref[...] = jnp.where(lane == st * (pps // ppb) + blk_in_step, gcol, gate_ref[...])

    @pl.when(st == pl.num_programs(1) - 1)
    def _():
        gate = gate_ref[...]
        rank = _topk_rank(gate, nblk)
        blk = lax.broadcasted_iota(jnp.int32, (1, nblk), 1)
        lane = lax.broadcasted_iota(jnp.int32, (MOBA_HEADS, MOBA_TOPK), 1)
        out = jnp.full((MOBA_HEADS, MOBA_TOPK), -1, jnp.int32)
        for r in range(min(MOBA_TOPK, nblk)):
            hit = (rank == r) & (jnp.abs(gate) < jnp.inf)
            idx_r = jnp.sum(jnp.where(hit, blk + 1, 0), axis=-1, keepdims=True) - 1
            out = jnp.where(lane == r, idx_r, out)
        idx_ref[...] = out


def _moba_gate_step(qm3, pages4, tbl):
    bsz, npg = tbl.shape
    ppb = MOBA_BLOCK // PAGE_ROWS
    pps = MOBA_GATE_PAGES_PER_STEP
    assert npg % pps == 0 and pps % ppb == 0

    def key_spec(p):
        return pl.BlockSpec((None, PAGE_ROWS, MOBA_HEADS, HEAD_DIM), lambda b, st, t: (t[b, st * pps + p], 0, 0, 0))

    grid_spec = pltpu.PrefetchScalarGridSpec(
        num_scalar_prefetch=1,
        grid=(bsz, npg // pps),
        in_specs=[pl.BlockSpec((None, 1, ODD_QW), lambda b, st, t: (b, 0, 0))] + [key_spec(p) for p in range(pps)],
        out_specs=pl.BlockSpec((None, MOBA_HEADS, MOBA_TOPK), lambda b, st, t: (b, 0, 0)),
        scratch_shapes=[pltpu.VMEM((MOBA_HEADS, npg // ppb), F32)],
    )
    return pl.pallas_call(
        functools.partial(_moba_gate_step_kernel, pps=pps, nblk=npg // ppb, ppb=ppb),
        grid_spec=grid_spec,
        out_shape=jax.ShapeDtypeStruct((bsz, MOBA_HEADS, MOBA_TOPK), jnp.int32),
        compiler_params=_cparams(2),
        name="moba_gate_step",
    )(tbl, qm3, *([pages4] * pps))


def _moba_attn_step_kernel(*refs, t_pos, ppb):
    idx_ref, q_ref = refs[1], refs[2]
    kv_refs = refs[3:3 + ppb]
    kn_ref, vn_ref, rbt_ref, o_ref, m_ref, l_ref, acc_ref = refs[3 + ppb:]
    b = pl.program_id(0)
    h = pl.program_id(1)
    s_id = pl.program_id(2)
    sub = 8
    q = q_ref[...].astype(F32)
    rbt = rbt_ref[...]
    col = lambda bk: rbt[:, bk:bk + 1]

    @pl.when(s_id == 0)
    def _():
        s_self = jnp.sum(q * kn_ref[...], axis=-1, keepdims=True) + col(0)
        m_ref[...] = jnp.broadcast_to(s_self, m_ref.shape)
        l_ref[...] = jnp.ones_like(l_ref)
        acc_ref[...] = jnp.broadcast_to(vn_ref[...], acc_ref.shape)

    blk_id = idx_ref[b * MOBA_HEADS + h, s_id]

    @pl.when(blk_id >= 0)
    def _():
        q8 = jnp.broadcast_to(q, (sub, HEAD_DIM)).astype(BF16)
        for pg in range(ppb):
            kt = kv_refs[pg][pl.ds(h, PAGE_ROWS, stride=2 * MOBA_HEADS), :].astype(BF16)
            vt = kv_refs[pg][pl.ds(MOBA_HEADS + h, PAGE_ROWS, stride=2 * MOBA_HEADS), :].astype(BF16)
            s = lax.dot_general(q8, kt, (((1,), (1,)), ((), ())), preferred_element_type=F32)
            key_pos = blk_id * MOBA_BLOCK + pg * PAGE_ROWS + lax.broadcasted_iota(jnp.int32, (sub, PAGE_ROWS), 1)
            s = s + _bucket_bias(t_pos - key_pos, col)
            m_old = m_ref[...]
            m_new = jnp.maximum(m_old, jnp.max(s, axis=-1, keepdims=True))
            p = jnp.exp(s - m_new)
            alpha = jnp.exp(m_old - m_new)
            l_ref[...] = alpha * l_ref[...] + jnp.sum(p, axis=-1, keepdims=True)
            acc_ref[...] = alpha * acc_ref[...] + jnp.dot(p.astype(BF16), vt, preferred_element_type=F32)
            m_ref[...] = m_new

    @pl.when(s_id == pl.num_programs(2) - 1)
    def _():
        o_ref[...] = (acc_ref[0:1, :] / l_ref[0:1, :]).astype(o_ref.dtype)


def _moba_attn_step(qm3, pages3, tbl, idx, moba_new3, rbt3, *, t_pos):
    bsz = qm3.shape[0]
    ppb = MOBA_BLOCK // PAGE_ROWS

    def cache_map(pg):
        def index_map(b, h, s, tbl_ref, idx_ref):
            blk = jnp.maximum(idx_ref[b * MOBA_HEADS + h, s], 0)
            return (tbl_ref[b, blk * ppb + pg], 0, 0)
        return index_map

    head = lambda off: pl.BlockSpec((None, 1, HEAD_DIM), lambda b, h, s, t, i: (b, 0, off + h))
    grid_spec = pltpu.PrefetchScalarGridSpec(
        num_scalar_prefetch=2,
        grid=(bsz, MOBA_HEADS, MOBA_TOPK),
        in_specs=[head(0)]
        + [pl.BlockSpec((None, PAGE_ROWS * 2 * MOBA_HEADS, HEAD_DIM), cache_map(pg)) for pg in range(ppb)]
        + [head(0), head(MOBA_HEADS),
           pl.BlockSpec((None, 1, REL_BUCKETS), lambda b, h, s, t, i: (h, 0, 0))],
        out_specs=head(0),
        scratch_shapes=[pltpu.VMEM((8, 1), F32), pltpu.VMEM((8, 1), F32), pltpu.VMEM((8, HEAD_DIM), F32)],
    )
    return pl.pallas_call(
        functools.partial(_moba_attn_step_kernel, t_pos=t_pos, ppb=ppb),
        grid_spec=grid_spec,
        out_shape=jax.ShapeDtypeStruct((bsz, 1, ODD_QW), BF16),
        compiler_params=_cparams(3),
        name="moba_attn_step",
    )(tbl, idx, qm3, *([pages3] * ppb), moba_new3, moba_new3, rbt3)


FOX_PAGES_PER_STEP = 4


def _fox_step_kernel(*refs, pps):
    q_ref = refs[1]
    k_refs = refs[2:2 + pps]
    v_refs = refs[2 + pps:2 + 2 * pps]
    lf_refs = refs[2 + 2 * pps:2 + 3 * pps]
    kn_ref, vn_ref, lfn_ref, o_ref, m_ref, l_ref, acc_ref, carry_ref = refs[2 + 3 * pps:]
    st = pl.program_id(1)
    n = PAGE_ROWS
    nh = FOX_HEADS
    q8 = _group_rows(q_ref, nh)
    lane8 = lax.broadcasted_iota(jnp.int32, (nh, LANES), 1)
    row8 = lax.broadcasted_iota(jnp.int32, (nh, LANES), 0)

    @pl.when(st == 0)
    def _():
        m_ref[...] = jnp.sum(q8 * _group_rows(kn_ref, nh), axis=-1, keepdims=True)
        l_ref[...] = jnp.ones_like(l_ref)
        acc_ref[...] = _group_rows(vn_ref, nh)
        carry_ref[...] = jnp.sum(jnp.where(lane8 == row8, lfn_ref[...], 0.0), axis=-1, keepdims=True)

    ones = jnp.ones((HEAD_DIM, LANES), BF16)
    after_t = (lax.broadcasted_iota(jnp.int32, (n, n), 0) > lax.broadcasted_iota(jnp.int32, (n, n), 1)).astype(F32)
    on_diag = (lax.broadcasted_iota(jnp.int32, (n, nh, LANES), 0)
               == lax.broadcasted_iota(jnp.int32, (n, nh, LANES), 2))
    carry = carry_ref[...]
    logits = []
    for p in range(pps):
        lf = lf_refs[p][...]
        bias = carry + jnp.dot(lf, after_t, preferred_element_type=F32, precision=lax.Precision.HIGHEST)
        carry = carry + jnp.sum(lf, axis=-1, keepdims=True)
        prod = (k_refs[p][...] * q8[None]).reshape(n * nh, HEAD_DIM).astype(BF16)
        qk_rep = jnp.dot(prod, ones, preferred_element_type=F32).reshape(n, nh, LANES)
        logits.append(jnp.sum(jnp.where(on_diag, qk_rep, 0.0), axis=0) + bias)
    carry_ref[...] = carry
    s = jnp.concatenate(logits, axis=-1)
    m_old = m_ref[...]
    m_new = jnp.maximum(m_old, jnp.max(s, axis=-1, keepdims=True))
    prob = jnp.exp(s - m_new)
    alpha = jnp.exp(m_old - m_new)
    l_ref[...] = alpha * l_ref[...] + jnp.sum(prob, axis=-1, keepdims=True)
    m_ref[...] = m_new
    acc = alpha * acc_ref[...]
    for p in range(pps):
        prob_p = prob[:, p * n:(p + 1) * n]
        spread = jnp.where(on_diag, prob_p[None], 0.0).reshape(n * nh, LANES).astype(BF16)
        p_rep = jnp.dot(spread, ones, preferred_element_type=F32).reshape(n, nh, HEAD_DIM)
        acc = acc + jnp.sum(p_rep * v_refs[p][...], axis=0)
    acc_ref[...] = acc

    @pl.when(st == pl.num_programs(1) - 1)
    def _():
        _store_group_rows(o_ref, acc_ref[...] / l_ref[...])


def _fox_step(qf3, pages4, logf_pages, tbl, fox_new3, logf_new3):
    bsz, npg = tbl.shape
    pps = math.gcd(FOX_PAGES_PER_STEP, npg)
    page = lambda p: (lambda b, st, t: t[b, npg - 1 - (st * pps + p)])
    kv_spec = lambda p, part: pl.BlockSpec((None, PAGE_ROWS, FOX_HEADS, HEAD_DIM),
                                           lambda b, st, t: (page(p)(b, st, t), 0, part, 0))
    lf_spec = lambda p: pl.BlockSpec((None, FOX_HEADS, PAGE_ROWS), lambda b, st, t: (page(p)(b, st, t), 0, 0))
    new = lambda col: pl.BlockSpec((None, 1, ODD_QW), lambda b, st, t: (b, 0, col))
    grid_spec = pltpu.PrefetchScalarGridSpec(
        num_scalar_prefetch=1,
        grid=(bsz, npg // pps),
        in_specs=[new(0)] + [kv_spec(p, 0) for p in range(pps)] + [kv_spec(p, 1) for p in range(pps)]
        + [lf_spec(p) for p in range(pps)]
        + [new(0), new(1), pl.BlockSpec((None, 1, LANES), lambda b, st, t: (b, 0, 0))],
        out_specs=new(0),
        scratch_shapes=[pltpu.VMEM((FOX_HEADS, 1), F32), pltpu.VMEM((FOX_HEADS, 1), F32),
                        pltpu.VMEM((FOX_HEADS, HEAD_DIM), F32), pltpu.VMEM((FOX_HEADS, 1), F32)],
    )
    return pl.pallas_call(
        functools.partial(_fox_step_kernel, pps=pps),
        grid_spec=grid_spec,
        out_shape=jax.ShapeDtypeStruct((bsz, 1, ODD_QW), BF16),
        compiler_params=_cparams(2),
        name="fox_step",
    )(tbl, qf3, *([pages4] * (2 * pps)), *([logf_pages] * pps), fox_new3, fox_new3, logf_new3)


def _odd_proj(h, hr, w, j):
    n_main = 2 * (ODD_QW + ODD_KVW)
    w_in = w["w_in_odd"]
    hr_parts = None if hr is None else [hr]
    pm, pmr = _dense2([h], hr_parts, w_in, j, 0, n_main, tn=512, name="odd_in")
    w_fz = jnp.pad(w_in[j, :, n_main:n_main + FOX_HEADS], ((0, 0), (0, LANES - FOX_HEADS)))
    fz, fzr = _dense2([h], hr_parts, w_fz, None, 0, LANES, tn=LANES, name="odd_in_forget")
    return (pm, fz), (None if hr is None else (pmr, fzr))


def _odd_out(x2d, a_parts, xr2d, ar_parts, w, j):
    d = x2d.shape[1]
    return _dense2(a_parts, ar_parts, w["w_out_odd"], j, 0, d, tn=min(d, 512), mode="res", res=x2d,
                   r_res=xr2d, scale=1.0, name="odd_out")


def _odd_mixer(x2d, h, bsz, t_len, j, w, tabs, past):
    proj, _ = _odd_proj(h, None, w, j)
    a_parts, new = _odd_core(proj, bsz, t_len, j, w, tabs, past)
    x_new, _ = _odd_out(x2d, a_parts, None, None, w, j)
    return (x_new,) + new


def _odd_core(proj, bsz, t_len, j, w, tabs, past):
    pm, fz = proj
    m = pm.shape[0]
    fb = jnp.pad(w["fox_f_bias"][j].astype(F32), (0, LANES - FOX_HEADS)).reshape(1, LANES)
    qm, moba_new, mobab, qf, fox_new, foxb, logf = _odd_post(pm, fz, w["moba_qk_norm"][j], w["fox_qk_norm"][j], fb)
    qm3 = qm.reshape(bsz, t_len, ODD_QW)
    qf3 = qf.reshape(bsz, t_len, ODD_QW)
    logf3 = logf.reshape(bsz, t_len, LANES)
    if past is None:
        bm = _moba_gate(qm3, moba_new.reshape(bsz, t_len, ODD_KVW))
        o_m = _flash("moba", qm3, mobab.reshape(bsz, t_len, ODD_KVW), 0, MOBA_HEADS, MOBA_HEADS,
                     bias_tiles=tabs["tiles"], bm=bm, blk=MOBA_BLOCK, out_dtype=BF16)
        cum = _cumsum_heads(logf3).reshape(bsz, FOX_HEADS, 1, t_len)
        o_f = _flash("fox", qf3, foxb.reshape(bsz, t_len, ODD_KVW), 0, FOX_HEADS, FOX_HEADS, cum=cum,
                     out_dtype=BF16)
    else:
        moba_pages4, fox_pages, logf_pages, tbl = past
        t_pos = tbl.shape[1] * PAGE_ROWS
        idx = _moba_gate_step(qm3, moba_pages4, tbl)
        rbt3 = w["rel_bias"].T.reshape(MOBA_HEADS, 1, REL_BUCKETS)
        moba_pages3 = moba_pages4.reshape(moba_pages4.shape[0], PAGE_ROWS * 2 * MOBA_HEADS, HEAD_DIM)
        o_m = _moba_attn_step(qm3, moba_pages3, tbl, idx.reshape(bsz * MOBA_HEADS, MOBA_TOPK),
                              moba_new.reshape(bsz, t_len, ODD_KVW), rbt3, t_pos=t_pos)
        o_f = _fox_step(qf3, fox_pages, logf_pages, tbl, fox_new.reshape(bsz, t_len, ODD_KVW), logf3)
    kv_shape = (bsz, t_len, 2, ODD_HEADS, HEAD_DIM)
    new = (moba_new.reshape(kv_shape), fox_new.reshape(kv_shape), logf3[:, :, :FOX_HEADS])
    return [o_m.reshape(m, ODD_QW), o_f.reshape(m, ODD_QW)], new


def _trunk(x_prompt, x_sample, w, tabs, caches):
    bp, tp, d = x_prompt.shape
    bs, ts, _ = x_sample.shape
    depth = w["norm_mix"].shape[0]
    xp = x_prompt.reshape(bp * tp, d)
    xs = x_sample.reshape(bs * ts, d)
    pe, po, se, so = [], [], [], []
    for layer in range(depth):
        xp, xs = _ffn(xp, xs, w["norm_ffn1"], w["w_ffn1_in"], w["w_ffn1_out"], layer)
        hp = _rms_cast(xp, w["norm_mix"], layer)
        hs = _rms_cast(xs, w["norm_mix"], layer)
        j = layer // 2
        tbl = caches["tbl"] + j * caches["n_phys"]
        if layer % 2 == 0:
            proj_p, proj_s = _even_proj(hp, hs, w, j)
            ap, new_p = _even_core(proj_p, bp, tp, j, w, tabs, None)
            past = (caches["cmp"], caches["sel"], tbl, caches["win"], caches["hgrn"][j])
            a_s, new_s = _even_core(proj_s, bs, ts, j, w, tabs, past)
            xp, xs = _even_out(xp, ap, xs, a_s, w, j)
            pe.append(new_p)
            se.append(new_s)
        else:
            proj_p, proj_s = _odd_proj(hp, hs, w, j)
            ap, new_p = _odd_core(proj_p, bp, tp, j, w, tabs, None)
            past = (caches["moba"], caches["fox"], caches["logf"], tbl)
            a_s, new_s = _odd_core(proj_s, bs, ts, j, w, tabs, past)
            xp, xs = _odd_out(xp, ap, xs, a_s, w, j)
            po.append(new_p)
            so.append(new_s)
        xp, xs = _ffn(xp, xs, w["norm_ffn2"], w["w_ffn2_in"], w["w_ffn2_out"], layer)
    return xp.reshape(bp, tp, d), xs.reshape(bs, ts, d), pe, po, se, so


def kernel(x_prompt, x_sample, cache_nsa_cmp, cache_nsa_sel, cache_moba, cache_fox, cache_fox_logf,
           cache_nsa_win, state_hgrn, page_table, norm_ffn1, w_ffn1_in, w_ffn1_out, norm_mix, norm_ffn2,
           w_ffn2_in, w_ffn2_out, rel_bias, w_in_even, w_out_even, nsa_q_norm, nsa_k_norm, nsa_phi_pos,
           nsa_phi_w1, nsa_phi_w2, hg_lb, hg_o_norm, w_in_odd, w_out_odd, fox_f_bias, moba_qk_norm,
           fox_qk_norm):
    w = dict(norm_ffn1=norm_ffn1, w_ffn1_in=w_ffn1_in, w_ffn1_out=w_ffn1_out, norm_mix=norm_mix,
             norm_ffn2=norm_ffn2, w_ffn2_in=w_ffn2_in, w_ffn2_out=w_ffn2_out, rel_bias=rel_bias,
             w_in_even=w_in_even, w_out_even=w_out_even, nsa_q_norm=nsa_q_norm, nsa_k_norm=nsa_k_norm,
             nsa_phi_pos=nsa_phi_pos, nsa_phi_w1=nsa_phi_w1, nsa_phi_w2=nsa_phi_w2, hg_lb=hg_lb,
             hg_o_norm=hg_o_norm, w_in_odd=w_in_odd, w_out_odd=w_out_odd, fox_f_bias=fox_f_bias,
             moba_qk_norm=moba_qk_norm, fox_qk_norm=fox_qk_norm)
    tabs = _tables(w)
    n_phys = cache_nsa_cmp.shape[1]
    n_layers = cache_nsa_cmp.shape[0]
    rows3 = lambda pool: pool.reshape(n_layers * n_phys, -1, HEAD_DIM)
    rows4 = lambda pool: pool.reshape(n_layers * n_phys, PAGE_ROWS, -1, HEAD_DIM)
    logf_t = jnp.swapaxes(cache_fox_logf.reshape(n_layers * n_phys, PAGE_ROWS, FOX_HEADS), 1, 2)
    caches = dict(cmp=rows3(cache_nsa_cmp), sel=rows3(cache_nsa_sel), moba=rows4(cache_moba),
                  fox=rows4(cache_fox), logf=logf_t, win=cache_nsa_win, hgrn=state_hgrn,
                  tbl=page_table.astype(jnp.int32), n_phys=n_phys)
    y_prompt, y_sample, pe, po, se, so = _trunk(x_prompt, x_sample, w, tabs, caches)

    stack = lambda items, i: jnp.stack([it[i] for it in items])
    return (y_prompt, y_sample,
            stack(pe, 0), stack(pe, 1), stack(pe, 2), stack(pe, 3),
            stack(po, 0), stack(po, 1), stack(po, 2),
            stack(se, 0), stack(se, 1), stack(se, 2), stack(se, 3),
            stack(so, 0), stack(so, 1), stack(so, 2))
```

```python
import functools
import math

import jax
import jax.numpy as jnp
import numpy as np
from jax import lax
from jax.experimental import pallas as pl
from jax.experimental.pallas import tpu as pltpu

F32 = jnp.float32
BF16 = jnp.bfloat16

HEAD_DIM = 128
NSA_HEADS = 8
NSA_KV_HEADS = 2
NSA_GROUP = NSA_HEADS // NSA_KV_HEADS
NSA_BLOCK = 64
NSA_TOPN = 16
NSA_WINDOW = 512
NSA_PHI_HIDDEN = 2 * HEAD_DIM
NSA_FORCE_SCORE = 1.0e4
HG_HEADS = 8
MOBA_HEADS = 8
MOBA_BLOCK = 256
MOBA_TOPK = 3
FOX_HEADS = 8
REL_BUCKETS = 32
REL_MAX_DIST = 128
EPS = 1e-6
PAGE_ROWS = 128

LANES = 128
VMEM_LIMIT_BYTES = 56 * 1024 * 1024
ROW_TILE = 512
DEEP_K = 2048
Q_TILE = 128
FLASH_TILE = 256
FLASH_HEADS_PER_STEP = 4
NEG_BIG = -1e30


def _cparams(n_axes):
    return pltpu.CompilerParams(dimension_semantics=("arbitrary",) * n_axes,
                                vmem_limit_bytes=VMEM_LIMIT_BYTES)


def _rel_bucket(dist):
    n = jnp.maximum(dist, 0)
    exact = REL_BUCKETS // 2
    nf = jnp.maximum(n, 1).astype(F32)
    big = exact + (jnp.log(nf / exact) / math.log(REL_MAX_DIST / exact) * (REL_BUCKETS - exact)).astype(jnp.int32)
    return jnp.where(n < exact, n, jnp.minimum(big, REL_BUCKETS - 1))


def _bucket_bias(dist, table_rows):
    bucket = _rel_bucket(dist)
    out = jnp.zeros(dist.shape, F32) + table_rows(0)
    for b in range(1, REL_BUCKETS):
        out = jnp.where(bucket == b, table_rows(b), out)
    return out


def _rms_rows(x, g):
    return x * lax.rsqrt(jnp.mean(x * x, axis=-1, keepdims=True) + EPS) * g


def _log_sigmoid(z):
    return jnp.minimum(z, 0.0) - jnp.log1p(jnp.exp(-jnp.abs(z)))


def _sigmoid(z):
    return 1.0 / (1.0 + jnp.exp(-z))


def _rms_cast_kernel(x_ref, g_ref, o_ref):
    o_ref[...] = _rms_rows(x_ref[...], g_ref[...]).astype(o_ref.dtype)


def _rms_cast(x2d, g_stack, layer):
    m, d = x2d.shape
    tm = min(m, ROW_TILE)
    g3 = g_stack.reshape(g_stack.shape[0], 1, d)
    return pl.pallas_call(
        _rms_cast_kernel,
        grid=(m // tm,),
        in_specs=[pl.BlockSpec((tm, d), lambda i: (i, 0)),
                  pl.BlockSpec((None, 1, d), lambda i: (layer, 0, 0))],
        out_specs=pl.BlockSpec((tm, d), lambda i: (i, 0)),
        out_shape=jax.ShapeDtypeStruct((m, d), BF16),
        compiler_params=_cparams(1),
        name="rms_cast",
    )(x2d, g3)


def _dense_kernel(*refs, n_a, mode, scale, rider):
    pos = 0
    a_refs = refs[pos:pos + n_a]
    pos += n_a
    ra_refs = refs[pos:pos + n_a] if rider else ()
    pos += len(ra_refs)
    w_ref = refs[pos]
    pos += 1
    w2_ref = res_ref = rres_ref = ro_ref = None
    if mode == "swiglu":
        w2_ref = refs[pos]
        pos += 1
    if mode == "res":
        res_ref = refs[pos]
        pos += 1
        if rider:
            rres_ref = refs[pos]
            pos += 1
    o_ref = refs[pos]
    pos += 1
    if rider:
        ro_ref = refs[pos]
        pos += 1
    wb_ref = refs[pos]
    pos += 1
    wb2_ref = refs[pos] if mode == "swiglu" else None

    def apply(in_refs, r_ref, out_ref):
        a = in_refs[0][...] if n_a == 1 else jnp.concatenate([r[...] for r in in_refs], axis=-1)
        y = jnp.dot(a, wb_ref[...], preferred_element_type=F32)
        if mode == "swiglu":
            y2 = jnp.dot(a, wb2_ref[...], preferred_element_type=F32)
            y = y * _sigmoid(y) * y2
        elif mode == "res":
            y = r_ref[...] + scale * y
        out_ref[...] = y.astype(out_ref.dtype)

    @pl.when(pl.program_id(1) == 0)
    def _():
        wb_ref[...] = w_ref[...].astype(BF16)
        if mode == "swiglu":
            wb2_ref[...] = w2_ref[...].astype(BF16)
        if rider:
            apply(ra_refs, rres_ref, ro_ref)

    apply(a_refs, res_ref, o_ref)


def _dense(a_parts, w, lead, col0, n_out, *, tn, mode="plain", res=None, scale=1.0,
           out_dtype=F32, col0_b=None, rider=None, name="dense"):
    m = a_parts[0].shape[0]
    k = sum(a.shape[1] for a in a_parts)
    tm = min(m, 2 * ROW_TILE if k <= DEEP_K else ROW_TILE)
    assert m % tm == 0 and n_out % tn == 0 and col0 % tn == 0
    if w.ndim == 3:
        wblock = (None, k, tn)

        def wmap(off):
            return lambda j, i: (lead, 0, j + off)
    else:
        wblock = (k, tn)

        def wmap(off):
            return lambda j, i: (0, j + off)
    in_specs = [pl.BlockSpec((tm, a.shape[1]), lambda j, i: (i, 0)) for a in a_parts]
    args = list(a_parts)
    ms = 0
    if rider is not None:
        r_parts, r_res = rider
        ms = r_parts[0].shape[0]
        assert [a.shape[1] for a in r_parts] == [a.shape[1] for a in a_parts]
        in_specs += [pl.BlockSpec((ms, a.shape[1]), lambda j, i: (0, 0)) for a in r_parts]
        args += list(r_parts)
    in_specs.append(pl.BlockSpec(wblock, wmap(col0 // tn)))
    args.append(w)
    scratch = [pltpu.VMEM((k, tn), BF16)]
    if mode == "swiglu":
        assert col0_b % tn == 0
        in_specs.append(pl.BlockSpec(wblock, wmap(col0_b // tn)))
        args.append(w)
        scratch.append(pltpu.VMEM((k, tn), BF16))
    if mode == "res":
        in_specs.append(pl.BlockSpec((tm, tn), lambda j, i: (i, j)))
        args.append(res)
        if rider is not None:
            in_specs.append(pl.BlockSpec((ms, tn), lambda j, i: (0, j)))
            args.append(r_res)
    out_specs = pl.BlockSpec((tm, tn), lambda j, i: (i, j))
    out_shape = jax.ShapeDtypeStruct((m, n_out), out_dtype)
    if rider is not None:
        out_specs = [out_specs, pl.BlockSpec((ms, tn), lambda j, i: (0, j))]
        out_shape = [out_shape, jax.ShapeDtypeStruct((ms, n_out), out_dtype)]
    return pl.pallas_call(
        functools.partial(_dense_kernel, n_a=len(a_parts), mode=mode, scale=scale, rider=rider is not None),
        grid=(n_out // tn, m // tm),
        in_specs=in_specs,
        out_specs=out_specs,
        out_shape=out_shape,
        scratch_shapes=scratch,
        compiler_params=_cparams(2),
        name=name,
    )(*args)


def _dense2(a_parts, r_parts, *args, res=None, r_res=None, **kw):
    if r_parts is None:
        return _dense(a_parts, *args, res=res, **kw), None
    return _dense(a_parts, *args, res=res, rider=(r_parts, r_res), **kw)


def _ffn(x2d, xr2d, norm_g, w_in, w_out, layer):
    d_ff = w_out.shape[1]
    xn = _rms_cast(x2d, norm_g, layer)
    xrn = None if xr2d is None else [_rms_cast(xr2d, norm_g, layer)]
    h, hr = _dense2([xn], xrn, w_in, layer, 0, d_ff, tn=512, mode="swiglu", col0_b=d_ff,
                    out_dtype=BF16, name="ffn_in")
    return _dense2([h], None if hr is None else [hr], w_out, layer, 0, x2d.shape[1], tn=512, mode="res",
                   res=x2d, r_res=xr2d, scale=0.5, name="ffn_out")


def _bias_tiles(rel_bias):
    i = jnp.arange(FLASH_TILE)[:, None]
    j = jnp.arange(FLASH_TILE)[None, :]
    dist = jnp.stack([i - j, FLASH_TILE + i - j])
    onehot = (_rel_bucket(dist)[..., None] == jnp.arange(REL_BUCKETS)).astype(F32)
    tiles = jnp.einsum("ktsb,bh->hkts", onehot, rel_bias.astype(F32), precision=lax.Precision.HIGHEST)
    return jnp.where((dist < 0)[None], NEG_BIG, tiles)


def _tables(w):
    lb_all = jnp.cumsum(jax.nn.softmax(w["hg_lb"].astype(F32), axis=0), axis=0)
    lb_all = lb_all - lb_all[0:1]
    lbh = lb_all.reshape(lb_all.shape[0], HG_HEADS, HEAD_DIM)
    lb_rows = jnp.stack([jnp.log(lbh), jnp.log1p(-lbh), 1.0 - lbh], axis=2)
    return {"tiles": _bias_tiles(w["rel_bias"]), "lb_rows": lb_rows, "phi_w1": w["nsa_phi_w1"].astype(BF16)}


def _bias_cmp_table(rel_bias, q_start, t_len, nb):
    t_pos = q_start + jnp.arange(t_len)
    dist = t_pos[:, None] - (jnp.arange(nb) * NSA_BLOCK + NSA_BLOCK - 1)[None, :]
    onehot = (_rel_bucket(dist)[..., None] == jnp.arange(REL_BUCKETS)).astype(F32)
    return jnp.einsum("tnb,bh->htn", onehot, rel_bias.astype(F32), precision=lax.Precision.HIGHEST)


def _even_post_kernel(pa_ref, graw_ref, qg_ref, kg_ref, q_ref, cmp_ref, sel_ref, win_ref,
                      selb_ref, winb_ref, gates_ref):
    scale = HEAD_DIM ** -0.5
    qg = qg_ref[...]
    for h in range(NSA_HEADS):
        sl = slice(h * HEAD_DIM, (h + 1) * HEAD_DIM)
        q_ref[:, sl] = (_rms_rows(pa_ref[:, sl], qg) * scale).astype(q_ref.dtype)
    base = NSA_HEADS * HEAD_DIM
    kv_w = 2 * NSA_KV_HEADS * HEAD_DIM
    cmp_ref[...] = pa_ref[:, base:base + kv_w]
    for which, (o_ref, ob_ref) in enumerate(((sel_ref, selb_ref), (win_ref, winb_ref))):
        off = base + (which + 1) * kv_w
        kg = kg_ref[which + 1:which + 2, :]
        for c in range(2 * NSA_KV_HEADS):
            src = pa_ref[:, off + c * HEAD_DIM: off + (c + 1) * HEAD_DIM]
            val = _rms_rows(src, kg) if c < NSA_KV_HEADS else src
            o_ref[:, c * HEAD_DIM:(c + 1) * HEAD_DIM] = val
            ob_ref[:, c * HEAD_DIM:(c + 1) * HEAD_DIM] = val.astype(BF16)
    gates_ref[...] = _sigmoid(graw_ref[...])


def _even_post(pa, graw, q_norm, k_norm):
    m = pa.shape[0]
    tm = min(m, ROW_TILE)
    kv_w = 2 * NSA_KV_HEADS * HEAD_DIM
    qw = NSA_HEADS * HEAD_DIM
    row = lambda w: pl.BlockSpec((tm, w), lambda i: (i, 0))
    full = lambda a: pl.BlockSpec(a.shape, lambda i: (0,) * a.ndim)
    qg = q_norm.reshape(1, HEAD_DIM)
    return pl.pallas_call(
        _even_post_kernel,
        grid=(m // tm,),
        in_specs=[row(pa.shape[1]), row(LANES), full(qg), full(k_norm)],
        out_specs=[row(qw), row(kv_w), row(kv_w), row(kv_w), row(kv_w), row(kv_w), row(LANES)],
        out_shape=[jax.ShapeDtypeStruct((m, qw), BF16)] + [jax.ShapeDtypeStruct((m, kv_w), F32)] * 3
        + [jax.ShapeDtypeStruct((m, kv_w), BF16)] * 2 + [jax.ShapeDtypeStruct((m, LANES), F32)],
        compiler_params=_cparams(1),
        name="even_post",
    )(pa, graw, qg, k_norm)


def _gelu_tanh(x):
    return 0.5 * x * (1.0 + jnp.tanh(math.sqrt(2.0 / math.pi) * (x + 0.044715 * (x * x * x))))


NSA_KV_COLS = 2 * NSA_KV_HEADS
CMP_PAGES_PER_STEP = 8
CMP_PAGES_PER_GROUP = 64


def _compress_mlp(x_of, nblk, pos_ref, w1_ref, w2_ref, kg_ref, o_ref, acc_ref):
    acc_ref[...] = jnp.zeros_like(acc_ref)

    def body(i2, carry):
        for c in range(NSA_KV_COLS):
            w = c // NSA_KV_HEADS
            xa = x_of(2 * i2, c) + pos_ref[w, pl.ds(2 * i2, 1), :]
            xb = x_of(2 * i2 + 1, c) + pos_ref[w, pl.ds(2 * i2 + 1, 1), :]
            x = jnp.concatenate([xa, xb], axis=-1).astype(BF16)
            wi = w1_ref[w, pl.ds(pl.multiple_of(i2 * 2 * HEAD_DIM, 2 * HEAD_DIM), 2 * HEAD_DIM), :]
            acc_ref[c] += jnp.dot(x, wi, preferred_element_type=F32)
        return carry

    lax.fori_loop(0, NSA_BLOCK // 2, body, 0)
    for c in range(NSA_KV_COLS):
        w, kh = divmod(c, NSA_KV_HEADS)
        hid = _gelu_tanh(acc_ref[c]).astype(BF16)
        y = jnp.dot(hid, w2_ref[w].astype(BF16), preferred_element_type=F32)
        if w == 0:
            y = _rms_rows(y, kg_ref[0:1, :])
        o_ref[w, :, kh * HEAD_DIM:(kh + 1) * HEAD_DIM] = y


def _compress_rows_kernel(*refs, nblk):
    x_refs = refs[:NSA_KV_COLS]
    pos_ref, w1_ref, w2_ref, kg_ref, o_ref, acc_ref = refs[NSA_KV_COLS:]

    def x_of(i, c):
        return x_refs[c][pl.ds(i, nblk, stride=NSA_BLOCK), :]

    _compress_mlp(x_of, nblk, pos_ref, w1_ref, w2_ref, kg_ref, o_ref, acc_ref)


def _compress_pages_kernel(*refs, pps, nblk):
    tbl_ref = refs[0]
    page_refs = refs[1:1 + pps]
    pos_ref, w1_ref, w2_ref, kg_ref, o_ref, xs_ref, acc_ref = refs[1 + pps:]
    st = pl.program_id(2)
    page_rows = PAGE_ROWS * NSA_KV_COLS
    for p in range(pps):
        xs_ref[pl.ds(pl.multiple_of((st * pps + p) * page_rows, page_rows), page_rows), :] = page_refs[p][...]

    @pl.when(st == pl.num_programs(2) - 1)
    def _():
        def x_of(i, c):
            return xs_ref[pl.ds(i * NSA_KV_COLS + c, nblk, stride=NSA_BLOCK * NSA_KV_COLS), :]

        _compress_mlp(x_of, nblk, pos_ref, w1_ref, w2_ref, kg_ref, o_ref, acc_ref)


def _compress(src, tbl, pos, w1b, w2, k_norm, j):
    weight_specs = lambda nidx: [
        pl.BlockSpec((None, 2, NSA_BLOCK, HEAD_DIM), lambda *a: (j, 0, 0, 0)),
        pl.BlockSpec((None, 2, NSA_BLOCK * HEAD_DIM, NSA_PHI_HIDDEN), lambda *a: (j, 0, 0, 0),
                     pipeline_mode=pl.Buffered(1)),
        pl.BlockSpec((None, 2, NSA_PHI_HIDDEN, HEAD_DIM), lambda *a: (j, 0, 0, 0)),
        pl.BlockSpec(k_norm.shape, lambda *a: (0, 0)),
    ]
    kvw = NSA_KV_HEADS * HEAD_DIM
    if tbl is None:
        bsz, t_len, _ = src.shape
        nblk = t_len // NSA_BLOCK
        return pl.pallas_call(
            functools.partial(_compress_rows_kernel, nblk=nblk),
            grid=(bsz,),
            in_specs=[pl.BlockSpec((None, t_len, HEAD_DIM), lambda b, c=c: (b, 0, c)) for c in range(NSA_KV_COLS)]
            + weight_specs(1),
            out_specs=pl.BlockSpec((2, None, nblk, kvw), lambda b: (0, b, 0, 0)),
            out_shape=jax.ShapeDtypeStruct((2, bsz, nblk, kvw), F32),
            scratch_shapes=[pltpu.VMEM((NSA_KV_COLS, nblk, NSA_PHI_HIDDEN), F32)],
            compiler_params=_cparams(1),
            name="nsa_compress_rows",
        )(*([src] * NSA_KV_COLS), pos, w1b, w2, k_norm)
    bsz, npg = tbl.shape
    pps = math.gcd(CMP_PAGES_PER_STEP, npg)
    ppg = math.gcd(CMP_PAGES_PER_GROUP, npg)
    bpp = PAGE_ROWS // NSA_BLOCK
    nblk = ppg * bpp
    page_rows = PAGE_ROWS * NSA_KV_COLS

    def page_spec(p):
        return pl.BlockSpec((None, page_rows, HEAD_DIM),
                            lambda b, grp, st, t: (t[b, grp * ppg + st * pps + p], 0, 0))

    grid_spec = pltpu.PrefetchScalarGridSpec(
        num_scalar_prefetch=1,
        grid=(bsz, npg // ppg, ppg // pps),
        in_specs=[page_spec(p) for p in range(pps)] + weight_specs(4),
        out_specs=pl.BlockSpec((2, None, nblk, kvw), lambda b, grp, st, t: (0, b, grp, 0)),
        scratch_shapes=[pltpu.VMEM((ppg * page_rows, HEAD_DIM), F32),
                        pltpu.VMEM((NSA_KV_COLS, nblk, NSA_PHI_HIDDEN), F32)],
    )
    return pl.pallas_call(
        functools.partial(_compress_pages_kernel, pps=pps, nblk=nblk),
        grid_spec=grid_spec,
        out_shape=jax.ShapeDtypeStruct((2, bsz, npg * bpp, kvw), F32),
        compiler_params=_cparams(3),
        name="nsa_compress_pages",
    )(tbl, *([src] * pps), pos, w1b, w2, k_norm)


def _nsa_cmp_kernel(q_ref, kc_ref, vc_ref, bias_ref, gates_ref, oc_ref, sel_ref, *, q_start, tq, nb, extra):
    kh = pl.program_id(1)
    qi = pl.program_id(2)
    t_pos = q_start + qi * tq + lax.broadcasted_iota(jnp.int32, (tq, 1), 0)
    blk = lax.broadcasted_iota(jnp.int32, (1, nb), 1)
    valid = t_pos >= blk * NSA_BLOCK + (NSA_BLOCK - 1)
    kc = kc_ref[...].astype(BF16)
    vc = vc_ref[...].astype(BF16)
    gates = gates_ref[...]
    imp = jnp.zeros((tq, nb), F32)
    for g in range(NSA_GROUP):
        qg = q_ref[:, g * HEAD_DIM:(g + 1) * HEAD_DIM]
        s = lax.dot_general(qg, kc, (((1,), (1,)), ((), ())), preferred_element_type=F32) + bias_ref[g]
        s = jnp.where(valid, s, NEG_BIG)
        m = jnp.max(s, axis=-1, keepdims=True)
        p = jnp.where(valid, jnp.exp(s - m), 0.0)
        l = jnp.sum(p, axis=-1, keepdims=True)
        p = p / jnp.where(l > 0, l, 1.0)
        imp = imp + p
        o = jnp.dot(p.astype(BF16), vc, preferred_element_type=F32)
        onehot = lax.broadcasted_iota(jnp.int32, (1, LANES), 1) == (kh * NSA_GROUP + g) * 3
        gcol = jnp.sum(jnp.where(onehot, gates, 0.0), axis=-1, keepdims=True)
        oc_ref[:, g * HEAD_DIM:(g + 1) * HEAD_DIM] = o * gcol
    cur = t_pos // NSA_BLOCK
    forced = (blk == 0) | (blk == cur) | (blk == cur - 1)
    score = jnp.where(forced, NSA_FORCE_SCORE, jnp.where(blk <= cur, imp, -1.0))
    if not extra and tq == LANES and nb <= LANES:
        score_t = jnp.concatenate([score, jnp.zeros((tq, LANES - nb), F32)], axis=1).T[0:nb, :]
        blk_t = lax.broadcasted_iota(jnp.int32, (nb, 1), 0)
        rank_t = jnp.zeros((nb, tq), jnp.int32)
        for mrow in range(nb):
            sm = score_t[mrow:mrow + 1, :]
            ahead = (sm > score_t) | ((sm == score_t) & (mrow < blk_t))
            rank_t = rank_t + ahead.astype(jnp.int32)
        sel_t = (rank_t < NSA_TOPN).astype(F32)
        sel_ref[...] = jnp.concatenate([sel_t, jnp.zeros((LANES - nb, tq), F32)], axis=0).T[:, 0:nb]
        return
    rank = jnp.zeros((tq, nb), jnp.int32)
    for mcol in range(nb):
        sm = score[:, mcol:mcol + 1]
        ahead = (sm > score) | ((sm == score) & (mcol < blk))
        rank = rank + ahead.astype(jnp.int32)
    if not extra:
        sel_ref[...] = (rank < NSA_TOPN).astype(F32)
    else:
        rank = rank + (score < NSA_FORCE_SCORE).astype(jnp.int32)
        rank_extra = jnp.sum((score >= NSA_FORCE_SCORE).astype(jnp.int32), axis=-1, keepdims=True)
        lane = lax.broadcasted_iota(jnp.int32, (tq, NSA_TOPN), 1)
        out = jnp.zeros((tq, NSA_TOPN), jnp.int32)
        for r in range(NSA_TOPN):
            idx_r = jnp.sum(jnp.where(rank == r, blk, 0), axis=-1, keepdims=True)
            idx_r = idx_r + jnp.where(rank_extra == r, nb, 0)
            out = jnp.where(lane == r, idx_r, out)
        sel_ref[...] = out


def _nsa_cmp(q3, kvc, bias_c, gates3, *, q_start, extra):
    bsz, t_len, _ = q3.shape
    nb = kvc.shape[2]
    tq = min(t_len, Q_TILE)
    gw = NSA_GROUP * HEAD_DIM
    if extra:
        assert t_len == 1 and q_start // NSA_BLOCK == nb
        sel_shape = jax.ShapeDtypeStruct((bsz, NSA_KV_HEADS, t_len, NSA_TOPN), jnp.int32)
        sel_spec = pl.BlockSpec((None, None, tq, NSA_TOPN), lambda b, kh, qi: (b, kh, qi, 0))
    else:
        assert (q_start + t_len) == nb * NSA_BLOCK
        sel_shape = jax.ShapeDtypeStruct((bsz, NSA_KV_HEADS, t_len, nb), F32)
        sel_spec = pl.BlockSpec((None, None, tq, nb), lambda b, kh, qi: (b, kh, qi, 0))
    return pl.pallas_call(
        functools.partial(_nsa_cmp_kernel, q_start=q_start, tq=tq, nb=nb, extra=extra),
        grid=(bsz, NSA_KV_HEADS, t_len // tq),
        in_specs=[
            pl.BlockSpec((None, tq, gw), lambda b, kh, qi: (b, qi, kh)),
            pl.BlockSpec((None, None, nb, HEAD_DIM), lambda b, kh, qi: (0, b, 0, kh)),
            pl.BlockSpec((None, None, nb, HEAD_DIM), lambda b, kh, qi: (1, b, 0, kh)),
            pl.BlockSpec((NSA_GROUP, tq, nb), lambda b, kh, qi: (kh, qi, 0)),
            pl.BlockSpec((None, tq, LANES), lambda b, kh, qi: (b, qi, 0)),
        ],
        out_specs=[pl.BlockSpec((None, tq, gw), lambda b, kh, qi: (b, qi, kh)), sel_spec],
        out_shape=[jax.ShapeDtypeStruct((bsz, t_len, NSA_HEADS * HEAD_DIM), F32), sel_shape],
        compiler_params=_cparams(3),
        name="nsa_cmp_attn",
    )(q3, kvc, kvc, bias_c, gates3)


def _lane_column(x, col):
    onehot = lax.broadcasted_iota(jnp.int32, (1, x.shape[1]), 1) == col
    return jnp.sum(jnp.where(onehot, x, 0.0), axis=-1, keepdims=True)


def _row_to_column(row):
    n = row.shape[1]
    eye = lax.broadcasted_iota(jnp.int32, (n, n), 0) == lax.broadcasted_iota(jnp.int32, (n, n), 1)
    return jnp.sum(jnp.where(eye, row, 0.0), axis=-1, keepdims=True)


def _flash_kernel(*refs, kind, hps, shared_kv, tq, blk, gate_branch):
    q_ref, k_ref, v_ref = refs[:3]
    pos = 3
    bias_ref = bm_ref = gates_ref = c_ref = None
    if kind in ("nsa_sel", "nsa_win", "moba"):
        bias_ref = refs[pos]
        pos += 1
    if kind in ("nsa_sel", "moba"):
        bm_ref = refs[pos]
        pos += 1
    if kind in ("nsa_sel", "nsa_win"):
        gates_ref = refs[pos]
        pos += 1
    if kind == "fox":
        c_ref = refs[pos]
        pos += 1
    o_ref = refs[pos]
    m_refs = refs[pos + 1:pos + 1 + hps]
    l_refs = refs[pos + 1 + hps:pos + 1 + 2 * hps]
    acc_refs = refs[pos + 1 + 2 * hps:pos + 1 + 3 * hps]
    tk = tq
    step = pl.program_id(1)
    qi = pl.program_id(2)
    q0 = pl.multiple_of(qi * tq, tq)
    ii = lax.broadcasted_iota(jnp.int32, (tq, tk), 0)
    jj = lax.broadcasted_iota(jnp.int32, (tq, tk), 1)
    n_back = NSA_WINDOW // tk
    lo = jnp.maximum(qi - n_back, 0) if kind == "nsa_win" else 0
    for g in range(hps):
        m_refs[g][...] = jnp.full(m_refs[g].shape, NEG_BIG, F32)
        l_refs[g][...] = jnp.zeros(l_refs[g].shape, F32)
        acc_refs[g][...] = jnp.zeros(acc_refs[g].shape, F32)

    def head_cols(g):
        return slice(g * HEAD_DIM, (g + 1) * HEAD_DIM)

    def tile_step(ki, diag):
        k0 = pl.multiple_of(ki * tk, tk)
        delta = qi - ki
        shared_add = None
        if kind == "nsa_sel":
            bm = bm_ref[...]
            shared_add = (_lane_column(bm, k0 // blk) - 1.0) * (-NEG_BIG)
            for sub in range(1, tk // blk):
                shared_add = jnp.where(jj >= sub * blk, (_lane_column(bm, k0 // blk + sub) - 1.0) * (-NEG_BIG),
                                       shared_add)
        elif kind == "nsa_win" and not diag:
            shared_add = jnp.where(delta == n_back, jnp.where(jj < ii, NEG_BIG, 0.0), 0.0)
        elif kind == "fox" and diag:
            shared_add = jnp.where(jj <= ii, 0.0, NEG_BIG)
        for g in range(hps):
            kv_cols = slice(0, HEAD_DIM) if shared_kv else head_cols(g)
            kt = k_ref[pl.ds(k0, tk), kv_cols]
            vt = v_ref[pl.ds(k0, tk), kv_cols]
            s = lax.dot_general(q_ref[:, head_cols(g)], kt, (((1,), (1,)), ((), ())), preferred_element_type=F32)
            if kind == "fox":
                s = s + (c_ref[g, :, pl.ds(q0, LANES)][:, 0:1] - c_ref[g, :, pl.ds(k0, tk)])
            elif diag:
                s = s + bias_ref[g, 0]
            else:
                near = bias_ref[g, 1]
                s = s + jnp.where(delta == 1, near, near[tq - 1:tq, 0:1])
            if shared_add is not None:
                s = s + shared_add
            if kind == "moba":
                s = s + (_lane_column(bm_ref[g], k0 // blk) - 1.0) * (-NEG_BIG)
            m_old = m_refs[g][...]
            m_new = jnp.maximum(m_old, jnp.max(s, axis=-1, keepdims=True))
            alpha = jnp.exp(m_old - m_new)
            parts = [jnp.exp(s[:, c * LANES:(c + 1) * LANES] - m_new) for c in range(tk // LANES)]
            l_refs[g][...] = alpha * l_refs[g][...] + functools.reduce(lambda a, b: a + b, parts)
            p = jnp.concatenate(parts, axis=-1).astype(BF16)
            acc_refs[g][...] = alpha * acc_refs[g][...] + jnp.dot(p, vt, preferred_element_type=F32)
            m_refs[g][...] = m_new

    def body(ki, carry):
        tile_step(ki, False)
        return carry

    lax.fori_loop(lo, qi, body, 0)
    tile_step(qi, True)
    for g in range(hps):
        o = acc_refs[g][...] / jnp.sum(l_refs[g][...], axis=-1, keepdims=True)
        if gates_ref is not None:
            o = o * _lane_column(gates_ref[...], (step * hps + g) * 3 + gate_branch)
        o_ref[:, head_cols(g)] = o.astype(o_ref.dtype)


def _flash(kind, q3, kv3, k_col, v_col, n_kv_heads, *, bias_tiles=None, bm=None, gates3=None, cum=None,
           blk=0, gate_branch=0, out_dtype=F32):
    bsz, t_len, qw = q3.shape
    n_heads = qw // HEAD_DIM
    shared_kv = n_kv_heads < n_heads
    hps = n_heads // n_kv_heads if shared_kv else FLASH_HEADS_PER_STEP
    tq = FLASH_TILE
    assert t_len % tq == 0 and (shared_kv or (k_col % hps == 0 and v_col % hps == 0))
    gw = hps * HEAD_DIM
    if shared_kv:
        kv_spec = lambda c0: pl.BlockSpec((None, t_len, HEAD_DIM), lambda b, h, qi: (b, 0, c0 + h))
    else:
        kv_spec = lambda c0: pl.BlockSpec((None, t_len, gw), lambda b, h, qi: (b, 0, c0 // hps + h))
    in_specs = [pl.BlockSpec((None, tq, gw), lambda b, h, qi: (b, qi, h)), kv_spec(k_col), kv_spec(v_col)]
    args = [q3, kv3, kv3]
    if bias_tiles is not None:
        in_specs.append(pl.BlockSpec((hps, 2, tq, tq), lambda b, h, qi: (h, 0, 0, 0)))
        args.append(bias_tiles)
    if bm is not None:
        nb = bm.shape[-1]
        if shared_kv:
            in_specs.append(pl.BlockSpec((None, None, tq, nb), lambda b, h, qi: (b, h, qi, 0)))
        else:
            in_specs.append(pl.BlockSpec((None, hps, tq, nb), lambda b, h, qi: (b, h, qi, 0)))
        args.append(bm)
    if gates3 is not None:
        in_specs.append(pl.BlockSpec((None, tq, LANES), lambda b, h, qi: (b, qi, 0)))
        args.append(gates3)
    if cum is not None:
        in_specs.append(pl.BlockSpec((None, hps, 1, t_len), lambda b, h, qi: (b, h, 0, 0)))
        args.append(cum)
    return pl.pallas_call(
        functools.partial(_flash_kernel, kind=kind, hps=hps, shared_kv=shared_kv, tq=tq, blk=blk,
                          gate_branch=gate_branch),
        grid=(bsz, n_heads // hps, t_len // tq),
        in_specs=in_specs,
        out_specs=pl.BlockSpec((None, tq, gw), lambda b, h, qi: (b, qi, h)),
        out_shape=jax.ShapeDtypeStruct((bsz, t_len, qw), out_dtype),
        scratch_shapes=[pltpu.VMEM((tq, LANES), F32)] * (2 * hps) + [pltpu.VMEM((tq, HEAD_DIM), F32)] * hps,
        compiler_params=_cparams(3),
        name="flash_" + kind,
    )(*args)


def _sum3_kernel(a_ref, b_ref, c_ref, o_ref):
    o_ref[...] = (a_ref[...] + b_ref[...] + c_ref[...]).astype(o_ref.dtype)


def _sum3_cast(a, b, c):
    m, n = a.shape
    tm = min(m, ROW_TILE)
    spec = pl.BlockSpec((tm, n), lambda i: (i, 0))
    return pl.pallas_call(
        _sum3_kernel, grid=(m // tm,), in_specs=[spec] * 3, out_specs=spec,
        out_shape=jax.ShapeDtypeStruct((m, n), BF16), compiler_params=_cparams(1), name="nsa_sum",
    )(a, b, c)


def _hgrn_kernel(q_ref, z_ref, v_ref, g_ref, lb_ref, on_ref, s0_ref, o_ref, s_ref, *st_refs, t_len, chunk, hps):
    on = on_ref[...]
    rows = lax.broadcasted_iota(jnp.int32, (chunk, 1), 0)

    single = t_len < chunk

    def load(ref, r0, cols):
        if single:
            return jnp.broadcast_to(ref[0:1, cols], (chunk, HEAD_DIM))
        return ref[pl.ds(r0, chunk), cols]

    for g in range(hps):
        st_refs[g][...] = s0_ref[g].T

    def body(c, carry):
        r0 = pl.multiple_of(c * chunk, chunk)
        for g in range(hps):
            cols = slice(g * HEAD_DIM, (g + 1) * HEAD_DIM)
            q = load(q_ref, r0, cols)
            z = load(z_ref, r0, cols)
            v = load(v_ref, r0, cols)
            a_term = lb_ref[g, 0:1, :]
            b_term = lb_ref[g, 1:2, :] + _log_sigmoid(z)
            logf = jnp.maximum(a_term, b_term) + jnp.log1p(jnp.exp(-jnp.abs(a_term - b_term)))
            k = lb_ref[g, 2:3, :] * _sigmoid(-z)
            if single:
                logf = jnp.where(rows < t_len, logf, 0.0)
                k = jnp.where(rows < t_len, k, 0.0)
            cum = logf
            shift = 1
            while shift < chunk:
                cum = cum + jnp.where(rows >= shift, pltpu.roll(cum, shift, axis=0), 0.0)
                shift *= 2
            a_last = cum[chunk - 1:chunk, :]
            o = jnp.zeros((chunk, HEAD_DIM), F32)
            for s_row in range(chunk):
                diff = jnp.where(rows >= s_row, cum - cum[s_row:s_row + 1, :], -jnp.inf)
                wgt = q * jnp.exp(diff) * k[s_row:s_row + 1, :]
                o = o + jnp.sum(wgt, axis=-1, keepdims=True) * v[s_row:s_row + 1, :]
            st = st_refs[g][...]
            qa = (q * jnp.exp(cum)).astype(BF16)
            o = o + lax.dot_general(qa, st.astype(BF16), (((1,), (1,)), ((), ())), preferred_element_type=F32)
            kd = (k * jnp.exp(a_last - cum)).astype(BF16)
            st_refs[g][...] = jnp.exp(a_last) * st + lax.dot_general(v.astype(BF16), kd, (((0,), (0,)), ((), ())),
                                                                     preferred_element_type=F32)
            gate = load(g_ref, r0, cols)
            o = _rms_rows(o, on) * (gate * _sigmoid(gate))
            if single:
                o_ref[:, cols] = o[0:t_len, :].astype(o_ref.dtype)
            else:
                o_ref[pl.ds(r0, chunk), cols] = o.astype(o_ref.dtype)
        return carry

    n_chunks = max(t_len // chunk, 1)
    lax.fori_loop(0, n_chunks, body, 0, unroll=2 if n_chunks % 2 == 0 else 1)
    for g in range(hps):
        s_ref[g] = st_refs[g][...].T


HGRN_HEADS_PER_STEP = 4


def _hgrn(ph3, lb_rows, o_norm, s0):
    bsz, t_len, _ = ph3.shape
    chunk = 16
    hps = HGRN_HEADS_PER_STEP
    assert t_len % chunk == 0 or t_len == 1
    gw = hps * HEAD_DIM
    n_steps = HG_HEADS // hps
    col = lambda part: pl.BlockSpec((None, t_len, gw), lambda b, h: (b, 0, part * n_steps + h))
    on = o_norm.reshape(1, HEAD_DIM)
    state_spec = pl.BlockSpec((None, hps, HEAD_DIM, HEAD_DIM), lambda b, h: (b, h, 0, 0))
    return pl.pallas_call(
        functools.partial(_hgrn_kernel, t_len=t_len, chunk=chunk, hps=hps),
        grid=(bsz, n_steps),
        in_specs=[col(0), col(1), col(2), col(3),
                  pl.BlockSpec((hps, 3, HEAD_DIM), lambda b, h: (h, 0, 0)),
                  pl.BlockSpec((1, HEAD_DIM), lambda b, h: (0, 0)),
                  state_spec],
        out_specs=[pl.BlockSpec((None, t_len, gw), lambda b, h: (b, 0, h)), state_spec],
        out_shape=[jax.ShapeDtypeStruct((bsz, t_len, HG_HEADS * HEAD_DIM), BF16),
                   jax.ShapeDtypeStruct((bsz, HG_HEADS, HEAD_DIM, HEAD_DIM), F32)],
        scratch_shapes=[pltpu.VMEM((HEAD_DIM, HEAD_DIM), F32)] * hps,
        compiler_params=_cparams(2),
        name="hgrn2",
    )(ph3, ph3, ph3, ph3, lb_rows, on, s0)


def _group_rows(q_ref, group):
    return jnp.concatenate([q_ref[:, g * HEAD_DIM:(g + 1) * HEAD_DIM].astype(F32) for g in range(group)], axis=0)


def _store_group_rows(o_ref, o):
    for g in range(o.shape[0]):
        o_ref[:, g * HEAD_DIM:(g + 1) * HEAD_DIM] = o[g:g + 1, :].astype(o_ref.dtype)


NSA_SEL_BLOCKS_PER_STEP = 4


def _nsa_sel_step_kernel(*refs, t_pos, nb, bps):
    idx_ref, q_ref = refs[1], refs[2]
    kv_refs = refs[3:3 + bps]
    kn_ref, vn_ref, rbt_ref, gates_ref, o_ref, m_ref, l_ref, acc_ref = refs[3 + bps:]
    b = pl.program_id(0)
    kh = pl.program_id(1)
    j = pl.program_id(2)
    q4 = _group_rows(q_ref, NSA_GROUP)
    rbt = rbt_ref[...]
    col = lambda bk: rbt[:, bk:bk + 1]

    @pl.when(j == 0)
    def _():
        m_ref[...] = jnp.sum(q4 * kn_ref[...], axis=-1, keepdims=True) + col(0)
        l_ref[...] = jnp.ones_like(l_ref)
        acc_ref[...] = jnp.broadcast_to(vn_ref[...], acc_ref.shape)

    for u in range(bps):
        blk_id = idx_ref[b * NSA_KV_HEADS + kh, j * bps + u]

        @pl.when(blk_id < nb)
        def _(u=u, blk_id=blk_id):
            kt = kv_refs[u][pl.ds(kh, NSA_BLOCK, stride=NSA_KV_COLS), :].astype(BF16)
            vt = kv_refs[u][pl.ds(NSA_KV_HEADS + kh, NSA_BLOCK, stride=NSA_KV_COLS), :].astype(BF16)
            s = lax.dot_general(q4.astype(BF16), kt, (((1,), (1,)), ((), ())), preferred_element_type=F32)
            dist = t_pos - (blk_id * NSA_BLOCK + lax.broadcasted_iota(jnp.int32, (1, NSA_BLOCK), 1))
            s = s + _bucket_bias(jnp.broadcast_to(dist, s.shape), col)
            m_old = m_ref[...]
            m_new = jnp.maximum(m_old, jnp.max(s, axis=-1, keepdims=True))
            p = jnp.exp(s - m_new)
            alpha = jnp.exp(m_old - m_new)
            l_ref[...] = alpha * l_ref[...] + jnp.sum(p, axis=-1, keepdims=True)
            acc_ref[...] = alpha * acc_ref[...] + jnp.dot(p.astype(BF16), vt, preferred_element_type=F32)
            m_ref[...] = m_new

    @pl.when(j == pl.num_programs(2) - 1)
    def _():
        o = acc_ref[...] / l_ref[...]
        gates = gates_ref[...]
        gcol = jnp.concatenate([_lane_column(gates, (kh * NSA_GROUP + g) * 3 + 1) for g in range(NSA_GROUP)], axis=0)
        _store_group_rows(o_ref, o * gcol)


def _nsa_sel_step(q3, cache_pages, tbl, idx, sel_new3, rbt, gates3, *, t_pos):
    bsz = q3.shape[0]
    nb = tbl.shape[1] * (PAGE_ROWS // NSA_BLOCK)
    gw = NSA_GROUP * HEAD_DIM
    halves = PAGE_ROWS // NSA_BLOCK

    bps = NSA_SEL_BLOCKS_PER_STEP
    assert NSA_TOPN % bps == 0

    def cache_map(u):
        def index_map(b, kh, j, tbl_ref, idx_ref):
            blk = jnp.minimum(idx_ref[b * NSA_KV_HEADS + kh, j * bps + u], nb - 1)
            return (tbl_ref[b, blk // halves], blk % halves, 0)
        return index_map

    grid_spec = pltpu.PrefetchScalarGridSpec(
        num_scalar_prefetch=2,
        grid=(bsz, NSA_KV_HEADS, NSA_TOPN // bps),
        in_specs=[pl.BlockSpec((None, 1, gw), lambda b, kh, j, t, i: (b, 0, kh))]
        + [pl.BlockSpec((None, NSA_BLOCK * NSA_KV_COLS, HEAD_DIM), cache_map(u)) for u in range(bps)]
        + [
            pl.BlockSpec((None, 1, HEAD_DIM), lambda b, kh, j, t, i: (b, 0, kh)),
            pl.BlockSpec((None, 1, HEAD_DIM), lambda b, kh, j, t, i: (b, 0, NSA_KV_HEADS + kh)),
            pl.BlockSpec((None, NSA_GROUP, REL_BUCKETS), lambda b, kh, j, t, i: (kh, 0, 0)),
            pl.BlockSpec((None, 1, LANES), lambda b, kh, j, t, i: (b, 0, 0)),
        ],
        out_specs=pl.BlockSpec((None, 1, gw), lambda b, kh, j, t, i: (b, 0, kh)),
        scratch_shapes=[pltpu.VMEM((NSA_GROUP, 1), F32), pltpu.VMEM((NSA_GROUP, 1), F32),
                        pltpu.VMEM((NSA_GROUP, HEAD_DIM), F32)],
    )
    return pl.pallas_call(
        functools.partial(_nsa_sel_step_kernel, t_pos=t_pos, nb=nb, bps=bps),
        grid_spec=grid_spec,
        out_shape=jax.ShapeDtypeStruct((bsz, 1, NSA_HEADS * HEAD_DIM), F32),
        compiler_params=_cparams(3),
        name="nsa_sel_step",
    )(tbl, idx, q3, *([cache_pages] * bps), sel_new3, sel_new3, rbt, gates3)


def _nsa_win_step_kernel(q_ref, k_ref, v_ref, kn_ref, vn_ref, rbt_ref, gates_ref, o_ref, *, pw):
    kh = pl.program_id(1)
    q4 = _group_rows(q_ref, NSA_GROUP)
    rbt = rbt_ref[...]
    col = lambda bk: rbt[:, bk:bk + 1]
    s = lax.dot_general(q4.astype(BF16), k_ref[...].astype(BF16), (((1,), (1,)), ((), ())),
                        preferred_element_type=F32)
    dist = pw - lax.broadcasted_iota(jnp.int32, (NSA_GROUP, pw), 1)
    mask = dist <= NSA_WINDOW
    s = jnp.where(mask, s + _bucket_bias(dist, col), NEG_BIG)
    s_self = jnp.sum(q4 * kn_ref[...], axis=-1, keepdims=True) + col(0)
    m = jnp.maximum(jnp.max(s, axis=-1, keepdims=True), s_self)
    p = jnp.where(mask, jnp.exp(s - m), 0.0)
    p_self = jnp.exp(s_self - m)
    l = jnp.sum(p, axis=-1, keepdims=True) + p_self
    o = jnp.dot(p.astype(BF16), v_ref[...].astype(BF16), preferred_element_type=F32) + p_self * vn_ref[...]
    gates = gates_ref[...]
    gcol = jnp.concatenate([_lane_column(gates, (kh * NSA_GROUP + g) * 3 + 2) for g in range(NSA_GROUP)], axis=0)
    _store_group_rows(o_ref, o / l * gcol)


def _nsa_win_step(q3, win_cache, lead, win_new3, rbt, gates3):
    bsz = q3.shape[0]
    pw = win_cache.shape[1]
    gw = NSA_GROUP * HEAD_DIM
    return pl.pallas_call(
        functools.partial(_nsa_win_step_kernel, pw=pw),
        grid=(bsz, NSA_KV_HEADS),
        in_specs=[
            pl.BlockSpec((None, 1, gw), lambda b, kh: (b, 0, kh)),
            pl.BlockSpec((None, pw, HEAD_DIM), lambda b, kh: (lead + b, 0, kh)),
            pl.BlockSpec((None, pw, HEAD_DIM), lambda b, kh: (lead + b, 0, NSA_KV_HEADS + kh)),
            pl.BlockSpec((None, 1, HEAD_DIM), lambda b, kh: (b, 0, kh)),
            pl.BlockSpec((None, 1, HEAD_DIM), lambda b, kh: (b, 0, NSA_KV_HEADS + kh)),
            pl.BlockSpec((None, NSA_GROUP, REL_BUCKETS), lambda b, kh: (kh, 0, 0)),
            pl.BlockSpec((None, 1, LANES), lambda b, kh: (b, 0, 0)),
        ],
        out_specs=pl.BlockSpec((None, 1, gw), lambda b, kh: (b, 0, kh)),
        out_shape=jax.ShapeDtypeStruct((bsz, 1, NSA_HEADS * HEAD_DIM), F32),
        compiler_params=_cparams(2),
        name="nsa_win_step",
    )(q3, win_cache, win_cache, win_new3, win_new3, rbt, gates3)


def _even_proj(h, hr, w, j):
    qw = NSA_HEADS * HEAD_DIM
    kv_w = 2 * NSA_KV_HEADS * HEAD_DIM
    n_main = qw + 3 * kv_w
    n_gate = 3 * NSA_HEADS
    w_in = w["w_in_even"]
    hr_parts = None if hr is None else [hr]
    pa, par = _dense2([h], hr_parts, w_in, j, 0, n_main, tn=512, name="even_in_attn")
    w_gate = jnp.pad(w_in[j, :, n_main:n_main + n_gate], ((0, 0), (0, LANES - n_gate)))
    graw, grawr = _dense2([h], hr_parts, w_gate, None, 0, LANES, tn=LANES, name="even_in_gate")
    w_hg = w_in[j, :, n_main + n_gate:]
    ph, phr = _dense2([h], hr_parts, w_hg, None, 0, w_hg.shape[1], tn=512, name="even_in_hgrn")
    return (pa, graw, ph), (None if hr is None else (par, grawr, phr))


def _even_out(x2d, a_parts, xr2d, ar_parts, w, j):
    d = x2d.shape[1]
    return _dense2(a_parts, ar_parts, w["w_out_even"], j, 0, d, tn=min(d, 512), mode="res", res=x2d,
                   r_res=xr2d, scale=1.0, name="even_out")


def _even_mixer(x2d, h, bsz, t_len, j, w, tabs, past):
    proj, _ = _even_proj(h, None, w, j)
    a_parts, new = _even_core(proj, bsz, t_len, j, w, tabs, past)
    x_new, _ = _even_out(x2d, a_parts, None, None, w, j)
    return (x_new,) + new


def _even_core(proj, bsz, t_len, j, w, tabs, past):
    pa, graw, ph = proj
    m = pa.shape[0]
    qw = NSA_HEADS * HEAD_DIM
    kv_w = 2 * NSA_KV_HEADS * HEAD_DIM
    q, cmp_new, sel_new, win_new, selb, winb, gates = _even_post(pa, graw, w["nsa_q_norm"][j], w["nsa_k_norm"][j])
    q3 = q.reshape(bsz, t_len, qw)
    gates3 = gates.reshape(bsz, t_len, LANES)
    rel_bias = w["rel_bias"]
    k_norm = w["nsa_k_norm"][j]
    if past is None:
        q_start = 0
        src, tbl = cmp_new.reshape(bsz, t_len, kv_w), None
        s0 = jnp.zeros((bsz, HG_HEADS, HEAD_DIM, HEAD_DIM), F32)
    else:
        src, sel_pages, tbl, win_cache, s0 = past
        q_start = tbl.shape[1] * PAGE_ROWS
    kvc = _compress(src, tbl, w["nsa_phi_pos"], tabs["phi_w1"], w["nsa_phi_w2"], k_norm, j)
    nb = kvc.shape[2]
    bias_c = _bias_cmp_table(rel_bias, q_start, t_len, nb)
    o_c, sel = _nsa_cmp(q3, kvc, bias_c, gates3, q_start=q_start, extra=past is not None)
    if past is None:
        selb3 = selb.reshape(bsz, t_len, kv_w)
        winb3 = winb.reshape(bsz, t_len, kv_w)
        o_s = _flash("nsa_sel", q3, selb3, 0, NSA_KV_HEADS, NSA_KV_HEADS, bias_tiles=tabs["tiles"], bm=sel,
                     gates3=gates3, blk=NSA_BLOCK, gate_branch=1)
        o_w = _flash("nsa_win", q3, winb3, 0, NSA_KV_HEADS, NSA_KV_HEADS, bias_tiles=tabs["tiles"],
                     gates3=gates3, gate_branch=2)
        keep = min(NSA_WINDOW, t_len)
        win_buf = win_new.reshape(bsz, t_len, kv_w)[:, t_len - keep:]
    else:
        rbt = rel_bias.T.reshape(NSA_KV_HEADS, NSA_GROUP, REL_BUCKETS)
        sel_new3 = sel_new.reshape(bsz, t_len, kv_w)
        win_new3 = win_new.reshape(bsz, t_len, kv_w)
        idx = sel.reshape(bsz * NSA_KV_HEADS, NSA_TOPN)
        o_s = _nsa_sel_step(q3, sel_pages, tbl, idx, sel_new3, rbt, gates3, t_pos=q_start)
        wc = win_cache.reshape(win_cache.shape[0] * win_cache.shape[1], win_cache.shape[2], kv_w)
        o_w = _nsa_win_step(q3, wc, j * bsz, win_new3, rbt, gates3)
        win_all = jnp.concatenate([wc[j * bsz:(j + 1) * bsz], win_new3], axis=1)
        keep = min(NSA_WINDOW, win_all.shape[1])
        win_buf = win_all[:, win_all.shape[1] - keep:]
    a_nsa = _sum3_cast(o_c.reshape(m, qw), o_s.reshape(m, qw), o_w.reshape(m, qw))
    o_hg, s_new = _hgrn(ph.reshape(bsz, t_len, ph.shape[1]), tabs["lb_rows"][j], w["hg_o_norm"][j], s0)
    kv_shape = (bsz, t_len, 2, NSA_KV_HEADS, HEAD_DIM)
    new = (cmp_new.reshape(kv_shape), sel_new.reshape(kv_shape),
           win_buf.reshape(bsz, win_buf.shape[1], 2, NSA_KV_HEADS, HEAD_DIM), s_new)
    return [a_nsa, o_hg.reshape(m, HG_HEADS * HEAD_DIM)], new


ODD_HEADS = MOBA_HEADS
ODD_QW = ODD_HEADS * HEAD_DIM
ODD_KVW = 2 * ODD_HEADS * HEAD_DIM


def _odd_post_kernel(pm_ref, fz_ref, mg_ref, fg_ref, fb_ref, qm_ref, moba_ref, mobab_ref, qf_ref, fox_ref,
                     foxb_ref, logf_ref):
    scale = HEAD_DIM ** -0.5
    off = 0
    for g_ref, q_ref, kv_ref, kvb_ref in ((mg_ref, qm_ref, moba_ref, mobab_ref), (fg_ref, qf_ref, fox_ref, foxb_ref)):
        qg = g_ref[0:1, :]
        kg = g_ref[1:2, :]
        for h in range(ODD_HEADS):
            sl = slice(h * HEAD_DIM, (h + 1) * HEAD_DIM)
            q_ref[:, sl] = (_rms_rows(pm_ref[:, off + h * HEAD_DIM: off + (h + 1) * HEAD_DIM], qg) * scale
                            ).astype(q_ref.dtype)
        off += ODD_QW
        for c in range(2 * ODD_HEADS):
            src = pm_ref[:, off + c * HEAD_DIM: off + (c + 1) * HEAD_DIM]
            val = _rms_rows(src, kg) if c < ODD_HEADS else src
            kv_ref[:, c * HEAD_DIM:(c + 1) * HEAD_DIM] = val
            kvb_ref[:, c * HEAD_DIM:(c + 1) * HEAD_DIM] = val.astype(BF16)
        off += ODD_KVW
    logf_ref[...] = _log_sigmoid(fz_ref[...] + fb_ref[...])


def _odd_post(pm, fz, moba_qk, fox_qk, fb):
    m = pm.shape[0]
    tm = min(m, ROW_TILE // 2)
    row = lambda w: pl.BlockSpec((tm, w), lambda i: (i, 0))
    full = lambda a: pl.BlockSpec(a.shape, lambda i: (0,) * a.ndim)
    return pl.pallas_call(
        _odd_post_kernel,
        grid=(m // tm,),
        in_specs=[row(pm.shape[1]), row(LANES), full(moba_qk), full(fox_qk), full(fb)],
        out_specs=[row(ODD_QW), row(ODD_KVW), row(ODD_KVW), row(ODD_QW), row(ODD_KVW), row(ODD_KVW), row(LANES)],
        out_shape=[jax.ShapeDtypeStruct((m, ODD_QW), BF16), jax.ShapeDtypeStruct((m, ODD_KVW), F32),
                   jax.ShapeDtypeStruct((m, ODD_KVW), BF16), jax.ShapeDtypeStruct((m, ODD_QW), BF16),
                   jax.ShapeDtypeStruct((m, ODD_KVW), F32), jax.ShapeDtypeStruct((m, ODD_KVW), BF16),
                   jax.ShapeDtypeStruct((m, LANES), F32)],
        compiler_params=_cparams(1),
        name="odd_post",
    )(pm, fz, moba_qk, fox_qk, fb)


def _topk_rank(score, n):
    idx = lax.broadcasted_iota(jnp.int32, (1, n), 1)
    rank = jnp.zeros(score.shape, jnp.int32)
    for mcol in range(n):
        sm = score[:, mcol:mcol + 1]
        ahead = (sm > score) | ((sm == score) & (mcol < idx))
        rank = rank + ahead.astype(jnp.int32)
    return rank


def _moba_gate_kernel(q_ref, k_ref, bm_ref, *, t_len, nbl):
    k_mean = jnp.mean(k_ref[...].reshape(nbl, MOBA_BLOCK, HEAD_DIM), axis=1)
    gate = lax.dot_general(k_mean.astype(BF16), q_ref[...], (((1,), (1,)), ((), ())), preferred_element_type=F32)
    cur = lax.broadcasted_iota(jnp.int32, (1, t_len), 1) // MOBA_BLOCK
    blk = lax.broadcasted_iota(jnp.int32, (nbl, 1), 0)
    past_ok = blk < cur
    gate = jnp.where(past_ok, gate, -jnp.inf)
    rank = jnp.zeros(gate.shape, jnp.int32)
    for mrow in range(nbl):
        sm = gate[mrow:mrow + 1, :]
        ahead = (sm > gate) | ((sm == gate) & (mrow < blk))
        rank = rank + ahead.astype(jnp.int32)
    sel = (rank < MOBA_TOPK) & past_ok & (jnp.abs(gate) < jnp.inf)
    mask_t = (sel | (blk == cur)).astype(F32)
    eye = (lax.broadcasted_iota(jnp.int32, (nbl, nbl), 0) == lax.broadcasted_iota(jnp.int32, (nbl, nbl), 1))
    bm_ref[...] = lax.dot_general(mask_t, eye.astype(F32), (((0,), (0,)), ((), ())), preferred_element_type=F32)


def _moba_gate(qm3, moba_new3):
    bsz, t_len, _ = qm3.shape
    assert t_len % MOBA_BLOCK == 0
    nbl = t_len // MOBA_BLOCK
    return pl.pallas_call(
        functools.partial(_moba_gate_kernel, t_len=t_len, nbl=nbl),
        grid=(bsz, MOBA_HEADS),
        in_specs=[pl.BlockSpec((None, t_len, HEAD_DIM), lambda b, h: (b, 0, h)),
                  pl.BlockSpec((None, t_len, HEAD_DIM), lambda b, h: (b, 0, h))],
        out_specs=pl.BlockSpec((None, None, t_len, nbl), lambda b, h: (b, h, 0, 0)),
        out_shape=jax.ShapeDtypeStruct((bsz, MOBA_HEADS, t_len, nbl), F32),
        compiler_params=_cparams(2),
        name="moba_gate",
    )(qm3, moba_new3)


def _cumsum_kernel(x_ref, o_ref, *, t_len):
    n = Q_TILE
    upper = (lax.broadcasted_iota(jnp.int32, (n, n), 0) <= lax.broadcasted_iota(jnp.int32, (n, n), 1)).astype(F32)
    carry = jnp.zeros((FOX_HEADS, 1), F32)
    for c in range(t_len // n):
        xt = x_ref[c * n:(c + 1) * n, :].T[0:FOX_HEADS, :]
        cum = jnp.dot(xt, upper, preferred_element_type=F32, precision=lax.Precision.HIGHEST) + carry
        o_ref[:, c * n:(c + 1) * n] = cum
        carry = cum[:, n - 1:n]


def _cumsum_heads(logf3):
    bsz, t_len, _ = logf3.shape
    assert t_len % Q_TILE == 0
    return pl.pallas_call(
        functools.partial(_cumsum_kernel, t_len=t_len),
        grid=(bsz,),
        in_specs=[pl.BlockSpec((None, t_len, LANES), lambda b: (b, 0, 0))],
        out_specs=pl.BlockSpec((None, FOX_HEADS, t_len), lambda b: (b, 0, 0)),
        out_shape=jax.ShapeDtypeStruct((bsz, FOX_HEADS, t_len), F32),
        compiler_params=_cparams(1),
        name="fox_cumsum",
    )(logf3)


MOBA_GATE_PAGES_PER_STEP = 8


def _moba_gate_step_kernel(*refs, pps, nblk, ppb):
    q_ref = refs[1]
    k_refs = refs[2:2 + pps]
    idx_ref, gate_ref = refs[2 + pps:]
    st = pl.program_id(1)

    @pl.when(st == 0)
    def _():
        gate_ref[...] = jnp.zeros_like(gate_ref)

    q8 = _group_rows(q_ref, MOBA_HEADS)
    lane = lax.broadcasted_iota(jnp.int32, (1, nblk), 1)
    for blk_in_step in range(pps // ppb):
        ksum = jnp.zeros((MOBA_HEADS, HEAD_DIM), F32)
        for p in range(ppb):
            ksum = ksum + jnp.sum(k_refs[blk_in_step * ppb + p][...], axis=0)
        gcol = jnp.sum(q8 * (ksum * (1.0 / MOBA_BLOCK)), axis=-1, keepdims=True)
        gate_ref[...] = jnp.where(lane == st * (pps // ppb) + blk_in_step, gcol, gate_ref[...])

    @pl.when(st == pl.num_programs(1) - 1)
    def _():
        gate = gate_ref[...]
        rank = _topk_rank(gate, nblk)
        blk = lax.broadcasted_iota(jnp.int32, (1, nblk), 1)
        lane = lax.broadcasted_iota(jnp.int32, (MOBA_HEADS, MOBA_TOPK), 1)
        out = jnp.full((MOBA_HEADS, MOBA_TOPK), -1, jnp.int32)
        for r in range(min(MOBA_TOPK, nblk)):
            hit = (rank == r) & (jnp.abs(gate) < jnp.inf)
            idx_r = jnp.sum(jnp.where(hit, blk + 1, 0), axis=-1, keepdims=True) - 1
            out = jnp.where(lane == r, idx_r, out)
        idx_ref[...] = out


def _moba_gate_step(qm3, pages4, tbl):
    bsz, npg = tbl.shape
    ppb = MOBA_BLOCK // PAGE_ROWS
    pps = MOBA_GATE_PAGES_PER_STEP
    assert npg % pps == 0 and pps % ppb == 0

    def key_spec(p):
        return pl.BlockSpec((None, PAGE_ROWS, MOBA_HEADS, HEAD_DIM), lambda b, st, t: (t[b, st * pps + p], 0, 0, 0))

    grid_spec = pltpu.PrefetchScalarGridSpec(
        num_scalar_prefetch=1,
        grid=(bsz, npg // pps),
        in_specs=[pl.BlockSpec((None, 1, ODD_QW), lambda b, st, t: (b, 0, 0))] + [key_spec(p) for p in range(pps)],
        out_specs=pl.BlockSpec((None, MOBA_HEADS, MOBA_TOPK), lambda b, st, t: (b, 0, 0)),
        scratch_shapes=[pltpu.VMEM((MOBA_HEADS, npg // ppb), F32)],
    )
    return pl.pallas_call(
        functools.partial(_moba_gate_step_kernel, pps=pps, nblk=npg // ppb, ppb=ppb),
        grid_spec=grid_spec,
        out_shape=jax.ShapeDtypeStruct((bsz, MOBA_HEADS, MOBA_TOPK), jnp.int32),
        compiler_params=_cparams(2),
        name="moba_gate_step",
    )(tbl, qm3, *([pages4] * pps))


def _moba_attn_step_kernel(*refs, t_pos, ppb):
    idx_ref, q_ref = refs[1], refs[2]
    kv_refs = refs[3:3 + ppb]
    kn_ref, vn_ref, rbt_ref, o_ref, m_ref, l_ref, acc_ref = refs[3 + ppb:]
    b = pl.program_id(0)
    h = pl.program_id(1)
    s_id = pl.program_id(2)
    sub = 8
    q = q_ref[...].astype(F32)
    rbt = rbt_ref[...]
    col = lambda bk: rbt[:, bk:bk + 1]

    @pl.when(s_id == 0)
    def _():
        s_self = jnp.sum(q * kn_ref[...], axis=-1, keepdims=True) + col(0)
        m_ref[...] = jnp.broadcast_to(s_self, m_ref.shape)
        l_ref[...] = jnp.ones_like(l_ref)
        acc_ref[...] = jnp.broadcast_to(vn_ref[...], acc_ref.shape)

    blk_id = idx_ref[b * MOBA_HEADS + h, s_id]

    @pl.when(blk_id >= 0)
    def _():
        q8 = jnp.broadcast_to(q, (sub, HEAD_DIM)).astype(BF16)
        for pg in range(ppb):
            kt = kv_refs[pg][pl.ds(h, PAGE_ROWS, stride=2 * MOBA_HEADS), :].astype(BF16)
            vt = kv_refs[pg][pl.ds(MOBA_HEADS + h, PAGE_ROWS, stride=2 * MOBA_HEADS), :].astype(BF16)
            s = lax.dot_general(q8, kt, (((1,), (1,)), ((), ())), preferred_element_type=F32)
            key_pos = blk_id * MOBA_BLOCK + pg * PAGE_ROWS + lax.broadcasted_iota(jnp.int32, (sub, PAGE_ROWS), 1)
            s = s + _bucket_bias(t_pos - key_pos, col)
            m_old = m_ref[...]
            m_new = jnp.maximum(m_old, jnp.max(s, axis=-1, keepdims=True))
            p = jnp.exp(s - m_new)
            alpha = jnp.exp(m_old - m_new)
            l_ref[...] = alpha * l_ref[...] + jnp.sum(p, axis=-1, keepdims=True)
            acc_ref[...] = alpha * acc_ref[...] + jnp.dot(p.astype(BF16), vt, preferred_element_type=F32)
            m_ref[...] = m_new

    @pl.when(s_id == pl.num_programs(2) - 1)
    def _():
        o_ref[...] = (acc_ref[0:1, :] / l_ref[0:1, :]).astype(o_ref.dtype)


def _moba_attn_step(qm3, pages3, tbl, idx, moba_new3, rbt3, *, t_pos):
    bsz = qm3.shape[0]
    ppb = MOBA_BLOCK // PAGE_ROWS

    def cache_map(pg):
        def index_map(b, h, s, tbl_ref, idx_ref):
            blk = jnp.maximum(idx_ref[b * MOBA_HEADS + h, s], 0)
            return (tbl_ref[b, blk * ppb + pg], 0, 0)
        return index_map

    head = lambda off: pl.BlockSpec((None, 1, HEAD_DIM), lambda b, h, s, t, i: (b, 0, off + h))
    grid_spec = pltpu.PrefetchScalarGridSpec(
        num_scalar_prefetch=2,
        grid=(bsz, MOBA_HEADS, MOBA_TOPK),
        in_specs=[head(0)]
        + [pl.BlockSpec((None, PAGE_ROWS * 2 * MOBA_HEADS, HEAD_DIM), cache_map(pg)) for pg in range(ppb)]
        + [head(0), head(MOBA_HEADS),
           pl.BlockSpec((None, 1, REL_BUCKETS), lambda b, h, s, t, i: (h, 0, 0))],
        out_specs=head(0),
        scratch_shapes=[pltpu.VMEM((8, 1), F32), pltpu.VMEM((8, 1), F32), pltpu.VMEM((8, HEAD_DIM), F32)],
    )
    return pl.pallas_call(
        functools.partial(_moba_attn_step_kernel, t_pos=t_pos, ppb=ppb),
        grid_spec=grid_spec,
        out_shape=jax.ShapeDtypeStruct((bsz, 1, ODD_QW), BF16),
        compiler_params=_cparams(3),
        name="moba_attn_step",
    )(tbl, idx, qm3, *([pages3] * ppb), moba_new3, moba_new3, rbt3)


FOX_PAGES_PER_STEP = 4


def _fox_step_kernel(*refs, pps):
    q_ref = refs[1]
    k_refs = refs[2:2 + pps]
    v_refs = refs[2 + pps:2 + 2 * pps]
    lf_refs = refs[2 + 2 * pps:2 + 3 * pps]
    kn_ref, vn_ref, lfn_ref, o_ref, m_ref, l_ref, acc_ref, carry_ref = refs[2 + 3 * pps:]
    st = pl.program_id(1)
    n = PAGE_ROWS
    nh = FOX_HEADS
    q8 = _group_rows(q_ref, nh)
    lane8 = lax.broadcasted_iota(jnp.int32, (nh, LANES), 1)
    row8 = lax.broadcasted_iota(jnp.int32, (nh, LANES), 0)

    @pl.when(st == 0)
    def _():
        m_ref[...] = jnp.sum(q8 * _group_rows(kn_ref, nh), axis=-1, keepdims=True)
        l_ref[...] = jnp.ones_like(l_ref)
        acc_ref[...] = _group_rows(vn_ref, nh)
        carry_ref[...] = jnp.sum(jnp.where(lane8 == row8, lfn_ref[...], 0.0), axis=-1, keepdims=True)

    ones = jnp.ones((HEAD_DIM, LANES), BF16)
    after_t = (lax.broadcasted_iota(jnp.int32, (n, n), 0) > lax.broadcasted_iota(jnp.int32, (n, n), 1)).astype(F32)
    on_diag = (lax.broadcasted_iota(jnp.int32, (n, nh, LANES), 0)
               == lax.broadcasted_iota(jnp.int32, (n, nh, LANES), 2))
    carry = carry_ref[...]
    logits = []
    for p in range(pps):
        lf = lf_refs[p][...]
        bias = carry + jnp.dot(lf, after_t, preferred_element_type=F32, precision=lax.Precision.HIGHEST)
        carry = carry + jnp.sum(lf, axis=-1, keepdims=True)
        prod = (k_refs[p][...] * q8[None]).reshape(n * nh, HEAD_DIM).astype(BF16)
        qk_rep = jnp.dot(prod, ones, preferred_element_type=F32).reshape(n, nh, LANES)
        logits.append(jnp.sum(jnp.where(on_diag, qk_rep, 0.0), axis=0) + bias)
    carry_ref[...] = carry
    s = jnp.concatenate(logits, axis=-1)
    m_old = m_ref[...]
    m_new = jnp.maximum(m_old, jnp.max(s, axis=-1, keepdims=True))
    prob = jnp.exp(s - m_new)
    alpha = jnp.exp(m_old - m_new)
    l_ref[...] = alpha * l_ref[...] + jnp.sum(prob, axis=-1, keepdims=True)
    m_ref[...] = m_new
    acc = alpha * acc_ref[...]
    for p in range(pps):
        prob_p = prob[:, p * n:(p + 1) * n]
        spread = jnp.where(on_diag, prob_p[None], 0.0).reshape(n * nh, LANES).astype(BF16)
        p_rep = jnp.dot(spread, ones, preferred_element_type=F32).reshape(n, nh, HEAD_DIM)
        acc = acc + jnp.sum(p_rep * v_refs[p][...], axis=0)
    acc_ref[...] = acc

    @pl.when(st == pl.num_programs(1) - 1)
    def _():
        _store_group_rows(o_ref, acc_ref[...] / l_ref[...])


def _fox_step(qf3, pages4, logf_pages, tbl, fox_new3, logf_new3):
    bsz, npg = tbl.shape
    pps = math.gcd(FOX_PAGES_PER_STEP, npg)
    page = lambda p: (lambda b, st, t: t[b, npg - 1 - (st * pps + p)])
    kv_spec = lambda p, part: pl.BlockSpec((None, PAGE_ROWS, FOX_HEADS, HEAD_DIM),
                                           lambda b, st, t: (page(p)(b, st, t), 0, part, 0))
    lf_spec = lambda p: pl.BlockSpec((None, FOX_HEADS, PAGE_ROWS), lambda b, st, t: (page(p)(b, st, t), 0, 0))
    new = lambda col: pl.BlockSpec((None, 1, ODD_QW), lambda b, st, t: (b, 0, col))
    grid_spec = pltpu.PrefetchScalarGridSpec(
        num_scalar_prefetch=1,
        grid=(bsz, npg // pps),
        in_specs=[new(0)] + [kv_spec(p, 0) for p in range(pps)] + [kv_spec(p, 1) for p in range(pps)]
        + [lf_spec(p) for p in range(pps)]
        + [new(0), new(1), pl.BlockSpec((None, 1, LANES), lambda b, st, t: (b, 0, 0))],
        out_specs=new(0),
        scratch_shapes=[pltpu.VMEM((FOX_HEADS, 1), F32), pltpu.VMEM((FOX_HEADS, 1), F32),
                        pltpu.VMEM((FOX_HEADS, HEAD_DIM), F32), pltpu.VMEM((FOX_HEADS, 1), F32)],
    )
    return pl.pallas_call(
        functools.partial(_fox_step_kernel, pps=pps),
        grid_spec=grid_spec,
        out_shape=jax.ShapeDtypeStruct((bsz, 1, ODD_QW), BF16),
        compiler_params=_cparams(2),
        name="fox_step",
    )(tbl, qf3, *([pages4] * (2 * pps)), *([logf_pages] * pps), fox_new3, fox_new3, logf_new3)


def _odd_proj(h, hr, w, j):
    n_main = 2 * (ODD_QW + ODD_KVW)
    w_in = w["w_in_odd"]
    hr_parts = None if hr is None else [hr]
    pm, pmr = _dense2([h], hr_parts, w_in, j, 0, n_main, tn=512, name="odd_in")
    w_fz = jnp.pad(w_in[j, :, n_main:n_main + FOX_HEADS], ((0, 0), (0, LANES - FOX_HEADS)))
    fz, fzr = _dense2([h], hr_parts, w_fz, None, 0, LANES, tn=LANES, name="odd_in_forget")
    return (pm, fz), (None if hr is None else (pmr, fzr))


def _odd_out(x2d, a_parts, xr2d, ar_parts, w, j):
    d = x2d.shape[1]
    return _dense2(a_parts, ar_parts, w["w_out_odd"], j, 0, d, tn=min(d, 512), mode="res", res=x2d,
                   r_res=xr2d, scale=1.0, name="odd_out")


def _odd_mixer(x2d, h, bsz, t_len, j, w, tabs, past):
    proj, _ = _odd_proj(h, None, w, j)
    a_parts, new = _odd_core(proj, bsz, t_len, j, w, tabs, past)
    x_new, _ = _odd_out(x2d, a_parts, None, None, w, j)
    return (x_new,) + new


def _odd_core(proj, bsz, t_len, j, w, tabs, past):
    pm, fz = proj
    m = pm.shape[0]
    fb = jnp.pad(w["fox_f_bias"][j].astype(F32), (0, LANES - FOX_HEADS)).reshape(1, LANES)
    qm, moba_new, mobab, qf, fox_new, foxb, logf = _odd_post(pm, fz, w["moba_qk_norm"][j], w["fox_qk_norm"][j], fb)
    qm3 = qm.reshape(bsz, t_len, ODD_QW)
    qf3 = qf.reshape(bsz, t_len, ODD_QW)
    logf3 = logf.reshape(bsz, t_len, LANES)
    if past is None:
        bm = _moba_gate(qm3, moba_new.reshape(bsz, t_len, ODD_KVW))
        o_m = _flash("moba", qm3, mobab.reshape(bsz, t_len, ODD_KVW), 0, MOBA_HEADS, MOBA_HEADS,
                     bias_tiles=tabs["tiles"], bm=bm, blk=MOBA_BLOCK, out_dtype=BF16)
        cum = _cumsum_heads(logf3).reshape(bsz, FOX_HEADS, 1, t_len)
        o_f = _flash("fox", qf3, foxb.reshape(bsz, t_len, ODD_KVW), 0, FOX_HEADS, FOX_HEADS, cum=cum,
                     out_dtype=BF16)
    else:
        moba_pages4, fox_pages, logf_pages, tbl = past
        t_pos = tbl.shape[1] * PAGE_ROWS
        idx = _moba_gate_step(qm3, moba_pages4, tbl)
        rbt3 = w["rel_bias"].T.reshape(MOBA_HEADS, 1, REL_BUCKETS)
        moba_pages3 = moba_pages4.reshape(moba_pages4.shape[0], PAGE_ROWS * 2 * MOBA_HEADS, HEAD_DIM)
        o_m = _moba_attn_step(qm3, moba_pages3, tbl, idx.reshape(bsz * MOBA_HEADS, MOBA_TOPK),
                              moba_new.reshape(bsz, t_len, ODD_KVW), rbt3, t_pos=t_pos)
        o_f = _fox_step(qf3, fox_pages, logf_pages, tbl, fox_new.reshape(bsz, t_len, ODD_KVW), logf3)
    kv_shape = (bsz, t_len, 2, ODD_HEADS, HEAD_DIM)
    new = (moba_new.reshape(kv_shape), fox_new.reshape(kv_shape), logf3[:, :, :FOX_HEADS])
    return [o_m.reshape(m, ODD_QW), o_f.reshape(m, ODD_QW)], new


def _trunk(x_prompt, x_sample, w, tabs, caches):
    bp, tp, d = x_prompt.shape
    bs, ts, _ = x_sample.shape
    depth = w["norm_mix"].shape[0]
    xp = x_prompt.reshape(bp * tp, d)
    xs = x_sample.reshape(bs * ts, d)
    pe, po, se, so = [], [], [], []
    for layer in range(depth):
        xp, xs = _ffn(xp, xs, w["norm_ffn1"], w["w_ffn1_in"], w["w_ffn1_out"], layer)
        hp = _rms_cast(xp, w["norm_mix"], layer)
        hs = _rms_cast(xs, w["norm_mix"], layer)
        j = layer // 2
        tbl = caches["tbl"] + j * caches["n_phys"]
        if layer % 2 == 0:
            proj_p, proj_s = _even_proj(hp, hs, w, j)
            ap, new_p = _even_core(proj_p, bp, tp, j, w, tabs, None)
            past = (caches["cmp"], caches["sel"], tbl, caches["win"], caches["hgrn"][j])
            a_s, new_s = _even_core(proj_s, bs, ts, j, w, tabs, past)
            xp, xs = _even_out(xp, ap, xs, a_s, w, j)
            pe.append(new_p)
            se.append(new_s)
        else:
            proj_p, proj_s = _odd_proj(hp, hs, w, j)
            ap, new_p = _odd_core(proj_p, bp, tp, j, w, tabs, None)
            past = (caches["moba"], caches["fox"], caches["logf"], tbl)
            a_s, new_s = _odd_core(proj_s, bs, ts, j, w, tabs, past)
            xp, xs = _odd_out(xp, ap, xs, a_s, w, j)
            po.append(new_p)
            so.append(new_s)
        xp, xs = _ffn(xp, xs, w["norm_ffn2"], w["w_ffn2_in"], w["w_ffn2_out"], layer)
    return xp.reshape(bp, tp, d), xs.reshape(bs, ts, d), pe, po, se, so


def kernel(x_prompt, x_sample, cache_nsa_cmp, cache_nsa_sel, cache_moba, cache_fox, cache_fox_logf,
           cache_nsa_win, state_hgrn, page_table, norm_ffn1, w_ffn1_in, w_ffn1_out, norm_mix, norm_ffn2,
           w_ffn2_in, w_ffn2_out, rel_bias, w_in_even, w_out_even, nsa_q_norm, nsa_k_norm, nsa_phi_pos,
           nsa_phi_w1, nsa_phi_w2, hg_lb, hg_o_norm, w_in_odd, w_out_odd, fox_f_bias, moba_qk_norm,
           fox_qk_norm):
    w = dict(norm_ffn1=norm_ffn1, w_ffn1_in=w_ffn1_in, w_ffn1_out=w_ffn1_out, norm_mix=norm_mix,
             norm_ffn2=norm_ffn2, w_ffn2_in=w_ffn2_in, w_ffn2_out=w_ffn2_out, rel_bias=rel_bias,
             w_in_even=w_in_even, w_out_even=w_out_even, nsa_q_norm=nsa_q_norm, nsa_k_norm=nsa_k_norm,
             nsa_phi_pos=nsa_phi_pos, nsa_phi_w1=nsa_phi_w1, nsa_phi_w2=nsa_phi_w2, hg_lb=hg_lb,
             hg_o_norm=hg_o_norm, w_in_odd=w_in_odd, w_out_odd=w_out_odd, fox_f_bias=fox_f_bias,
             moba_qk_norm=moba_qk_norm, fox_qk_norm=fox_qk_norm)
    tabs = _tables(w)
    n_phys = cache_nsa_cmp.shape[1]
    n_layers = cache_nsa_cmp.shape[0]
    rows3 = lambda pool: pool.reshape(n_layers * n_phys, -1, HEAD_DIM)
    rows4 = lambda pool: pool.reshape(n_layers * n_phys, PAGE_ROWS, -1, HEAD_DIM)
    logf_t = jnp.swapaxes(cache_fox_logf.reshape(n_layers * n_phys, PAGE_ROWS, FOX_HEADS), 1, 2)
    caches = dict(cmp=rows3(cache_nsa_cmp), sel=rows3(cache_nsa_sel), moba=rows4(cache_moba),
                  fox=rows4(cache_fox), logf=logf_t, win=cache_nsa_win, hgrn=state_hgrn,
                  tbl=page_table.astype(jnp.int32), n_phys=n_phys)
    y_prompt, y_sample, pe, po, se, so = _trunk(x_prompt, x_sample, w, tabs, caches)

    stack = lambda items, i: jnp.stack([it[i] for it in items])
    return (y_prompt, y_sample,
            stack(pe, 0), stack(pe, 1), stack(pe, 2), stack(pe, 3),
            stack(po, 0), stack(po, 1), stack(po, 2),
            stack(se, 0), stack(se, 1), stack(se, 2), stack(se, 3),
            stack(so, 0), stack(so, 1), stack(so, 2))
```

```python
import functools
import math

import jax
import jax.numpy as jnp
import numpy as np
from jax import lax
from jax.experimental import pallas as pl
from jax.experimental.pallas import tpu as pltpu

F32 = jnp.float32
BF16 = jnp.bfloat16

HEAD_DIM = 128
NSA_HEADS = 8
NSA_KV_HEADS = 2
NSA_GROUP = NSA_HEADS // NSA_KV_HEADS
NSA_BLOCK = 64
NSA_TOPN = 16
NSA_WINDOW = 512
NSA_PHI_HIDDEN = 2 * HEAD_DIM
NSA_FORCE_SCORE = 1.0e4
HG_HEADS = 8
MOBA_HEADS = 8
MOBA_BLOCK = 256
MOBA_TOPK = 3
FOX_HEADS = 8
REL_BUCKETS = 32
REL_MAX_DIST = 128
EPS = 1e-6
PAGE_ROWS = 128

LANES = 128
VMEM_LIMIT_BYTES = 56 * 1024 * 1024
ROW_TILE = 512
DEEP_K = 2048
Q_TILE = 128
FLASH_TILE = 256
FLASH_HEADS_PER_STEP = 4
NEG_BIG = -1e30


def _cparams(n_axes):
    return pltpu.CompilerParams(dimension_semantics=("arbitrary",) * n_axes,
                                vmem_limit_bytes=VMEM_LIMIT_BYTES)


def _rel_bucket(dist):
    n = jnp.maximum(dist, 0)
    exact = REL_BUCKETS // 2
    nf = jnp.maximum(n, 1).astype(F32)
    big = exact + (jnp.log(nf / exact) / math.log(REL_MAX_DIST / exact) * (REL_BUCKETS - exact)).astype(jnp.int32)
    return jnp.where(n < exact, n, jnp.minimum(big, REL_BUCKETS - 1))


def _bucket_bias(dist, table_rows):
    bucket = _rel_bucket(dist)
    out = jnp.zeros(dist.shape, F32) + table_rows(0)
    for b in range(1, REL_BUCKETS):
        out = jnp.where(bucket == b, table_rows(b), out)
    return out


def _rms_rows(x, g):
    return x * lax.rsqrt(jnp.mean(x * x, axis=-1, keepdims=True) + EPS) * g


def _log_sigmoid(z):
    return jnp.minimum(z, 0.0) - jnp.log1p(jnp.exp(-jnp.abs(z)))


def _sigmoid(z):
    return 1.0 / (1.0 + jnp.exp(-z))


def _rms_cast_kernel(x_ref, g_ref, o_ref):
    o_ref[...] = _rms_rows(x_ref[...], g_ref[...]).astype(o_ref.dtype)


def _rms_cast(x2d, g_stack, layer):
    m, d = x2d.shape
    tm = min(m, ROW_TILE)
    g3 = g_stack.reshape(g_stack.shape[0], 1, d)
    return pl.pallas_call(
        _rms_cast_kernel,
        grid=(m // tm,),
        in_specs=[pl.BlockSpec((tm, d), lambda i: (i, 0)),
                  pl.BlockSpec((None, 1, d), lambda i: (layer, 0, 0))],
        out_specs=pl.BlockSpec((tm, d), lambda i: (i, 0)),
        out_shape=jax.ShapeDtypeStruct((m, d), BF16),
        compiler_params=_cparams(1),
        name="rms_cast",
    )(x2d, g3)


def _dense_kernel(*refs, n_a, mode, scale, rider):
    pos = 0
    a_refs = refs[pos:pos + n_a]
    pos += n_a
    ra_refs = refs[pos:pos + n_a] if rider else ()
    pos += len(ra_refs)
    w_ref = refs[pos]
    pos += 1
    w2_ref = res_ref = rres_ref = ro_ref = None
    if mode == "swiglu":
        w2_ref = refs[pos]
        pos += 1
    if mode == "res":
        res_ref = refs[pos]
        pos += 1
        if rider:
            rres_ref = refs[pos]
            pos += 1
    o_ref = refs[pos]
    pos += 1
    if rider:
        ro_ref = refs[pos]
        pos += 1
    wb_ref = refs[pos]
    pos += 1
    wb2_ref = refs[pos] if mode == "swiglu" else None

    def apply(in_refs, r_ref, out_ref):
        a = in_refs[0][...] if n_a == 1 else jnp.concatenate([r[...] for r in in_refs], axis=-1)
        y = jnp.dot(a, wb_ref[...], preferred_element_type=F32)
        if mode == "swiglu":
            y2 = jnp.dot(a, wb2_ref[...], preferred_element_type=F32)
            y = y * _sigmoid(y) * y2
        elif mode == "res":
            y = r_ref[...] + scale * y
        out_ref[...] = y.astype(out_ref.dtype)

    @pl.when(pl.program_id(1) == 0)
    def _():
        wb_ref[...] = w_ref[...].astype(BF16)
        if mode == "swiglu":
            wb2_ref[...] = w2_ref[...].astype(BF16)
        if rider:
            apply(ra_refs, rres_ref, ro_ref)

    apply(a_refs, res_ref, o_ref)


def _dense(a_parts, w, lead, col0, n_out, *, tn, mode="plain", res=None, scale=1.0,
           out_dtype=F32, col0_b=None, rider=None, name="dense"):
    m = a_parts[0].shape[0]
    k = sum(a.shape[1] for a in a_parts)
    tm = min(m, 2 * ROW_TILE if k <= DEEP_K else ROW_TILE)
    assert m % tm == 0 and n_out % tn == 0 and col0 % tn == 0
    if w.ndim == 3:
        wblock = (None, k, tn)

        def wmap(off):
            return lambda j, i: (lead, 0, j + off)
    else:
        wblock = (k, tn)

        def wmap(off):
            return lambda j, i: (0, j + off)
    in_specs = [pl.BlockSpec((tm, a.shape[1]), lambda j, i: (i, 0)) for a in a_parts]
    args = list(a_parts)
    ms = 0
    if rider is not None:
        r_parts, r_res = rider
        ms = r_parts[0].shape[0]
        assert [a.shape[1] for a in r_parts] == [a.shape[1] for a in a_parts]
        in_specs += [pl.BlockSpec((ms, a.shape[1]), lambda j, i: (0, 0)) for a in r_parts]
        args += list(r_parts)
    in_specs.append(pl.BlockSpec(wblock, wmap(col0 // tn)))
    args.append(w)
    scratch = [pltpu.VMEM((k, tn), BF16)]
    if mode == "swiglu":
        assert col0_b % tn == 0
        in_specs.append(pl.BlockSpec(wblock, wmap(col0_b // tn)))
        args.append(w)
        scratch.append(pltpu.VMEM((k, tn), BF16))
    if mode == "res":
        in_specs.append(pl.BlockSpec((tm, tn), lambda j, i: (i, j)))
        args.append(res)
        if rider is not None:
            in_specs.append(pl.BlockSpec((ms, tn), lambda j, i: (0, j)))
            args.append(r_res)
    out_specs = pl.BlockSpec((tm, tn), lambda j, i: (i, j))
    out_shape = jax.ShapeDtypeStruct((m, n_out), out_dtype)
    if rider is not None:
        out_specs = [out_specs, pl.BlockSpec((ms, tn), lambda j, i: (0, j))]
        out_shape = [out_shape, jax.ShapeDtypeStruct((ms, n_out), out_dtype)]
    return pl.pallas_call(
        functools.partial(_dense_kernel, n_a=len(a_parts), mode=mode, scale=scale, rider=rider is not None),
        grid=(n_out // tn, m // tm),
        in_specs=in_specs,
        out_specs=out_specs,
        out_shape=out_shape,
        scratch_shapes=scratch,
        compiler_params=_cparams(2),
        name=name,
    )(*args)


def _dense2(a_parts, r_parts, *args, res=None, r_res=None, **kw):
    if r_parts is None:
        return _dense(a_parts, *args, res=res, **kw), None
    return _dense(a_parts, *args, res=res, rider=(r_parts, r_res), **kw)


def _ffn(x2d, xr2d, norm_g, w_in, w_out, layer):
    d_ff = w_out.shape[1]
    xn = _rms_cast(x2d, norm_g, layer)
    xrn = None if xr2d is None else [_rms_cast(xr2d, norm_g, layer)]
    h, hr = _dense2([xn], xrn, w_in, layer, 0, d_ff, tn=512, mode="swiglu", col0_b=d_ff,
                    out_dtype=BF16, name="ffn_in")
    return _dense2([h], None if hr is None else [hr], w_out, layer, 0, x2d.shape[1], tn=512, mode="res",
                   res=x2d, r_res=xr2d, scale=0.5, name="ffn_out")


def _bias_tiles(rel_bias):
    i = jnp.arange(FLASH_TILE)[:, None]
    j = jnp.arange(FLASH_TILE)[None, :]
    dist = jnp.stack([i - j, FLASH_TILE + i - j])
    onehot = (_rel_bucket(dist)[..., None] == jnp.arange(REL_BUCKETS)).astype(F32)
    tiles = jnp.einsum("ktsb,bh->hkts", onehot, rel_bias.astype(F32), precision=lax.Precision.HIGHEST)
    return jnp.where((dist < 0)[None], NEG_BIG, tiles)


def _tables(w):
    lb_all = jnp.cumsum(jax.nn.softmax(w["hg_lb"].astype(F32), axis=0), axis=0)
    lb_all = lb_all - lb_all[0:1]
    lbh = lb_all.reshape(lb_all.shape[0], HG_HEADS, HEAD_DIM)
    lb_rows = jnp.stack([jnp.log(lbh), jnp.log1p(-lbh), 1.0 - lbh], axis=2)
    return {"tiles": _bias_tiles(w["rel_bias"]), "lb_rows": lb_rows, "phi_w1": w["nsa_phi_w1"].astype(BF16)}


def _bias_cmp_table(rel_bias, q_start, t_len, nb):
    t_pos = q_start + jnp.arange(t_len)
    dist = t_pos[:, None] - (jnp.arange(nb) * NSA_BLOCK + NSA_BLOCK - 1)[None, :]
    onehot = (_rel_bucket(dist)[..., None] == jnp.arange(REL_BUCKETS)).astype(F32)
    return jnp.einsum("tnb,bh->htn", onehot, rel_bias.astype(F32), precision=lax.Precision.HIGHEST)


def _even_post_kernel(pa_ref, graw_ref, qg_ref, kg_ref, q_ref, cmp_ref, sel_ref, win_ref,
                      selb_ref, winb_ref, gates_ref):
    scale = HEAD_DIM ** -0.5
    qg = qg_ref[...]
    for h in range(NSA_HEADS):
        sl = slice(h * HEAD_DIM, (h + 1) * HEAD_DIM)
        q_ref[:, sl] = (_rms_rows(pa_ref[:, sl], qg) * scale).astype(q_ref.dtype)
    base = NSA_HEADS * HEAD_DIM
    kv_w = 2 * NSA_KV_HEADS * HEAD_DIM
    cmp_ref[...] = pa_ref[:, base:base + kv_w]
    for which, (o_ref, ob_ref) in enumerate(((sel_ref, selb_ref), (win_ref, winb_ref))):
        off = base + (which + 1) * kv_w
        kg = kg_ref[which + 1:which + 2, :]
        for c in range(2 * NSA_KV_HEADS):
            src = pa_ref[:, off + c * HEAD_DIM: off + (c + 1) * HEAD_DIM]
            val = _rms_rows(src, kg) if c < NSA_KV_HEADS else src
            o_ref[:, c * HEAD_DIM:(c + 1) * HEAD_DIM] = val
            ob_ref[:, c * HEAD_DIM:(c + 1) * HEAD_DIM] = val.astype(BF16)
    gates_ref[...] = _sigmoid(graw_ref[...])


def _even_post(pa, graw, q_norm, k_norm):
    m = pa.shape[0]
    tm = min(m, ROW_TILE)
    kv_w = 2 * NSA_KV_HEADS * HEAD_DIM
    qw = NSA_HEADS * HEAD_DIM
    row = lambda w: pl.BlockSpec((tm, w), lambda i: (i, 0))
    full = lambda a: pl.BlockSpec(a.shape, lambda i: (0,) * a.ndim)
    qg = q_norm.reshape(1, HEAD_DIM)
    return pl.pallas_call(
        _even_post_kernel,
        grid=(m // tm,),
        in_specs=[row(pa.shape[1]), row(LANES), full(qg), full(k_norm)],
        out_specs=[row(qw), row(kv_w), row(kv_w), row(kv_w), row(kv_w), row(kv_w), row(LANES)],
        out_shape=[jax.ShapeDtypeStruct((m, qw), BF16)] + [jax.ShapeDtypeStruct((m, kv_w), F32)] * 3
        + [jax.ShapeDtypeStruct((m, kv_w), BF16)] * 2 + [jax.ShapeDtypeStruct((m, LANES), F32)],
        compiler_params=_cparams(1),
        name="even_post",
    )(pa, graw, qg, k_norm)


def _gelu_tanh(x):
    return 0.5 * x * (1.0 + jnp.tanh(math.sqrt(2.0 / math.pi) * (x + 0.044715 * (x * x * x))))


NSA_KV_COLS = 2 * NSA_KV_HEADS
CMP_PAGES_PER_STEP = 8
CMP_PAGES_PER_GROUP = 64


def _compress_mlp(x_of, nblk, pos_ref, w1_ref, w2_ref, kg_ref, o_ref, acc_ref):
    acc_ref[...] = jnp.zeros_like(acc_ref)

    def body(i2, carry):
        for c in range(NSA_KV_COLS):
            w = c // NSA_KV_HEADS
            xa = x_of(2 * i2, c) + pos_ref[w, pl.ds(2 * i2, 1), :]
            xb = x_of(2 * i2 + 1, c) + pos_ref[w, pl.ds(2 * i2 + 1, 1), :]
            x = jnp.concatenate([xa, xb], axis=-1).astype(BF16)
            wi = w1_ref[w, pl.ds(pl.multiple_of(i2 * 2 * HEAD_DIM, 2 * HEAD_DIM), 2 * HEAD_DIM), :]
            acc_ref[c] += jnp.dot(x, wi, preferred_element_type=F32)
        return carry

    lax.fori_loop(0, NSA_BLOCK // 2, body, 0)
    for c in range(NSA_KV_COLS):
        w, kh = divmod(c, NSA_KV_HEADS)
        hid = _gelu_tanh(acc_ref[c]).astype(BF16)
        y = jnp.dot(hid, w2_ref[w].astype(BF16), preferred_element_type=F32)
        if w == 0:
            y = _rms_rows(y, kg_ref[0:1, :])
        o_ref[w, :, kh * HEAD_DIM:(kh + 1) * HEAD_DIM] = y


def _compress_rows_kernel(*refs, nblk):
    x_refs = refs[:NSA_KV_COLS]
    pos_ref, w1_ref, w2_ref, kg_ref, o_ref, acc_ref = refs[NSA_KV_COLS:]

    def x_of(i, c):
        return x_refs[c][pl.ds(i, nblk, stride=NSA_BLOCK), :]

    _compress_mlp(x_of, nblk, pos_ref, w1_ref, w2_ref, kg_ref, o_ref, acc_ref)


def _compress_pages_kernel(*refs, pps, nblk):
    tbl_ref = refs[0]
    page_refs = refs[1:1 + pps]
    pos_ref, w1_ref, w2_ref, kg_ref, o_ref, xs_ref, acc_ref = refs[1 + pps:]
    st = pl.program_id(2)
    page_rows = PAGE_ROWS * NSA_KV_COLS
    for p in range(pps):
        xs_ref[pl.ds(pl.multiple_of((st * pps + p) * page_rows, page_rows), page_rows), :] = page_refs[p][...]

    @pl.when(st == pl.num_programs(2) - 1)
    def _():
        def x_of(i, c):
            return xs_ref[pl.ds(i * NSA_KV_COLS + c, nblk, stride=NSA_BLOCK * NSA_KV_COLS), :]

        _compress_mlp(x_of, nblk, pos_ref, w1_ref, w2_ref, kg_ref, o_ref, acc_ref)


def _compress(src, tbl, pos, w1b, w2, k_norm, j):
    weight_specs = lambda nidx: [
        pl.BlockSpec((None, 2, NSA_BLOCK, HEAD_DIM), lambda *a: (j, 0, 0, 0)),
        pl.BlockSpec((None, 2, NSA_BLOCK * HEAD_DIM, NSA_PHI_HIDDEN), lambda *a: (j, 0, 0, 0),
                     pipeline_mode=pl.Buffered(1)),
        pl.BlockSpec((None, 2, NSA_PHI_HIDDEN, HEAD_DIM), lambda *a: (j, 0, 0, 0)),
        pl.BlockSpec(k_norm.shape, lambda *a: (0, 0)),
    ]
    kvw = NSA_KV_HEADS * HEAD_DIM
    if tbl is None:
        bsz, t_len, _ = src.shape
        nblk = t_len // NSA_BLOCK
        return pl.pallas_call(
            functools.partial(_compress_rows_kernel, nblk=nblk),
            grid=(bsz,),
            in_specs=[pl.BlockSpec((None, t_len, HEAD_DIM), lambda b, c=c: (b, 0, c)) for c in range(NSA_KV_COLS)]
            + weight_specs(1),
            out_specs=pl.BlockSpec((2, None, nblk, kvw), lambda b: (0, b, 0, 0)),
            out_shape=jax.ShapeDtypeStruct((2, bsz, nblk, kvw), F32),
            scratch_shapes=[pltpu.VMEM((NSA_KV_COLS, nblk, NSA_PHI_HIDDEN), F32)],
            compiler_params=_cparams(1),
            name="nsa_compress_rows",
        )(*([src] * NSA_KV_COLS), pos, w1b, w2, k_norm)
    bsz, npg = tbl.shape
    pps = math.gcd(CMP_PAGES_PER_STEP, npg)
    ppg = math.gcd(CMP_PAGES_PER_GROUP, npg)
    bpp = PAGE_ROWS // NSA_BLOCK
    nblk = ppg * bpp
    page_rows = PAGE_ROWS * NSA_KV_COLS

    def page_spec(p):
        return pl.BlockSpec((None, page_rows, HEAD_DIM),
                            lambda b, grp, st, t: (t[b, grp * ppg + st * pps + p], 0, 0))

    grid_spec = pltpu.PrefetchScalarGridSpec(
        num_scalar_prefetch=1,
        grid=(bsz, npg // ppg, ppg // pps),
        in_specs=[page_spec(p) for p in range(pps)] + weight_specs(4),
        out_specs=pl.BlockSpec((2, None, nblk, kvw), lambda b, grp, st, t: (0, b, grp, 0)),
        scratch_shapes=[pltpu.VMEM((ppg * page_rows, HEAD_DIM), F32),
                        pltpu.VMEM((NSA_KV_COLS, nblk, NSA_PHI_HIDDEN), F32)],
    )
    return pl.pallas_call(
        functools.partial(_compress_pages_kernel, pps=pps, nblk=nblk),
        grid_spec=grid_spec,
        out_shape=jax.ShapeDtypeStruct((2, bsz, npg * bpp, kvw), F32),
        compiler_params=_cparams(3),
        name="nsa_compress_pages",
    )(tbl, *([src] * pps), pos, w1b, w2, k_norm)


def _nsa_cmp_kernel(q_ref, kc_ref, vc_ref, bias_ref, gates_ref, oc_ref, sel_ref, *, q_start, tq, nb, extra):
    kh = pl.program_id(1)
    qi = pl.program_id(2)
    t_pos = q_start + qi * tq + lax.broadcasted_iota(jnp.int32, (tq, 1), 0)
    blk = lax.broadcasted_iota(jnp.int32, (1, nb), 1)
    valid = t_pos >= blk * NSA_BLOCK + (NSA_BLOCK - 1)
    kc = kc_ref[...].astype(BF16)
    vc = vc_ref[...].astype(BF16)
    gates = gates_ref[...]
    imp = jnp.zeros((tq, nb), F32)
    for g in range(NSA_GROUP):
        qg = q_ref[:, g * HEAD_DIM:(g + 1) * HEAD_DIM]
        s = lax.dot_general(qg, kc, (((1,), (1,)), ((), ())), preferred_element_type=F32) + bias_ref[g]
        s = jnp.where(valid, s, NEG_BIG)
        m = jnp.max(s, axis=-1, keepdims=True)
        p = jnp.where(valid, jnp.exp(s - m), 0.0)
        l = jnp.sum(p, axis=-1, keepdims=True)
        p = p / jnp.where(l > 0, l, 1.0)
        imp = imp + p
        o = jnp.dot(p.astype(BF16), vc, preferred_element_type=F32)
        onehot = lax.broadcasted_iota(jnp.int32, (1, LANES), 1) == (kh * NSA_GROUP + g) * 3
        gcol = jnp.sum(jnp.where(onehot, gates, 0.0), axis=-1, keepdims=True)
        oc_ref[:, g * HEAD_DIM:(g + 1) * HEAD_DIM] = o * gcol
    cur = t_pos // NSA_BLOCK
    forced = (blk == 0) | (blk == cur) | (blk == cur - 1)
    score = jnp.where(forced, NSA_FORCE_SCORE, jnp.where(blk <= cur, imp, -1.0))
    if not extra and tq == LANES and nb <= LANES:
        score_t = jnp.concatenate([score, jnp.zeros((tq, LANES - nb), F32)], axis=1).T[0:nb, :]
        blk_t = lax.broadcasted_iota(jnp.int32, (nb, 1), 0)
        rank_t = jnp.zeros((nb, tq), jnp.int32)
        for mrow in range(nb):
            sm = score_t[mrow:mrow + 1, :]
            ahead = (sm > score_t) | ((sm == score_t) & (mrow < blk_t))
            rank_t = rank_t + ahead.astype(jnp.int32)
        sel_t = (rank_t < NSA_TOPN).astype(F32)
        sel_ref[...] = jnp.concatenate([sel_t, jnp.zeros((LANES - nb, tq), F32)], axis=0).T[:, 0:nb]
        return
    rank = jnp.zeros((tq, nb), jnp.int32)
    for mcol in range(nb):
        sm = score[:, mcol:mcol + 1]
        ahead = (sm > score) | ((sm == score) & (mcol < blk))
        rank = rank + ahead.astype(jnp.int32)
    if not extra:
        sel_ref[...] = (rank < NSA_TOPN).astype(F32)
    else:
        rank = rank + (score < NSA_FORCE_SCORE).astype(jnp.int32)
        rank_extra = jnp.sum((score >= NSA_FORCE_SCORE).astype(jnp.int32), axis=-1, keepdims=True)
        lane = lax.broadcasted_iota(jnp.int32, (tq, NSA_TOPN), 1)
        out = jnp.zeros((tq, NSA_TOPN), jnp.int32)
        for r in range(NSA_TOPN):
            idx_r = jnp.sum(jnp.where(rank == r, blk, 0), axis=-1, keepdims=True)
            idx_r = idx_r + jnp.where(rank_extra == r, nb, 0)
            out = jnp.where(lane == r, idx_r, out)
        sel_ref[...] = out


def _nsa_cmp(q3, kvc, bias_c, gates3, *, q_start, extra):
    bsz, t_len, _ = q3.shape
    nb = kvc.shape[2]
    tq = min(t_len, Q_TILE)
    gw = NSA_GROUP * HEAD_DIM
    if extra:
        assert t_len == 1 and q_start // NSA_BLOCK == nb
        sel_shape = jax.ShapeDtypeStruct((bsz, NSA_KV_HEADS, t_len, NSA_TOPN), jnp.int32)
        sel_spec = pl.BlockSpec((None, None, tq, NSA_TOPN), lambda b, kh, qi: (b, kh, qi, 0))
    else:
        assert (q_start + t_len) == nb * NSA_BLOCK
        sel_shape = jax.ShapeDtypeStruct((bsz, NSA_KV_HEADS, t_len, nb), F32)
        sel_spec = pl.BlockSpec((None, None, tq, nb), lambda b, kh, qi: (b, kh, qi, 0))
    return pl.pallas_call(
        functools.partial(_nsa_cmp_kernel, q_start=q_start, tq=tq, nb=nb, extra=extra),
        grid=(bsz, NSA_KV_HEADS, t_len // tq),
        in_specs=[
            pl.BlockSpec((None, tq, gw), lambda b, kh, qi: (b, qi, kh)),
            pl.BlockSpec((None, None, nb, HEAD_DIM), lambda b, kh, qi: (0, b, 0, kh)),
            pl.BlockSpec((None, None, nb, HEAD_DIM), lambda b, kh, qi: (1, b, 0, kh)),
            pl.BlockSpec((NSA_GROUP, tq, nb), lambda b, kh, qi: (kh, qi, 0)),
            pl.BlockSpec((None, tq, LANES), lambda b, kh, qi: (b, qi, 0)),
        ],
        out_specs=[pl.BlockSpec((None, tq, gw), lambda b, kh, qi: (b, qi, kh)), sel_spec],
        out_shape=[jax.ShapeDtypeStruct((bsz, t_len, NSA_HEADS * HEAD_DIM), F32), sel_shape],
        compiler_params=_cparams(3),
        name="nsa_cmp_attn",
    )(q3, kvc, kvc, bias_c, gates3)


def _lane_column(x, col):
    onehot = lax.broadcasted_iota(jnp.int32, (1, x.shape[1]), 1) == col
    return jnp.sum(jnp.where(onehot, x, 0.0), axis=-1, keepdims=True)


def _row_to_column(row):
    n = row.shape[1]
    eye = lax.broadcasted_iota(jnp.int32, (n, n), 0) == lax.broadcasted_iota(jnp.int32, (n, n), 1)
    return jnp.sum(jnp.where(eye, row, 0.0), axis=-1, keepdims=True)


def _flash_kernel(*refs, kind, hps, shared_kv, tq, blk, gate_branch):
    q_ref, k_ref, v_ref = refs[:3]
    pos = 3
    bias_ref = bm_ref = gates_ref = c_ref = None
    if kind in ("nsa_sel", "nsa_win", "moba"):
        bias_ref = refs[pos]
        pos += 1
    if kind in ("nsa_sel", "moba"):
        bm_ref = refs[pos]
        pos += 1
    if kind in ("nsa_sel", "nsa_win"):
        gates_ref = refs[pos]
        pos += 1
    if kind == "fox":
        c_ref = refs[pos]
        pos += 1
    o_ref = refs[pos]
    m_refs = refs[pos + 1:pos + 1 + hps]
    l_refs = refs[pos + 1 + hps:pos + 1 + 2 * hps]
    acc_refs = refs[pos + 1 + 2 * hps:pos + 1 + 3 * hps]
    tk = tq
    step = pl.program_id(1)
    qi = pl.program_id(2)
    q0 = pl.multiple_of(qi * tq, tq)
    ii = lax.broadcasted_iota(jnp.int32, (tq, tk), 0)
    jj = lax.broadcasted_iota(jnp.int32, (tq, tk), 1)
    n_back = NSA_WINDOW // tk
    lo = jnp.maximum(qi - n_back, 0) if kind == "nsa_win" else 0
    for g in range(hps):
        m_refs[g][...] = jnp.full(m_refs[g].shape, NEG_BIG, F32)
        l_refs[g][...] = jnp.zeros(l_refs[g].shape, F32)
        acc_refs[g][...] = jnp.zeros(acc_refs[g].shape, F32)

    def head_cols(g):
        return slice(g * HEAD_DIM, (g + 1) * HEAD_DIM)

    def tile_step(ki, diag):
        k0 = pl.multiple_of(ki * tk, tk)
        delta = qi - ki
        shared_add = None
        if kind == "nsa_sel":
            bm = bm_ref[...]
            shared_add = (_lane_column(bm, k0 // blk) - 1.0) * (-NEG_BIG)
            for sub in range(1, tk // blk):
                shared_add = jnp.where(jj >= sub * blk, (_lane_column(bm, k0 // blk + sub) - 1.0) * (-NEG_BIG),
                                       shared_add)
        elif kind == "nsa_win" and not diag:
            shared_add = jnp.where(delta == n_back, jnp.where(jj < ii, NEG_BIG, 0.0), 0.0)
        elif kind == "fox" and diag:
            shared_add = jnp.where(jj <= ii, 0.0, NEG_BIG)
        for g in range(hps):
            kv_cols = slice(0, HEAD_DIM) if shared_kv else head_cols(g)
            kt = k_ref[pl.ds(k0, tk), kv_cols]
            vt = v_ref[pl.ds(k0, tk), kv_cols]
            s = lax.dot_general(q_ref[:, head_cols(g)], kt, (((1,), (1,)), ((), ())), preferred_element_type=F32)
            if kind == "fox":
                s = s + (c_ref[g, :, pl.ds(q0, LANES)][:, 0:1] - c_ref[g, :, pl.ds(k0, tk)])
            elif diag:
                s = s + bias_ref[g, 0]
            else:
                near = bias_ref[g, 1]
                s = s + jnp.where(delta == 1, near, near[tq - 1:tq, 0:1])
            if shared_add is not None:
                s = s + shared_add
            if kind == "moba":
                s = s + (_lane_column(bm_ref[g], k0 // blk) - 1.0) * (-NEG_BIG)
            m_old = m_refs[g][...]
            m_new = jnp.maximum(m_old, jnp.max(s, axis=-1, keepdims=True))
            alpha = jnp.exp(m_old - m_new)
            parts = [jnp.exp(s[:, c * LANES:(c + 1) * LANES] - m_new) for c in range(tk // LANES)]
            l_refs[g][...] = alpha * l_refs[g][...] + functools.reduce(lambda a, b: a + b, parts)
            p = jnp.concatenate(parts, axis=-1).astype(BF16)
            acc_refs[g][...] = alpha * acc_refs[g][...] + jnp.dot(p, vt, preferred_element_type=F32)
            m_refs[g][...] = m_new

    def body(ki, carry):
        tile_step(ki, False)
        return carry

    lax.fori_loop(lo, qi, body, 0)
    tile_step(qi, True)
    for g in range(hps):
        o = acc_refs[g][...] / jnp.sum(l_refs[g][...], axis=-1, keepdims=True)
        if gates_ref is not None:
            o = o * _lane_column(gates_ref[...], (step * hps + g) * 3 + gate_branch)
        o_ref[:, head_cols(g)] = o.astype(o_ref.dtype)


def _flash(kind, q3, kv3, k_col, v_col, n_kv_heads, *, bias_tiles=None, bm=None, gates3=None, cum=None,
           blk=0, gate_branch=0, out_dtype=F32):
    bsz, t_len, qw = q3.shape
    n_heads = qw // HEAD_DIM
    shared_kv = n_kv_heads < n_heads
    hps = n_heads // n_kv_heads if shared_kv else FLASH_HEADS_PER_STEP
    tq = FLASH_TILE
    assert t_len % tq == 0 and (shared_kv or (k_col % hps == 0 and v_col % hps == 0))
    gw = hps * HEAD_DIM
    if shared_kv:
        kv_spec = lambda c0: pl.BlockSpec((None, t_len, HEAD_DIM), lambda b, h, qi: (b, 0, c0 + h))
    else:
        kv_spec = lambda c0: pl.BlockSpec((None, t_len, gw), lambda b, h, qi: (b, 0, c0 // hps + h))
    in_specs = [pl.BlockSpec((None, tq, gw), lambda b, h, qi: (b, qi, h)), kv_spec(k_col), kv_spec(v_col)]
    args = [q3, kv3, kv3]
    if bias_tiles is not None:
        in_specs.append(pl.BlockSpec((hps, 2, tq, tq), lambda b, h, qi: (h, 0, 0, 0)))
        args.append(bias_tiles)
    if bm is not None:
        nb = bm.shape[-1]
        if shared_kv:
            in_specs.append(pl.BlockSpec((None, None, tq, nb), lambda b, h, qi: (b, h, qi, 0)))
        else:
            in_specs.append(pl.BlockSpec((None, hps, tq, nb), lambda b, h, qi: (b, h, qi, 0)))
        args.append(bm)
    if gates3 is not None:
        in_specs.append(pl.BlockSpec((None, tq, LANES), lambda b, h, qi: (b, qi, 0)))
        args.append(gates3)
    if cum is not None:
        in_specs.append(pl.BlockSpec((None, hps, 1, t_len), lambda b, h, qi: (b, h, 0, 0)))
        args.append(cum)
    return pl.pallas_call(
        functools.partial(_flash_kernel, kind=kind, hps=hps, shared_kv=shared_kv, tq=tq, blk=blk,
                          gate_branch=gate_branch),
        grid=(bsz, n_heads // hps, t_len // tq),
        in_specs=in_specs,
        out_specs=pl.BlockSpec((None, tq, gw), lambda b, h, qi: (b, qi, h)),
        out_shape=jax.ShapeDtypeStruct((bsz, t_len, qw), out_dtype),
        scratch_shapes=[pltpu.VMEM((tq, LANES), F32)] * (2 * hps) + [pltpu.VMEM((tq, HEAD_DIM), F32)] * hps,
        compiler_params=_cparams(3),
        name="flash_" + kind,
    )(*args)


def _sum3_kernel(a_ref, b_ref, c_ref, o_ref):
    o_ref[...] = (a_ref[...] + b_ref[...] + c_ref[...]).astype(o_ref.dtype)


def _sum3_cast(a, b, c):
    m, n = a.shape
    tm = min(m, ROW_TILE)
    spec = pl.BlockSpec((tm, n), lambda i: (i, 0))
    return pl.pallas_call(
        _sum3_kernel, grid=(m // tm,), in_specs=[spec] * 3, out_specs=spec,
        out_shape=jax.ShapeDtypeStruct((m, n), BF16), compiler_params=_cparams(1), name="nsa_sum",
    )(a, b, c)


def _hgrn_kernel(q_ref, z_ref, v_ref, g_ref, lb_ref, on_ref, s0_ref, o_ref, s_ref, *st_refs, t_len, chunk, hps):
    on = on_ref[...]
    rows = lax.broadcasted_iota(jnp.int32, (chunk, 1), 0)

    single = t_len < chunk

    def load(ref, r0, cols):
        if single:
            return jnp.broadcast_to(ref[0:1, cols], (chunk, HEAD_DIM))
        return ref[pl.ds(r0, chunk), cols]

    for g in range(hps):
        st_refs[g][...] = s0_ref[g].T

    def body(c, carry):
        r0 = pl.multiple_of(c * chunk, chunk)
        for g in range(hps):
            cols = slice(g * HEAD_DIM, (g + 1) * HEAD_DIM)
            q = load(q_ref, r0, cols)
            z = load(z_ref, r0, cols)
            v = load(v_ref, r0, cols)
            a_term = lb_ref[g, 0:1, :]
            b_term = lb_ref[g, 1:2, :] + _log_sigmoid(z)
            logf = jnp.maximum(a_term, b_term) + jnp.log1p(jnp.exp(-jnp.abs(a_term - b_term)))
            k = lb_ref[g, 2:3, :] * _sigmoid(-z)
            if single:
                logf = jnp.where(rows < t_len, logf, 0.0)
                k = jnp.where(rows < t_len, k, 0.0)
            cum = logf
            shift = 1
            while shift < chunk:
                cum = cum + jnp.where(rows >= shift, pltpu.roll(cum, shift, axis=0), 0.0)
                shift *= 2
            a_last = cum[chunk - 1:chunk, :]
            o = jnp.zeros((chunk, HEAD_DIM), F32)
            for s_row in range(chunk):
                diff = jnp.where(rows >= s_row, cum - cum[s_row:s_row + 1, :], -jnp.inf)
                wgt = q * jnp.exp(diff) * k[s_row:s_row + 1, :]
                o = o + jnp.sum(wgt, axis=-1, keepdims=True) * v[s_row:s_row + 1, :]
            st = st_refs[g][...]
            qa = (q * jnp.exp(cum)).astype(BF16)
            o = o + lax.dot_general(qa, st.astype(BF16), (((1,), (1,)), ((), ())), preferred_element_type=F32)
            kd = (k * jnp.exp(a_last - cum)).astype(BF16)
            st_refs[g][...] = jnp.exp(a_last) * st + lax.dot_general(v.astype(BF16), kd, (((0,), (0,)), ((), ())),
                                                                     preferred_element_type=F32)
            gate = load(g_ref, r0, cols)
            o = _rms_rows(o, on) * (gate * _sigmoid(gate))
            if single:
                o_ref[:, cols] = o[0:t_len, :].astype(o_ref.dtype)
            else:
                o_ref[pl.ds(r0, chunk), cols] = o.astype(o_ref.dtype)
        return carry

    n_chunks = max(t_len // chunk, 1)
    lax.fori_loop(0, n_chunks, body, 0, unroll=2 if n_chunks % 2 == 0 else 1)
    for g in range(hps):
        s_ref[g] = st_refs[g][...].T


HGRN_HEADS_PER_STEP = 4


def _hgrn(ph3, lb_rows, o_norm, s0):
    bsz, t_len, _ = ph3.shape
    chunk = 16
    hps = HGRN_HEADS_PER_STEP
    assert t_len % chunk == 0 or t_len == 1
    gw = hps * HEAD_DIM
    n_steps = HG_HEADS // hps
    col = lambda part: pl.BlockSpec((None, t_len, gw), lambda b, h: (b, 0, part * n_steps + h))
    on = o_norm.reshape(1, HEAD_DIM)
    state_spec = pl.BlockSpec((None, hps, HEAD_DIM, HEAD_DIM), lambda b, h: (b, h, 0, 0))
    return pl.pallas_call(
        functools.partial(_hgrn_kernel, t_len=t_len, chunk=chunk, hps=hps),
        grid=(bsz, n_steps),
        in_specs=[col(0), col(1), col(2), col(3),
                  pl.BlockSpec((hps, 3, HEAD_DIM), lambda b, h: (h, 0, 0)),
                  pl.BlockSpec((1, HEAD_DIM), lambda b, h: (0, 0)),
                  state_spec],
        out_specs=[pl.BlockSpec((None, t_len, gw), lambda b, h: (b, 0, h)), state_spec],
        out_shape=[jax.ShapeDtypeStruct((bsz, t_len, HG_HEADS * HEAD_DIM), BF16),
                   jax.ShapeDtypeStruct((bsz, HG_HEADS, HEAD_DIM, HEAD_DIM), F32)],
        scratch_shapes=[pltpu.VMEM((HEAD_DIM, HEAD_DIM), F32)] * hps,
        compiler_params=_cparams(2),
        name="hgrn2",
    )(ph3, ph3, ph3, ph3, lb_rows, on, s0)


def _group_rows(q_ref, group):
    return jnp.concatenate([q_ref[:, g * HEAD_DIM:(g + 1) * HEAD_DIM].astype(F32) for g in range(group)], axis=0)


def _store_group_rows(o_ref, o):
    for g in range(o.shape[0]):
        o_ref[:, g * HEAD_DIM:(g + 1) * HEAD_DIM] = o[g:g + 1, :].astype(o_ref.dtype)


NSA_SEL_BLOCKS_PER_STEP = 4


def _nsa_sel_step_kernel(*refs, t_pos, nb, bps):
    idx_ref, q_ref = refs[1], refs[2]
    kv_refs = refs[3:3 + bps]
    kn_ref, vn_ref, rbt_ref, gates_ref, o_ref, m_ref, l_ref, acc_ref = refs[3 + bps:]
    b = pl.program_id(0)
    kh = pl.program_id(1)
    j = pl.program_id(2)
    q4 = _group_rows(q_ref, NSA_GROUP)
    rbt = rbt_ref[...]
    col = lambda bk: rbt[:, bk:bk + 1]

    @pl.when(j == 0)
    def _():
        m_ref[...] = jnp.sum(q4 * kn_ref[...], axis=-1, keepdims=True) + col(0)
        l_ref[...] = jnp.ones_like(l_ref)
        acc_ref[...] = jnp.broadcast_to(vn_ref[...], acc_ref.shape)

    q4b = q4.astype(BF16)
    logits = []
    for u in range(bps):
        blk_id = idx_ref[b * NSA_KV_HEADS + kh, j * bps + u]
        kt = kv_refs[u][pl.ds(kh, NSA_BLOCK, stride=NSA_KV_COLS), :].astype(BF16)
        s_u = lax.dot_general(q4b, kt, (((1,), (1,)), ((), ())), preferred_element_type=F32)
        dist = t_pos - (blk_id * NSA_BLOCK + lax.broadcasted_iota(jnp.int32, (NSA_GROUP, NSA_BLOCK), 1))
        logits.append(s_u + _bucket_bias(dist, col) + jnp.where(blk_id < nb, 0.0, NEG_BIG))
    m_old = m_ref[...]
    m_new = m_old
    for s_u in logits:
        m_new = jnp.maximum(m_new, jnp.max(s_u, axis=-1, keepdims=True))
    alpha = jnp.exp(m_old - m_new)
    l = alpha * l_ref[...]
    acc = alpha * acc_ref[...]
    for u in range(bps):
        p_u = jnp.exp(logits[u] - m_new)
        vt = kv_refs[u][pl.ds(NSA_KV_HEADS + kh, NSA_BLOCK, stride=NSA_KV_COLS), :].astype(BF16)
        l = l + jnp.sum(p_u, axis=-1, keepdims=True)
        acc = acc + jnp.dot(p_u.astype(BF16), vt, preferred_element_type=F32)
    l_ref[...] = l
    acc_ref[...] = acc
    m_ref[...] = m_new

    @pl.when(j == pl.num_programs(2) - 1)
    def _():
        o = acc_ref[...] / l_ref[...]
        gates = gates_ref[...]
        gcol = jnp.concatenate([_lane_column(gates, (kh * NSA_GROUP + g) * 3 + 1) for g in range(NSA_GROUP)], axis=0)
        _store_group_rows(o_ref, o * gcol)


def _nsa_sel_step(q3, cache_pages, tbl, idx, sel_new3, rbt, gates3, *, t_pos):
    bsz = q3.shape[0]
    nb = tbl.shape[1] * (PAGE_ROWS // NSA_BLOCK)
    gw = NSA_GROUP * HEAD_DIM
    halves = PAGE_ROWS // NSA_BLOCK

    bps = NSA_SEL_BLOCKS_PER_STEP
    assert NSA_TOPN % bps == 0

    def cache_map(u):
        def index_map(b, kh, j, tbl_ref, idx_ref):
            blk = jnp.minimum(idx_ref[b * NSA_KV_HEADS + kh, j * bps + u], nb - 1)
            return (tbl_ref[b, blk // halves], blk % halves, 0)
        return index_map

    grid_spec = pltpu.PrefetchScalarGridSpec(
        num_scalar_prefetch=2,
        grid=(bsz, NSA_KV_HEADS, NSA_TOPN // bps),
        in_specs=[pl.BlockSpec((None, 1, gw), lambda b, kh, j, t, i: (b, 0, kh))]
        + [pl.BlockSpec((None, NSA_BLOCK * NSA_KV_COLS, HEAD_DIM), cache_map(u)) for u in range(bps)]
        + [
            pl.BlockSpec((None, 1, HEAD_DIM), lambda b, kh, j, t, i: (b, 0, kh)),
            pl.BlockSpec((None, 1, HEAD_DIM), lambda b, kh, j, t, i: (b, 0, NSA_KV_HEADS + kh)),
            pl.BlockSpec((None, NSA_GROUP, REL_BUCKETS), lambda b, kh, j, t, i: (kh, 0, 0)),
            pl.BlockSpec((None, 1, LANES), lambda b, kh, j, t, i: (b, 0, 0)),
        ],
        out_specs=pl.BlockSpec((None, 1, gw), lambda b, kh, j, t, i: (b, 0, kh)),
        scratch_shapes=[pltpu.VMEM((NSA_GROUP, 1), F32), pltpu.VMEM((NSA_GROUP, 1), F32),
                        pltpu.VMEM((NSA_GROUP, HEAD_DIM), F32)],
    )
    return pl.pallas_call(
        functools.partial(_nsa_sel_step_kernel, t_pos=t_pos, nb=nb, bps=bps),
        grid_spec=grid_spec,
        out_shape=jax.ShapeDtypeStruct((bsz, 1, NSA_HEADS * HEAD_DIM), F32),
        compiler_params=_cparams(3),
        name="nsa_sel_step",
    )(tbl, idx, q3, *([cache_pages] * bps), sel_new3, sel_new3, rbt, gates3)


def _nsa_win_step_kernel(q_ref, k_ref, v_ref, kn_ref, vn_ref, rbt_ref, gates_ref, o_ref, *, pw):
    kh = pl.program_id(1)
    q4 = _group_rows(q_ref, NSA_GROUP)
    rbt = rbt_ref[...]
    col = lambda bk: rbt[:, bk:bk + 1]
    s = lax.dot_general(q4.astype(BF16), k_ref[...].astype(BF16), (((1,), (1,)), ((), ())),
                        preferred_element_type=F32)
    dist = pw - lax.broadcasted_iota(jnp.int32, (NSA_GROUP, pw), 1)
    mask = dist <= NSA_WINDOW
    s = jnp.where(mask, s + _bucket_bias(dist, col), NEG_BIG)
    s_self = jnp.sum(q4 * kn_ref[...], axis=-1, keepdims=True) + col(0)
    m = jnp.maximum(jnp.max(s, axis=-1, keepdims=True), s_self)
    p = jnp.where(mask, jnp.exp(s - m), 0.0)
    p_self = jnp.exp(s_self - m)
    l = jnp.sum(p, axis=-1, keepdims=True) + p_self
    o = jnp.dot(p.astype(BF16), v_ref[...].astype(BF16), preferred_element_type=F32) + p_self * vn_ref[...]
    gates = gates_ref[...]
    gcol = jnp.concatenate([_lane_column(gates, (kh * NSA_GROUP + g) * 3 + 2) for g in range(NSA_GROUP)], axis=0)
    _store_group_rows(o_ref, o / l * gcol)


def _nsa_win_step(q3, win_cache, lead, win_new3, rbt, gates3):
    bsz = q3.shape[0]
    pw = win_cache.shape[1]
    gw = NSA_GROUP * HEAD_DIM
    return pl.pallas_call(
        functools.partial(_nsa_win_step_kernel, pw=pw),
        grid=(bsz, NSA_KV_HEADS),
        in_specs=[
            pl.BlockSpec((None, 1, gw), lambda b, kh: (b, 0, kh)),
            pl.BlockSpec((None, pw, HEAD_DIM), lambda b, kh: (lead + b, 0, kh)),
            pl.BlockSpec((None, pw, HEAD_DIM), lambda b, kh: (lead + b, 0, NSA_KV_HEADS + kh)),
            pl.BlockSpec((None, 1, HEAD_DIM), lambda b, kh: (b, 0, kh)),
            pl.BlockSpec((None, 1, HEAD_DIM), lambda b, kh: (b, 0, NSA_KV_HEADS + kh)),
            pl.BlockSpec((None, NSA_GROUP, REL_BUCKETS), lambda b, kh: (kh, 0, 0)),
            pl.BlockSpec((None, 1, LANES), lambda b, kh: (b, 0, 0)),
        ],
        out_specs=pl.BlockSpec((None, 1, gw), lambda b, kh: (b, 0, kh)),
        out_shape=jax.ShapeDtypeStruct((bsz, 1, NSA_HEADS * HEAD_DIM), F32),
        compiler_params=_cparams(2),
        name="nsa_win_step",
    )(q3, win_cache, win_cache, win_new3, win_new3, rbt, gates3)


def _even_proj(h, hr, w, j):
    qw = NSA_HEADS * HEAD_DIM
    kv_w = 2 * NSA_KV_HEADS * HEAD_DIM
    n_main = qw + 3 * kv_w
    n_gate = 3 * NSA_HEADS
    w_in = w["w_in_even"]
    hr_parts = None if hr is None else [hr]
    pa, par = _dense2([h], hr_parts, w_in, j, 0, n_main, tn=512, name="even_in_attn")
    w_gate = jnp.pad(w_in[j, :, n_main:n_main + n_gate], ((0, 0), (0, LANES - n_gate)))
    graw, grawr = _dense2([h], hr_parts, w_gate, None, 0, LANES, tn=LANES, name="even_in_gate")
    w_hg = w_in[j, :, n_main + n_gate:]
    ph, phr = _dense2([h], hr_parts, w_hg, None, 0, w_hg.shape[1], tn=512, name="even_in_hgrn")
    return (pa, graw, ph), (None if hr is None else (par, grawr, phr))


def _even_out(x2d, a_parts, xr2d, ar_parts, w, j):
    d = x2d.shape[1]
    return _dense2(a_parts, ar_parts, w["w_out_even"], j, 0, d, tn=min(d, 512), mode="res", res=x2d,
                   r_res=xr2d, scale=1.0, name="even_out")


def _even_mixer(x2d, h, bsz, t_len, j, w, tabs, past):
    proj, _ = _even_proj(h, None, w, j)
    a_parts, new = _even_core(proj, bsz, t_len, j, w, tabs, past)
    x_new, _ = _even_out(x2d, a_parts, None, None, w, j)
    return (x_new,) + new


def _even_core(proj, bsz, t_len, j, w, tabs, past):
    pa, graw, ph = proj
    m = pa.shape[0]
    qw = NSA_HEADS * HEAD_DIM
    kv_w = 2 * NSA_KV_HEADS * HEAD_DIM
    q, cmp_new, sel_new, win_new, selb, winb, gates = _even_post(pa, graw, w["nsa_q_norm"][j], w["nsa_k_norm"][j])
    q3 = q.reshape(bsz, t_len, qw)
    gates3 = gates.reshape(bsz, t_len, LANES)
    rel_bias = w["rel_bias"]
    k_norm = w["nsa_k_norm"][j]
    if past is None:
        q_start = 0
        src, tbl = cmp_new.reshape(bsz, t_len, kv_w), None
        s0 = jnp.zeros((bsz, HG_HEADS, HEAD_DIM, HEAD_DIM), F32)
    else:
        src, sel_pages, tbl, win_cache, s0 = past
        q_start = tbl.shape[1] * PAGE_ROWS
    kvc = _compress(src, tbl, w["nsa_phi_pos"], tabs["phi_w1"], w["nsa_phi_w2"], k_norm, j)
    nb = kvc.shape[2]
    bias_c = _bias_cmp_table(rel_bias, q_start, t_len, nb)
    o_c, sel = _nsa_cmp(q3, kvc, bias_c, gates3, q_start=q_start, extra=past is not None)
    if past is None:
        selb3 = selb.reshape(bsz, t_len, kv_w)
        winb3 = winb.reshape(bsz, t_len, kv_w)
        o_s = _flash("nsa_sel", q3, selb3, 0, NSA_KV_HEADS, NSA_KV_HEADS, bias_tiles=tabs["tiles"], bm=sel,
                     gates3=gates3, blk=NSA_BLOCK, gate_branch=1)
        o_w = _flash("nsa_win", q3, winb3, 0, NSA_KV_HEADS, NSA_KV_HEADS, bias_tiles=tabs["tiles"],
                     gates3=gates3, gate_branch=2)
        keep = min(NSA_WINDOW, t_len)
        win_buf = win_new.reshape(bsz, t_len, kv_w)[:, t_len - keep:]
    else:
        rbt = rel_bias.T.reshape(NSA_KV_HEADS, NSA_GROUP, REL_BUCKETS)
        sel_new3 = sel_new.reshape(bsz, t_len, kv_w)
        win_new3 = win_new.reshape(bsz, t_len, kv_w)
        idx = sel.reshape(bsz * NSA_KV_HEADS, NSA_TOPN)
        o_s = _nsa_sel_step(q3, sel_pages, tbl, idx, sel_new3, rbt, gates3, t_pos=q_start)
        wc = win_cache.reshape(win_cache.shape[0] * win_cache.shape[1], win_cache.shape[2], kv_w)
        o_w = _nsa_win_step(q3, wc, j * bsz, win_new3, rbt, gates3)
        win_all = jnp.concatenate([wc[j * bsz:(j + 1) * bsz], win_new3], axis=1)
        keep = min(NSA_WINDOW, win_all.shape[1])
        win_buf = win_all[:, win_all.shape[1] - keep:]
    a_nsa = _sum3_cast(o_c.reshape(m, qw), o_s.reshape(m, qw), o_w.reshape(m, qw))
    o_hg, s_new = _hgrn(ph.reshape(bsz, t_len, ph.shape[1]), tabs["lb_rows"][j], w["hg_o_norm"][j], s0)
    kv_shape = (bsz, t_len, 2, NSA_KV_HEADS, HEAD_DIM)
    new = (cmp_new.reshape(kv_shape), sel_new.reshape(kv_shape),
           win_buf.reshape(bsz, win_buf.shape[1], 2, NSA_KV_HEADS, HEAD_DIM), s_new)
    return [a_nsa, o_hg.reshape(m, HG_HEADS * HEAD_DIM)], new


ODD_HEADS = MOBA_HEADS
ODD_QW = ODD_HEADS * HEAD_DIM
ODD_KVW = 2 * ODD_HEADS * HEAD_DIM


def _odd_post_kernel(pm_ref, fz_ref, mg_ref, fg_ref, fb_ref, qm_ref, moba_ref, mobab_ref, qf_ref, fox_ref,
                     foxb_ref, logf_ref):
    scale = HEAD_DIM ** -0.5
    off = 0
    for g_ref, q_ref, kv_ref, kvb_ref in ((mg_ref, qm_ref, moba_ref, mobab_ref), (fg_ref, qf_ref, fox_ref, foxb_ref)):
        qg = g_ref[0:1, :]
        kg = g_ref[1:2, :]
        for h in range(ODD_HEADS):
            sl = slice(h * HEAD_DIM, (h + 1) * HEAD_DIM)
            q_ref[:, sl] = (_rms_rows(pm_ref[:, off + h * HEAD_DIM: off + (h + 1) * HEAD_DIM], qg) * scale
                            ).astype(q_ref.dtype)
        off += ODD_QW
        for c in range(2 * ODD_HEADS):
            src = pm_ref[:, off + c * HEAD_DIM: off + (c + 1) * HEAD_DIM]
            val = _rms_rows(src, kg) if c < ODD_HEADS else src
            kv_ref[:, c * HEAD_DIM:(c + 1) * HEAD_DIM] = val
            kvb_ref[:, c * HEAD_DIM:(c + 1) * HEAD_DIM] = val.astype(BF16)
        off += ODD_KVW
    logf_ref[...] = _log_sigmoid(fz_ref[...] + fb_ref[...])


def _odd_post(pm, fz, moba_qk, fox_qk, fb):
    m = pm.shape[0]
    tm = min(m, ROW_TILE // 2)
    row = lambda w: pl.BlockSpec((tm, w), lambda i: (i, 0))
    full = lambda a: pl.BlockSpec(a.shape, lambda i: (0,) * a.ndim)
    return pl.pallas_call(
        _odd_post_kernel,
        grid=(m // tm,),
        in_specs=[row(pm.shape[1]), row(LANES), full(moba_qk), full(fox_qk), full(fb)],
        out_specs=[row(ODD_QW), row(ODD_KVW), row(ODD_KVW), row(ODD_QW), row(ODD_KVW), row(ODD_KVW), row(LANES)],
        out_shape=[jax.ShapeDtypeStruct((m, ODD_QW), BF16), jax.ShapeDtypeStruct((m, ODD_KVW), F32),
                   jax.ShapeDtypeStruct((m, ODD_KVW), BF16), jax.ShapeDtypeStruct((m, ODD_QW), BF16),
                   jax.ShapeDtypeStruct((m, ODD_KVW), F32), jax.ShapeDtypeStruct((m, ODD_KVW), BF16),
                   jax.ShapeDtypeStruct((m, LANES), F32)],
        compiler_params=_cparams(1),
        name="odd_post",
    )(pm, fz, moba_qk, fox_qk, fb)


def _topk_rank(score, n):
    idx = lax.broadcasted_iota(jnp.int32, (1, n), 1)
    rank = jnp.zeros(score.shape, jnp.int32)
    for mcol in range(n):
        sm = score[:, mcol:mcol + 1]
        ahead = (sm > score) | ((sm == score) & (mcol < idx))
        rank = rank + ahead.astype(jnp.int32)
    return rank


def _moba_gate_kernel(q_ref, k_ref, bm_ref, *, t_len, nbl):
    k_mean = jnp.mean(k_ref[...].reshape(nbl, MOBA_BLOCK, HEAD_DIM), axis=1)
    gate = lax.dot_general(k_mean.astype(BF16), q_ref[...], (((1,), (1,)), ((), ())), preferred_element_type=F32)
    cur = lax.broadcasted_iota(jnp.int32, (1, t_len), 1) // MOBA_BLOCK
    blk = lax.broadcasted_iota(jnp.int32, (nbl, 1), 0)
    past_ok = blk < cur
    gate = jnp.where(past_ok, gate, -jnp.inf)
    rank = jnp.zeros(gate.shape, jnp.int32)
    for mrow in range(nbl):
        sm = gate[mrow:mrow + 1, :]
        ahead = (sm > gate) | ((sm == gate) & (mrow < blk))
        rank = rank + ahead.astype(jnp.int32)
    sel = (rank < MOBA_TOPK) & past_ok & (jnp.abs(gate) < jnp.inf)
    mask_t = (sel | (blk == cur)).astype(F32)
    eye = (lax.broadcasted_iota(jnp.int32, (nbl, nbl), 0) == lax.broadcasted_iota(jnp.int32, (nbl, nbl), 1))
    bm_ref[...] = lax.dot_general(mask_t, eye.astype(F32), (((0,), (0,)), ((), ())), preferred_element_type=F32)


def _moba_gate(qm3, moba_new3):
    bsz, t_len, _ = qm3.shape
    assert t_len % MOBA_BLOCK == 0
    nbl = t_len // MOBA_BLOCK
    return pl.pallas_call(
        functools.partial(_moba_gate_kernel, t_len=t_len, nbl=nbl),
        grid=(bsz, MOBA_HEADS),
        in_specs=[pl.BlockSpec((None, t_len, HEAD_DIM), lambda b, h: (b, 0, h)),
                  pl.BlockSpec((None, t_len, HEAD_DIM), lambda b, h: (b, 0, h))],
        out_specs=pl.BlockSpec((None, None, t_len, nbl), lambda b, h: (b, h, 0, 0)),
        out_shape=jax.ShapeDtypeStruct((bsz, MOBA_HEADS, t_len, nbl), F32),
        compiler_params=_cparams(2),
        name="moba_gate",
    )(qm3, moba_new3)


def _cumsum_kernel(x_ref, o_ref, *, t_len):
    n = Q_TILE
    upper = (lax.broadcasted_iota(jnp.int32, (n, n), 0) <= lax.broadcasted_iota(jnp.int32, (n, n), 1)).astype(F32)
    carry = jnp.zeros((FOX_HEADS, 1), F32)
    for c in range(t_len // n):
        xt = x_ref[c * n:(c + 1) * n, :].T[0:FOX_HEADS, :]
        cum = jnp.dot(xt, upper, preferred_element_type=F32, precision=lax.Precision.HIGHEST) + carry
        o_ref[:, c * n:(c + 1) * n] = cum
        carry = cum[:, n - 1:n]


def _cumsum_heads(logf3):
    bsz, t_len, _ = logf3.shape
    assert t_len % Q_TILE == 0
    return pl.pallas_call(
        functools.partial(_cumsum_kernel, t_len=t_len),
        grid=(bsz,),
        in_specs=[pl.BlockSpec((None, t_len, LANES), lambda b: (b, 0, 0))],
        out_specs=pl.BlockSpec((None, FOX_HEADS, t_len), lambda b: (b, 0, 0)),
        out_shape=jax.ShapeDtypeStruct((bsz, FOX_HEADS, t_len), F32),
        compiler_params=_cparams(1),
        name="fox_cumsum",
    )(logf3)


MOBA_GATE_PAGES_PER_STEP = 8


def _moba_gate_step_kernel(*refs, pps, nblk, ppb):
    q_ref = refs[1]
    k_refs = refs[2:2 + pps]
    idx_ref, gate_ref = refs[2 + pps:]
    st = pl.program_id(1)

    @pl.when(st == 0)
    def _():
        gate_ref[...] = jnp.zeros_like(gate_ref)

    q8 = _group_rows(q_ref, MOBA_HEADS)
    lane = lax.broadcasted_iota(jnp.int32, (1, nblk), 1)
    for blk_in_step in range(pps // ppb):
        ksum = jnp.zeros((MOBA_HEADS, HEAD_DIM), F32)
        for p in range(ppb):
            ksum = ksum + jnp.sum(k_refs[blk_in_step * ppb + p][...], axis=0)
        gcol = jnp.sum(q8 * (ksum * (1.0 / MOBA_BLOCK)), axis=-1, keepdims=True)
        gate_ref[...] = jnp.where(lane == st * (pps // ppb) + blk_in_step, gcol, gate_ref[...])

    @pl.when(st == pl.num_programs(1) - 1)
    def _():
        gate = gate_ref[...]
        rank = _topk_rank(gate, nblk)
        blk = lax.broadcasted_iota(jnp.int32, (1, nblk), 1)
        lane = lax.broadcasted_iota(jnp.int32, (MOBA_HEADS, MOBA_TOPK), 1)
        out = jnp.full((MOBA_HEADS, MOBA_TOPK), -1, jnp.int32)
        for r in range(min(MOBA_TOPK, nblk)):
            hit = (rank == r) & (jnp.abs(gate) < jnp.inf)
            idx_r = jnp.sum(jnp.where(hit, blk + 1, 0), axis=-1, keepdims=True) - 1
            out = jnp.where(lane == r, idx_r, out)
        idx_ref[...] = out


def _moba_gate_step(qm3, pages4, tbl):
    bsz, npg = tbl.shape
    ppb = MOBA_BLOCK // PAGE_ROWS
    pps = MOBA_GATE_PAGES_PER_STEP
    assert npg % pps == 0 and pps % ppb == 0

    def key_spec(p):
        return pl.BlockSpec((None, PAGE_ROWS, MOBA_HEADS, HEAD_DIM), lambda b, st, t: (t[b, st * pps + p], 0, 0, 0))

    grid_spec = pltpu.PrefetchScalarGridSpec(
        num_scalar_prefetch=1,
        grid=(bsz, npg // pps),
        in_specs=[pl.BlockSpec((None, 1, ODD_QW), lambda b, st, t: (b, 0, 0))] + [key_spec(p) for p in range(pps)],
        out_specs=pl.BlockSpec((None, MOBA_HEADS, MOBA_TOPK), lambda b, st, t: (b, 0, 0)),
        scratch_shapes=[pltpu.VMEM((MOBA_HEADS, npg // ppb), F32)],
    )
    return pl.pallas_call(
        functools.partial(_moba_gate_step_kernel, pps=pps, nblk=npg // ppb, ppb=ppb),
        grid_spec=grid_spec,
        out_shape=jax.ShapeDtypeStruct((bsz, MOBA_HEADS, MOBA_TOPK), jnp.int32),
        compiler_params=_cparams(2),
        name="moba_gate_step",
    )(tbl, qm3, *([pages4] * pps))


def _moba_attn_step_kernel(*refs, t_pos, ppb):
    idx_ref, q_ref = refs[1], refs[2]
    kv_refs = refs[3:3 + ppb]
    kn_ref, vn_ref, rbt_ref, o_ref, m_ref, l_ref, acc_ref = refs[3 + ppb:]
    b = pl.program_id(0)
    h = pl.program_id(1)
    s_id = pl.program_id(2)
    sub = 8
    q = q_ref[...].astype(F32)
    rbt = rbt_ref[...]
    col = lambda bk: rbt[:, bk:bk + 1]

    @pl.when(s_id == 0)
    def _():
        s_self = jnp.sum(q * kn_ref[...], axis=-1, keepdims=True) + col(0)
        m_ref[...] = jnp.broadcast_to(s_self, m_ref.shape)
        l_ref[...] = jnp.ones_like(l_ref)
        acc_ref[...] = jnp.broadcast_to(vn_ref[...], acc_ref.shape)

    blk_id = idx_ref[b * MOBA_HEADS + h, s_id]

    @pl.when(blk_id >= 0)
    def _():
        q8 = jnp.broadcast_to(q, (sub, HEAD_DIM)).astype(BF16)
        logits = []
        for pg in range(ppb):
            kt = kv_refs[pg][pl.ds(h, PAGE_ROWS, stride=2 * MOBA_HEADS), :].astype(BF16)
            logits.append(lax.dot_general(q8, kt, (((1,), (1,)), ((), ())), preferred_element_type=F32))
        s = jnp.concatenate(logits, axis=-1)
        key_pos = blk_id * MOBA_BLOCK + lax.broadcasted_iota(jnp.int32, (sub, ppb * PAGE_ROWS), 1)
        s = s + _bucket_bias(t_pos - key_pos, col)
        m_old = m_ref[...]
        m_new = jnp.maximum(m_old, jnp.max(s, axis=-1, keepdims=True))
        p = jnp.exp(s - m_new)
        alpha = jnp.exp(m_old - m_new)
        l_ref[...] = alpha * l_ref[...] + jnp.sum(p, axis=-1, keepdims=True)
        acc = alpha * acc_ref[...]
        for pg in range(ppb):
            vt = kv_refs[pg][pl.ds(MOBA_HEADS + h, PAGE_ROWS, stride=2 * MOBA_HEADS), :].astype(BF16)
            acc = acc + jnp.dot(p[:, pg * PAGE_ROWS:(pg + 1) * PAGE_ROWS].astype(BF16), vt,
                                preferred_element_type=F32)
        acc_ref[...] = acc
        m_ref[...] = m_new

    @pl.when(s_id == pl.num_programs(2) - 1)
    def _():
        o_ref[...] = (acc_ref[0:1, :] / l_ref[0:1, :]).astype(o_ref.dtype)


def _moba_attn_step(qm3, pages3, tbl, idx, moba_new3, rbt3, *, t_pos):
    bsz = qm3.shape[0]
    ppb = MOBA_BLOCK // PAGE_ROWS

    def cache_map(pg):
        def index_map(b, h, s, tbl_ref, idx_ref):
            blk = jnp.maximum(idx_ref[b * MOBA_HEADS + h, s], 0)
            return (tbl_ref[b, blk * ppb + pg], 0, 0)
        return index_map

    head = lambda off: pl.BlockSpec((None, 1, HEAD_DIM), lambda b, h, s, t, i: (b, 0, off + h))
    grid_spec = pltpu.PrefetchScalarGridSpec(
        num_scalar_prefetch=2,
        grid=(bsz, MOBA_HEADS, MOBA_TOPK),
        in_specs=[head(0)]
        + [pl.BlockSpec((None, PAGE_ROWS * 2 * MOBA_HEADS, HEAD_DIM), cache_map(pg)) for pg in range(ppb)]
        + [head(0), head(MOBA_HEADS),
           pl.BlockSpec((None, 1, REL_BUCKETS), lambda b, h, s, t, i: (h, 0, 0))],
        out_specs=head(0),
        scratch_shapes=[pltpu.VMEM((8, 1), F32), pltpu.VMEM((8, 1), F32), pltpu.VMEM((8, HEAD_DIM), F32)],
    )
    return pl.pallas_call(
        functools.partial(_moba_attn_step_kernel, t_pos=t_pos, ppb=ppb),
        grid_spec=grid_spec,
        out_shape=jax.ShapeDtypeStruct((bsz, 1, ODD_QW), BF16),
        compiler_params=_cparams(3),
        name="moba_attn_step",
    )(tbl, idx, qm3, *([pages3] * ppb), moba_new3, moba_new3, rbt3)


FOX_PAGES_PER_STEP = 8


def _fox_step_kernel(*refs, pps):
    q_ref = refs[1]
    k_refs = refs[2:2 + pps]
    v_refs = refs[2 + pps:2 + 2 * pps]
    lf_refs = refs[2 + 2 * pps:2 + 3 * pps]
    kn_ref, vn_ref, lfn_ref, o_ref, m_ref, l_ref, acc_ref, carry_ref = refs[2 + 3 * pps:]
    st = pl.program_id(1)
    n = PAGE_ROWS
    nh = FOX_HEADS
    q8 = _group_rows(q_ref, nh)
    lane8 = lax.broadcasted_iota(jnp.int32, (nh, LANES), 1)
    row8 = lax.broadcasted_iota(jnp.int32, (nh, LANES), 0)

    @pl.when(st == 0)
    def _():
        m_ref[...] = jnp.sum(q8 * _group_rows(kn_ref, nh), axis=-1, keepdims=True)
        l_ref[...] = jnp.ones_like(l_ref)
        acc_ref[...] = _group_rows(vn_ref, nh)
        carry_ref[...] = jnp.sum(jnp.where(lane8 == row8, lfn_ref[...], 0.0), axis=-1, keepdims=True)

    ones = jnp.ones((HEAD_DIM, LANES), BF16)
    after_t = (lax.broadcasted_iota(jnp.int32, (n, n), 0) > lax.broadcasted_iota(jnp.int32, (n, n), 1)).astype(F32)
    on_diag = (lax.broadcasted_iota(jnp.int32, (n, nh, LANES), 0)
               == lax.broadcasted_iota(jnp.int32, (n, nh, LANES), 2))
    carry = carry_ref[...]
    logits = []
    for p in range(pps):
        lf = lf_refs[p][...]
        bias = carry + jnp.dot(lf, after_t, preferred_element_type=F32, precision=lax.Precision.HIGHEST)
        carry = carry + jnp.sum(lf, axis=-1, keepdims=True)
        prod = (k_refs[p][...] * q8[None]).reshape(n * nh, HEAD_DIM).astype(BF16)
        qk_rep = jnp.dot(prod, ones, preferred_element_type=F32).reshape(n, nh, LANES)
        logits.append(jnp.sum(jnp.where(on_diag, qk_rep, 0.0), axis=0) + bias)
    carry_ref[...] = carry
    s = jnp.concatenate(logits, axis=-1)
    m_old = m_ref[...]
    m_new = jnp.maximum(m_old, jnp.max(s, axis=-1, keepdims=True))
    prob = jnp.exp(s - m_new)
    alpha = jnp.exp(m_old - m_new)
    l_ref[...] = alpha * l_ref[...] + jnp.sum(prob, axis=-1, keepdims=True)
    m_ref[...] = m_new
    acc = alpha * acc_ref[...]
    for p in range(pps):
        prob_p = prob[:, p * n:(p + 1) * n]
        spread = jnp.where(on_diag, prob_p[None], 0.0).reshape(n * nh, LANES).astype(BF16)
        p_rep = jnp.dot(spread, ones, preferred_element_type=F32).reshape(n, nh, HEAD_DIM)
        acc = acc + jnp.sum(p_rep * v_refs[p][...], axis=0)
    acc_ref[...] = acc

    @pl.when(st == pl.num_programs(1) - 1)
    def _():
        _store_group_rows(o_ref, acc_ref[...] / l_ref[...])


def _fox_step(qf3, pages4, logf_pages, tbl, fox_new3, logf_new3):
    bsz, npg = tbl.shape
    pps = math.gcd(FOX_PAGES_PER_STEP, npg)
    page = lambda p: (lambda b, st, t: t[b, npg - 1 - (st * pps + p)])
    kv_spec = lambda p, part: pl.BlockSpec((None, PAGE_ROWS, FOX_HEADS, HEAD_DIM),
                                           lambda b, st, t: (page(p)(b, st, t), 0, part, 0))
    lf_spec = lambda p: pl.BlockSpec((None, FOX_HEADS, PAGE_ROWS), lambda b, st, t: (page(p)(b, st, t), 0, 0))
    new = lambda col: pl.BlockSpec((None, 1, ODD_QW), lambda b, st, t: (b, 0, col))
    grid_spec = pltpu.PrefetchScalarGridSpec(
        num_scalar_prefetch=1,
        grid=(bsz, npg // pps),
        in_specs=[new(0)] + [kv_spec(p, 0) for p in range(pps)] + [kv_spec(p, 1) for p in range(pps)]
        + [lf_spec(p) for p in range(pps)]
        + [new(0), new(1), pl.BlockSpec((None, 1, LANES), lambda b, st, t: (b, 0, 0))],
        out_specs=new(0),
        scratch_shapes=[pltpu.VMEM((FOX_HEADS, 1), F32), pltpu.VMEM((FOX_HEADS, 1), F32),
                        pltpu.VMEM((FOX_HEADS, HEAD_DIM), F32), pltpu.VMEM((FOX_HEADS, 1), F32)],
    )
    return pl.pallas_call(
        functools.partial(_fox_step_kernel, pps=pps),
        grid_spec=grid_spec,
        out_shape=jax.ShapeDtypeStruct((bsz, 1, ODD_QW), BF16),
        compiler_params=_cparams(2),
        name="fox_step",
    )(tbl, qf3, *([pages4] * (2 * pps)), *([logf_pages] * pps), fox_new3, fox_new3, logf_new3)


def _odd_proj(h, hr, w, j):
    n_main = 2 * (ODD_QW + ODD_KVW)
    w_in = w["w_in_odd"]
    hr_parts = None if hr is None else [hr]
    pm, pmr = _dense2([h], hr_parts, w_in, j, 0, n_main, tn=512, name="odd_in")
    w_fz = jnp.pad(w_in[j, :, n_main:n_main + FOX_HEADS], ((0, 0), (0, LANES - FOX_HEADS)))
    fz, fzr = _dense2([h], hr_parts, w_fz, None, 0, LANES, tn=LANES, name="odd_in_forget")
    return (pm, fz), (None if hr is None else (pmr, fzr))


def _odd_out(x2d, a_parts, xr2d, ar_parts, w, j):
    d = x2d.shape[1]
    return _dense2(a_parts, ar_parts, w["w_out_odd"], j, 0, d, tn=min(d, 512), mode="res", res=x2d,
                   r_res=xr2d, scale=1.0, name="odd_out")


def _odd_mixer(x2d, h, bsz, t_len, j, w, tabs, past):
    proj, _ = _odd_proj(h, None, w, j)
    a_parts, new = _odd_core(proj, bsz, t_len, j, w, tabs, past)
    x_new, _ = _odd_out(x2d, a_parts, None, None, w, j)
    return (x_new,) + new


def _odd_core(proj, bsz, t_len, j, w, tabs, past):
    pm, fz = proj
    m = pm.shape[0]
    fb = jnp.pad(w["fox_f_bias"][j].astype(F32), (0, LANES - FOX_HEADS)).reshape(1, LANES)
    qm, moba_new, mobab, qf, fox_new, foxb, logf = _odd_post(pm, fz, w["moba_qk_norm"][j], w["fox_qk_norm"][j], fb)
    qm3 = qm.reshape(bsz, t_len, ODD_QW)
    qf3 = qf.reshape(bsz, t_len, ODD_QW)
    logf3 = logf.reshape(bsz, t_len, LANES)
    if past is None:
        bm = _moba_gate(qm3, moba_new.reshape(bsz, t_len, ODD_KVW))
        o_m = _flash("moba", qm3, mobab.reshape(bsz, t_len, ODD_KVW), 0, MOBA_HEADS, MOBA_HEADS,
                     bias_tiles=tabs["tiles"], bm=bm, blk=MOBA_BLOCK, out_dtype=BF16)
        cum = _cumsum_heads(logf3).reshape(bsz, FOX_HEADS, 1, t_len)
        o_f = _flash("fox", qf3, foxb.reshape(bsz, t_len, ODD_KVW), 0, FOX_HEADS, FOX_HEADS, cum=cum,
                     out_dtype=BF16)
    else:
        moba_pages4, fox_pages, logf_pages, tbl = past
        t_pos = tbl.shape[1] * PAGE_ROWS
        idx = _moba_gate_step(qm3, moba_pages4, tbl)
        rbt3 = w["rel_bias"].T.reshape(MOBA_HEADS, 1, REL_BUCKETS)
        moba_pages3 = moba_pages4.reshape(moba_pages4.shape[0], PAGE_ROWS * 2 * MOBA_HEADS, HEAD_DIM)
        o_m = _moba_attn_step(qm3, moba_pages3, tbl, idx.reshape(bsz * MOBA_HEADS, MOBA_TOPK),
                              moba_new.reshape(bsz, t_len, ODD_KVW), rbt3, t_pos=t_pos)
        o_f = _fox_step(qf3, fox_pages, logf_pages, tbl, fox_new.reshape(bsz, t_len, ODD_KVW), logf3)
    kv_shape = (bsz, t_len, 2, ODD_HEADS, HEAD_DIM)
    new = (moba_new.reshape(kv_shape), fox_new.reshape(kv_shape), logf3[:, :, :FOX_HEADS])
    return [o_m.reshape(m, ODD_QW), o_f.reshape(m, ODD_QW)], new


def _trunk(x_prompt, x_sample, w, tabs, caches):
    bp, tp, d = x_prompt.shape
    bs, ts, _ = x_sample.shape
    depth = w["norm_mix"].shape[0]
    xp = x_prompt.reshape(bp * tp, d)
    xs = x_sample.reshape(bs * ts, d)
    pe, po, se, so = [], [], [], []
    for layer in range(depth):
        xp, xs = _ffn(xp, xs, w["norm_ffn1"], w["w_ffn1_in"], w["w_ffn1_out"], layer)
        hp = _rms_cast(xp, w["norm_mix"], layer)
        hs = _rms_cast(xs, w["norm_mix"], layer)
        j = layer // 2
        tbl = caches["tbl"] + j * caches["n_phys"]
        if layer % 2 == 0:
            proj_p, proj_s = _even_proj(hp, hs, w, j)
            ap, new_p = _even_core(proj_p, bp, tp, j, w, tabs, None)
            past = (caches["cmp"], caches["sel"], tbl, caches["win"], caches["hgrn"][j])
            a_s, new_s = _even_core(proj_s, bs, ts, j, w, tabs, past)
            xp, xs = _even_out(xp, ap, xs, a_s, w, j)
            pe.append(new_p)
            se.append(new_s)
        else:
            proj_p, proj_s = _odd_proj(hp, hs, w, j)
            ap, new_p = _odd_core(proj_p, bp, tp, j, w, tabs, None)
            past = (caches["moba"], caches["fox"], caches["logf"], tbl)
            a_s, new_s = _odd_core(proj_s, bs, ts, j, w, tabs, past)
            xp, xs = _odd_out(xp, ap, xs, a_s, w, j)
            po.append(new_p)
            so.append(new_s)
        xp, xs = _ffn(xp, xs, w["norm_ffn2"], w["w_ffn2_in"], w["w_ffn2_out"], layer)
    return xp.reshape(bp, tp, d), xs.reshape(bs, ts, d), pe, po, se, so


def kernel(x_prompt, x_sample, cache_nsa_cmp, cache_nsa_sel, cache_moba, cache_fox, cache_fox_logf,
           cache_nsa_win, state_hgrn, page_table, norm_ffn1, w_ffn1_in, w_ffn1_out, norm_mix, norm_ffn2,
           w_ffn2_in, w_ffn2_out, rel_bias, w_in_even, w_out_even, nsa_q_norm, nsa_k_norm, nsa_phi_pos,
           nsa_phi_w1, nsa_phi_w2, hg_lb, hg_o_norm, w_in_odd, w_out_odd, fox_f_bias, moba_qk_norm,
           fox_qk_norm):
    w = dict(norm_ffn1=norm_ffn1, w_ffn1_in=w_ffn1_in, w_ffn1_out=w_ffn1_out, norm_mix=norm_mix,
             norm_ffn2=norm_ffn2, w_ffn2_in=w_ffn2_in, w_ffn2_out=w_ffn2_out, rel_bias=rel_bias,
             w_in_even=w_in_even, w_out_even=w_out_even, nsa_q_norm=nsa_q_norm, nsa_k_norm=nsa_k_norm,
             nsa_phi_pos=nsa_phi_pos, nsa_phi_w1=nsa_phi_w1, nsa_phi_w2=nsa_phi_w2, hg_lb=hg_lb,
             hg_o_norm=hg_o_norm, w_in_odd=w_in_odd, w_out_odd=w_out_odd, fox_f_bias=fox_f_bias,
             moba_qk_norm=moba_qk_norm, fox_qk_norm=fox_qk_norm)
    tabs = _tables(w)
    n_phys = cache_nsa_cmp.shape[1]
    n_layers = cache_nsa_cmp.shape[0]
    rows3 = lambda pool: pool.reshape(n_layers * n_phys, -1, HEAD_DIM)
    rows4 = lambda pool: pool.reshape(n_layers * n_phys, PAGE_ROWS, -1, HEAD_DIM)
    logf_t = jnp.swapaxes(cache_fox_logf.reshape(n_layers * n_phys, PAGE_ROWS, FOX_HEADS), 1, 2)
    caches = dict(cmp=rows3(cache_nsa_cmp), sel=rows3(cache_nsa_sel), moba=rows4(cache_moba),
                  fox=rows4(cache_fox), logf=logf_t, win=cache_nsa_win, hgrn=state_hgrn,
                  tbl=page_table.astype(jnp.int32), n_phys=n_phys)
    y_prompt, y_sample, pe, po, se, so = _trunk(x_prompt, x_sample, w, tabs, caches)

    stack = lambda items, i: jnp.stack([it[i] for it in items])
    return (y_prompt, y_sample,
            stack(pe, 0), stack(pe, 1), stack(pe, 2), stack(pe, 3),
            stack(po, 0), stack(po, 1), stack(po, 2),
            stack(se, 0), stack(se, 1), stack(se, 2), stack(se, 3),
            stack(so, 0), stack(so, 1), stack(so, 2))
```

```python
import functools
import math

import jax
import jax.numpy as jnp
import numpy as np
from jax import lax
from jax.experimental import pallas as pl
from jax.experimental.pallas import tpu as pltpu

F32 = jnp.float32
BF16 = jnp.bfloat16

HEAD_DIM = 128
NSA_HEADS = 8
NSA_KV_HEADS = 2
NSA_GROUP = NSA_HEADS // NSA_KV_HEADS
NSA_BLOCK = 64
NSA_TOPN = 16
NSA_WINDOW = 512
NSA_PHI_HIDDEN = 2 * HEAD_DIM
NSA_FORCE_SCORE = 1.0e4
HG_HEADS = 8
MOBA_HEADS = 8
MOBA_BLOCK = 256
MOBA_TOPK = 3
FOX_HEADS = 8
REL_BUCKETS = 32
REL_MAX_DIST = 128
EPS = 1e-6
PAGE_ROWS = 128

LANES = 128
VMEM_LIMIT_BYTES = 56 * 1024 * 1024
ROW_TILE = 512
DEEP_K = 2048
Q_TILE = 128
FLASH_TILE = 256
FLASH_HEADS_PER_STEP = 4
NEG_BIG = -1e30


def _cparams(n_axes):
    return pltpu.CompilerParams(dimension_semantics=("arbitrary",) * n_axes,
                                vmem_limit_bytes=VMEM_LIMIT_BYTES)


def _rel_bucket(dist):
    n = jnp.maximum(dist, 0)
    exact = REL_BUCKETS // 2
    nf = jnp.maximum(n, 1).astype(F32)
    big = exact + (jnp.log(nf / exact) / math.log(REL_MAX_DIST / exact) * (REL_BUCKETS - exact)).astype(jnp.int32)
    return jnp.where(n < exact, n, jnp.minimum(big, REL_BUCKETS - 1))


def _bucket_bias(dist, table_rows):
    bucket = _rel_bucket(dist)
    out = jnp.zeros(dist.shape, F32) + table_rows(0)
    for b in range(1, REL_BUCKETS):
        out = jnp.where(bucket == b, table_rows(b), out)
    return out


def _rms_rows(x, g):
    return x * lax.rsqrt(jnp.mean(x * x, axis=-1, keepdims=True) + EPS) * g


def _log_sigmoid(z):
    return jnp.minimum(z, 0.0) - jnp.log1p(jnp.exp(-jnp.abs(z)))


def _sigmoid(z):
    return 1.0 / (1.0 + jnp.exp(-z))


def _rms_cast_kernel(x_ref, g_ref, o_ref):
    o_ref[...] = _rms_rows(x_ref[...], g_ref[...]).astype(o_ref.dtype)


def _rms_cast(x2d, g_stack, layer):
    m, d = x2d.shape
    tm = min(m, ROW_TILE)
    g3 = g_stack.reshape(g_stack.shape[0], 1, d)
    return pl.pallas_call(
        _rms_cast_kernel,
        grid=(m // tm,),
        in_specs=[pl.BlockSpec((tm, d), lambda i: (i, 0)),
                  pl.BlockSpec((None, 1, d), lambda i: (layer, 0, 0))],
        out_specs=pl.BlockSpec((tm, d), lambda i: (i, 0)),
        out_shape=jax.ShapeDtypeStruct((m, d), BF16),
        compiler_params=_cparams(1),
        name="rms_cast",
    )(x2d, g3)


def _dense_kernel(*refs, n_a, mode, scale, rider):
    pos = 0
    a_refs = refs[pos:pos + n_a]
    pos += n_a
    ra_refs = refs[pos:pos + n_a] if rider else ()
    pos += len(ra_refs)
    w_ref = refs[pos]
    pos += 1
    w2_ref = res_ref = rres_ref = ro_ref = None
    if mode == "swiglu":
        w2_ref = refs[pos]
        pos += 1
    if mode == "res":
        res_ref = refs[pos]
        pos += 1
        if rider:
            rres_ref = refs[pos]
            pos += 1
    o_ref = refs[pos]
    pos += 1
    if rider:
        ro_ref = refs[pos]
        pos += 1
    wb_ref = refs[pos]
    pos += 1
    wb2_ref = refs[pos] if mode == "swiglu" else None

    def apply(in_refs, r_ref, out_ref):
        a = in_refs[0][...] if n_a == 1 else jnp.concatenate([r[...] for r in in_refs], axis=-1)
        y = jnp.dot(a, wb_ref[...], preferred_element_type=F32)
        if mode == "swiglu":
            y2 = jnp.dot(a, wb2_ref[...], preferred_element_type=F32)
            y = y * _sigmoid(y) * y2
        elif mode == "res":
            y = r_ref[...] + scale * y
        out_ref[...] = y.astype(out_ref.dtype)

    @pl.when(pl.program_id(1) == 0)
    def _():
        wb_ref[...] = w_ref[...].astype(BF16)
        if mode == "swiglu":
            wb2_ref[...] = w2_ref[...].astype(BF16)
        if rider:
            apply(ra_refs, rres_ref, ro_ref)

    apply(a_refs, res_ref, o_ref)


def _dense(a_parts, w, lead, col0, n_out, *, tn, mode="plain", res=None, scale=1.0,
           out_dtype=F32, col0_b=None, rider=None, name="dense"):
    m = a_parts[0].shape[0]
    k = sum(a.shape[1] for a in a_parts)
    tm = min(m, 2 * ROW_TILE if k <= DEEP_K else ROW_TILE)
    assert m % tm == 0 and n_out % tn == 0 and col0 % tn == 0
    if w.ndim == 3:
        wblock = (None, k, tn)

        def wmap(off):
            return lambda j, i: (lead, 0, j + off)
    else:
        wblock = (k, tn)

        def wmap(off):
            return lambda j, i: (0, j + off)
    in_specs = [pl.BlockSpec((tm, a.shape[1]), lambda j, i: (i, 0)) for a in a_parts]
    args = list(a_parts)
    ms = 0
    if rider is not None:
        r_parts, r_res = rider
        ms = r_parts[0].shape[0]
        assert [a.shape[1] for a in r_parts] == [a.shape[1] for a in a_parts]
        in_specs += [pl.BlockSpec((ms, a.shape[1]), lambda j, i: (0, 0)) for a in r_parts]
        args += list(r_parts)
    in_specs.append(pl.BlockSpec(wblock, wmap(col0 // tn)))
    args.append(w)
    scratch = [pltpu.VMEM((k, tn), BF16)]
    if mode == "swiglu":
        assert col0_b % tn == 0
        in_specs.append(pl.BlockSpec(wblock, wmap(col0_b // tn)))
        args.append(w)
        scratch.append(pltpu.VMEM((k, tn), BF16))
    if mode == "res":
        in_specs.append(pl.BlockSpec((tm, tn), lambda j, i: (i, j)))
        args.append(res)
        if rider is not None:
            in_specs.append(pl.BlockSpec((ms, tn), lambda j, i: (0, j)))
            args.append(r_res)
    out_specs = pl.BlockSpec((tm, tn), lambda j, i: (i, j))
    out_shape = jax.ShapeDtypeStruct((m, n_out), out_dtype)
    if rider is not None:
        out_specs = [out_specs, pl.BlockSpec((ms, tn), lambda j, i: (0, j))]
        out_shape = [out_shape, jax.ShapeDtypeStruct((ms, n_out), out_dtype)]
    return pl.pallas_call(
        functools.partial(_dense_kernel, n_a=len(a_parts), mode=mode, scale=scale, rider=rider is not None),
        grid=(n_out // tn, m // tm),
        in_specs=in_specs,
        out_specs=out_specs,
        out_shape=out_shape,
        scratch_shapes=scratch,
        compiler_params=_cparams(2),
        name=name,
    )(*args)


def _dense2(a_parts, r_parts, *args, res=None, r_res=None, **kw):
    if r_parts is None:
        return _dense(a_parts, *args, res=res, **kw), None
    return _dense(a_parts, *args, res=res, rider=(r_parts, r_res), **kw)


def _ffn(x2d, xr2d, norm_g, w_in, w_out, layer):
    d_ff = w_out.shape[1]
    xn = _rms_cast(x2d, norm_g, layer)
    xrn = None if xr2d is None else [_rms_cast(xr2d, norm_g, layer)]
    h, hr = _dense2([xn], xrn, w_in, layer, 0, d_ff, tn=512, mode="swiglu", col0_b=d_ff,
                    out_dtype=BF16, name="ffn_in")
    return _dense2([h], None if hr is None else [hr], w_out, layer, 0, x2d.shape[1], tn=512, mode="res",
                   res=x2d, r_res=xr2d, scale=0.5, name="ffn_out")


def _bias_tiles(rel_bias):
    i = jnp.arange(FLASH_TILE)[:, None]
    j = jnp.arange(FLASH_TILE)[None, :]
    dist = jnp.stack([i - j, FLASH_TILE + i - j])
    onehot = (_rel_bucket(dist)[..., None] == jnp.arange(REL_BUCKETS)).astype(F32)
    tiles = jnp.einsum("ktsb,bh->hkts", onehot, rel_bias.astype(F32), precision=lax.Precision.HIGHEST)
    return jnp.where((dist < 0)[None], NEG_BIG, tiles)


def _tables(w):
    lb_all = jnp.cumsum(jax.nn.softmax(w["hg_lb"].astype(F32), axis=0), axis=0)
    lb_all = lb_all - lb_all[0:1]
    lbh = lb_all.reshape(lb_all.shape[0], HG_HEADS, HEAD_DIM)
    lb_rows = jnp.stack([jnp.log(lbh), jnp.log1p(-lbh), 1.0 - lbh], axis=2)
    return {"tiles": _bias_tiles(w["rel_bias"]), "lb_rows": lb_rows, "phi_w1": w["nsa_phi_w1"].astype(BF16)}


def _bias_cmp_table(rel_bias, q_start, t_len, nb):
    t_pos = q_start + jnp.arange(t_len)
    dist = t_pos[:, None] - (jnp.arange(nb) * NSA_BLOCK + NSA_BLOCK - 1)[None, :]
    onehot = (_rel_bucket(dist)[..., None] == jnp.arange(REL_BUCKETS)).astype(F32)
    return jnp.einsum("tnb,bh->htn", onehot, rel_bias.astype(F32), precision=lax.Precision.HIGHEST)


def _even_post_kernel(pa_ref, graw_ref, qg_ref, kg_ref, q_ref, cmp_ref, sel_ref, win_ref,
                      selb_ref, winb_ref, gates_ref):
    scale = HEAD_DIM ** -0.5
    qg = qg_ref[...]
    for h in range(NSA_HEADS):
        sl = slice(h * HEAD_DIM, (h + 1) * HEAD_DIM)
        q_ref[:, sl] = (_rms_rows(pa_ref[:, sl], qg) * scale).astype(q_ref.dtype)
    base = NSA_HEADS * HEAD_DIM
    kv_w = 2 * NSA_KV_HEADS * HEAD_DIM
    cmp_ref[...] = pa_ref[:, base:base + kv_w]
    for which, (o_ref, ob_ref) in enumerate(((sel_ref, selb_ref), (win_ref, winb_ref))):
        off = base + (which + 1) * kv_w
        kg = kg_ref[which + 1:which + 2, :]
        for c in range(2 * NSA_KV_HEADS):
            src = pa_ref[:, off + c * HEAD_DIM: off + (c + 1) * HEAD_DIM]
            val = _rms_rows(src, kg) if c < NSA_KV_HEADS else src
            o_ref[:, c * HEAD_DIM:(c + 1) * HEAD_DIM] = val
            ob_ref[:, c * HEAD_DIM:(c + 1) * HEAD_DIM] = val.astype(BF16)
    gates_ref[...] = _sigmoid(graw_ref[...])


def _even_post(pa, graw, q_norm, k_norm):
    m = pa.shape[0]
    tm = min(m, ROW_TILE)
    kv_w = 2 * NSA_KV_HEADS * HEAD_DIM
    qw = NSA_HEADS * HEAD_DIM
    row = lambda w: pl.BlockSpec((tm, w), lambda i: (i, 0))
    full = lambda a: pl.BlockSpec(a.shape, lambda i: (0,) * a.ndim)
    qg = q_norm.reshape(1, HEAD_DIM)
    return pl.pallas_call(
        _even_post_kernel,
        grid=(m // tm,),
        in_specs=[row(pa.shape[1]), row(LANES), full(qg), full(k_norm)],
        out_specs=[row(qw), row(kv_w), row(kv_w), row(kv_w), row(kv_w), row(kv_w), row(LANES)],
        out_shape=[jax.ShapeDtypeStruct((m, qw), BF16)] + [jax.ShapeDtypeStruct((m, kv_w), F32)] * 3
        + [jax.ShapeDtypeStruct((m, kv_w), BF16)] * 2 + [jax.ShapeDtypeStruct((m, LANES), F32)],
        compiler_params=_cparams(1),
        name="even_post",
    )(pa, graw, qg, k_norm)


def _gelu_tanh(x):
    return 0.5 * x * (1.0 + jnp.tanh(math.sqrt(2.0 / math.pi) * (x + 0.044715 * (x * x * x))))


NSA_KV_COLS = 2 * NSA_KV_HEADS
CMP_PAGES_PER_STEP = 8
CMP_PAGES_PER_GROUP = 64


def _compress_mlp(x_of, nblk, pos_ref, w1_ref, w2_ref, kg_ref, o_ref, acc_ref):
    acc_ref[...] = jnp.zeros_like(acc_ref)

    def body(i2, carry):
        for c in range(NSA_KV_COLS):
            w = c // NSA_KV_HEADS
            xa = x_of(2 * i2, c) + pos_ref[w, pl.ds(2 * i2, 1), :]
            xb = x_of(2 * i2 + 1, c) + pos_ref[w, pl.ds(2 * i2 + 1, 1), :]
            x = jnp.concatenate([xa, xb], axis=-1).astype(BF16)
            wi = w1_ref[w, pl.ds(pl.multiple_of(i2 * 2 * HEAD_DIM, 2 * HEAD_DIM), 2 * HEAD_DIM), :]
            acc_ref[c] += jnp.dot(x, wi, preferred_element_type=F32)
        return carry

    lax.fori_loop(0, NSA_BLOCK // 2, body, 0)
    for c in range(NSA_KV_COLS):
        w, kh = divmod(c, NSA_KV_HEADS)
        hid = _gelu_tanh(acc_ref[c]).astype(BF16)
        y = jnp.dot(hid, w2_ref[w].astype(BF16), preferred_element_type=F32)
        if w == 0:
            y = _rms_rows(y, kg_ref[0:1, :])
        o_ref[w, :, kh * HEAD_DIM:(kh + 1) * HEAD_DIM] = y


def _compress_rows_kernel(*refs, nblk):
    x_refs = refs[:NSA_KV_COLS]
    pos_ref, w1_ref, w2_ref, kg_ref, o_ref, acc_ref = refs[NSA_KV_COLS:]

    def x_of(i, c):
        return x_refs[c][pl.ds(i, nblk, stride=NSA_BLOCK), :]

    _compress_mlp(x_of, nblk, pos_ref, w1_ref, w2_ref, kg_ref, o_ref, acc_ref)


def _compress_pages_kernel(*refs, pps, nblk):
    tbl_ref = refs[0]
    page_refs = refs[1:1 + pps]
    pos_ref, w1_ref, w2_ref, kg_ref, o_ref, xs_ref, acc_ref = refs[1 + pps:]
    st = pl.program_id(2)
    page_rows = PAGE_ROWS * NSA_KV_COLS
    for p in range(pps):
        xs_ref[pl.ds(pl.multiple_of((st * pps + p) * page_rows, page_rows), page_rows), :] = page_refs[p][...]

    @pl.when(st == pl.num_programs(2) - 1)
    def _():
        def x_of(i, c):
            return xs_ref[pl.ds(i * NSA_KV_COLS + c, nblk, stride=NSA_BLOCK * NSA_KV_COLS), :]

        _compress_mlp(x_of, nblk, pos_ref, w1_ref, w2_ref, kg_ref, o_ref, acc_ref)


def _compress(src, tbl, pos, w1b, w2, k_norm, j):
    weight_specs = lambda nidx: [
        pl.BlockSpec((None, 2, NSA_BLOCK, HEAD_DIM), lambda *a: (j, 0, 0, 0)),
        pl.BlockSpec((None, 2, NSA_BLOCK * HEAD_DIM, NSA_PHI_HIDDEN), lambda *a: (j, 0, 0, 0),
                     pipeline_mode=pl.Buffered(1)),
        pl.BlockSpec((None, 2, NSA_PHI_HIDDEN, HEAD_DIM), lambda *a: (j, 0, 0, 0)),
        pl.BlockSpec(k_norm.shape, lambda *a: (0, 0)),
    ]
    kvw = NSA_KV_HEADS * HEAD_DIM
    if tbl is None:
        bsz, t_len, _ = src.shape
        nblk = t_len // NSA_BLOCK
        return pl.pallas_call(
            functools.partial(_compress_rows_kernel, nblk=nblk),
            grid=(bsz,),
            in_specs=[pl.BlockSpec((None, t_len, HEAD_DIM), lambda b, c=c: (b, 0, c)) for c in range(NSA_KV_COLS)]
            + weight_specs(1),
            out_specs=pl.BlockSpec((2, None, nblk, kvw), lambda b: (0, b, 0, 0)),
            out_shape=jax.ShapeDtypeStruct((2, bsz, nblk, kvw), F32),
            scratch_shapes=[pltpu.VMEM((NSA_KV_COLS, nblk, NSA_PHI_HIDDEN), F32)],
            compiler_params=_cparams(1),
            name="nsa_compress_rows",
        )(*([src] * NSA_KV_COLS), pos, w1b, w2, k_norm)
    bsz, npg = tbl.shape
    pps = math.gcd(CMP_PAGES_PER_STEP, npg)
    ppg = math.gcd(CMP_PAGES_PER_GROUP, npg)
    bpp = PAGE_ROWS // NSA_BLOCK
    nblk = ppg * bpp
    page_rows = PAGE_ROWS * NSA_KV_COLS

    def page_spec(p):
        return pl.BlockSpec((None, page_rows, HEAD_DIM),
                            lambda b, grp, st, t: (t[b, grp * ppg + st * pps + p], 0, 0))

    grid_spec = pltpu.PrefetchScalarGridSpec(
        num_scalar_prefetch=1,
        grid=(bsz, npg // ppg, ppg // pps),
        in_specs=[page_spec(p) for p in range(pps)] + weight_specs(4),
        out_specs=pl.BlockSpec((2, None, nblk, kvw), lambda b, grp, st, t: (0, b, grp, 0)),
        scratch_shapes=[pltpu.VMEM((ppg * page_rows, HEAD_DIM), F32),
                        pltpu.VMEM((NSA_KV_COLS, nblk, NSA_PHI_HIDDEN), F32)],
    )
    return pl.pallas_call(
        functools.partial(_compress_pages_kernel, pps=pps, nblk=nblk),
        grid_spec=grid_spec,
        out_shape=jax.ShapeDtypeStruct((2, bsz, npg * bpp, kvw), F32),
        compiler_params=_cparams(3),
        name="nsa_compress_pages",
    )(tbl, *([src] * pps), pos, w1b, w2, k_norm)


def _nsa_cmp_kernel(q_ref, kc_ref, vc_ref, bias_ref, gates_ref, oc_ref, sel_ref, *, q_start, tq, nb, extra):
    kh = pl.program_id(1)
    qi = pl.program_id(2)
    t_pos = q_start + qi * tq + lax.broadcasted_iota(jnp.int32, (tq, 1), 0)
    blk = lax.broadcasted_iota(jnp.int32, (1, nb), 1)
    valid = t_pos >= blk * NSA_BLOCK + (NSA_BLOCK - 1)
    kc = kc_ref[...].astype(BF16)
    vc = vc_ref[...].astype(BF16)
    gates = gates_ref[...]
    imp = jnp.zeros((tq, nb), F32)
    for g in range(NSA_GROUP):
        qg = q_ref[:, g * HEAD_DIM:(g + 1) * HEAD_DIM]
        s = lax.dot_general(qg, kc, (((1,), (1,)), ((), ())), preferred_element_type=F32) + bias_ref[g]
        s = jnp.where(valid, s, NEG_BIG)
        m = jnp.max(s, axis=-1, keepdims=True)
        p = jnp.where(valid, jnp.exp(s - m), 0.0)
        l = jnp.sum(p, axis=-1, keepdims=True)
        p = p / jnp.where(l > 0, l, 1.0)
        imp = imp + p
        o = jnp.dot(p.astype(BF16), vc, preferred_element_type=F32)
        onehot = lax.broadcasted_iota(jnp.int32, (1, LANES), 1) == (kh * NSA_GROUP + g) * 3
        gcol = jnp.sum(jnp.where(onehot, gates, 0.0), axis=-1, keepdims=True)
        oc_ref[:, g * HEAD_DIM:(g + 1) * HEAD_DIM] = o * gcol
    cur = t_pos // NSA_BLOCK
    forced = (blk == 0) | (blk == cur) | (blk == cur - 1)
    score = jnp.where(forced, NSA_FORCE_SCORE, jnp.where(blk <= cur, imp, -1.0))
    if not extra and tq == LANES and nb <= LANES:
        score_t = jnp.concatenate([score, jnp.zeros((tq, LANES - nb), F32)], axis=1).T[0:nb, :]
        blk_t = lax.broadcasted_iota(jnp.int32, (nb, 1), 0)
        rank_t = jnp.zeros((nb, tq), jnp.int32)
        for mrow in range(nb):
            sm = score_t[mrow:mrow + 1, :]
            ahead = (sm > score_t) | ((sm == score_t) & (mrow < blk_t))
            rank_t = rank_t + ahead.astype(jnp.int32)
        sel_t = (rank_t < NSA_TOPN).astype(F32)
        sel_ref[...] = jnp.concatenate([sel_t, jnp.zeros((LANES - nb, tq), F32)], axis=0).T[:, 0:nb]
        return
    rank = jnp.zeros((tq, nb), jnp.int32)
    for mcol in range(nb):
        sm = score[:, mcol:mcol + 1]
        ahead = (sm > score) | ((sm == score) & (mcol < blk))
        rank = rank + ahead.astype(jnp.int32)
    if not extra:
        sel_ref[...] = (rank < NSA_TOPN).astype(F32)
    else:
        rank = rank + (score < NSA_FORCE_SCORE).astype(jnp.int32)
        rank_extra = jnp.sum((score >= NSA_FORCE_SCORE).astype(jnp.int32), axis=-1, keepdims=True)
        lane = lax.broadcasted_iota(jnp.int32, (tq, NSA_TOPN), 1)
        out = jnp.zeros((tq, NSA_TOPN), jnp.int32)
        for r in range(NSA_TOPN):
            idx_r = jnp.sum(jnp.where(rank == r, blk, 0), axis=-1, keepdims=True)
            idx_r = idx_r + jnp.where(rank_extra == r, nb, 0)
            out = jnp.where(lane == r, idx_r, out)
        sel_ref[...] = out


def _nsa_cmp(q3, kvc, bias_c, gates3, *, q_start, extra):
    bsz, t_len, _ = q3.shape
    nb = kvc.shape[2]
    tq = min(t_len, Q_TILE)
    gw = NSA_GROUP * HEAD_DIM
    if extra:
        assert t_len == 1 and q_start // NSA_BLOCK == nb
        sel_shape = jax.ShapeDtypeStruct((bsz, NSA_KV_HEADS, t_len, NSA_TOPN), jnp.int32)
        sel_spec = pl.BlockSpec((None, None, tq, NSA_TOPN), lambda b, kh, qi: (b, kh, qi, 0))
    else:
        assert (q_start + t_len) == nb * NSA_BLOCK
        sel_shape = jax.ShapeDtypeStruct((bsz, NSA_KV_HEADS, t_len, nb), F32)
        sel_spec = pl.BlockSpec((None, None, tq, nb), lambda b, kh, qi: (b, kh, qi, 0))
    return pl.pallas_call(
        functools.partial(_nsa_cmp_kernel, q_start=q_start, tq=tq, nb=nb, extra=extra),
        grid=(bsz, NSA_KV_HEADS, t_len // tq),
        in_specs=[
            pl.BlockSpec((None, tq, gw), lambda b, kh, qi: (b, qi, kh)),
            pl.BlockSpec((None, None, nb, HEAD_DIM), lambda b, kh, qi: (0, b, 0, kh)),
            pl.BlockSpec((None, None, nb, HEAD_DIM), lambda b, kh, qi: (1, b, 0, kh)),
            pl.BlockSpec((NSA_GROUP, tq, nb), lambda b, kh, qi: (kh, qi, 0)),
            pl.BlockSpec((None, tq, LANES), lambda b, kh, qi: (b, qi, 0)),
        ],
        out_specs=[pl.BlockSpec((None, tq, gw), lambda b, kh, qi: (b, qi, kh)), sel_spec],
        out_shape=[jax.ShapeDtypeStruct((bsz, t_len, NSA_HEADS * HEAD_DIM), F32), sel_shape],
        compiler_params=_cparams(3),
        name="nsa_cmp_attn",
    )(q3, kvc, kvc, bias_c, gates3)


def _lane_column(x, col):
    onehot = lax.broadcasted_iota(jnp.int32, (1, x.shape[1]), 1) == col
    return jnp.sum(jnp.where(onehot, x, 0.0), axis=-1, keepdims=True)


def _row_to_column(row):
    n = row.shape[1]
    eye = lax.broadcasted_iota(jnp.int32, (n, n), 0) == lax.broadcasted_iota(jnp.int32, (n, n), 1)
    return jnp.sum(jnp.where(eye, row, 0.0), axis=-1, keepdims=True)


def _flash_kernel(*refs, kind, hps, shared_kv, tq, blk, gate_branch):
    q_ref, k_ref, v_ref = refs[:3]
    pos = 3
    bias_ref = bm_ref = gates_ref = c_ref = None
    if kind in ("nsa_sel", "nsa_win", "moba"):
        bias_ref = refs[pos]
        pos += 1
    if kind in ("nsa_sel", "moba"):
        bm_ref = refs[pos]
        pos += 1
    if kind in ("nsa_sel", "nsa_win"):
        gates_ref = refs[pos]
        pos += 1
    if kind == "fox":
        c_ref = refs[pos]
        pos += 1
    o_ref = refs[pos]
    m_refs = refs[pos + 1:pos + 1 + hps]
    l_refs = refs[pos + 1 + hps:pos + 1 + 2 * hps]
    acc_refs = refs[pos + 1 + 2 * hps:pos + 1 + 3 * hps]
    tk = tq
    step = pl.program_id(1)
    qi = pl.program_id(2)
    q0 = pl.multiple_of(qi * tq, tq)
    ii = lax.broadcasted_iota(jnp.int32, (tq, tk), 0)
    jj = lax.broadcasted_iota(jnp.int32, (tq, tk), 1)
    n_back = NSA_WINDOW // tk
    lo = jnp.maximum(qi - n_back, 0) if kind == "nsa_win" else 0
    for g in range(hps):
        m_refs[g][...] = jnp.full(m_refs[g].shape, NEG_BIG, F32)
        l_refs[g][...] = jnp.zeros(l_refs[g].shape, F32)
        acc_refs[g][...] = jnp.zeros(acc_refs[g].shape, F32)

    def head_cols(g):
        return slice(g * HEAD_DIM, (g + 1) * HEAD_DIM)

    def tile_step(ki, mode):
        diag = mode == "diag"
        k0 = pl.multiple_of(ki * tk, tk)
        delta = qi - ki
        shared_add = None
        if kind == "nsa_sel":
            bm = bm_ref[...]
            shared_add = (_lane_column(bm, k0 // blk) - 1.0) * (-NEG_BIG)
            for sub in range(1, tk // blk):
                shared_add = jnp.where(jj >= sub * blk, (_lane_column(bm, k0 // blk + sub) - 1.0) * (-NEG_BIG),
                                       shared_add)
        elif kind == "nsa_win" and (mode == "far" or (mode == "near" and n_back == 1)):
            shared_add = jnp.where(delta == n_back, jnp.where(jj < ii, NEG_BIG, 0.0), 0.0)
        elif kind == "fox" and diag:
            shared_add = jnp.where(jj <= ii, 0.0, NEG_BIG)
        for g in range(hps):
            kv_cols = slice(0, HEAD_DIM) if shared_kv else head_cols(g)
            kt = k_ref[pl.ds(k0, tk), kv_cols]
            vt = v_ref[pl.ds(k0, tk), kv_cols]
            s = lax.dot_general(q_ref[:, head_cols(g)], kt, (((1,), (1,)), ((), ())), preferred_element_type=F32)
            if kind == "fox":
                s = s + (c_ref[g, :, pl.ds(q0, LANES)][:, 0:1] - c_ref[g, :, pl.ds(k0, tk)])
            elif diag:
                s = s + bias_ref[g, 0]
            elif mode == "near":
                s = s + bias_ref[g, 1]
            else:
                s = s + bias_ref[g, 1, tq - 1:tq, 0:1]
            if shared_add is not None:
                s = s + shared_add
            if kind == "moba":
                s = s + (_lane_column(bm_ref[g], k0 // blk) - 1.0) * (-NEG_BIG)
            m_old = m_refs[g][...]
            m_new = jnp.maximum(m_old, jnp.max(s, axis=-1, keepdims=True))
            alpha = jnp.exp(m_old - m_new)
            parts = [jnp.exp(s[:, c * LANES:(c + 1) * LANES] - m_new) for c in range(tk // LANES)]
            l_refs[g][...] = alpha * l_refs[g][...] + functools.reduce(lambda a, b: a + b, parts)
            p = jnp.concatenate(parts, axis=-1).astype(BF16)
            acc_refs[g][...] = alpha * acc_refs[g][...] + jnp.dot(p, vt, preferred_element_type=F32)
            m_refs[g][...] = m_new

    def body(ki, carry):
        tile_step(ki, "far")
        return carry

    if kind == "fox":
        lax.fori_loop(lo, qi, body, 0)
    else:
        lax.fori_loop(lo, qi - 1, body, 0)

        @pl.when(qi >= 1)
        def _():
            tile_step(qi - 1, "near")

    tile_step(qi, "diag")
    for g in range(hps):
        o = acc_refs[g][...] / jnp.sum(l_refs[g][...], axis=-1, keepdims=True)
        if gates_ref is not None:
            o = o * _lane_column(gates_ref[...], (step * hps + g) * 3 + gate_branch)
        o_ref[:, head_cols(g)] = o.astype(o_ref.dtype)


def _flash(kind, q3, kv3, k_col, v_col, n_kv_heads, *, bias_tiles=None, bm=None, gates3=None, cum=None,
           blk=0, gate_branch=0, out_dtype=F32):
    bsz, t_len, qw = q3.shape
    n_heads = qw // HEAD_DIM
    shared_kv = n_kv_heads < n_heads
    hps = n_heads // n_kv_heads if shared_kv else FLASH_HEADS_PER_STEP
    tq = FLASH_TILE
    assert t_len % tq == 0 and (shared_kv or (k_col % hps == 0 and v_col % hps == 0))
    gw = hps * HEAD_DIM
    if shared_kv:
        kv_spec = lambda c0: pl.BlockSpec((None, t_len, HEAD_DIM), lambda b, h, qi: (b, 0, c0 + h))
    else:
        kv_spec = lambda c0: pl.BlockSpec((None, t_len, gw), lambda b, h, qi: (b, 0, c0 // hps + h))
    in_specs = [pl.BlockSpec((None, tq, gw), lambda b, h, qi: (b, qi, h)), kv_spec(k_col), kv_spec(v_col)]
    args = [q3, kv3, kv3]
    if bias_tiles is not None:
        in_specs.append(pl.BlockSpec((hps, 2, tq, tq), lambda b, h, qi: (h, 0, 0, 0)))
        args.append(bias_tiles)
    if bm is not None:
        nb = bm.shape[-1]
        if shared_kv:
            in_specs.append(pl.BlockSpec((None, None, tq, nb), lambda b, h, qi: (b, h, qi, 0)))
        else:
            in_specs.append(pl.BlockSpec((None, hps, tq, nb), lambda b, h, qi: (b, h, qi, 0)))
        args.append(bm)
    if gates3 is not None:
        in_specs.append(pl.BlockSpec((None, tq, LANES), lambda b, h, qi: (b, qi, 0)))
        args.append(gates3)
    if cum is not None:
        in_specs.append(pl.BlockSpec((None, hps, 1, t_len), lambda b, h, qi: (b, h, 0, 0)))
        args.append(cum)
    return pl.pallas_call(
        functools.partial(_flash_kernel, kind=kind, hps=hps, shared_kv=shared_kv, tq=tq, blk=blk,
                          gate_branch=gate_branch),
        grid=(bsz, n_heads // hps, t_len // tq),
        in_specs=in_specs,
        out_specs=pl.BlockSpec((None, tq, gw), lambda b, h, qi: (b, qi, h)),
        out_shape=jax.ShapeDtypeStruct((bsz, t_len, qw), out_dtype),
        scratch_shapes=[pltpu.VMEM((tq, LANES), F32)] * (2 * hps) + [pltpu.VMEM((tq, HEAD_DIM), F32)] * hps,
        compiler_params=_cparams(3),
        name="flash_" + kind,
    )(*args)


def _sum3_kernel(a_ref, b_ref, c_ref, o_ref):
    o_ref[...] = (a_ref[...] + b_ref[...] + c_ref[...]).astype(o_ref.dtype)


def _sum3_cast(a, b, c):
    m, n = a.shape
    tm = min(m, ROW_TILE)
    spec = pl.BlockSpec((tm, n), lambda i: (i, 0))
    return pl.pallas_call(
        _sum3_kernel, grid=(m // tm,), in_specs=[spec] * 3, out_specs=spec,
        out_shape=jax.ShapeDtypeStruct((m, n), BF16), compiler_params=_cparams(1), name="nsa_sum",
    )(a, b, c)


def _hgrn_kernel(q_ref, z_ref, v_ref, g_ref, lb_ref, on_ref, s0_ref, o_ref, s_ref, *st_refs, t_len, chunk, hps):
    on = on_ref[...]
    rows = lax.broadcasted_iota(jnp.int32, (chunk, 1), 0)

    single = t_len < chunk

    def load(ref, r0, cols):
        if single:
            return jnp.broadcast_to(ref[0:1, cols], (chunk, HEAD_DIM))
        return ref[pl.ds(r0, chunk), cols]

    for g in range(hps):
        st_refs[g][...] = s0_ref[g].T

    def body(c, carry):
        r0 = pl.multiple_of(c * chunk, chunk)
        for g in range(hps):
            cols = slice(g * HEAD_DIM, (g + 1) * HEAD_DIM)
            q = load(q_ref, r0, cols)
            z = load(z_ref, r0, cols)
            v = load(v_ref, r0, cols)
            a_term = lb_ref[g, 0:1, :]
            b_term = lb_ref[g, 1:2, :] + _log_sigmoid(z)
            logf = jnp.maximum(a_term, b_term) + jnp.log1p(jnp.exp(-jnp.abs(a_term - b_term)))
            k = lb_ref[g, 2:3, :] * _sigmoid(-z)
            if single:
                logf = jnp.where(rows < t_len, logf, 0.0)
                k = jnp.where(rows < t_len, k, 0.0)
            cum = logf
            shift = 1
            while shift < chunk:
                cum = cum + jnp.where(rows >= shift, pltpu.roll(cum, shift, axis=0), 0.0)
                shift *= 2
            a_last = cum[chunk - 1:chunk, :]
            o = jnp.zeros((chunk, HEAD_DIM), F32)
            for s_row in range(chunk):
                diff = jnp.where(rows >= s_row, cum - cum[s_row:s_row + 1, :], -jnp.inf)
                wgt = q * jnp.exp(diff) * k[s_row:s_row + 1, :]
                o = o + jnp.sum(wgt, axis=-1, keepdims=True) * v[s_row:s_row + 1, :]
            st = st_refs[g][...]
            qa = (q * jnp.exp(cum)).astype(BF16)
            o = o + lax.dot_general(qa, st.astype(BF16), (((1,), (1,)), ((), ())), preferred_element_type=F32)
            kd = (k * jnp.exp(a_last - cum)).astype(BF16)
            st_refs[g][...] = jnp.exp(a_last) * st + lax.dot_general(v.astype(BF16), kd, (((0,), (0,)), ((), ())),
                                                                     preferred_element_type=F32)
            gate = load(g_ref, r0, cols)
            o = _rms_rows(o, on) * (gate * _sigmoid(gate))
            if single:
                o_ref[:, cols] = o[0:t_len, :].astype(o_ref.dtype)
            else:
                o_ref[pl.ds(r0, chunk), cols] = o.astype(o_ref.dtype)
        return carry

    n_chunks = max(t_len // chunk, 1)
    lax.fori_loop(0, n_chunks, body, 0, unroll=2 if n_chunks % 2 == 0 else 1)
    for g in range(hps):
        s_ref[g] = st_refs[g][...].T


HGRN_HEADS_PER_STEP = 4


def _hgrn(ph3, lb_rows, o_norm, s0):
    bsz, t_len, _ = ph3.shape
    chunk = 16
    hps = HGRN_HEADS_PER_STEP
    assert t_len % chunk == 0 or t_len == 1
    gw = hps * HEAD_DIM
    n_steps = HG_HEADS // hps
    col = lambda part: pl.BlockSpec((None, t_len, gw), lambda b, h: (b, 0, part * n_steps + h))
    on = o_norm.reshape(1, HEAD_DIM)
    state_spec = pl.BlockSpec((None, hps, HEAD_DIM, HEAD_DIM), lambda b, h: (b, h, 0, 0))
    return pl.pallas_call(
        functools.partial(_hgrn_kernel, t_len=t_len, chunk=chunk, hps=hps),
        grid=(bsz, n_steps),
        in_specs=[col(0), col(1), col(2), col(3),
                  pl.BlockSpec((hps, 3, HEAD_DIM), lambda b, h: (h, 0, 0)),
                  pl.BlockSpec((1, HEAD_DIM), lambda b, h: (0, 0)),
                  state_spec],
        out_specs=[pl.BlockSpec((None, t_len, gw), lambda b, h: (b, 0, h)), state_spec],
        out_shape=[jax.ShapeDtypeStruct((bsz, t_len, HG_HEADS * HEAD_DIM), BF16),
                   jax.ShapeDtypeStruct((bsz, HG_HEADS, HEAD_DIM, HEAD_DIM), F32)],
        scratch_shapes=[pltpu.VMEM((HEAD_DIM, HEAD_DIM), F32)] * hps,
        compiler_params=_cparams(2),
        name="hgrn2",
    )(ph3, ph3, ph3, ph3, lb_rows, on, s0)


def _group_rows(q_ref, group):
    return jnp.concatenate([q_ref[:, g * HEAD_DIM:(g + 1) * HEAD_DIM].astype(F32) for g in range(group)], axis=0)


def _store_group_rows(o_ref, o):
    for g in range(o.shape[0]):
        o_ref[:, g * HEAD_DIM:(g + 1) * HEAD_DIM] = o[g:g + 1, :].astype(o_ref.dtype)


NSA_SEL_BLOCKS_PER_STEP = 4


def _nsa_sel_step_kernel(*refs, t_pos, nb, bps):
    idx_ref, q_ref = refs[1], refs[2]
    kv_refs = refs[3:3 + bps]
    kn_ref, vn_ref, rbt_ref, gates_ref, o_ref, m_ref, l_ref, acc_ref = refs[3 + bps:]
    b = pl.program_id(0)
    kh = pl.program_id(1)
    j = pl.program_id(2)
    q4 = _group_rows(q_ref, NSA_GROUP)
    rbt = rbt_ref[...]
    col = lambda bk: rbt[:, bk:bk + 1]

    @pl.when(j == 0)
    def _():
        m_ref[...] = jnp.sum(q4 * kn_ref[...], axis=-1, keepdims=True) + col(0)
        l_ref[...] = jnp.ones_like(l_ref)
        acc_ref[...] = jnp.broadcast_to(vn_ref[...], acc_ref.shape)

    q4b = q4.astype(BF16)
    logits = []
    for u in range(bps):
        blk_id = idx_ref[b * NSA_KV_HEADS + kh, j * bps + u]
        kt = kv_refs[u][pl.ds(kh, NSA_BLOCK, stride=NSA_KV_COLS), :].astype(BF16)
        s_u = lax.dot_general(q4b, kt, (((1,), (1,)), ((), ())), preferred_element_type=F32)
        dist = t_pos - (blk_id * NSA_BLOCK + lax.broadcasted_iota(jnp.int32, (NSA_GROUP, NSA_BLOCK), 1))
        logits.append(s_u + _bucket_bias(dist, col) + jnp.where(blk_id < nb, 0.0, NEG_BIG))
    m_old = m_ref[...]
    m_new = m_old
    for s_u in logits:
        m_new = jnp.maximum(m_new, jnp.max(s_u, axis=-1, keepdims=True))
    alpha = jnp.exp(m_old - m_new)
    l = alpha * l_ref[...]
    acc = alpha * acc_ref[...]
    for u in range(bps):
        p_u = jnp.exp(logits[u] - m_new)
        vt = kv_refs[u][pl.ds(NSA_KV_HEADS + kh, NSA_BLOCK, stride=NSA_KV_COLS), :].astype(BF16)
        l = l + jnp.sum(p_u, axis=-1, keepdims=True)
        acc = acc + jnp.dot(p_u.astype(BF16), vt, preferred_element_type=F32)
    l_ref[...] = l
    acc_ref[...] = acc
    m_ref[...] = m_new

    @pl.when(j == pl.num_programs(2) - 1)
    def _():
        o = acc_ref[...] / l_ref[...]
        gates = gates_ref[...]
        gcol = jnp.concatenate([_lane_column(gates, (kh * NSA_GROUP + g) * 3 + 1) for g in range(NSA_GROUP)], axis=0)
        _store_group_rows(o_ref, o * gcol)


def _nsa_sel_step(q3, cache_pages, tbl, idx, sel_new3, rbt, gates3, *, t_pos):
    bsz = q3.shape[0]
    nb = tbl.shape[1] * (PAGE_ROWS // NSA_BLOCK)
    gw = NSA_GROUP * HEAD_DIM
    halves = PAGE_ROWS // NSA_BLOCK

    bps = NSA_SEL_BLOCKS_PER_STEP
    assert NSA_TOPN % bps == 0

    def cache_map(u):
        def index_map(b, kh, j, tbl_ref, idx_ref):
            blk = jnp.minimum(idx_ref[b * NSA_KV_HEADS + kh, j * bps + u], nb - 1)
            return (tbl_ref[b, blk // halves], blk % halves, 0)
        return index_map

    grid_spec = pltpu.PrefetchScalarGridSpec(
        num_scalar_prefetch=2,
        grid=(bsz, NSA_KV_HEADS, NSA_TOPN // bps),
        in_specs=[pl.BlockSpec((None, 1, gw), lambda b, kh, j, t, i: (b, 0, kh))]
        + [pl.BlockSpec((None, NSA_BLOCK * NSA_KV_COLS, HEAD_DIM), cache_map(u)) for u in range(bps)]
        + [
            pl.BlockSpec((None, 1, HEAD_DIM), lambda b, kh, j, t, i: (b, 0, kh)),
            pl.BlockSpec((None, 1, HEAD_DIM), lambda b, kh, j, t, i: (b, 0, NSA_KV_HEADS + kh)),
            pl.BlockSpec((None, NSA_GROUP, REL_BUCKETS), lambda b, kh, j, t, i: (kh, 0, 0)),
            pl.BlockSpec((None, 1, LANES), lambda b, kh, j, t, i: (b, 0, 0)),
        ],
        out_specs=pl.BlockSpec((None, 1, gw), lambda b, kh, j, t, i: (b, 0, kh)),
        scratch_shapes=[pltpu.VMEM((NSA_GROUP, 1), F32), pltpu.VMEM((NSA_GROUP, 1), F32),
                        pltpu.VMEM((NSA_GROUP, HEAD_DIM), F32)],
    )
    return pl.pallas_call(
        functools.partial(_nsa_sel_step_kernel, t_pos=t_pos, nb=nb, bps=bps),
        grid_spec=grid_spec,
        out_shape=jax.ShapeDtypeStruct((bsz, 1, NSA_HEADS * HEAD_DIM), F32),
        compiler_params=_cparams(3),
        name="nsa_sel_step",
    )(tbl, idx, q3, *([cache_pages] * bps), sel_new3, sel_new3, rbt, gates3)


def _nsa_win_step_kernel(q_ref, k_ref, v_ref, kn_ref, vn_ref, rbt_ref, gates_ref, o_ref, *, pw):
    kh = pl.program_id(1)
    q4 = _group_rows(q_ref, NSA_GROUP)
    rbt = rbt_ref[...]
    col = lambda bk: rbt[:, bk:bk + 1]
    s = lax.dot_general(q4.astype(BF16), k_ref[...].astype(BF16), (((1,), (1,)), ((), ())),
                        preferred_element_type=F32)
    dist = pw - lax.broadcasted_iota(jnp.int32, (NSA_GROUP, pw), 1)
    mask = dist <= NSA_WINDOW
    s = jnp.where(mask, s + _bucket_bias(dist, col), NEG_BIG)
    s_self = jnp.sum(q4 * kn_ref[...], axis=-1, keepdims=True) + col(0)
    m = jnp.maximum(jnp.max(s, axis=-1, keepdims=True), s_self)
    p = jnp.where(mask, jnp.exp(s - m), 0.0)
    p_self = jnp.exp(s_self - m)
    l = jnp.sum(p, axis=-1, keepdims=True) + p_self
    o = jnp.dot(p.astype(BF16), v_ref[...].astype(BF16), preferred_element_type=F32) + p_self * vn_ref[...]
    gates = gates_ref[...]
    gcol = jnp.concatenate([_lane_column(gates, (kh * NSA_GROUP + g) * 3 + 2) for g in range(NSA_GROUP)], axis=0)
    _store_group_rows(o_ref, o / l * gcol)


def _nsa_win_step(q3, win_cache, lead, win_new3, rbt, gates3):
    bsz = q3.shape[0]
    pw = win_cache.shape[1]
    gw = NSA_GROUP * HEAD_DIM
    return pl.pallas_call(
        functools.partial(_nsa_win_step_kernel, pw=pw),
        grid=(bsz, NSA_KV_HEADS),
        in_specs=[
            pl.BlockSpec((None, 1, gw), lambda b, kh: (b, 0, kh)),
            pl.BlockSpec((None, pw, HEAD_DIM), lambda b, kh: (lead + b, 0, kh)),
            pl.BlockSpec((None, pw, HEAD_DIM), lambda b, kh: (lead + b, 0, NSA_KV_HEADS + kh)),
            pl.BlockSpec((None, 1, HEAD_DIM), lambda b, kh: (b, 0, kh)),
            pl.BlockSpec((None, 1, HEAD_DIM), lambda b, kh: (b, 0, NSA_KV_HEADS + kh)),
            pl.BlockSpec((None, NSA_GROUP, REL_BUCKETS), lambda b, kh: (kh, 0, 0)),
            pl.BlockSpec((None, 1, LANES), lambda b, kh: (b, 0, 0)),
        ],
        out_specs=pl.BlockSpec((None, 1, gw), lambda b, kh: (b, 0, kh)),
        out_shape=jax.ShapeDtypeStruct((bsz, 1, NSA_HEADS * HEAD_DIM), F32),
        compiler_params=_cparams(2),
        name="nsa_win_step",
    )(q3, win_cache, win_cache, win_new3, win_new3, rbt, gates3)


def _even_proj(h, hr, w, j):
    qw = NSA_HEADS * HEAD_DIM
    kv_w = 2 * NSA_KV_HEADS * HEAD_DIM
    n_main = qw + 3 * kv_w
    n_gate = 3 * NSA_HEADS
    w_in = w["w_in_even"]
    hr_parts = None if hr is None else [hr]
    pa, par = _dense2([h], hr_parts, w_in, j, 0, n_main, tn=512, name="even_in_attn")
    w_gate = jnp.pad(w_in[j, :, n_main:n_main + n_gate], ((0, 0), (0, LANES - n_gate)))
    graw, grawr = _dense2([h], hr_parts, w_gate, None, 0, LANES, tn=LANES, name="even_in_gate")
    w_hg = w_in[j, :, n_main + n_gate:]
    ph, phr = _dense2([h], hr_parts, w_hg, None, 0, w_hg.shape[1], tn=512, name="even_in_hgrn")
    return (pa, graw, ph), (None if hr is None else (par, grawr, phr))


def _even_out(x2d, a_parts, xr2d, ar_parts, w, j):
    d = x2d.shape[1]
    return _dense2(a_parts, ar_parts, w["w_out_even"], j, 0, d, tn=min(d, 512), mode="res", res=x2d,
                   r_res=xr2d, scale=1.0, name="even_out")


def _even_mixer(x2d, h, bsz, t_len, j, w, tabs, past):
    proj, _ = _even_proj(h, None, w, j)
    a_parts, new = _even_core(proj, bsz, t_len, j, w, tabs, past)
    x_new, _ = _even_out(x2d, a_parts, None, None, w, j)
    return (x_new,) + new


def _even_core(proj, bsz, t_len, j, w, tabs, past):
    pa, graw, ph = proj
    m = pa.shape[0]
    qw = NSA_HEADS * HEAD_DIM
    kv_w = 2 * NSA_KV_HEADS * HEAD_DIM
    q, cmp_new, sel_new, win_new, selb, winb, gates = _even_post(pa, graw, w["nsa_q_norm"][j], w["nsa_k_norm"][j])
    q3 = q.reshape(bsz, t_len, qw)
    gates3 = gates.reshape(bsz, t_len, LANES)
    rel_bias = w["rel_bias"]
    k_norm = w["nsa_k_norm"][j]
    if past is None:
        q_start = 0
        src, tbl = cmp_new.reshape(bsz, t_len, kv_w), None
        s0 = jnp.zeros((bsz, HG_HEADS, HEAD_DIM, HEAD_DIM), F32)
    else:
        src, sel_pages, tbl, win_cache, s0 = past
        q_start = tbl.shape[1] * PAGE_ROWS
    kvc = _compress(src, tbl, w["nsa_phi_pos"], tabs["phi_w1"], w["nsa_phi_w2"], k_norm, j)
    nb = kvc.shape[2]
    bias_c = _bias_cmp_table(rel_bias, q_start, t_len, nb)
    o_c, sel = _nsa_cmp(q3, kvc, bias_c, gates3, q_start=q_start, extra=past is not None)
    if past is None:
        selb3 = selb.reshape(bsz, t_len, kv_w)
        winb3 = winb.reshape(bsz, t_len, kv_w)
        o_s = _flash("nsa_sel", q3, selb3, 0, NSA_KV_HEADS, NSA_KV_HEADS, bias_tiles=tabs["tiles"], bm=sel,
                     gates3=gates3, blk=NSA_BLOCK, gate_branch=1)
        o_w = _flash("nsa_win", q3, winb3, 0, NSA_KV_HEADS, NSA_KV_HEADS, bias_tiles=tabs["tiles"],
                     gates3=gates3, gate_branch=2)
        keep = min(NSA_WINDOW, t_len)
        win_buf = win_new.reshape(bsz, t_len, kv_w)[:, t_len - keep:]
    else:
        rbt = rel_bias.T.reshape(NSA_KV_HEADS, NSA_GROUP, REL_BUCKETS)
        sel_new3 = sel_new.reshape(bsz, t_len, kv_w)
        win_new3 = win_new.reshape(bsz, t_len, kv_w)
        idx = sel.reshape(bsz * NSA_KV_HEADS, NSA_TOPN)
        o_s = _nsa_sel_step(q3, sel_pages, tbl, idx, sel_new3, rbt, gates3, t_pos=q_start)
        wc = win_cache.reshape(win_cache.shape[0] * win_cache.shape[1], win_cache.shape[2], kv_w)
        o_w = _nsa_win_step(q3, wc, j * bsz, win_new3, rbt, gates3)
        win_all = jnp.concatenate([wc[j * bsz:(j + 1) * bsz], win_new3], axis=1)
        keep = min(NSA_WINDOW, win_all.shape[1])
        win_buf = win_all[:, win_all.shape[1] - keep:]
    a_nsa = _sum3_cast(o_c.reshape(m, qw), o_s.reshape(m, qw), o_w.reshape(m, qw))
    o_hg, s_new = _hgrn(ph.reshape(bsz, t_len, ph.shape[1]), tabs["lb_rows"][j], w["hg_o_norm"][j], s0)
    kv_shape = (bsz, t_len, 2, NSA_KV_HEADS, HEAD_DIM)
    new = (cmp_new.reshape(kv_shape), sel_new.reshape(kv_shape),
           win_buf.reshape(bsz, win_buf.shape[1], 2, NSA_KV_HEADS, HEAD_DIM), s_new)
    return [a_nsa, o_hg.reshape(m, HG_HEADS * HEAD_DIM)], new


ODD_HEADS = MOBA_HEADS
ODD_QW = ODD_HEADS * HEAD_DIM
ODD_KVW = 2 * ODD_HEADS * HEAD_DIM


def _odd_post_kernel(pm_ref, fz_ref, mg_ref, fg_ref, fb_ref, qm_ref, moba_ref, mobab_ref, qf_ref, fox_ref,
                     foxb_ref, logf_ref):
    scale = HEAD_DIM ** -0.5
    off = 0
    for g_ref, q_ref, kv_ref, kvb_ref in ((mg_ref, qm_ref, moba_ref, mobab_ref), (fg_ref, qf_ref, fox_ref, foxb_ref)):
        qg = g_ref[0:1, :]
        kg = g_ref[1:2, :]
        for h in range(ODD_HEADS):
            sl = slice(h * HEAD_DIM, (h + 1) * HEAD_DIM)
            q_ref[:, sl] = (_rms_rows(pm_ref[:, off + h * HEAD_DIM: off + (h + 1) * HEAD_DIM], qg) * scale
                            ).astype(q_ref.dtype)
        off += ODD_QW
        for c in range(2 * ODD_HEADS):
            src = pm_ref[:, off + c * HEAD_DIM: off + (c + 1) * HEAD_DIM]
            val = _rms_rows(src, kg) if c < ODD_HEADS else src
            kv_ref[:, c * HEAD_DIM:(c + 1) * HEAD_DIM] = val
            kvb_ref[:, c * HEAD_DIM:(c + 1) * HEAD_DIM] = val.astype(BF16)
        off += ODD_KVW
    logf_ref[...] = _log_sigmoid(fz_ref[...] + fb_ref[...])


def _odd_post(pm, fz, moba_qk, fox_qk, fb):
    m = pm.shape[0]
    tm = min(m, ROW_TILE // 2)
    row = lambda w: pl.BlockSpec((tm, w), lambda i: (i, 0))
    full = lambda a: pl.BlockSpec(a.shape, lambda i: (0,) * a.ndim)
    return pl.pallas_call(
        _odd_post_kernel,
        grid=(m // tm,),
        in_specs=[row(pm.shape[1]), row(LANES), full(moba_qk), full(fox_qk), full(fb)],
        out_specs=[row(ODD_QW), row(ODD_KVW), row(ODD_KVW), row(ODD_QW), row(ODD_KVW), row(ODD_KVW), row(LANES)],
        out_shape=[jax.ShapeDtypeStruct((m, ODD_QW), BF16), jax.ShapeDtypeStruct((m, ODD_KVW), F32),
                   jax.ShapeDtypeStruct((m, ODD_KVW), BF16), jax.ShapeDtypeStruct((m, ODD_QW), BF16),
                   jax.ShapeDtypeStruct((m, ODD_KVW), F32), jax.ShapeDtypeStruct((m, ODD_KVW), BF16),
                   jax.ShapeDtypeStruct((m, LANES), F32)],
        compiler_params=_cparams(1),
        name="odd_post",
    )(pm, fz, moba_qk, fox_qk, fb)


def _topk_rank(score, n):
    idx = lax.broadcasted_iota(jnp.int32, (1, n), 1)
    rank = jnp.zeros(score.shape, jnp.int32)
    for mcol in range(n):
        sm = score[:, mcol:mcol + 1]
        ahead = (sm > score) | ((sm == score) & (mcol < idx))
        rank = rank + ahead.astype(jnp.int32)
    return rank


def _moba_gate_kernel(q_ref, k_ref, bm_ref, *, t_len, nbl):
    k_mean = jnp.mean(k_ref[...].reshape(nbl, MOBA_BLOCK, HEAD_DIM), axis=1)
    gate = lax.dot_general(k_mean.astype(BF16), q_ref[...], (((1,), (1,)), ((), ())), preferred_element_type=F32)
    cur = lax.broadcasted_iota(jnp.int32, (1, t_len), 1) // MOBA_BLOCK
    blk = lax.broadcasted_iota(jnp.int32, (nbl, 1), 0)
    past_ok = blk < cur
    gate = jnp.where(past_ok, gate, -jnp.inf)
    rank = jnp.zeros(gate.shape, jnp.int32)
    for mrow in range(nbl):
        sm = gate[mrow:mrow + 1, :]
        ahead = (sm > gate) | ((sm == gate) & (mrow < blk))
        rank = rank + ahead.astype(jnp.int32)
    sel = (rank < MOBA_TOPK) & past_ok & (jnp.abs(gate) < jnp.inf)
    mask_t = (sel | (blk == cur)).astype(F32)
    eye = (lax.broadcasted_iota(jnp.int32, (nbl, nbl), 0) == lax.broadcasted_iota(jnp.int32, (nbl, nbl), 1))
    bm_ref[...] = lax.dot_general(mask_t, eye.astype(F32), (((0,), (0,)), ((), ())), preferred_element_type=F32)


def _moba_gate(qm3, moba_new3):
    bsz, t_len, _ = qm3.shape
    assert t_len % MOBA_BLOCK == 0
    nbl = t_len // MOBA_BLOCK
    return pl.pallas_call(
        functools.partial(_moba_gate_kernel, t_len=t_len, nbl=nbl),
        grid=(bsz, MOBA_HEADS),
        in_specs=[pl.BlockSpec((None, t_len, HEAD_DIM), lambda b, h: (b, 0, h)),
                  pl.BlockSpec((None, t_len, HEAD_DIM), lambda b, h: (b, 0, h))],
        out_specs=pl.BlockSpec((None, None, t_len, nbl), lambda b, h: (b, h, 0, 0)),
        out_shape=jax.ShapeDtypeStruct((bsz, MOBA_HEADS, t_len, nbl), F32),
        compiler_params=_cparams(2),
        name="moba_gate",
    )(qm3, moba_new3)


def _cumsum_kernel(x_ref, o_ref, *, t_len):
    n = Q_TILE
    upper = (lax.broadcasted_iota(jnp.int32, (n, n), 0) <= lax.broadcasted_iota(jnp.int32, (n, n), 1)).astype(F32)
    carry = jnp.zeros((FOX_HEADS, 1), F32)
    for c in range(t_len // n):
        xt = x_ref[c * n:(c + 1) * n, :].T[0:FOX_HEADS, :]
        cum = jnp.dot(xt, upper, preferred_element_type=F32, precision=lax.Precision.HIGHEST) + carry
        o_ref[:, c * n:(c + 1) * n] = cum
        carry = cum[:, n - 1:n]


def _cumsum_heads(logf3):
    bsz, t_len, _ = logf3.shape
    assert t_len % Q_TILE == 0
    return pl.pallas_call(
        functools.partial(_cumsum_kernel, t_len=t_len),
        grid=(bsz,),
        in_specs=[pl.BlockSpec((None, t_len, LANES), lambda b: (b, 0, 0))],
        out_specs=pl.BlockSpec((None, FOX_HEADS, t_len), lambda b: (b, 0, 0)),
        out_shape=jax.ShapeDtypeStruct((bsz, FOX_HEADS, t_len), F32),
        compiler_params=_cparams(1),
        name="fox_cumsum",
    )(logf3)


MOBA_GATE_PAGES_PER_STEP = 8


def _moba_gate_step_kernel(*refs, pps, nblk, ppb):
    q_ref = refs[1]
    k_refs = refs[2:2 + pps]
    idx_ref, gate_ref = refs[2 + pps:]
    st = pl.program_id(1)

    @pl.when(st == 0)
    def _():
        gate_ref[...] = jnp.zeros_like(gate_ref)

    q8 = _group_rows(q_ref, MOBA_HEADS)
    lane = lax.broadcasted_iota(jnp.int32, (1, nblk), 1)
    for blk_in_step in range(pps // ppb):
        ksum = jnp.zeros((MOBA_HEADS, HEAD_DIM), F32)
        for p in range(ppb):
            ksum = ksum + jnp.sum(k_refs[blk_in_step * ppb + p][...], axis=0)
        gcol = jnp.sum(q8 * (ksum * (1.0 / MOBA_BLOCK)), axis=-1, keepdims=True)
        gate_ref[...] = jnp.where(lane == st * (pps // ppb) + blk_in_step, gcol, gate_ref[...])

    @pl.when(st == pl.num_programs(1) - 1)
    def _():
        gate = gate_ref[...]
        rank = _topk_rank(gate, nblk)
        blk = lax.broadcasted_iota(jnp.int32, (1, nblk), 1)
        lane = lax.broadcasted_iota(jnp.int32, (MOBA_HEADS, MOBA_TOPK), 1)
        out = jnp.full((MOBA_HEADS, MOBA_TOPK), -1, jnp.int32)
        for r in range(min(MOBA_TOPK, nblk)):
            hit = (rank == r) & (jnp.abs(gate) < jnp.inf)
            idx_r = jnp.sum(jnp.where(hit, blk + 1, 0), axis=-1, keepdims=True) - 1
            out = jnp.where(lane == r, idx_r, out)
        idx_ref[...] = out


def _moba_gate_step(qm3, pages4, tbl):
    bsz, npg = tbl.shape
    ppb = MOBA_BLOCK // PAGE_ROWS
    pps = MOBA_GATE_PAGES_PER_STEP
    assert npg % pps == 0 and pps % ppb == 0

    def key_spec(p):
        return pl.BlockSpec((None, PAGE_ROWS, MOBA_HEADS, HEAD_DIM), lambda b, st, t: (t[b, st * pps + p], 0, 0, 0))

    grid_spec = pltpu.PrefetchScalarGridSpec(
        num_scalar_prefetch=1,
        grid=(bsz, npg // pps),
        in_specs=[pl.BlockSpec((None, 1, ODD_QW), lambda b, st, t: (b, 0, 0))] + [key_spec(p) for p in range(pps)],
        out_specs=pl.BlockSpec((None, MOBA_HEADS, MOBA_TOPK), lambda b, st, t: (b, 0, 0)),
        scratch_shapes=[pltpu.VMEM((MOBA_HEADS, npg // ppb), F32)],
    )
    return pl.pallas_call(
        functools.partial(_moba_gate_step_kernel, pps=pps, nblk=npg // ppb, ppb=ppb),
        grid_spec=grid_spec,
        out_shape=jax.ShapeDtypeStruct((bsz, MOBA_HEADS, MOBA_TOPK), jnp.int32),
        compiler_params=_cparams(2),
        name="moba_gate_step",
    )(tbl, qm3, *([pages4] * pps))


def _moba_attn_step_kernel(*refs, t_pos, ppb):
    idx_ref, q_ref = refs[1], refs[2]
    kv_refs = refs[3:3 + ppb]
    kn_ref, vn_ref, rbt_ref, o_ref, m_ref, l_ref, acc_ref = refs[3 + ppb:]
    b = pl.program_id(0)
    h = pl.program_id(1)
    s_id = pl.program_id(2)
    sub = 8
    q = q_ref[...].astype(F32)
    rbt = rbt_ref[...]
    col = lambda bk: rbt[:, bk:bk + 1]

    @pl.when(s_id == 0)
    def _():
        s_self = jnp.sum(q * kn_ref[...], axis=-1, keepdims=True) + col(0)
        m_ref[...] = jnp.broadcast_to(s_self, m_ref.shape)
        l_ref[...] = jnp.ones_like(l_ref)
        acc_ref[...] = jnp.broadcast_to(vn_ref[...], acc_ref.shape)

    blk_id = idx_ref[b * MOBA_HEADS + h, s_id]

    @pl.when(blk_id >= 0)
    def _():
        q8 = jnp.broadcast_to(q, (sub, HEAD_DIM)).astype(BF16)
        logits = []
        for pg in range(ppb):
            kt = kv_refs[pg][pl.ds(h, PAGE_ROWS, stride=2 * MOBA_HEADS), :].astype(BF16)
            logits.append(lax.dot_general(q8, kt, (((1,), (1,)), ((), ())), preferred_element_type=F32))
        s = jnp.concatenate(logits, axis=-1)
        key_pos = blk_id * MOBA_BLOCK + lax.broadcasted_iota(jnp.int32, (sub, ppb * PAGE_ROWS), 1)
        s = s + _bucket_bias(t_pos - key_pos, col)
        m_old = m_ref[...]
        m_new = jnp.maximum(m_old, jnp.max(s, axis=-1, keepdims=True))
        p = jnp.exp(s - m_new)
        alpha = jnp.exp(m_old - m_new)
        l_ref[...] = alpha * l_ref[...] + jnp.sum(p, axis=-1, keepdims=True)
        acc = alpha * acc_ref[...]
        for pg in range(ppb):
            vt = kv_refs[pg][pl.ds(MOBA_HEADS + h, PAGE_ROWS, stride=2 * MOBA_HEADS), :].astype(BF16)
            acc = acc + jnp.dot(p[:, pg * PAGE_ROWS:(pg + 1) * PAGE_ROWS].astype(BF16), vt,
                                preferred_element_type=F32)
        acc_ref[...] = acc
        m_ref[...] = m_new

    @pl.when(s_id == pl.num_programs(2) - 1)
    def _():
        o_ref[...] = (acc_ref[0:1, :] / l_ref[0:1, :]).astype(o_ref.dtype)


def _moba_attn_step(qm3, pages3, tbl, idx, moba_new3, rbt3, *, t_pos):
    bsz = qm3.shape[0]
    ppb = MOBA_BLOCK // PAGE_ROWS

    def cache_map(pg):
        def index_map(b, h, s, tbl_ref, idx_ref):
            blk = jnp.maximum(idx_ref[b * MOBA_HEADS + h, s], 0)
            return (tbl_ref[b, blk * ppb + pg], 0, 0)
        return index_map

    head = lambda off: pl.BlockSpec((None, 1, HEAD_DIM), lambda b, h, s, t, i: (b, 0, off + h))
    grid_spec = pltpu.PrefetchScalarGridSpec(
        num_scalar_prefetch=2,
        grid=(bsz, MOBA_HEADS, MOBA_TOPK),
        in_specs=[head(0)]
        + [pl.BlockSpec((None, PAGE_ROWS * 2 * MOBA_HEADS, HEAD_DIM), cache_map(pg)) for pg in range(ppb)]
        + [head(0), head(MOBA_HEADS),
           pl.BlockSpec((None, 1, REL_BUCKETS), lambda b, h, s, t, i: (h, 0, 0))],
        out_specs=head(0),
        scratch_shapes=[pltpu.VMEM((8, 1), F32), pltpu.VMEM((8, 1), F32), pltpu.VMEM((8, HEAD_DIM), F32)],
    )
    return pl.pallas_call(
        functools.partial(_moba_attn_step_kernel, t_pos=t_pos, ppb=ppb),
        grid_spec=grid_spec,
        out_shape=jax.ShapeDtypeStruct((bsz, 1, ODD_QW), BF16),
        compiler_params=_cparams(3),
        name="moba_attn_step",
    )(tbl, idx, qm3, *([pages3] * ppb), moba_new3, moba_new3, rbt3)


FOX_PAGES_PER_STEP = 8


def _fox_step_kernel(*refs, pps):
    q_ref = refs[1]
    k_refs = refs[2:2 + pps]
    v_refs = refs[2 + pps:2 + 2 * pps]
    lf_refs = refs[2 + 2 * pps:2 + 3 * pps]
    kn_ref, vn_ref, lfn_ref, o_ref, m_ref, l_ref, acc_ref, carry_ref = refs[2 + 3 * pps:]
    st = pl.program_id(1)
    n = PAGE_ROWS
    nh = FOX_HEADS
    q8 = _group_rows(q_ref, nh)
    lane8 = lax.broadcasted_iota(jnp.int32, (nh, LANES), 1)
    row8 = lax.broadcasted_iota(jnp.int32, (nh, LANES), 0)

    @pl.when(st == 0)
    def _():
        m_ref[...] = jnp.sum(q8 * _group_rows(kn_ref, nh), axis=-1, keepdims=True)
        l_ref[...] = jnp.ones_like(l_ref)
        acc_ref[...] = _group_rows(vn_ref, nh)
        carry_ref[...] = jnp.sum(jnp.where(lane8 == row8, lfn_ref[...], 0.0), axis=-1, keepdims=True)

    ones = jnp.ones((HEAD_DIM, LANES), BF16)
    after_t = (lax.broadcasted_iota(jnp.int32, (n, n), 0) > lax.broadcasted_iota(jnp.int32, (n, n), 1)).astype(F32)
    on_diag = (lax.broadcasted_iota(jnp.int32, (n, nh, LANES), 0)
               == lax.broadcasted_iota(jnp.int32, (n, nh, LANES), 2))
    carry = carry_ref[...]
    logits = []
    for p in range(pps):
        lf = lf_refs[p][...]
        bias = carry + jnp.dot(lf, after_t, preferred_element_type=F32, precision=lax.Precision.HIGHEST)
        carry = carry + jnp.sum(lf, axis=-1, keepdims=True)
        prod = (k_refs[p][...] * q8[None]).reshape(n * nh, HEAD_DIM).astype(BF16)
        qk_rep = jnp.dot(prod, ones, preferred_element_type=F32).reshape(n, nh, LANES)
        logits.append(jnp.sum(jnp.where(on_diag, qk_rep, 0.0), axis=0) + bias)
    carry_ref[...] = carry
    s = jnp.concatenate(logits, axis=-1)
    m_old = m_ref[...]
    m_new = jnp.maximum(m_old, jnp.max(s, axis=-1, keepdims=True))
    prob = jnp.exp(s - m_new)
    alpha = jnp.exp(m_old - m_new)
    l_ref[...] = alpha * l_ref[...] + jnp.sum(prob, axis=-1, keepdims=True)
    m_ref[...] = m_new
    acc = alpha * acc_ref[...]
    for p in range(pps):
        prob_p = prob[:, p * n:(p + 1) * n]
        spread = jnp.where(on_diag, prob_p[None], 0.0).reshape(n * nh, LANES).astype(BF16)
        p_rep = jnp.dot(spread, ones, preferred_element_type=F32).reshape(n, nh, HEAD_DIM)
        acc = acc + jnp.sum(p_rep * v_refs[p][...], axis=0)
    acc_ref[...] = acc

    @pl.when(st == pl.num_programs(1) - 1)
    def _():
        _store_group_rows(o_ref, acc_ref[...] / l_ref[...])


def _fox_step(qf3, pages4, logf_pages, tbl, fox_new3, logf_new3):
    bsz, npg = tbl.shape
    pps = math.gcd(FOX_PAGES_PER_STEP, npg)
    page = lambda p: (lambda b, st, t: t[b, npg - 1 - (st * pps + p)])
    kv_spec = lambda p, part: pl.BlockSpec((None, PAGE_ROWS, FOX_HEADS, HEAD_DIM),
                                           lambda b, st, t: (page(p)(b, st, t), 0, part, 0))
    lf_spec = lambda p: pl.BlockSpec((None, FOX_HEADS, PAGE_ROWS), lambda b, st, t: (page(p)(b, st, t), 0, 0))
    new = lambda col: pl.BlockSpec((None, 1, ODD_QW), lambda b, st, t: (b, 0, col))
    grid_spec = pltpu.PrefetchScalarGridSpec(
        num_scalar_prefetch=1,
        grid=(bsz, npg // pps),
        in_specs=[new(0)] + [kv_spec(p, 0) for p in range(pps)] + [kv_spec(p, 1) for p in range(pps)]
        + [lf_spec(p) for p in range(pps)]
        + [new(0), new(1), pl.BlockSpec((None, 1, LANES), lambda b, st, t: (b, 0, 0))],
        out_specs=new(0),
        scratch_shapes=[pltpu.VMEM((FOX_HEADS, 1), F32), pltpu.VMEM((FOX_HEADS, 1), F32),
                        pltpu.VMEM((FOX_HEADS, HEAD_DIM), F32), pltpu.VMEM((FOX_HEADS, 1), F32)],
    )
    return pl.pallas_call(
        functools.partial(_fox_step_kernel, pps=pps),
        grid_spec=grid_spec,
        out_shape=jax.ShapeDtypeStruct((bsz, 1, ODD_QW), BF16),
        compiler_params=_cparams(2),
        name="fox_step",
    )(tbl, qf3, *([pages4] * (2 * pps)), *([logf_pages] * pps), fox_new3, fox_new3, logf_new3)


def _odd_proj(h, hr, w, j):
    n_main = 2 * (ODD_QW + ODD_KVW)
    w_in = w["w_in_odd"]
    hr_parts = None if hr is None else [hr]
    pm, pmr = _dense2([h], hr_parts, w_in, j, 0, n_main, tn=512, name="odd_in")
    w_fz = jnp.pad(w_in[j, :, n_main:n_main + FOX_HEADS], ((0, 0), (0, LANES - FOX_HEADS)))
    fz, fzr = _dense2([h], hr_parts, w_fz, None, 0, LANES, tn=LANES, name="odd_in_forget")
    return (pm, fz), (None if hr is None else (pmr, fzr))


def _odd_out(x2d, a_parts, xr2d, ar_parts, w, j):
    d = x2d.shape[1]
    return _dense2(a_parts, ar_parts, w["w_out_odd"], j, 0, d, tn=min(d, 512), mode="res", res=x2d,
                   r_res=xr2d, scale=1.0, name="odd_out")


def _odd_mixer(x2d, h, bsz, t_len, j, w, tabs, past):
    proj, _ = _odd_proj(h, None, w, j)
    a_parts, new = _odd_core(proj, bsz, t_len, j, w, tabs, past)
    x_new, _ = _odd_out(x2d, a_parts, None, None, w, j)
    return (x_new,) + new


def _odd_core(proj, bsz, t_len, j, w, tabs, past):
    pm, fz = proj
    m = pm.shape[0]
    fb = jnp.pad(w["fox_f_bias"][j].astype(F32), (0, LANES - FOX_HEADS)).reshape(1, LANES)
    qm, moba_new, mobab, qf, fox_new, foxb, logf = _odd_post(pm, fz, w["moba_qk_norm"][j], w["fox_qk_norm"][j], fb)
    qm3 = qm.reshape(bsz, t_len, ODD_QW)
    qf3 = qf.reshape(bsz, t_len, ODD_QW)
    logf3 = logf.reshape(bsz, t_len, LANES)
    if past is None:
        bm = _moba_gate(qm3, moba_new.reshape(bsz, t_len, ODD_KVW))
        o_m = _flash("moba", qm3, mobab.reshape(bsz, t_len, ODD_KVW), 0, MOBA_HEADS, MOBA_HEADS,
                     bias_tiles=tabs["tiles"], bm=bm, blk=MOBA_BLOCK, out_dtype=BF16)
        cum = _cumsum_heads(logf3).reshape(bsz, FOX_HEADS, 1, t_len)
        o_f = _flash("fox", qf3, foxb.reshape(bsz, t_len, ODD_KVW), 0, FOX_HEADS, FOX_HEADS, cum=cum,
                     out_dtype=BF16)
    else:
        moba_pages4, fox_pages, logf_pages, tbl = past
        t_pos = tbl.shape[1] * PAGE_ROWS
        idx = _moba_gate_step(qm3, moba_pages4, tbl)
        rbt3 = w["rel_bias"].T.reshape(MOBA_HEADS, 1, REL_BUCKETS)
        moba_pages3 = moba_pages4.reshape(moba_pages4.shape[0], PAGE_ROWS * 2 * MOBA_HEADS, HEAD_DIM)
        o_m = _moba_attn_step(qm3, moba_pages3, tbl, idx.reshape(bsz * MOBA_HEADS, MOBA_TOPK),
                              moba_new.reshape(bsz, t_len, ODD_KVW), rbt3, t_pos=t_pos)
        o_f = _fox_step(qf3, fox_pages, logf_pages, tbl, fox_new.reshape(bsz, t_len, ODD_KVW), logf3)
    kv_shape = (bsz, t_len, 2, ODD_HEADS, HEAD_DIM)
    new = (moba_new.reshape(kv_shape), fox_new.reshape(kv_shape), logf3[:, :, :FOX_HEADS])
    return [o_m.reshape(m, ODD_QW), o_f.reshape(m, ODD_QW)], new


def _trunk(x_prompt, x_sample, w, tabs, caches):
    bp, tp, d = x_prompt.shape
    bs, ts, _ = x_sample.shape
    depth = w["norm_mix"].shape[0]
    xp = x_prompt.reshape(bp * tp, d)
    xs = x_sample.reshape(bs * ts, d)
    pe, po, se, so = [], [], [], []
    for layer in range(depth):
        xp, xs = _ffn(xp, xs, w["norm_ffn1"], w["w_ffn1_in"], w["w_ffn1_out"], layer)
        hp = _rms_cast(xp, w["norm_mix"], layer)
        hs = _rms_cast(xs, w["norm_mix"], layer)
        j = layer // 2
        tbl = caches["tbl"] + j * caches["n_phys"]
        if layer % 2 == 0:
            proj_p, proj_s = _even_proj(hp, hs, w, j)
            ap, new_p = _even_core(proj_p, bp, tp, j, w, tabs, None)
            past = (caches["cmp"], caches["sel"], tbl, caches["win"], caches["hgrn"][j])
            a_s, new_s = _even_core(proj_s, bs, ts, j, w, tabs, past)
            xp, xs = _even_out(xp, ap, xs, a_s, w, j)
            pe.append(new_p)
            se.append(new_s)
        else:
            proj_p, proj_s = _odd_proj(hp, hs, w, j)
            ap, new_p = _odd_core(proj_p, bp, tp, j, w, tabs, None)
            past = (caches["moba"], caches["fox"], caches["logf"], tbl)
            a_s, new_s = _odd_core(proj_s, bs, ts, j, w, tabs, past)
            xp, xs = _odd_out(xp, ap, xs, a_s, w, j)
            po.append(new_p)
            so.append(new_s)
        xp, xs = _ffn(xp, xs, w["norm_ffn2"], w["w_ffn2_in"], w["w_ffn2_out"], layer)
    return xp.reshape(bp, tp, d), xs.reshape(bs, ts, d), pe, po, se, so


def kernel(x_prompt, x_sample, cache_nsa_cmp, cache_nsa_sel, cache_moba, cache_fox, cache_fox_logf,
           cache_nsa_win, state_hgrn, page_table, norm_ffn1, w_ffn1_in, w_ffn1_out, norm_mix, norm_ffn2,
           w_ffn2_in, w_ffn2_out, rel_bias, w_in_even, w_out_even, nsa_q_norm, nsa_k_norm, nsa_phi_pos,
           nsa_phi_w1, nsa_phi_w2, hg_lb, hg_o_norm, w_in_odd, w_out_odd, fox_f_bias, moba_qk_norm,
           fox_qk_norm):
    w = dict(norm_ffn1=norm_ffn1, w_ffn1_in=w_ffn1_in, w_ffn1_out=w_ffn1_out, norm_mix=norm_mix,
             norm_ffn2=norm_ffn2, w_ffn2_in=w_ffn2_in, w_ffn2_out=w_ffn2_out, rel_bias=rel_bias,
             w_in_even=w_in_even, w_out_even=w_out_even, nsa_q_norm=nsa_q_norm, nsa_k_norm=nsa_k_norm,
             nsa_phi_pos=nsa_phi_pos, nsa_phi_w1=nsa_phi_w1, nsa_phi_w2=nsa_phi_w2, hg_lb=hg_lb,
             hg_o_norm=hg_o_norm, w_in_odd=w_in_odd, w_out_odd=w_out_odd, fox_f_bias=fox_f_bias,
             moba_qk_norm=moba_qk_norm, fox_qk_norm=fox_qk_norm)
    tabs = _tables(w)
    n_phys = cache_nsa_cmp.shape[1]
    n_layers = cache_nsa_cmp.shape[0]
    rows3 = lambda pool: pool.reshape(n_layers * n_phys, -1, HEAD_DIM)
    rows4 = lambda pool: pool.reshape(n_layers * n_phys, PAGE_ROWS, -1, HEAD_DIM)
    logf_t = jnp.swapaxes(cache_fox_logf.reshape(n_layers * n_phys, PAGE_ROWS, FOX_HEADS), 1, 2)
    caches = dict(cmp=rows3(cache_nsa_cmp), sel=rows3(cache_nsa_sel), moba=rows4(cache_moba),
                  fox=rows4(cache_fox), logf=logf_t, win=cache_nsa_win, hgrn=state_hgrn,
                  tbl=page_table.astype(jnp.int32), n_phys=n_phys)
    y_prompt, y_sample, pe, po, se, so = _trunk(x_prompt, x_sample, w, tabs, caches)

    stack = lambda items, i: jnp.stack([it[i] for it in items])
    return (y_prompt, y_sample,
            stack(pe, 0), stack(pe, 1), stack(pe, 2), stack(pe, 3),
            stack(po, 0), stack(po, 1), stack(po, 2),
            stack(se, 0), stack(se, 1), stack(se, 2), stack(se, 3),
            stack(so, 0), stack(so, 1), stack(so, 2))
```
